```python
import jax
import jax.numpy as jnp
from jax import lax
import numpy as np

D_MODEL = 4096
BATCH = 2
SEQ = 8192
DEPTH = 1

GRID_W = 64
CTX_LEN = 256

D_RWKV = D_MODEL // 2
RWKV_HEAD = 64
RWKV_HEADS = D_RWKV // RWKV_HEAD
DECAY_LORA = 96
ICLR_LORA = 96
GATE_LORA = 256
GN_EPS = 64e-5

ATT_HEAD = 128
ATT_HEADS = (D_MODEL // 2) // ATT_HEAD
ATT_KV_HEADS = 4
ATT_GROUPS = ATT_HEADS // ATT_KV_HEADS
D_ATT = ATT_HEADS * ATT_HEAD
D_ATT_KV = ATT_KV_HEADS * ATT_HEAD
WINDOW = 128
BLOCK = 128
ROPE_BASE = 10000.0

D_FF = -(-(8 * D_MODEL) // (3 * 256)) * 256

LN_EPS = 1e-5
DEEPNORM_ALPHA = (2 * DEPTH) ** 0.25
DEEPNORM_BETA = (8 * DEPTH) ** -0.25
NEG_INF = -1e30

N_RWKV_COLS = 3 * D_RWKV + 2 * DECAY_LORA + 2 * ICLR_LORA + GATE_LORA
N_IN = N_RWKV_COLS + D_ATT + 2 * D_ATT_KV + 2 * D_MODEL
IN_SPLIT = (N_RWKV_COLS, N_RWKV_COLS + D_ATT, N_RWKV_COLS + D_ATT + D_ATT_KV,
            N_RWKV_COLS + D_ATT + 2 * D_ATT_KV, N_RWKV_COLS + D_ATT + 2 * D_ATT_KV + D_MODEL)
RWKV_SPLIT = (D_RWKV, 2 * D_RWKV, 3 * D_RWKV, 3 * D_RWKV + 2 * DECAY_LORA,
              3 * D_RWKV + 2 * DECAY_LORA + 2 * ICLR_LORA)

kernel_name = 'hybrid_rwkv7_swa_dit_block'


def layer_norm(x, g, b):
    xf = x.astype(jnp.float32)
    mu = jnp.mean(xf, -1, keepdims=True)
    var = jnp.mean(jnp.square(xf - mu), -1, keepdims=True)
    return ((xf - mu) * lax.rsqrt(var + LN_EPS) * g + b).astype(x.dtype)


def centred_conv3(x, w):
    xp = jnp.pad(x, ((0, 0), (1, 1), (0, 0)))
    return xp[:, :-2] * w[0] + xp[:, 1:-1] * w[1] + xp[:, 2:] * w[2]


def axial_rope(T):
    rows = T // GRID_W
    row = jnp.broadcast_to(jnp.arange(rows, dtype=jnp.float32)[:, None], (rows, GRID_W)).reshape(T)
    col = jnp.broadcast_to(jnp.arange(GRID_W, dtype=jnp.float32)[None, :], (rows, GRID_W)).reshape(T)
    axis_dim = ATT_HEAD // 2
    inv = ROPE_BASE ** (-jnp.arange(0, axis_dim, 2, dtype=jnp.float32) / axis_dim)
    ang = jnp.concatenate([row[:, None] * inv, col[:, None] * inv], -1)
    return jnp.cos(ang)[None, :, None, :], jnp.sin(ang)[None, :, None, :]


def apply_rope(x, cos, sin):
    xf = x.astype(jnp.float32)
    x1, x2 = xf[..., :ATT_HEAD // 2], xf[..., ATT_HEAD // 2:]
    return jnp.concatenate([x1 * cos - x2 * sin, x2 * cos + x1 * sin], -1).astype(x.dtype)


def heads64(t):
    return t.reshape(t.shape[:-1] + (RWKV_HEADS, RWKV_HEAD))


def rwkv_features(cols, shift_w, w0, w_up, a0, a_up, g_up, k_k, k_a):
    B, T = cols.shape[:2]
    x = centred_conv3(cols, shift_w)
    r, k, v, wd, ad, gd = jnp.split(x, RWKV_SPLIT, axis=-1)
    wd = wd.reshape(B, T, 2, DECAY_LORA)
    ad = ad.reshape(B, T, 2, ICLR_LORA)
    w_log = -jax.nn.softplus(-(w0 + jnp.einsum('btdr,drc->btdc', jnp.tanh(wd), w_up))) - 0.5
    decay = jnp.exp(-jnp.exp(w_log.astype(jnp.float32)))
    iclr = jax.nn.sigmoid(a0 + jnp.einsum('btdr,drc->btdc', ad, a_up))
    kk = heads64((k * k_k).astype(jnp.float32))
    kk = kk / jnp.maximum(jnp.linalg.norm(kk, axis=-1, keepdims=True), 1e-12)
    k_dir = k[:, :, None, :] * (1.0 + (iclr - 1.0) * k_a)
    b = kk[:, :, None] * heads64(iclr).astype(jnp.float32)
    g = jax.nn.sigmoid(gd) @ g_up
    return heads64(r), heads64(v), g, heads64(decay), heads64(k_dir), kk, b


def rwkv7_scan(state, r, decay, k, v, kk, b, reverse, with_outputs):
    def step(S, inp):
        r_t, w_t, k_t, v_t, kk_t, b_t = inp
        sa = jnp.einsum('bhvk,bhk->bhv', S, kk_t)
        S = S * w_t[:, :, None, :] - sa[..., None] * b_t[:, :, None, :] + v_t[..., None] * k_t[:, :, None, :]
        if with_outputs:
            return S, jnp.einsum('bhvk,bhk->bhv', S, r_t)
        return S, None
    xs = tuple(jnp.moveaxis(t.astype(jnp.float32), 1, 0) for t in (r, decay, k, v, kk, b))
    S, ys = lax.scan(step, state, xs, reverse=reverse)
    if with_outputs:
        ys = jnp.moveaxis(ys, 0, 1)
    return S, ys


def rwkv_output(y_fwd, y_bwd, r, k_dir, v, g, r_k, gn_g, gn_b):
    B, T = g.shape[:2]
    y = y_fwd + y_bwd
    mu = jnp.mean(y, -1, keepdims=True)
    var = jnp.mean(jnp.square(y - mu), -1, keepdims=True)
    yn = ((y - mu) * lax.rsqrt(var + GN_EPS)).reshape(B, T, D_RWKV) * gn_g + gn_b
    k_sum = (k_dir[:, :, 0] + k_dir[:, :, 1]).astype(jnp.float32)
    bonus = jnp.sum(r.astype(jnp.float32) * k_sum * r_k, -1, keepdims=True) * v.astype(jnp.float32)
    return ((yn + bonus.reshape(B, T, D_RWKV)) * g).astype(g.dtype)


def windowed_attention(q, k, v, k_ctx, v_ctx, sink):
    B, T = q.shape[:2]
    L = k_ctx.shape[1]
    nb = T // BLOCK
    qb = q.reshape(B, nb, BLOCK, ATT_KV_HEADS, ATT_GROUPS, ATT_HEAD)
    pad = ((0, 0), (BLOCK, BLOCK), (0, 0), (0, 0))
    kp = jnp.pad(k, pad).reshape(B, nb + 2, BLOCK, ATT_KV_HEADS, ATT_HEAD)
    vp = jnp.pad(v, pad).reshape(B, nb + 2, BLOCK, ATT_KV_HEADS, ATT_HEAD)
    band = lambda t: jnp.concatenate([t[:, :-2], t[:, 1:-1], t[:, 2:]], axis=2)
    kw, vw = band(kp), band(vp)
    scale = ATT_HEAD ** -0.5
    s_win = jnp.einsum('bnqhgd,bnkhd->bnhgqk', qb, kw).astype(jnp.float32) * scale
    q_off = jnp.arange(BLOCK)[:, None]
    k_off = jnp.arange(3 * BLOCK)[None, :] - BLOCK
    in_window = jnp.abs(q_off - k_off) <= WINDOW
    k_abs = jnp.arange(nb)[:, None] * BLOCK + k_off
    in_seq = (k_abs >= 0) & (k_abs < T)
    mask = in_window[None] & in_seq[:, None, :]
    s_win = jnp.where(mask[None, :, None, None], s_win, NEG_INF)
    s_ctx = jnp.einsum('bnqhgd,blhd->bnhgql', qb, k_ctx).astype(jnp.float32) * scale
    s_sink = jnp.broadcast_to(sink.astype(jnp.float32).reshape(ATT_KV_HEADS, ATT_GROUPS)[None, None, :, :, None, None],
                              s_win.shape[:-1] + (1,))
    p = jax.nn.softmax(jnp.concatenate([s_win, s_ctx, s_sink], -1), axis=-1).astype(v.dtype)
    nw = 3 * BLOCK
    o = (jnp.einsum('bnhgqk,bnkhd->bnqhgd', p[..., :nw], vw)
         + jnp.einsum('bnhgql,blhd->bnqhgd', p[..., nw:nw + L], v_ctx))
    return o.reshape(B, T, D_ATT)


def context_attention(q_c, k_c, v_c, sink):
    B, L = q_c.shape[:2]
    qg = q_c.reshape(B, L, ATT_KV_HEADS, ATT_GROUPS, ATT_HEAD)
    s = jnp.einsum('bqhgd,bkhd->bhgqk', qg, k_c).astype(jnp.float32) * ATT_HEAD ** -0.5
    s_sink = jnp.broadcast_to(sink.astype(jnp.float32).reshape(ATT_KV_HEADS, ATT_GROUPS)[None, :, :, None, None],
                              s.shape[:-1] + (1,))
    p = jax.nn.softmax(jnp.concatenate([s, s_sink], -1), axis=-1)[..., :L].astype(v_c.dtype)
    return jnp.einsum('bhgqk,bkhd->bqhgd', p, v_c).reshape(B, L, D_ATT)


def merge_branches(o_rwkv, o_att, gate_r, gate_a, w_rwkv_o, w_att_o, w_out):
    y = jax.nn.sigmoid(gate_r) * (o_rwkv @ w_rwkv_o) + jax.nn.sigmoid(gate_a) * (o_att @ w_att_o)
    return y @ w_out


def token_mixer(h, h_c, cos, sin, w_in, shift_w, w0, w_up, a0, a_up, g_up, k_k, k_a, r_k, gn_g, gn_b,
                sink, w_rwkv_o, w_att_o, w_out, ctx_out):
    B, T = h.shape[:2]
    L = h_c.shape[1]
    rw, q, k, v, gate_r, gate_a = jnp.split(h @ w_in, IN_SPLIT, axis=-1)
    rw_c, q_c, k_c, v_c, gate_r_c, gate_a_c = jnp.split(h_c @ w_in, IN_SPLIT, axis=-1)

    r, vr, g, decay, k_dir, kk, b = rwkv_features(rw, shift_w, w0, w_up, a0, a_up, g_up, k_k, k_a)
    r_c, vr_c, g_c, decay_c, k_dir_c, kk_c, b_c = rwkv_features(rw_c, shift_w, w0, w_up, a0, a_up, g_up, k_k, k_a)
    zero = jnp.zeros((B, RWKV_HEADS, RWKV_HEAD, RWKV_HEAD), jnp.float32)
    ys, ys_c = [], []
    for d in range(2):
        rev = d == 1
        S_c, y_c = rwkv7_scan(zero, r_c, decay_c[:, :, d], k_dir_c[:, :, d], vr_c, kk_c, b_c[:, :, d], rev, ctx_out)
        _, y = rwkv7_scan(S_c, r, decay[:, :, d], k_dir[:, :, d], vr, kk, b[:, :, d], rev, True)
        ys.append(y)
        ys_c.append(y_c)
    o_rwkv = rwkv_output(ys[0], ys[1], r, k_dir, vr, g, r_k, gn_g, gn_b)

    qh = apply_rope(q.reshape(B, T, ATT_HEADS, ATT_HEAD), cos, sin)
    kh = apply_rope(k.reshape(B, T, ATT_KV_HEADS, ATT_HEAD), cos, sin)
    vh = v.reshape(B, T, ATT_KV_HEADS, ATT_HEAD)
    kc = k_c.reshape(B, L, ATT_KV_HEADS, ATT_HEAD)
    vc = v_c.reshape(B, L, ATT_KV_HEADS, ATT_HEAD)
    o_att = windowed_attention(qh, kh, vh, kc, vc, sink)

    y = merge_branches(o_rwkv, o_att, gate_r, gate_a, w_rwkv_o, w_att_o, w_out)
    if not ctx_out:
        return y, None
    o_rwkv_c = rwkv_output(ys_c[0], ys_c[1], r_c, k_dir_c, vr_c, g_c, r_k, gn_g, gn_b)
    o_att_c = context_attention(q_c.reshape(B, L, ATT_HEADS, ATT_HEAD), kc, vc, sink)
    y_c = merge_branches(o_rwkv_c, o_att_c, gate_r_c, gate_a_c, w_rwkv_o, w_att_o, w_out)
    return y, y_c


def swiglu(h, w_gate, w_up, w_down):
    return (jax.nn.silu(h @ w_gate) * (h @ w_up)) @ w_down


def setup_inputs(seed: int = 0) -> dict:
    key = jax.random.key(seed)
    ks = iter(jax.random.split(key, 40))

    def nrm(shape, scale):
        return scale * jax.random.normal(next(ks), shape, jnp.float32)

    C = D_RWKV
    return {
        'x': nrm((BATCH, SEQ, D_MODEL), 1.0),
        'c': nrm((BATCH, D_MODEL), 1.0),
        'ctx': nrm((BATCH, CTX_LEN, D_MODEL), 1.0),
        'c_ctx': nrm((D_MODEL,), 1.0),
        'w_ada': nrm((DEPTH, D_MODEL, 6 * D_MODEL), 0.5 * D_MODEL ** -0.5),
        'b_ada': nrm((DEPTH, 6 * D_MODEL), 0.02),
        'w_in': nrm((DEPTH, D_MODEL, N_IN), D_MODEL ** -0.5),
        'rwkv_shift': nrm((DEPTH, 3, N_RWKV_COLS), 0.2) + jnp.array([0.0, 1.0, 0.0], jnp.float32)[None, :, None],
        'rwkv_w0': jax.random.uniform(next(ks), (DEPTH, 2, C), jnp.float32, -6.0, -1.0),
        'rwkv_w_up': nrm((DEPTH, 2, DECAY_LORA, C), 0.3 * DECAY_LORA ** -0.5),
        'rwkv_a0': nrm((DEPTH, 2, C), 0.5),
        'rwkv_a_up': nrm((DEPTH, 2, ICLR_LORA, C), ICLR_LORA ** -0.5),
        'rwkv_g_up': nrm((DEPTH, GATE_LORA, C), GATE_LORA ** -0.5),
        'rwkv_k_k': 0.85 + nrm((DEPTH, C), 0.05),
        'rwkv_k_a': 1.0 + nrm((DEPTH, C), 0.05),
        'rwkv_r_k': nrm((DEPTH, RWKV_HEADS, RWKV_HEAD), 0.1),
        'rwkv_gn_g': 1.0 + nrm((DEPTH, C), 0.02),
        'rwkv_gn_b': nrm((DEPTH, C), 0.02),
        'attn_sink': nrm((DEPTH, ATT_HEADS), 0.5),
        'w_rwkv_o': nrm((DEPTH, C, D_MODEL), C ** -0.5),
        'w_att_o': nrm((DEPTH, D_ATT, D_MODEL), D_ATT ** -0.5),
        'w_out': nrm((DEPTH, D_MODEL, D_MODEL), DEEPNORM_BETA * D_MODEL ** -0.5),
        'ln1_g': 1.0 + nrm((DEPTH, D_MODEL), 0.02),
        'ln1_b': nrm((DEPTH, D_MODEL), 0.02),
        'w_ff_gate': nrm((DEPTH, D_MODEL, D_FF), D_MODEL ** -0.5),
        'w_ff_up': nrm((DEPTH, D_MODEL, D_FF), D_MODEL ** -0.5),
        'w_ff_down': nrm((DEPTH, D_FF, D_MODEL), DEEPNORM_BETA * D_FF ** -0.5),
        'ln2_g': 1.0 + nrm((DEPTH, D_MODEL), 0.02),
        'ln2_b': nrm((DEPTH, D_MODEL), 0.02),
    }


def reference(x, c, ctx, c_ctx, w_ada, b_ada, w_in, rwkv_shift, rwkv_w0, rwkv_w_up, rwkv_a0, rwkv_a_up,
              rwkv_g_up, rwkv_k_k, rwkv_k_a, rwkv_r_k, rwkv_gn_g, rwkv_gn_b, attn_sink, w_rwkv_o, w_att_o,
              w_out, ln1_g, ln1_b, w_ff_gate, w_ff_up, w_ff_down, ln2_g, ln2_b):
    T = x.shape[1]
    cos, sin = axial_rope(T)
    x_c = ctx
    for l in range(DEPTH):
        last = l == DEPTH - 1
        sh1, sc1, gt1, sh2, sc2, gt2 = jnp.split((jax.nn.silu(c) @ w_ada[l] + b_ada[l])[:, None, :], 6, axis=-1)
        sh1c, sc1c, gt1c, sh2c, sc2c, gt2c = jnp.split(jax.nn.silu(c_ctx) @ w_ada[l] + b_ada[l], 6, axis=-1)

        h = x * (1.0 + sc1) + sh1
        h_c = x_c * (1.0 + sc1c) + sh1c
        y, y_c = token_mixer(h, h_c, cos, sin, w_in[l], rwkv_shift[l], rwkv_w0[l], rwkv_w_up[l], rwkv_a0[l],
                             rwkv_a_up[l], rwkv_g_up[l], rwkv_k_k[l], rwkv_k_a[l], rwkv_r_k[l], rwkv_gn_g[l],
                             rwkv_gn_b[l], attn_sink[l], w_rwkv_o[l], w_att_o[l], w_out[l], not last)
        x = layer_norm(DEEPNORM_ALPHA * x + gt1 * y, ln1_g[l], ln1_b[l])

        h = x * (1.0 + sc2) + sh2
        x = layer_norm(DEEPNORM_ALPHA * x + gt2 * swiglu(h, w_ff_gate[l], w_ff_up[l], w_ff_down[l]), ln2_g[l], ln2_b[l])

        if not last:
            x_c = layer_norm(DEEPNORM_ALPHA * x_c + gt1c * y_c, ln1_g[l], ln1_b[l])
            h_c = x_c * (1.0 + sc2c) + sh2c
            x_c = layer_norm(DEEPNORM_ALPHA * x_c + gt2c * swiglu(h_c, w_ff_gate[l], w_ff_up[l], w_ff_down[l]),
                             ln2_g[l], ln2_b[l])
    return x
```

```python
import functools

import jax
import jax.numpy as jnp
import numpy as np
from jax import lax
from jax.experimental import pallas as pl
from jax.experimental.pallas import tpu as pltpu

F32 = jnp.float32
BF16 = jnp.bfloat16
HIGHEST = lax.Precision.HIGHEST

RWKV_HEAD = 64
DECAY_LORA = 96
ICLR_LORA = 96
GATE_LORA = 256
GN_EPS = 64e-5
ATT_HEAD = 128
ATT_KV_HEADS = 4
ATT_GROUPS = 4
GRID_W = 64
ROPE_BASE = 10000.0
LN_EPS = 1e-5
DEPTH = 1
DEEPNORM_ALPHA = (2 * DEPTH) ** 0.25
NEG_INF = -1e30

LANES = 128
SUBLANES = 8
VMEM_LIMIT = 56 * 1024 * 1024
MOD_ROWS = 8
SCAN_CHUNK = 64
SCAN_LANES = 256
LORA_PAD = 1024


def _cparams(sem):
    return pltpu.CompilerParams(dimension_semantics=sem, vmem_limit_bytes=VMEM_LIMIT)


def _dot(a, b, precision=None):
    return jnp.dot(a, b, preferred_element_type=F32, precision=precision)


def _dot_nt(a, b, precision=None):
    return lax.dot_general(a, b, (((1,), (1,)), ((), ())), preferred_element_type=F32, precision=precision)


def _dot_tn(a, b, precision=None):
    return lax.dot_general(a, b, (((0,), (0,)), ((), ())), preferred_element_type=F32, precision=precision)


def _ada_kernel(c_ref, w_ref, b_ref, o_ref):
    a = c_ref[...]
    a = a * jax.nn.sigmoid(a)
    o_ref[...] = _dot(a, w_ref[...], HIGHEST) + b_ref[...]


def _ada(cc, w_ada, b_ada):
    d, n = w_ada.shape
    tn = 512
    return pl.pallas_call(
        _ada_kernel,
        grid=(n // tn,),
        in_specs=[pl.BlockSpec((MOD_ROWS, d), lambda j: (0, 0)),
                  pl.BlockSpec((d, tn), lambda j: (0, j)),
                  pl.BlockSpec((1, tn), lambda j: (0, j))],
        out_specs=pl.BlockSpec((MOD_ROWS, tn), lambda j: (0, j)),
        out_shape=jax.ShapeDtypeStruct((MOD_ROWS, n), F32),
        compiler_params=_cparams(("arbitrary",)),
        name="ada",
    )(cc, w_ada, b_ada.reshape(1, n))


def _prep_kernel(x_ref, ctx_ref, sh_ref, sc_ref, o_ref, *, n_lat, ctx_row):
    b = pl.program_id(0)
    j = pl.program_id(1)
    is_ctx = j >= n_lat
    row = jnp.where(is_ctx, ctx_row, b)
    sh = sh_ref[pl.ds(row, 1), :]
    sc = sc_ref[pl.ds(row, 1), :]
    xin = jnp.where(is_ctx, ctx_ref[0], x_ref[0])
    o_ref[0] = (xin * (1.0 + sc) + sh).astype(BF16)


def _prep(x, ctx, mod):
    bsz, t, d = x.shape
    l = ctx.shape[1]
    tb = l
    n_lat = t // tb
    return pl.pallas_call(
        functools.partial(_prep_kernel, n_lat=n_lat, ctx_row=bsz),
        grid=(bsz, n_lat + 1),
        in_specs=[pl.BlockSpec((1, tb, d), lambda b, j: (b, jnp.minimum(j, n_lat - 1), 0)),
                  pl.BlockSpec((1, l, d), lambda b, j: (b, 0, 0)),
                  pl.BlockSpec((MOD_ROWS, d), lambda b, j: (0, 0)),
                  pl.BlockSpec((MOD_ROWS, d), lambda b, j: (0, 1))],
        out_specs=pl.BlockSpec((1, tb, d), lambda b, j: (b, j, 0)),
        out_shape=jax.ShapeDtypeStruct((bsz, t + l, d), BF16),
        compiler_params=_cparams(("arbitrary", "arbitrary")),
        name="prep",
    )(x, ctx, mod, mod)


def _mm_kernel(a_ref, w_ref, o_ref):
    o_ref[...] = _dot(a_ref[...], w_ref[...]).astype(o_ref.dtype)


def _mm_wstat(a, w, tm, tn, out_dtype, name):
    m, k = a.shape
    n = w.shape[1]
    return pl.pallas_call(
        _mm_kernel,
        grid=(n // tn, m // tm),
        in_specs=[pl.BlockSpec((tm, k), lambda j, i: (i, 0)),
                  pl.BlockSpec((k, tn), lambda j, i: (0, j))],
        out_specs=pl.BlockSpec((tm, tn), lambda j, i: (i, j)),
        out_shape=jax.ShapeDtypeStruct((m, n), out_dtype),
        compiler_params=_cparams(("arbitrary", "arbitrary")),
        name=name,
    )(a, w)


def _mm_astat(a, w, tm, tn, out_dtype, name):
    m, k = a.shape
    n = w.shape[1]
    return pl.pallas_call(
        _mm_kernel,
        grid=(m // tm, n // tn),
        in_specs=[pl.BlockSpec((tm, k), lambda i, j: (i, 0)),
                  pl.BlockSpec((k, tn), lambda i, j: (0, j))],
        out_specs=pl.BlockSpec((tm, tn), lambda i, j: (i, j)),
        out_shape=jax.ShapeDtypeStruct((m, n), out_dtype),
        compiler_params=_cparams(("arbitrary", "arbitrary")),
        name=name,
    )(a, w)


def _mm_acc_kernel(a_ref, w_ref, o_ref, acc_ref):
    kk = pl.program_id(2)

    @pl.when(kk == 0)
    def _():
        acc_ref[...] = jnp.zeros_like(acc_ref)

    acc_ref[...] += _dot(a_ref[...], w_ref[...])

    @pl.when(kk == pl.num_programs(2) - 1)
    def _():
        o_ref[...] = acc_ref[...].astype(o_ref.dtype)


def _mm_ktiled(a, w, tm, tn, tk, out_dtype, name):
    m, k = a.shape
    n = w.shape[1]
    return pl.pallas_call(
        _mm_acc_kernel,
        grid=(m // tm, n // tn, k // tk),
        in_specs=[pl.BlockSpec((tm, tk), lambda i, j, q: (i, q)),
                  pl.BlockSpec((tk, tn), lambda i, j, q: (q, j))],
        out_specs=pl.BlockSpec((tm, tn), lambda i, j, q: (i, j)),
        out_shape=jax.ShapeDtypeStruct((m, n), out_dtype),
        scratch_shapes=[pltpu.VMEM((tm, tn), F32)],
        compiler_params=_cparams(("arbitrary", "arbitrary", "arbitrary")),
        name=name,
    )(a, w)


def _conv3(prev_ref, cur_ref, next_ref, w_ref, at_start, at_end):
    cur = cur_ref[0]
    rows = cur.shape[0]
    prev_row = jnp.where(at_start, 0.0, prev_ref[0][SUBLANES - 1:SUBLANES, :])
    next_row = jnp.where(at_end, 0.0, next_ref[0][0:1, :])
    up = jnp.concatenate([prev_row, cur[:rows - 1]], axis=0)
    down = jnp.concatenate([cur[1:], next_row], axis=0)
    w = w_ref[...]
    return up * w[0:1] + cur * w[1:2] + down * w[2:3]


def _feat_kernel(rp, rc, rn, kp, kc, kn, vp, vc, vn, lp, lc, ln_,
                 shr, shk, shv, shl, w0, wup, a0, aup, gup, kk_w, ka_w, e_ref, et_ref,
                 r_o, v_o, a_o, g_o, lw_o, kd_o, b_o, *, n_lat, n_tot):
    j = pl.program_id(1)
    at_start = jnp.logical_or(j == 0, j == n_lat)
    at_end = jnp.logical_or(j == n_lat - 1, j == n_tot - 1)
    r = _conv3(rp, rc, rn, shr, at_start, at_end)
    k = _conv3(kp, kc, kn, shk, at_start, at_end)
    v = _conv3(vp, vc, vn, shv, at_start, at_end)
    lo = _conv3(lp, lc, ln_, shl, at_start, at_end)
    r_o[0] = r
    v_o[0] = v
    gd = lo[:, 2 * DECAY_LORA + 2 * ICLR_LORA:2 * DECAY_LORA + 2 * ICLR_LORA + GATE_LORA]
    g_o[0] = _dot(jax.nn.sigmoid(gd), gup[...], HIGHEST)
    kk = k * kk_w[...]
    ss = _dot(_dot(kk * kk, e_ref[...], HIGHEST), et_ref[...], HIGHEST)
    kk = kk / jnp.maximum(jnp.sqrt(ss), 1e-12)
    a_o[0] = kk
    for d in range(2):
        wd = lo[:, d * DECAY_LORA:(d + 1) * DECAY_LORA]
        ad = lo[:, 2 * DECAY_LORA + d * ICLR_LORA:2 * DECAY_LORA + (d + 1) * ICLR_LORA]
        z = w0[d:d + 1, :] + _dot(jnp.tanh(wd), wup[d], HIGHEST)
        lw_o[d, 0] = jax.nn.sigmoid(z) * (-np.exp(-0.5))
        iclr = jax.nn.sigmoid(a0[d:d + 1, :] + _dot(ad, aup[d], HIGHEST))
        kd_o[d, 0] = k * (1.0 + (iclr - 1.0) * ka_w[...])
        b_o[d, 0] = kk * iclr


def _features(proj, col, shift_rkv, shift_lora, w0, w_up, a0, a_up, g_up, k_k, k_a, t_lat, tb):
    bsz, tt, _ = proj.shape
    c = w0.shape[1]
    n_tot = tt // tb
    n_lat = t_lat // tb
    hb = tb // SUBLANES
    n_h = tt // SUBLANES
    heads = c // RWKV_HEAD
    e = (jnp.arange(c)[:, None] // RWKV_HEAD == jnp.arange(heads)[None, :]).astype(F32)
    et = e.T

    def main(cb, width):
        blk = cb * LANES // width
        return pl.BlockSpec((1, tb, width), lambda b, j: (b, j, blk))

    def prev(cb, width):
        blk = cb * LANES // width
        return pl.BlockSpec((1, SUBLANES, width), lambda b, j: (b, jnp.maximum(j * hb - 1, 0), blk))

    def nxt(cb, width):
        blk = cb * LANES // width
        return pl.BlockSpec((1, SUBLANES, width), lambda b, j: (b, jnp.minimum((j + 1) * hb, n_h - 1), blk))

    def const(shape):
        nd = len(shape)
        return pl.BlockSpec(shape, lambda b, j: (0,) * nd)

    in_specs = []
    for name, width in (("r", c), ("k", c), ("v", c), ("lora", LORA_PAD)):
        in_specs += [prev(col[name], width), main(col[name], width), nxt(col[name], width)]
    in_specs += [const((3, c)), const((3, c)), const((3, c)), const((3, LORA_PAD)),
                 const((2, c)), const((2, DECAY_LORA, c)), const((2, c)), const((2, ICLR_LORA, c)),
                 const((GATE_LORA, c)), const((1, c)), const((1, c)), const((c, heads)), const((heads, c))]
    one = pl.BlockSpec((1, tb, c), lambda b, j: (b, j, 0))
    two = pl.BlockSpec((2, 1, tb, c), lambda b, j: (0, b, j, 0))
    s1 = jax.ShapeDtypeStruct((bsz, tt, c), F32)
    s2 = jax.ShapeDtypeStruct((2, bsz, tt, c), F32)
    args = [proj] * 12 + [shift_rkv[:, :c], shift_rkv[:, c:2 * c], shift_rkv[:, 2 * c:], shift_lora,
                          w0, w_up, a0, a_up, g_up, k_k.reshape(1, c), k_a.reshape(1, c), e, et]
    return pl.pallas_call(
        functools.partial(_feat_kernel, n_lat=n_lat, n_tot=n_tot),
        grid=(bsz, n_tot),
        in_specs=in_specs,
        out_specs=[one, one, one, one, two, two, two],
        out_shape=[s1, s1, s1, s1, s2, s2, s2],
        compiler_params=_cparams(("arbitrary", "arbitrary")),
        name="rwkv_features",
    )(*args)


def _unit_tri_inverse(lm, eye, steps):
    t = eye - lm
    pw = lm
    for _ in range(steps):
        pw = _dot(pw, pw, HIGHEST)
        t = t + _dot(t, pw, HIGHEST)
    return t


def _scan_kernel(r_ref, lw_ref, k_ref, v_ref, a_ref, b_ref, y_ref, s_ref, *, chunk):
    d = pl.program_id(0)
    cc = pl.program_id(3)

    @pl.when(cc == 0)
    def _():
        s_ref[...] = jnp.zeros_like(s_ref)

    r = r_ref[0]
    lw = lw_ref[0, 0]
    k = k_ref[0, 0]
    v = v_ref[0]
    a = a_ref[0]
    b = b_ref[0, 0]
    lanes = r.shape[1]
    heads = lanes // RWKV_HEAD

    ii = lax.broadcasted_iota(jnp.int32, (chunk, chunk), 0)
    jj = lax.broadcasted_iota(jnp.int32, (chunk, chunk), 1)
    order = (ii - jj) * (1 - 2 * d)
    strict_f = (order > 0).astype(F32)
    incl_f = (order >= 0).astype(F32)
    incl2_f = jnp.concatenate([incl_f, incl_f], axis=1)
    eye = (ii == jj).astype(F32)

    cum = _dot(incl_f, lw, HIGHEST)
    tot = jnp.sum(lw, axis=0, keepdims=True)
    p_in = jnp.exp(cum)
    p_inv = jnp.exp(-cum)
    a_t = a * jnp.exp(cum - lw)
    r_t = r * p_in
    k_t = k * p_inv
    b_t = b * p_inv
    x = jnp.concatenate([a_t, r_t], axis=0)
    z = jnp.concatenate([k_t, b_t], axis=0)
    s_old = s_ref[...]
    xs = _dot_nt(x, s_old, HIGHEST)

    lane_head = lax.broadcasted_iota(jnp.int32, (1, lanes), 1) // RWKV_HEAD
    u_all = jnp.zeros((chunk, lanes), F32)
    y_all = jnp.zeros((chunk, lanes), F32)
    steps = chunk.bit_length() - 2
    for h in range(heads):
        mh = lane_head == h
        g = _dot_nt(x * mh.astype(F32), z, HIGHEST)
        m_ak = g[:chunk, :chunk] * strict_f
        m_ab = g[:chunk, chunk:] * strict_f
        m_r = g[chunk:, :] * incl2_f
        t_inv = _unit_tri_inverse(m_ab, eye, steps)
        rhs = xs[:chunk] + _dot(m_ak, v, HIGHEST)
        u_h = -_dot(t_inv, rhs, HIGHEST)
        y_h = xs[chunk:] + _dot(m_r, jnp.concatenate([v, u_h], axis=0), HIGHEST)
        u_all = jnp.where(mh, u_h, u_all)
        y_all = jnp.where(mh, y_h, y_all)
    y_ref[0, 0] = y_all

    vu = jnp.concatenate([v, u_all], axis=0)
    upd = _dot_tn(vu, z, HIGHEST)
    vi = lax.broadcasted_iota(jnp.int32, (lanes, lanes), 0) // RWKV_HEAD
    ki = lax.broadcasted_iota(jnp.int32, (lanes, lanes), 1) // RWKV_HEAD
    s_ref[...] = jnp.where(vi == ki, (s_old + upd) * jnp.exp(tot), 0.0)


def _scan(r, v, a, lw, kd, bb, t_lat):
    bsz, tt, c = r.shape
    chunk = SCAN_CHUNK
    n_tot = tt // chunk
    n_lat = t_lat // chunk
    n_ctx = n_tot - n_lat

    def chunk_index(d, s):
        fwd = jnp.where(s < n_ctx, n_lat + s, s - n_ctx)
        return jnp.where(d == 0, fwd, n_tot - 1 - s)

    one = pl.BlockSpec((1, chunk, SCAN_LANES), lambda d, b, h, s: (b, chunk_index(d, s), h))
    two = pl.BlockSpec((1, 1, chunk, SCAN_LANES), lambda d, b, h, s: (d, b, chunk_index(d, s), h))
    return pl.pallas_call(
        functools.partial(_scan_kernel, chunk=chunk),
        grid=(2, bsz, c // SCAN_LANES, n_tot),
        in_specs=[one, two, two, one, one, two],
        out_specs=two,
        out_shape=jax.ShapeDtypeStruct((2, bsz, tt, c), F32),
        scratch_shapes=[pltpu.VMEM((SCAN_LANES, SCAN_LANES), F32)],
        compiler_params=_cparams(("arbitrary", "arbitrary", "arbitrary", "arbitrary")),
        name="rwkv_scan",
    )(r, lw, kd, v, a, bb)


def _rout_kernel(yf, yb, r, k0, k1, v, g, rk, gng, gnb, e_ref, et_ref, o_ref):
    e = e_ref[...]
    et = et_ref[...]
    inv_n = 1.0 / RWKV_HEAD

    def head_sum(t):
        return _dot(_dot(t, e, HIGHEST), et, HIGHEST)

    y = yf[0, 0] + yb[0, 0]
    mu = head_sum(y) * inv_n
    yc = y - mu
    var = head_sum(yc * yc) * inv_n
    yn = yc * lax.rsqrt(var + GN_EPS) * gng[...] + gnb[...]
    bonus = head_sum(r[0] * (k0[0, 0] + k1[0, 0]) * rk[...]) * v[0]
    o_ref[0] = ((yn + bonus) * g[0]).astype(o_ref.dtype)


def _rwkv_out(y, r, kd, v, g, r_k, gn_g, gn_b, t_lat, tb):
    bsz, _, c = r.shape
    heads = c // RWKV_HEAD
    e = (jnp.arange(c)[:, None] // RWKV_HEAD == jnp.arange(heads)[None, :]).astype(F32)
    one = pl.BlockSpec((1, tb, c), lambda b, j: (b, j, 0))

    def two(d):
        return pl.BlockSpec((1, 1, tb, c), lambda b, j: (d, b, j, 0))

    def const(shape):
        return pl.BlockSpec(shape, lambda b, j: (0, 0))

    return pl.pallas_call(
        _rout_kernel,
        grid=(bsz, t_lat // tb),
        in_specs=[two(0), two(1), one, two(0), two(1), one, one,
                  const((1, c)), const((1, c)), const((1, c)), const((c, heads)), const((heads, c))],
        out_specs=one,
        out_shape=jax.ShapeDtypeStruct((bsz, t_lat, c), BF16),
        compiler_params=_cparams(("arbitrary", "arbitrary")),
        name="rwkv_out",
    )(y, y, r, kd, kd, v, g, r_k.reshape(1, c), gn_g.reshape(1, c), gn_b.reshape(1, c), e, e.T)


def _rope(x, c, s):
    return x * c + pltpu.roll(x, ATT_HEAD // 2, 1) * s


def _attn_kernel(sink_ref, q_ref, km_ref, k0_ref, kp_ref, vm_ref, v0_ref, vp_ref, kc_ref, vc_ref,
                 c0_ref, s0_ref, cm_ref, sm_ref, cp_ref, sp_ref, o_ref, *, n_blk):
    g = pl.program_id(1)
    n = pl.program_id(2)
    blk = ATT_HEAD
    scale = ATT_HEAD ** -0.5
    c0, s0 = c0_ref[...], s0_ref[...]
    q = q_ref[0]
    qh = [_rope(q[:, h * ATT_HEAD:(h + 1) * ATT_HEAD], c0, s0) for h in range(ATT_GROUPS)]
    qq = jnp.concatenate(qh, axis=0).astype(BF16)
    kw = jnp.concatenate([_rope(km_ref[0], cm_ref[...], sm_ref[...]),
                          _rope(k0_ref[0], c0, s0),
                          _rope(kp_ref[0], cp_ref[...], sp_ref[...])], axis=0).astype(BF16)
    vw = jnp.concatenate([vm_ref[0], v0_ref[0], vp_ref[0]], axis=0).astype(BF16)
    s_w = _dot_nt(qq, kw) * scale
    ii = lax.broadcasted_iota(jnp.int32, (ATT_GROUPS * blk, 3 * blk), 0) % blk
    jj = lax.broadcasted_iota(jnp.int32, (ATT_GROUPS * blk, 3 * blk), 1)
    in_prev = jnp.logical_and(jnp.logical_and(jj < blk, jj >= ii), n > 0)
    in_self = jnp.logical_and(jj >= blk, jj < 2 * blk)
    in_next = jnp.logical_and(jnp.logical_and(jj >= 2 * blk, jj - 2 * blk <= ii), n < n_blk - 1)
    valid = jnp.logical_or(jnp.logical_or(in_prev, in_self), in_next)
    s_w = jnp.where(valid, s_w, NEG_INF)
    s_c = _dot_nt(qq, kc_ref[0].astype(BF16)) * scale
    s_s = jnp.concatenate([jnp.full((blk, 1), sink_ref[g * ATT_GROUPS + h], F32) for h in range(ATT_GROUPS)],
                          axis=0)
    m = jnp.maximum(jnp.maximum(jnp.max(s_w, axis=-1, keepdims=True), jnp.max(s_c, axis=-1, keepdims=True)), s_s)
    p_w = jnp.exp(s_w - m)
    p_c = jnp.exp(s_c - m)
    den = jnp.sum(p_w, axis=-1, keepdims=True) + jnp.sum(p_c, axis=-1, keepdims=True) + jnp.exp(s_s - m)
    o = _dot(p_w.astype(BF16), vw) + _dot(p_c.astype(BF16), vc_ref[0].astype(BF16))
    o = o / den
    o_ref[0] = jnp.concatenate([o[h * blk:(h + 1) * blk] for h in range(ATT_GROUPS)], axis=1).astype(o_ref.dtype)


def _attention(proj, col, sink, cos2, sin2, t_lat):
    bsz, tt, _ = proj.shape
    blk = ATT_HEAD
    n_blk = t_lat // blk
    l = tt - t_lat
    qw = ATT_GROUPS * ATT_HEAD
    q_blk = col["q"] * LANES // qw

    def kv(name, off):
        cb = col[name]
        return pl.BlockSpec((1, blk, ATT_HEAD),
                            lambda b, g, n: (b, jnp.clip(n + off, 0, n_blk - 1), cb + g))

    def ctx(name):
        cb = col[name]
        return pl.BlockSpec((1, l, ATT_HEAD), lambda b, g, n: (b, t_lat // l, cb + g))

    def tab(off):
        return pl.BlockSpec((blk, ATT_HEAD), lambda b, g, n: (jnp.clip(n + off, 0, n_blk - 1), 0))

    in_specs = [pl.BlockSpec(memory_space=pltpu.SMEM),
                pl.BlockSpec((1, blk, qw), lambda b, g, n: (b, n, q_blk + g)),
                kv("ak", -1), kv("ak", 0), kv("ak", 1), kv("av", -1), kv("av", 0), kv("av", 1),
                ctx("ak"), ctx("av"), tab(0), tab(0), tab(-1), tab(-1), tab(1), tab(1)]
    return pl.pallas_call(
        functools.partial(_attn_kernel, n_blk=n_blk),
        grid=(bsz, ATT_KV_HEADS, n_blk),
        in_specs=in_specs,
        out_specs=pl.BlockSpec((1, blk, qw), lambda b, g, n: (b, n, g)),
        out_shape=jax.ShapeDtypeStruct((bsz, t_lat, ATT_KV_HEADS * qw), BF16),
        compiler_params=_cparams(("arbitrary", "arbitrary", "arbitrary")),
        name="window_attention",
    )(sink, proj, proj, proj, proj, proj, proj, proj, proj, proj, cos2, sin2, cos2, sin2, cos2, sin2)


def _merge_kernel(o1_ref, o2_ref, w1_ref, w2_ref, g1_ref, g2_ref, o_ref):
    y1 = _dot(o1_ref[0], w1_ref[...])
    y2 = _dot(o2_ref[0], w2_ref[...])
    o_ref[0] = (jax.nn.sigmoid(g1_ref[0]) * y1 + jax.nn.sigmoid(g2_ref[0]) * y2).astype(o_ref.dtype)


def _merge(o_rwkv, o_att, w1, w2, proj, col, tm, tn):
    bsz, t, c = o_rwkv.shape
    n = w1.shape[1]
    gr = col["gate_r"] * LANES // tn
    ga = col["gate_a"] * LANES // tn
    return pl.pallas_call(
        _merge_kernel,
        grid=(bsz, t // tm, n // tn),
        in_specs=[pl.BlockSpec((1, tm, c), lambda b, i, j: (b, i, 0)),
                  pl.BlockSpec((1, tm, c), lambda b, i, j: (b, i, 0)),
                  pl.BlockSpec((c, tn), lambda b, i, j: (0, j)),
                  pl.BlockSpec((c, tn), lambda b, i, j: (0, j)),
                  pl.BlockSpec((1, tm, tn), lambda b, i, j: (b, i, gr + j)),
                  pl.BlockSpec((1, tm, tn), lambda b, i, j: (b, i, ga + j))],
        out_specs=pl.BlockSpec((1, tm, tn), lambda b, i, j: (b, i, j)),
        out_shape=jax.ShapeDtypeStruct((bsz, t, n), BF16),
        compiler_params=_cparams(("arbitrary", "arbitrary", "arbitrary")),
        name="gated_merge",
    )(o_rwkv, o_att, w1, w2, proj, proj)


def _ln_kernel(x_ref, y_ref, gt_ref, g_ref, b_ref, *rest, with_mod):
    bi = pl.program_id(0)
    gt = gt_ref[pl.ds(bi, 1), :]
    z = DEEPNORM_ALPHA * x_ref[0] + gt * y_ref[0]
    mu = jnp.mean(z, axis=-1, keepdims=True)
    zc = z - mu
    var = jnp.mean(zc * zc, axis=-1, keepdims=True)
    out = zc * lax.rsqrt(var + LN_EPS) * g_ref[...] + b_ref[...]
    if with_mod:
        sh_ref, sc_ref, o_ref, h_ref = rest
        o_ref[0] = out
        h_ref[0] = (out * (1.0 + sc_ref[pl.ds(bi, 1), :]) + sh_ref[pl.ds(bi, 1), :]).astype(BF16)
    else:
        (o_ref,) = rest
        o_ref[0] = out


def _res_ln(x, y, mod, gate_col, g, b, tb, mod_cols=None):
    bsz, t, d = x.shape
    blk = pl.BlockSpec((1, tb, d), lambda bi, j: (bi, j, 0))
    vec = pl.BlockSpec((1, d), lambda bi, j: (0, 0))

    def modspec(cb):
        return pl.BlockSpec((MOD_ROWS, d), lambda bi, j: (0, cb))

    in_specs = [blk, blk, modspec(gate_col), vec, vec]
    args = [x, y, mod, g.reshape(1, d), b.reshape(1, d)]
    out_specs = [blk]
    out_shape = [jax.ShapeDtypeStruct((bsz, t, d), F32)]
    if mod_cols is not None:
        in_specs += [modspec(mod_cols[0]), modspec(mod_cols[1])]
        args += [mod, mod]
        out_specs.append(blk)
        out_shape.append(jax.ShapeDtypeStruct((bsz, t, d), BF16))
    return pl.pallas_call(
        functools.partial(_ln_kernel, with_mod=mod_cols is not None),
        grid=(bsz, t // tb),
        in_specs=in_specs,
        out_specs=out_specs,
        out_shape=out_shape,
        compiler_params=_cparams(("arbitrary", "arbitrary")),
        name="residual_layernorm",
    )(*args)


def _ffn_up_kernel(a_ref, wg_ref, wu_ref, o_ref):
    a = a_ref[...]
    gg = _dot(a, wg_ref[...])
    uu = _dot(a, wu_ref[...])
    o_ref[...] = (gg * jax.nn.sigmoid(gg) * uu).astype(o_ref.dtype)


def _ffn_up(a, wg, wu, tm, tf):
    m, k = a.shape
    f = wg.shape[1]
    return pl.pallas_call(
        _ffn_up_kernel,
        grid=(m // tm, f // tf),
        in_specs=[pl.BlockSpec((tm, k), lambda i, j: (i, 0)),
                  pl.BlockSpec((k, tf), lambda i, j: (0, j)),
                  pl.BlockSpec((k, tf), lambda i, j: (0, j))],
        out_specs=pl.BlockSpec((tm, tf), lambda i, j: (i, j)),
        out_shape=jax.ShapeDtypeStruct((m, f), BF16),
        compiler_params=_cparams(("arbitrary", "arbitrary")),
        name="swiglu_up",
    )(a, wg, wu)


def _rope_tables(t):
    rows = t // GRID_W
    row = jnp.broadcast_to(jnp.arange(rows, dtype=F32)[:, None], (rows, GRID_W)).reshape(t)
    colp = jnp.broadcast_to(jnp.arange(GRID_W, dtype=F32)[None, :], (rows, GRID_W)).reshape(t)
    axis_dim = ATT_HEAD // 2
    inv = ROPE_BASE ** (-jnp.arange(0, axis_dim, 2, dtype=F32) / axis_dim)
    ang = jnp.concatenate([row[:, None] * inv, colp[:, None] * inv], -1)
    cos, sin = jnp.cos(ang), jnp.sin(ang)
    return jnp.concatenate([cos, cos], -1), jnp.concatenate([-sin, sin], -1)


def _block(x, c, ctx, c_ctx, w_ada, b_ada, w_in, rwkv_shift, rwkv_w0, rwkv_w_up, rwkv_a0, rwkv_a_up,
           rwkv_g_up, rwkv_k_k, rwkv_k_a, rwkv_r_k, rwkv_gn_g, rwkv_gn_b, attn_sink, w_rwkv_o, w_att_o,
           w_out, ln1_g, ln1_b, w_ff_gate, w_ff_up, w_ff_down, ln2_g, ln2_b, tiles):
    bsz, t, d = x.shape
    l = ctx.shape[1]
    tt = t + l
    crw = rwkv_w0.shape[1]
    d_att = ATT_KV_HEADS * ATT_GROUPS * ATT_HEAD
    d_kv = ATT_KV_HEADS * ATT_HEAD
    n_lora = 2 * DECAY_LORA + 2 * ICLR_LORA + GATE_LORA
    n_rw = 3 * crw + n_lora

    cc = jnp.zeros((MOD_ROWS, d), F32).at[:bsz].set(c).at[bsz].set(c_ctx)
    mod = _ada(cc, w_ada, b_ada)

    o_q = n_rw
    o_k = o_q + d_att
    o_v = o_k + d_kv
    o_g = o_v + d_kv
    pad = LORA_PAD - n_lora
    w_perm = jnp.concatenate([w_in[:, o_g:], w_in[:, :3 * crw], w_in[:, o_q:o_g],
                              w_in[:, 3 * crw:n_rw], jnp.zeros((d, pad), F32)], axis=1).astype(BF16)
    col = {}
    off = 0
    for name, width in (("gate_r", d), ("gate_a", d), ("r", crw), ("k", crw), ("v", crw),
                        ("q", d_att), ("ak", d_kv), ("av", d_kv), ("lora", LORA_PAD)):
        col[name] = off // LANES
        off += width
    shift_rkv = rwkv_shift[:, :3 * crw]
    shift_lora = jnp.concatenate([rwkv_shift[:, 3 * crw:], jnp.zeros((3, pad), F32)], axis=1)

    h = _prep(x, ctx, mod)
    proj = _mm_wstat(h.reshape(bsz * tt, d), w_perm, tiles["proj_tm"], tiles["proj_tn"], F32, "in_proj")
    proj = proj.reshape(bsz, tt, -1)

    r, v, a, g, lw, kd, bb = _features(proj, col, shift_rkv, shift_lora, rwkv_w0, rwkv_w_up, rwkv_a0,
                                       rwkv_a_up, rwkv_g_up, rwkv_k_k, rwkv_k_a, t, tiles["feat_tb"])
    y = _scan(r, v, a, lw, kd, bb, t)
    o_rwkv = _rwkv_out(y, r, kd, v, g, rwkv_r_k.reshape(-1), rwkv_gn_g, rwkv_gn_b, t, tiles["feat_tb"])

    cos2, sin2 = _rope_tables(t)
    o_att = _attention(proj, col, attn_sink, cos2, sin2, t)

    ym = _merge(o_rwkv, o_att, w_rwkv_o.astype(BF16), w_att_o.astype(BF16), proj, col,
                tiles["merge_tm"], tiles["merge_tn"])
    yo = _mm_astat(ym.reshape(bsz * t, d), w_out.astype(BF16), tiles["out_tm"], tiles["out_tn"], F32, "out_proj")
    x1, h2 = _res_ln(x, yo.reshape(bsz, t, d), mod, 2, ln1_g, ln1_b, tiles["ln_tb"], mod_cols=(3, 4))

    f = w_ff_gate.shape[1]
    fp = -(-f // tiles["ffn_pad"]) * tiles["ffn_pad"]
    wg = jnp.pad(w_ff_gate, ((0, 0), (0, fp - f))).astype(BF16)
    wu = jnp.pad(w_ff_up, ((0, 0), (0, fp - f))).astype(BF16)
    wd = jnp.pad(w_ff_down, ((0, fp - f), (0, 0))).astype(BF16)
    u = _ffn_up(h2.reshape(bsz * t, d), wg, wu, tiles["ffn_tm"], tiles["ffn_tf"])
    ff = _mm_ktiled(u, wd, tiles["down_tm"], tiles["down_tn"], fp // tiles["down_ksplit"], F32, "swiglu_down")
    (out,) = _res_ln(x1, ff.reshape(bsz, t, d), mod, 5, ln2_g, ln2_b, tiles["ln_tb"])
    return out


_TILES = dict(proj_tm=512, proj_tn=1024, feat_tb=128, merge_tm=1024, merge_tn=512, out_tm=1024, out_tn=1024,
              ln_tb=256, ffn_pad=1024, ffn_tm=1024, ffn_tf=512, down_tm=2048, down_tn=1024, down_ksplit=8)


def kernel(x, c, ctx, c_ctx, w_ada, b_ada, w_in, rwkv_shift, rwkv_w0, rwkv_w_up, rwkv_a0, rwkv_a_up, rwkv_g_up, rwkv_k_k, rwkv_k_a, rwkv_r_k, rwkv_gn_g, rwkv_gn_b, attn_sink, w_rwkv_o, w_att_o, w_out, ln1_g, ln1_b, w_ff_gate, w_ff_up, w_ff_down, ln2_g, ln2_b):
    assert w_ada.shape[0] == DEPTH
    return _block(x, c, ctx, c_ctx, w_ada[0], b_ada[0], w_in[0], rwkv_shift[0], rwkv_w0[0], rwkv_w_up[0],
                  rwkv_a0[0], rwkv_a_up[0], rwkv_g_up[0], rwkv_k_k[0], rwkv_k_a[0], rwkv_r_k[0], rwkv_gn_g[0],
                  rwkv_gn_b[0], attn_sink[0], w_rwkv_o[0], w_att_o[0], w_out[0], ln1_g[0], ln1_b[0],
                  w_ff_gate[0], w_ff_up[0], w_ff_down[0], ln2_g[0], ln2_b[0], _TILES)
```

```python
import functools

import jax
import jax.numpy as jnp
import numpy as np
from jax import lax
from jax.experimental import pallas as pl
from jax.experimental.pallas import tpu as pltpu

F32 = jnp.float32
BF16 = jnp.bfloat16
HIGHEST = lax.Precision.HIGHEST

RWKV_HEAD = 64
DECAY_LORA = 96
ICLR_LORA = 96
GATE_LORA = 256
GN_EPS = 64e-5
ATT_HEAD = 128
ATT_KV_HEADS = 4
ATT_GROUPS = 4
GRID_W = 64
ROPE_BASE = 10000.0
LN_EPS = 1e-5
DEPTH = 1
DEEPNORM_ALPHA = (2 * DEPTH) ** 0.25
NEG_INF = -1e30

LANES = 128
SUBLANES = 8
VMEM_LIMIT = 56 * 1024 * 1024
MOD_ROWS = 8
SCAN_CHUNK = 64
SCAN_LANES = 256
LORA_PAD = 1024


def _cparams(sem):
    return pltpu.CompilerParams(dimension_semantics=sem, vmem_limit_bytes=VMEM_LIMIT)


def _dot(a, b, precision=None):
    return jnp.dot(a, b, preferred_element_type=F32, precision=precision)


def _dot_nt(a, b, precision=None):
    return lax.dot_general(a, b, (((1,), (1,)), ((), ())), preferred_element_type=F32, precision=precision)


def _dot_tn(a, b, precision=None):
    return lax.dot_general(a, b, (((0,), (0,)), ((), ())), preferred_element_type=F32, precision=precision)


def _ada_kernel(c_ref, w_ref, b_ref, o_ref):
    a = c_ref[...]
    a = a * jax.nn.sigmoid(a)
    o_ref[...] = _dot(a, w_ref[...], HIGHEST) + b_ref[...]


def _ada(cc, w_ada, b_ada):
    d, n = w_ada.shape
    tn = 512
    return pl.pallas_call(
        _ada_kernel,
        grid=(n // tn,),
        in_specs=[pl.BlockSpec((MOD_ROWS, d), lambda j: (0, 0)),
                  pl.BlockSpec((d, tn), lambda j: (0, j)),
                  pl.BlockSpec((1, tn), lambda j: (0, j))],
        out_specs=pl.BlockSpec((MOD_ROWS, tn), lambda j: (0, j)),
        out_shape=jax.ShapeDtypeStruct((MOD_ROWS, n), F32),
        compiler_params=_cparams(("arbitrary",)),
        name="ada",
    )(cc, w_ada, b_ada.reshape(1, n))


def _prep_kernel(x_ref, ctx_ref, sh_ref, sc_ref, o_ref, *, n_lat, ctx_row):
    b = pl.program_id(0)
    j = pl.program_id(1)
    is_ctx = j >= n_lat
    row = jnp.where(is_ctx, ctx_row, b)
    sh = sh_ref[pl.ds(row, 1), :]
    sc = sc_ref[pl.ds(row, 1), :]
    xin = jnp.where(is_ctx, ctx_ref[0], x_ref[0])
    o_ref[0] = (xin * (1.0 + sc) + sh).astype(BF16)


def _prep(x, ctx, mod):
    bsz, t, d = x.shape
    l = ctx.shape[1]
    tb = l
    n_lat = t // tb
    return pl.pallas_call(
        functools.partial(_prep_kernel, n_lat=n_lat, ctx_row=bsz),
        grid=(bsz, n_lat + 1),
        in_specs=[pl.BlockSpec((1, tb, d), lambda b, j: (b, jnp.minimum(j, n_lat - 1), 0)),
                  pl.BlockSpec((1, l, d), lambda b, j: (b, 0, 0)),
                  pl.BlockSpec((MOD_ROWS, d), lambda b, j: (0, 0)),
                  pl.BlockSpec((MOD_ROWS, d), lambda b, j: (0, 1))],
        out_specs=pl.BlockSpec((1, tb, d), lambda b, j: (b, j, 0)),
        out_shape=jax.ShapeDtypeStruct((bsz, t + l, d), BF16),
        compiler_params=_cparams(("arbitrary", "arbitrary")),
        name="prep",
    )(x, ctx, mod, mod)


def _mm_kernel(a_ref, w_ref, o_ref):
    o_ref[...] = _dot(a_ref[...], w_ref[...]).astype(o_ref.dtype)


def _mm_wstat(a, w, tm, tn, out_dtype, name):
    m, k = a.shape
    n = w.shape[1]
    return pl.pallas_call(
        _mm_kernel,
        grid=(n // tn, m // tm),
        in_specs=[pl.BlockSpec((tm, k), lambda j, i: (i, 0)),
                  pl.BlockSpec((k, tn), lambda j, i: (0, j))],
        out_specs=pl.BlockSpec((tm, tn), lambda j, i: (i, j)),
        out_shape=jax.ShapeDtypeStruct((m, n), out_dtype),
        compiler_params=_cparams(("arbitrary", "arbitrary")),
        name=name,
    )(a, w)


def _mm_astat(a, w, tm, tn, out_dtype, name):
    m, k = a.shape
    n = w.shape[1]
    return pl.pallas_call(
        _mm_kernel,
        grid=(m // tm, n // tn),
        in_specs=[pl.BlockSpec((tm, k), lambda i, j: (i, 0)),
                  pl.BlockSpec((k, tn), lambda i, j: (0, j))],
        out_specs=pl.BlockSpec((tm, tn), lambda i, j: (i, j)),
        out_shape=jax.ShapeDtypeStruct((m, n), out_dtype),
        compiler_params=_cparams(("arbitrary", "arbitrary")),
        name=name,
    )(a, w)


def _mm_acc_kernel(a_ref, w_ref, o_ref, acc_ref):
    kk = pl.program_id(2)

    @pl.when(kk == 0)
    def _():
        acc_ref[...] = jnp.zeros_like(acc_ref)

    acc_ref[...] += _dot(a_ref[...], w_ref[...])

    @pl.when(kk == pl.num_programs(2) - 1)
    def _():
        o_ref[...] = acc_ref[...].astype(o_ref.dtype)


def _mm_ktiled(a, w, tm, tn, tk, out_dtype, name):
    m, k = a.shape
    n = w.shape[1]
    return pl.pallas_call(
        _mm_acc_kernel,
        grid=(m // tm, n // tn, k // tk),
        in_specs=[pl.BlockSpec((tm, tk), lambda i, j, q: (i, q)),
                  pl.BlockSpec((tk, tn), lambda i, j, q: (q, j))],
        out_specs=pl.BlockSpec((tm, tn), lambda i, j, q: (i, j)),
        out_shape=jax.ShapeDtypeStruct((m, n), out_dtype),
        scratch_shapes=[pltpu.VMEM((tm, tn), F32)],
        compiler_params=_cparams(("arbitrary", "arbitrary", "arbitrary")),
        name=name,
    )(a, w)


def _conv3(prev_ref, cur_ref, next_ref, w_ref, at_start, at_end):
    cur = cur_ref[0]
    rows = cur.shape[0]
    prev_row = jnp.where(at_start, 0.0, prev_ref[0][SUBLANES - 1:SUBLANES, :])
    next_row = jnp.where(at_end, 0.0, next_ref[0][0:1, :])
    up = jnp.concatenate([prev_row, cur[:rows - 1]], axis=0)
    down = jnp.concatenate([cur[1:], next_row], axis=0)
    w = w_ref[...]
    return up * w[0:1] + cur * w[1:2] + down * w[2:3]


def _feat_kernel(rp, rc, rn, kp, kc, kn, vp, vc, vn, lp, lc, ln_,
                 shr, shk, shv, shl, w0, wup, a0, aup, gup, kk_w, ka_w, rk_w, e_ref, et_ref, cm_ref,
                 v_o, g_o, bonus_o, at_o, rt_o, kt_o, bt_o, dec_o, *, n_lat, n_tot, chunk):
    j = pl.program_id(1)
    at_start = jnp.logical_or(j == 0, j == n_lat)
    at_end = jnp.logical_or(j == n_lat - 1, j == n_tot - 1)
    r = _conv3(rp, rc, rn, shr, at_start, at_end)
    k = _conv3(kp, kc, kn, shk, at_start, at_end)
    v = _conv3(vp, vc, vn, shv, at_start, at_end)
    lo = _conv3(lp, lc, ln_, shl, at_start, at_end)
    rows = r.shape[0]
    v_o[0] = v.astype(BF16)
    gd = lo[:, 2 * DECAY_LORA + 2 * ICLR_LORA:2 * DECAY_LORA + 2 * ICLR_LORA + GATE_LORA]
    g_o[0] = _dot(jax.nn.sigmoid(gd), gup[...], HIGHEST)
    kk = k * kk_w[...]
    ss = _dot(_dot(kk * kk, e_ref[...], HIGHEST), et_ref[...], HIGHEST)
    kk = kk / jnp.maximum(jnp.sqrt(ss), 1e-12)
    k_sum = jnp.zeros_like(k)
    for d in range(2):
        wd = lo[:, d * DECAY_LORA:(d + 1) * DECAY_LORA]
        ad = lo[:, 2 * DECAY_LORA + d * ICLR_LORA:2 * DECAY_LORA + (d + 1) * ICLR_LORA]
        z = w0[d:d + 1, :] + _dot(jnp.tanh(wd), wup[d], HIGHEST)
        lw = jax.nn.sigmoid(z) * (-np.exp(-0.5))
        iclr = jax.nn.sigmoid(a0[d:d + 1, :] + _dot(ad, aup[d], HIGHEST))
        k_dir = k * (1.0 + (iclr - 1.0) * ka_w[...])
        k_sum = k_sum + k_dir
        cum = _dot(cm_ref[d], lw, HIGHEST)
        p_inv = jnp.exp(-cum)
        at_o[d, 0] = (kk * jnp.exp(cum - lw)).astype(BF16)
        rt_o[d, 0] = (r * jnp.exp(cum)).astype(BF16)
        kt_o[d, 0] = (k_dir * p_inv).astype(BF16)
        bt_o[d, 0] = (kk * iclr * p_inv).astype(BF16)
        for q in range(rows // chunk):
            last = (q + 1) * chunk - 1 if d == 0 else q * chunk
            dec_o[d, 0, q] = jnp.exp(cum[last:last + 1, :])
    bonus = _dot(_dot(r * k_sum * rk_w[...], e_ref[...], HIGHEST), et_ref[...], HIGHEST) * v
    bonus_o[0] = bonus


def _features(proj, col, shift_rkv, shift_lora, w0, w_up, a0, a_up, g_up, k_k, k_a, r_k, t_lat, tb):
    bsz, tt, _ = proj.shape
    c = w0.shape[1]
    chunk = SCAN_CHUNK
    n_tot = tt // tb
    n_lat = t_lat // tb
    hb = tb // SUBLANES
    n_h = tt // SUBLANES
    heads = c // RWKV_HEAD
    e = (jnp.arange(c)[:, None] // RWKV_HEAD == jnp.arange(heads)[None, :]).astype(F32)
    et = e.T
    ti = np.arange(tb)
    same = (ti[:, None] // chunk) == (ti[None, :] // chunk)
    cmask = jnp.asarray(np.stack([same & (ti[None, :] <= ti[:, None]), same & (ti[None, :] >= ti[:, None])]), F32)

    def main(cb, width):
        blk = cb * LANES // width
        return pl.BlockSpec((1, tb, width), lambda b, j: (b, j, blk))

    def prev(cb, width):
        blk = cb * LANES // width
        return pl.BlockSpec((1, SUBLANES, width), lambda b, j: (b, jnp.maximum(j * hb - 1, 0), blk))

    def nxt(cb, width):
        blk = cb * LANES // width
        return pl.BlockSpec((1, SUBLANES, width), lambda b, j: (b, jnp.minimum((j + 1) * hb, n_h - 1), blk))

    def const(shape):
        nd = len(shape)
        return pl.BlockSpec(shape, lambda b, j: (0,) * nd)

    in_specs = []
    for name, width in (("r", c), ("k", c), ("v", c), ("lora", LORA_PAD)):
        in_specs += [prev(col[name], width), main(col[name], width), nxt(col[name], width)]
    in_specs += [const((3, c)), const((3, c)), const((3, c)), const((3, LORA_PAD)),
                 const((2, c)), const((2, DECAY_LORA, c)), const((2, c)), const((2, ICLR_LORA, c)),
                 const((GATE_LORA, c)), const((1, c)), const((1, c)), const((1, c)),
                 const((c, heads)), const((heads, c)), const((2, tb, tb))]
    one = pl.BlockSpec((1, tb, c), lambda b, j: (b, j, 0))
    two = pl.BlockSpec((2, 1, tb, c), lambda b, j: (0, b, j, 0))
    dec = pl.BlockSpec((2, 1, tb // chunk, 1, c), lambda b, j: (0, b, j, 0, 0))
    s1 = jax.ShapeDtypeStruct((bsz, tt, c), F32)
    s1h = jax.ShapeDtypeStruct((bsz, tt, c), BF16)
    s2h = jax.ShapeDtypeStruct((2, bsz, tt, c), BF16)
    sdec = jax.ShapeDtypeStruct((2, bsz, tt // chunk, 1, c), F32)
    args = [proj] * 12 + [shift_rkv[:, :c], shift_rkv[:, c:2 * c], shift_rkv[:, 2 * c:], shift_lora,
                          w0, w_up, a0, a_up, g_up, k_k.reshape(1, c), k_a.reshape(1, c), r_k.reshape(1, c),
                          e, et, cmask]
    return pl.pallas_call(
        functools.partial(_feat_kernel, n_lat=n_lat, n_tot=n_tot, chunk=chunk),
        grid=(bsz, n_tot),
        in_specs=in_specs,
        out_specs=[one, one, one, two, two, two, two, dec],
        out_shape=[s1h, s1, s1, s2h, s2h, s2h, s2h, sdec],
        compiler_params=_cparams(("arbitrary", "arbitrary")),
        name="rwkv_features",
    )(*args)


def _scan_kernel(gm_ref, bdm_ref, hm_ref, eye_ref, *refs, chunk, nsub):
    ins = refs[:12]
    y_refs = refs[12:14]
    s_ref = refs[14]
    heads = SCAN_LANES // RWKV_HEAD

    @pl.when(pl.program_id(2) == 0)
    def _():
        s_ref[...] = jnp.zeros_like(s_ref)

    bdm = bdm_ref[...]
    bdm_h = bdm.astype(BF16)
    eye = eye_ref[...]
    hms = [hm_ref[h] for h in range(heads)]

    def rows_bd(m):
        return jnp.concatenate([m * hms[h] for h in range(heads)], axis=0)

    def blocks_bd(m):
        return jnp.concatenate([m] * heads, axis=0) * bdm_h

    steps = chunk.bit_length() - 2
    units = [(d, q) for d in range(2) for q in range(nsub)]
    n_u = len(units)

    def lanes(q):
        return slice(q * SCAN_LANES, (q + 1) * SCAN_LANES)

    kt = [ins[6 * d + 2][0, 0, :, lanes(q)] for d, q in units]
    bt = [ins[6 * d + 3][0, 0, :, lanes(q)] for d, q in units]
    v = [ins[6 * d + 4][0, :, lanes(q)] for d, q in units]
    x = [jnp.concatenate([ins[6 * d][0, 0, :, lanes(q)], ins[6 * d + 1][0, 0, :, lanes(q)]], axis=0)
         for d, q in units]
    g = [_dot_nt(x[i], jnp.concatenate([rows_bd(kt[i]), rows_bd(bt[i])], axis=0)) * gm_ref[units[i][0]]
         for i in range(n_u)]
    s_old = [s_ref[i] for i in range(n_u)]
    xs = [_dot_nt(x[i], s_old[i].astype(BF16)) for i in range(n_u)]
    vbd = [rows_bd(v[i]) for i in range(n_u)]
    rhs = [xs[i][:chunk] + _dot(g[i][:chunk, :SCAN_LANES].astype(BF16), vbd[i]) for i in range(n_u)]
    t = [eye - g[i][:chunk, SCAN_LANES:] for i in range(n_u)]
    pw = [g[i][:chunk, SCAN_LANES:].astype(BF16) for i in range(n_u)]
    for _ in range(steps):
        pw = [_dot(pw[i], blocks_bd(pw[i])).astype(BF16) for i in range(n_u)]
        t = [t[i] + _dot(t[i].astype(BF16), blocks_bd(pw[i])) for i in range(n_u)]
    u = [(-_dot(t[i].astype(BF16), rows_bd(rhs[i].astype(BF16)))).astype(BF16) for i in range(n_u)]
    for i, (d, q) in enumerate(units):
        y_refs[d][0, :, lanes(q)] = xs[i][chunk:] + _dot(
            g[i][chunk:].astype(BF16), jnp.concatenate([vbd[i], rows_bd(u[i])], axis=0))
    for i, (d, q) in enumerate(units):
        upd = _dot_tn(jnp.concatenate([v[i], u[i]], axis=0), jnp.concatenate([kt[i], bt[i]], axis=0))
        s_ref[i] = (s_old[i] + upd) * ins[6 * d + 5][0, 0, 0, :, lanes(q)] * bdm


def _scan(v, at, rt, kt, bt, dec, t_lat, lane_block):
    bsz, tt, c = v.shape
    chunk = SCAN_CHUNK
    heads = SCAN_LANES // RWKV_HEAD
    n_tot = tt // chunk
    n_lat = t_lat // chunk
    n_ctx = n_tot - n_lat
    nsub = lane_block // SCAN_LANES

    ti = np.arange(chunk)
    before = [ti[None, :] < ti[:, None], ti[None, :] > ti[:, None]]
    gmask = np.stack([np.concatenate([np.tile(before[d], (1, 2 * heads)),
                                      np.tile(before[d] | np.eye(chunk, dtype=bool), (1, 2 * heads))], axis=0)
                      for d in range(2)]).astype(np.float32)
    li = np.arange(SCAN_LANES) // RWKV_HEAD
    bdm = (li[:, None] == li[None, :]).astype(np.float32)
    hmask = jnp.asarray((li[None, None, :] == np.arange(heads)[:, None, None]), BF16)
    eye = np.tile(np.eye(chunk, dtype=np.float32), (1, heads))

    def chunk_index(d, s):
        return jnp.where(s < n_ctx, n_lat + s, s - n_ctx) if d == 0 else n_tot - 1 - s

    def const(shape):
        nd = len(shape)
        return pl.BlockSpec(shape, lambda b, h, s: (0,) * nd)

    in_specs = [const(gmask.shape), const(bdm.shape), const(hmask.shape), const(eye.shape)]
    args = [jnp.asarray(gmask), jnp.asarray(bdm), hmask, jnp.asarray(eye)]
    out_specs = []
    for d in range(2):
        two = pl.BlockSpec((1, 1, chunk, lane_block), lambda b, h, s, d=d: (d, b, chunk_index(d, s), h))
        one = pl.BlockSpec((1, chunk, lane_block), lambda b, h, s, d=d: (b, chunk_index(d, s), h))
        in_specs += [two, two, two, two, one,
                     pl.BlockSpec((1, 1, 1, 1, lane_block), lambda b, h, s, d=d: (d, b, chunk_index(d, s), 0, h))]
        args += [at, rt, kt, bt, v, dec]
        out_specs.append(one)
    ys = jax.ShapeDtypeStruct((bsz, tt, c), F32)
    return pl.pallas_call(
        functools.partial(_scan_kernel, chunk=chunk, nsub=nsub),
        grid=(bsz, c // lane_block, n_tot),
        in_specs=in_specs,
        out_specs=out_specs,
        out_shape=[ys, ys],
        scratch_shapes=[pltpu.VMEM((2 * nsub, SCAN_LANES, SCAN_LANES), F32)],
        compiler_params=_cparams(("arbitrary", "arbitrary", "arbitrary")),
        name="rwkv_scan",
    )(*args)


def _rout_kernel(yf, yb, bonus, g, gng, gnb, e_ref, et_ref, o_ref):
    e = e_ref[...]
    et = et_ref[...]
    inv_n = 1.0 / RWKV_HEAD

    def head_sum(t):
        return _dot(_dot(t, e, HIGHEST), et, HIGHEST)

    y = yf[0] + yb[0]
    mu = head_sum(y) * inv_n
    yc = y - mu
    var = head_sum(yc * yc) * inv_n
    yn = yc * lax.rsqrt(var + GN_EPS) * gng[...] + gnb[...]
    o_ref[0] = ((yn + bonus[0]) * g[0]).astype(o_ref.dtype)


def _rwkv_out(yf, yb, bonus, g, gn_g, gn_b, t_lat, tb):
    bsz, _, c = g.shape
    heads = c // RWKV_HEAD
    e = (jnp.arange(c)[:, None] // RWKV_HEAD == jnp.arange(heads)[None, :]).astype(F32)
    one = pl.BlockSpec((1, tb, c), lambda b, j: (b, j, 0))

    def const(shape):
        return pl.BlockSpec(shape, lambda b, j: (0, 0))

    return pl.pallas_call(
        _rout_kernel,
        grid=(bsz, t_lat // tb),
        in_specs=[one, one, one, one, const((1, c)), const((1, c)), const((c, heads)), const((heads, c))],
        out_specs=one,
        out_shape=jax.ShapeDtypeStruct((bsz, t_lat, c), BF16),
        compiler_params=_cparams(("arbitrary", "arbitrary")),
        name="rwkv_out",
    )(yf, yb, bonus, g, gn_g.reshape(1, c), gn_b.reshape(1, c), e, e.T)


def _rope(x, c, s):
    return x * c + pltpu.roll(x, ATT_HEAD // 2, 1) * s


def _attn_kernel(sink_ref, q_ref, km_ref, k0_ref, kp_ref, vm_ref, v0_ref, vp_ref, kc_ref, vc_ref,
                 c0_ref, s0_ref, cm_ref, sm_ref, cp_ref, sp_ref, o_ref, *, n_blk):
    g = pl.program_id(1)
    n = pl.program_id(2)
    blk = ATT_HEAD
    scale = ATT_HEAD ** -0.5
    c0, s0 = c0_ref[...], s0_ref[...]
    q = q_ref[0]
    qh = [_rope(q[:, h * ATT_HEAD:(h + 1) * ATT_HEAD], c0, s0) for h in range(ATT_GROUPS)]
    qq = jnp.concatenate(qh, axis=0).astype(BF16)
    kw = jnp.concatenate([_rope(km_ref[0], cm_ref[...], sm_ref[...]),
                          _rope(k0_ref[0], c0, s0),
                          _rope(kp_ref[0], cp_ref[...], sp_ref[...])], axis=0).astype(BF16)
    vw = jnp.concatenate([vm_ref[0], v0_ref[0], vp_ref[0]], axis=0).astype(BF16)
    s_w = _dot_nt(qq, kw) * scale
    ii = lax.broadcasted_iota(jnp.int32, (ATT_GROUPS * blk, 3 * blk), 0) % blk
    jj = lax.broadcasted_iota(jnp.int32, (ATT_GROUPS * blk, 3 * blk), 1)
    in_prev = jnp.logical_and(jnp.logical_and(jj < blk, jj >= ii), n > 0)
    in_self = jnp.logical_and(jj >= blk, jj < 2 * blk)
    in_next = jnp.logical_and(jnp.logical_and(jj >= 2 * blk, jj - 2 * blk <= ii), n < n_blk - 1)
    valid = jnp.logical_or(jnp.logical_or(in_prev, in_self), in_next)
    s_w = jnp.where(valid, s_w, NEG_INF)
    s_c = _dot_nt(qq, kc_ref[0].astype(BF16)) * scale
    s_s = jnp.concatenate([jnp.full((blk, 1), sink_ref[g * ATT_GROUPS + h], F32) for h in range(ATT_GROUPS)],
                          axis=0)
    m = jnp.maximum(jnp.maximum(jnp.max(s_w, axis=-1, keepdims=True), jnp.max(s_c, axis=-1, keepdims=True)), s_s)
    p_w = jnp.exp(s_w - m)
    p_c = jnp.exp(s_c - m)
    den = jnp.sum(p_w, axis=-1, keepdims=True) + jnp.sum(p_c, axis=-1, keepdims=True) + jnp.exp(s_s - m)
    o = _dot(p_w.astype(BF16), vw) + _dot(p_c.astype(BF16), vc_ref[0].astype(BF16))
    o = o / den
    o_ref[0] = jnp.concatenate([o[h * blk:(h + 1) * blk] for h in range(ATT_GROUPS)], axis=1).astype(o_ref.dtype)


def _attention(proj, col, sink, cos2, sin2, t_lat):
    bsz, tt, _ = proj.shape
    blk = ATT_HEAD
    n_blk = t_lat // blk
    l = tt - t_lat
    qw = ATT_GROUPS * ATT_HEAD
    q_blk = col["q"] * LANES // qw

    def kv(name, off):
        cb = col[name]
        return pl.BlockSpec((1, blk, ATT_HEAD),
                            lambda b, g, n: (b, jnp.clip(n + off, 0, n_blk - 1), cb + g))

    def ctx(name):
        cb = col[name]
        return pl.BlockSpec((1, l, ATT_HEAD), lambda b, g, n: (b, t_lat // l, cb + g))

    def tab(off):
        return pl.BlockSpec((blk, ATT_HEAD), lambda b, g, n: (jnp.clip(n + off, 0, n_blk - 1), 0))

    in_specs = [pl.BlockSpec(memory_space=pltpu.SMEM),
                pl.BlockSpec((1, blk, qw), lambda b, g, n: (b, n, q_blk + g)),
                kv("ak", -1), kv("ak", 0), kv("ak", 1), kv("av", -1), kv("av", 0), kv("av", 1),
                ctx("ak"), ctx("av"), tab(0), tab(0), tab(-1), tab(-1), tab(1), tab(1)]
    return pl.pallas_call(
        functools.partial(_attn_kernel, n_blk=n_blk),
        grid=(bsz, ATT_KV_HEADS, n_blk),
        in_specs=in_specs,
        out_specs=pl.BlockSpec((1, blk, qw), lambda b, g, n: (b, n, g)),
        out_shape=jax.ShapeDtypeStruct((bsz, t_lat, ATT_KV_HEADS * qw), BF16),
        compiler_params=_cparams(("arbitrary", "arbitrary", "arbitrary")),
        name="window_attention",
    )(sink, proj, proj, proj, proj, proj, proj, proj, proj, proj, cos2, sin2, cos2, sin2, cos2, sin2)


def _merge_kernel(o1_ref, o2_ref, w1_ref, w2_ref, g1_ref, g2_ref, o_ref):
    y1 = _dot(o1_ref[0], w1_ref[...])
    y2 = _dot(o2_ref[0], w2_ref[...])
    o_ref[0] = (jax.nn.sigmoid(g1_ref[0]) * y1 + jax.nn.sigmoid(g2_ref[0]) * y2).astype(o_ref.dtype)


def _merge(o_rwkv, o_att, w1, w2, proj, col, tm, tn):
    bsz, t, c = o_rwkv.shape
    n = w1.shape[1]
    gr = col["gate_r"] * LANES // tn
    ga = col["gate_a"] * LANES // tn
    return pl.pallas_call(
        _merge_kernel,
        grid=(bsz, t // tm, n // tn),
        in_specs=[pl.BlockSpec((1, tm, c), lambda b, i, j: (b, i, 0)),
                  pl.BlockSpec((1, tm, c), lambda b, i, j: (b, i, 0)),
                  pl.BlockSpec((c, tn), lambda b, i, j: (0, j)),
                  pl.BlockSpec((c, tn), lambda b, i, j: (0, j)),
                  pl.BlockSpec((1, tm, tn), lambda b, i, j: (b, i, gr + j)),
                  pl.BlockSpec((1, tm, tn), lambda b, i, j: (b, i, ga + j))],
        out_specs=pl.BlockSpec((1, tm, tn), lambda b, i, j: (b, i, j)),
        out_shape=jax.ShapeDtypeStruct((bsz, t, n), BF16),
        compiler_params=_cparams(("arbitrary", "arbitrary", "arbitrary")),
        name="gated_merge",
    )(o_rwkv, o_att, w1, w2, proj, proj)


def _ln_kernel(x_ref, y_ref, gt_ref, g_ref, b_ref, *rest, with_mod):
    bi = pl.program_id(0)
    gt = gt_ref[pl.ds(bi, 1), :]
    z = DEEPNORM_ALPHA * x_ref[0] + gt * y_ref[0]
    mu = jnp.mean(z, axis=-1, keepdims=True)
    zc = z - mu
    var = jnp.mean(zc * zc, axis=-1, keepdims=True)
    out = zc * lax.rsqrt(var + LN_EPS) * g_ref[...] + b_ref[...]
    if with_mod:
        sh_ref, sc_ref, o_ref, h_ref = rest
        o_ref[0] = out
        h_ref[0] = (out * (1.0 + sc_ref[pl.ds(bi, 1), :]) + sh_ref[pl.ds(bi, 1), :]).astype(BF16)
    else:
        (o_ref,) = rest
        o_ref[0] = out


def _res_ln(x, y, mod, gate_col, g, b, tb, mod_cols=None):
    bsz, t, d = x.shape
    blk = pl.BlockSpec((1, tb, d), lambda bi, j: (bi, j, 0))
    vec = pl.BlockSpec((1, d), lambda bi, j: (0, 0))

    def modspec(cb):
        return pl.BlockSpec((MOD_ROWS, d), lambda bi, j: (0, cb))

    in_specs = [blk, blk, modspec(gate_col), vec, vec]
    args = [x, y, mod, g.reshape(1, d), b.reshape(1, d)]
    out_specs = [blk]
    out_shape = [jax.ShapeDtypeStruct((bsz, t, d), F32)]
    if mod_cols is not None:
        in_specs += [modspec(mod_cols[0]), modspec(mod_cols[1])]
        args += [mod, mod]
        out_specs.append(blk)
        out_shape.append(jax.ShapeDtypeStruct((bsz, t, d), BF16))
    return pl.pallas_call(
        functools.partial(_ln_kernel, with_mod=mod_cols is not None),
        grid=(bsz, t // tb),
        in_specs=in_specs,
        out_specs=out_specs,
        out_shape=out_shape,
        compiler_params=_cparams(("arbitrary", "arbitrary")),
        name="residual_layernorm",
    )(*args)


def _ffn_up_kernel(a_ref, wg_ref, wu_ref, o_ref):
    a = a_ref[...]
    gg = _dot(a, wg_ref[...])
    uu = _dot(a, wu_ref[...])
    o_ref[...] = (gg * jax.nn.sigmoid(gg) * uu).astype(o_ref.dtype)


def _ffn_up(a, wg, wu, tm, tf):
    m, k = a.shape
    f = wg.shape[1]
    return pl.pallas_call(
        _ffn_up_kernel,
        grid=(m // tm, f // tf),
        in_specs=[pl.BlockSpec((tm, k), lambda i, j: (i, 0)),
                  pl.BlockSpec((k, tf), lambda i, j: (0, j)),
                  pl.BlockSpec((k, tf), lambda i, j: (0, j))],
        out_specs=pl.BlockSpec((tm, tf), lambda i, j: (i, j)),
        out_shape=jax.ShapeDtypeStruct((m, f), BF16),
        compiler_params=_cparams(("arbitrary", "arbitrary")),
        name="swiglu_up",
    )(a, wg, wu)


def _rope_tables(t):
    rows = t // GRID_W
    row = jnp.broadcast_to(jnp.arange(rows, dtype=F32)[:, None], (rows, GRID_W)).reshape(t)
    colp = jnp.broadcast_to(jnp.arange(GRID_W, dtype=F32)[None, :], (rows, GRID_W)).reshape(t)
    axis_dim = ATT_HEAD // 2
    inv = ROPE_BASE ** (-jnp.arange(0, axis_dim, 2, dtype=F32) / axis_dim)
    ang = jnp.concatenate([row[:, None] * inv, colp[:, None] * inv], -1)
    cos, sin = jnp.cos(ang), jnp.sin(ang)
    return jnp.concatenate([cos, cos], -1), jnp.concatenate([-sin, sin], -1)


def _block(x, c, ctx, c_ctx, w_ada, b_ada, w_in, rwkv_shift, rwkv_w0, rwkv_w_up, rwkv_a0, rwkv_a_up,
           rwkv_g_up, rwkv_k_k, rwkv_k_a, rwkv_r_k, rwkv_gn_g, rwkv_gn_b, attn_sink, w_rwkv_o, w_att_o,
           w_out, ln1_g, ln1_b, w_ff_gate, w_ff_up, w_ff_down, ln2_g, ln2_b, tiles):
    bsz, t, d = x.shape
    l = ctx.shape[1]
    tt = t + l
    crw = rwkv_w0.shape[1]
    d_att = ATT_KV_HEADS * ATT_GROUPS * ATT_HEAD
    d_kv = ATT_KV_HEADS * ATT_HEAD
    n_lora = 2 * DECAY_LORA + 2 * ICLR_LORA + GATE_LORA
    n_rw = 3 * crw + n_lora

    cc = jnp.zeros((MOD_ROWS, d), F32).at[:bsz].set(c).at[bsz].set(c_ctx)
    mod = _ada(cc, w_ada, b_ada)

    o_q = n_rw
    o_k = o_q + d_att
    o_v = o_k + d_kv
    o_g = o_v + d_kv
    pad = LORA_PAD - n_lora
    w_perm = jnp.concatenate([w_in[:, o_g:], w_in[:, :3 * crw], w_in[:, o_q:o_g],
                              w_in[:, 3 * crw:n_rw], jnp.zeros((d, pad), F32)], axis=1).astype(BF16)
    col = {}
    off = 0
    for name, width in (("gate_r", d), ("gate_a", d), ("r", crw), ("k", crw), ("v", crw),
                        ("q", d_att), ("ak", d_kv), ("av", d_kv), ("lora", LORA_PAD)):
        col[name] = off // LANES
        off += width
    shift_rkv = rwkv_shift[:, :3 * crw]
    shift_lora = jnp.concatenate([rwkv_shift[:, 3 * crw:], jnp.zeros((3, pad), F32)], axis=1)

    h = _prep(x, ctx, mod)
    proj = _mm_wstat(h.reshape(bsz * tt, d), w_perm, tiles["proj_tm"], tiles["proj_tn"], F32, "in_proj")
    proj = proj.reshape(bsz, tt, -1)

    v, g, bonus, at, rt, kt, bt, dec = _features(proj, col, shift_rkv, shift_lora, rwkv_w0, rwkv_w_up, rwkv_a0,
                                                 rwkv_a_up, rwkv_g_up, rwkv_k_k, rwkv_k_a, rwkv_r_k.reshape(-1),
                                                 t, tiles["feat_tb"])
    yf, yb = _scan(v, at, rt, kt, bt, dec, t, tiles["scan_lanes"])
    o_rwkv = _rwkv_out(yf, yb, bonus, g, rwkv_gn_g, rwkv_gn_b, t, tiles["feat_tb"])

    cos2, sin2 = _rope_tables(t)
    o_att = _attention(proj, col, attn_sink, cos2, sin2, t)

    ym = _merge(o_rwkv, o_att, w_rwkv_o.astype(BF16), w_att_o.astype(BF16), proj, col,
                tiles["merge_tm"], tiles["merge_tn"])
    yo = _mm_astat(ym.reshape(bsz * t, d), w_out.astype(BF16), tiles["out_tm"], tiles["out_tn"], F32, "out_proj")
    x1, h2 = _res_ln(x, yo.reshape(bsz, t, d), mod, 2, ln1_g, ln1_b, tiles["ln_tb"], mod_cols=(3, 4))

    f = w_ff_gate.shape[1]
    fp = -(-f // tiles["ffn_pad"]) * tiles["ffn_pad"]
    wg = jnp.pad(w_ff_gate, ((0, 0), (0, fp - f))).astype(BF16)
    wu = jnp.pad(w_ff_up, ((0, 0), (0, fp - f))).astype(BF16)
    wd = jnp.pad(w_ff_down, ((0, fp - f), (0, 0))).astype(BF16)
    u = _ffn_up(h2.reshape(bsz * t, d), wg, wu, tiles["ffn_tm"], tiles["ffn_tf"])
    ff = _mm_ktiled(u, wd, tiles["down_tm"], tiles["down_tn"], fp // tiles["down_ksplit"], F32, "swiglu_down")
    (out,) = _res_ln(x1, ff.reshape(bsz, t, d), mod, 5, ln2_g, ln2_b, tiles["ln_tb"])
    return out


_TILES = dict(proj_tm=512, proj_tn=1024, feat_tb=128, scan_lanes=1024, merge_tm=1024, merge_tn=512, out_tm=1024, out_tn=1024,
              ln_tb=256, ffn_pad=1024, ffn_tm=1024, ffn_tf=512, down_tm=2048, down_tn=1024, down_ksplit=8)


def kernel(x, c, ctx, c_ctx, w_ada, b_ada, w_in, rwkv_shift, rwkv_w0, rwkv_w_up, rwkv_a0, rwkv_a_up, rwkv_g_up, rwkv_k_k, rwkv_k_a, rwkv_r_k, rwkv_gn_g, rwkv_gn_b, attn_sink, w_rwkv_o, w_att_o, w_out, ln1_g, ln1_b, w_ff_gate, w_ff_up, w_ff_down, ln2_g, ln2_b):
    assert w_ada.shape[0] == DEPTH
    return _block(x, c, ctx, c_ctx, w_ada[0], b_ada[0], w_in[0], rwkv_shift[0], rwkv_w0[0], rwkv_w_up[0],
                  rwkv_a0[0], rwkv_a_up[0], rwkv_g_up[0], rwkv_k_k[0], rwkv_k_a[0], rwkv_r_k[0], rwkv_gn_g[0],
                  rwkv_gn_b[0], attn_sink[0], w_rwkv_o[0], w_att_o[0], w_out[0], ln1_g[0], ln1_b[0],
                  w_ff_gate[0], w_ff_up[0], w_ff_down[0], ln2_g[0], ln2_b[0], _TILES)
```

```python
import functools

import jax
import jax.numpy as jnp
import numpy as np
from jax import lax
from jax.experimental import pallas as pl
from jax.experimental.pallas import tpu as pltpu

F32 = jnp.float32
BF16 = jnp.bfloat16
HIGHEST = lax.Precision.HIGHEST

RWKV_HEAD = 64
DECAY_LORA = 96
ICLR_LORA = 96
GATE_LORA = 256
GN_EPS = 64e-5
ATT_HEAD = 128
ATT_KV_HEADS = 4
ATT_GROUPS = 4
GRID_W = 64
ROPE_BASE = 10000.0
LN_EPS = 1e-5
DEPTH = 1
DEEPNORM_ALPHA = (2 * DEPTH) ** 0.25
NEG_INF = -1e30

LANES = 128
SUBLANES = 8
VMEM_LIMIT = 56 * 1024 * 1024
MOD_ROWS = 8
SCAN_CHUNK = 64
SCAN_LANES = 256
LORA_PAD = 1024


def _cparams(sem):
    return pltpu.CompilerParams(dimension_semantics=sem, vmem_limit_bytes=VMEM_LIMIT)


def _dot(a, b, precision=None):
    return jnp.dot(a, b, preferred_element_type=F32, precision=precision)


def _dot_nt(a, b, precision=None):
    return lax.dot_general(a, b, (((1,), (1,)), ((), ())), preferred_element_type=F32, precision=precision)


def _dot_tn(a, b, precision=None):
    return lax.dot_general(a, b, (((0,), (0,)), ((), ())), preferred_element_type=F32, precision=precision)


def _split2(x):
    hi = x.astype(BF16)
    return hi, (x - hi.astype(F32)).astype(BF16)


def _head_sum(x, ones_bd):
    rows = x.shape[0]
    hi, lo = _split2(x)
    both = jnp.concatenate([hi, lo], axis=0)
    out = []
    for s in range(x.shape[1] // SCAN_LANES):
        part = _dot(both[:, s * SCAN_LANES:(s + 1) * SCAN_LANES], ones_bd)
        out.append(part[:rows] + part[rows:])
    return jnp.concatenate(out, axis=1)


def _ada_kernel(c_ref, w_ref, b_ref, o_ref):
    a = c_ref[...]
    a = a * jax.nn.sigmoid(a)
    o_ref[...] = _dot(a, w_ref[...], HIGHEST) + b_ref[...]


def _ada(cc, w_ada, b_ada):
    d, n = w_ada.shape
    tn = 512
    return pl.pallas_call(
        _ada_kernel,
        grid=(n // tn,),
        in_specs=[pl.BlockSpec((MOD_ROWS, d), lambda j: (0, 0)),
                  pl.BlockSpec((d, tn), lambda j: (0, j)),
                  pl.BlockSpec((1, tn), lambda j: (0, j))],
        out_specs=pl.BlockSpec((MOD_ROWS, tn), lambda j: (0, j)),
        out_shape=jax.ShapeDtypeStruct((MOD_ROWS, n), F32),
        compiler_params=_cparams(("arbitrary",)),
        name="ada",
    )(cc, w_ada, b_ada.reshape(1, n))


def _prep_kernel(x_ref, ctx_ref, sh_ref, sc_ref, o_ref, *, n_lat, ctx_row):
    b = pl.program_id(0)
    j = pl.program_id(1)
    is_ctx = j >= n_lat
    row = jnp.where(is_ctx, ctx_row, b)
    sh = sh_ref[pl.ds(row, 1), :]
    sc = sc_ref[pl.ds(row, 1), :]
    xin = jnp.where(is_ctx, ctx_ref[0], x_ref[0])
    o_ref[0] = (xin * (1.0 + sc) + sh).astype(BF16)


def _prep(x, ctx, mod):
    bsz, t, d = x.shape
    l = ctx.shape[1]
    tb = l
    n_lat = t // tb
    return pl.pallas_call(
        functools.partial(_prep_kernel, n_lat=n_lat, ctx_row=bsz),
        grid=(bsz, n_lat + 1),
        in_specs=[pl.BlockSpec((1, tb, d), lambda b, j: (b, jnp.minimum(j, n_lat - 1), 0)),
                  pl.BlockSpec((1, l, d), lambda b, j: (b, 0, 0)),
                  pl.BlockSpec((MOD_ROWS, d), lambda b, j: (0, 0)),
                  pl.BlockSpec((MOD_ROWS, d), lambda b, j: (0, 1))],
        out_specs=pl.BlockSpec((1, tb, d), lambda b, j: (b, j, 0)),
        out_shape=jax.ShapeDtypeStruct((bsz, t + l, d), BF16),
        compiler_params=_cparams(("arbitrary", "arbitrary")),
        name="prep",
    )(x, ctx, mod, mod)


def _mm_kernel(a_ref, w_ref, o_ref):
    o_ref[...] = _dot(a_ref[...], w_ref[...]).astype(o_ref.dtype)


def _mm_wstat(a, w, tm, tn, out_dtype, name):
    m, k = a.shape
    n = w.shape[1]
    return pl.pallas_call(
        _mm_kernel,
        grid=(n // tn, m // tm),
        in_specs=[pl.BlockSpec((tm, k), lambda j, i: (i, 0)),
                  pl.BlockSpec((k, tn), lambda j, i: (0, j))],
        out_specs=pl.BlockSpec((tm, tn), lambda j, i: (i, j)),
        out_shape=jax.ShapeDtypeStruct((m, n), out_dtype),
        compiler_params=_cparams(("arbitrary", "arbitrary")),
        name=name,
    )(a, w)


def _mm_astat(a, w, tm, tn, out_dtype, name):
    m, k = a.shape
    n = w.shape[1]
    return pl.pallas_call(
        _mm_kernel,
        grid=(m // tm, n // tn),
        in_specs=[pl.BlockSpec((tm, k), lambda i, j: (i, 0)),
                  pl.BlockSpec((k, tn), lambda i, j: (0, j))],
        out_specs=pl.BlockSpec((tm, tn), lambda i, j: (i, j)),
        out_shape=jax.ShapeDtypeStruct((m, n), out_dtype),
        compiler_params=_cparams(("arbitrary", "arbitrary")),
        name=name,
    )(a, w)


def _mm_acc_kernel(a_ref, w_ref, o_ref, acc_ref):
    kk = pl.program_id(2)

    @pl.when(kk == 0)
    def _():
        acc_ref[...] = jnp.zeros_like(acc_ref)

    acc_ref[...] += _dot(a_ref[...], w_ref[...])

    @pl.when(kk == pl.num_programs(2) - 1)
    def _():
        o_ref[...] = acc_ref[...].astype(o_ref.dtype)


def _mm_ktiled(a, w, tm, tn, tk, out_dtype, name):
    m, k = a.shape
    n = w.shape[1]
    return pl.pallas_call(
        _mm_acc_kernel,
        grid=(m // tm, n // tn, k // tk),
        in_specs=[pl.BlockSpec((tm, tk), lambda i, j, q: (i, q)),
                  pl.BlockSpec((tk, tn), lambda i, j, q: (q, j))],
        out_specs=pl.BlockSpec((tm, tn), lambda i, j, q: (i, j)),
        out_shape=jax.ShapeDtypeStruct((m, n), out_dtype),
        scratch_shapes=[pltpu.VMEM((tm, tn), F32)],
        compiler_params=_cparams(("arbitrary", "arbitrary", "arbitrary")),
        name=name,
    )(a, w)


def _conv3(prev_ref, cur_ref, next_ref, w_ref, at_start, at_end):
    cur = cur_ref[0]
    rows = cur.shape[0]
    prev_row = jnp.where(at_start, 0.0, prev_ref[0][SUBLANES - 1:SUBLANES, :])
    next_row = jnp.where(at_end, 0.0, next_ref[0][0:1, :])
    up = jnp.concatenate([prev_row, cur[:rows - 1]], axis=0)
    down = jnp.concatenate([cur[1:], next_row], axis=0)
    w = w_ref[...]
    return up * w[0:1] + cur * w[1:2] + down * w[2:3]


def _feat_kernel(rp, rc, rn, kp, kc, kn, vp, vc, vn, lp, lc, ln_,
                 shr, shk, shv, shl, w0, wup, a0, aup, gup, kk_w, ka_w, rk_w, ones_ref, cm_ref,
                 v_o, g_o, bonus_o, at_o, rt_o, kt_o, bt_o, dec_o, *, n_lat, n_tot, chunk):
    j = pl.program_id(1)
    at_start = jnp.logical_or(j == 0, j == n_lat)
    at_end = jnp.logical_or(j == n_lat - 1, j == n_tot - 1)
    r = _conv3(rp, rc, rn, shr, at_start, at_end)
    k = _conv3(kp, kc, kn, shk, at_start, at_end)
    v = _conv3(vp, vc, vn, shv, at_start, at_end)
    lo = _conv3(lp, lc, ln_, shl, at_start, at_end)
    rows = r.shape[0]
    v_o[0] = v.astype(BF16)
    gd = lo[:, 2 * DECAY_LORA + 2 * ICLR_LORA:2 * DECAY_LORA + 2 * ICLR_LORA + GATE_LORA]
    ones_bd = ones_ref[...]
    g_o[0] = _dot(jax.nn.sigmoid(gd).astype(BF16), gup[...])
    kk = k * kk_w[...]
    kk = kk / jnp.maximum(jnp.sqrt(_head_sum(kk * kk, ones_bd)), 1e-12)
    k_sum = jnp.zeros_like(k)
    for d in range(2):
        wd = lo[:, d * DECAY_LORA:(d + 1) * DECAY_LORA]
        ad = lo[:, 2 * DECAY_LORA + d * ICLR_LORA:2 * DECAY_LORA + (d + 1) * ICLR_LORA]
        z = w0[d:d + 1, :] + _dot(jnp.tanh(wd).astype(BF16), wup[d])
        lw = jax.nn.sigmoid(z) * (-np.exp(-0.5))
        iclr = jax.nn.sigmoid(a0[d:d + 1, :] + _dot(ad.astype(BF16), aup[d]))
        k_dir = k * (1.0 + (iclr - 1.0) * ka_w[...])
        k_sum = k_sum + k_dir
        lw_hi, lw_lo = _split2(lw)
        cum = _dot(cm_ref[d], lw_hi) + _dot(cm_ref[d], lw_lo)
        p_inv = jnp.exp(-cum)
        at_o[d, 0] = (kk * jnp.exp(cum - lw)).astype(BF16)
        rt_o[d, 0] = (r * jnp.exp(cum)).astype(BF16)
        kt_o[d, 0] = (k_dir * p_inv).astype(BF16)
        bt_o[d, 0] = (kk * iclr * p_inv).astype(BF16)
        for q in range(rows // chunk):
            last = (q + 1) * chunk - 1 if d == 0 else q * chunk
            dec_o[d, 0, q] = jnp.exp(cum[last:last + 1, :])
    bonus_o[0] = _head_sum(r * k_sum * rk_w[...], ones_bd) * v


def _features(proj, col, shift_rkv, shift_lora, w0, w_up, a0, a_up, g_up, k_k, k_a, r_k, t_lat, tb):
    bsz, tt, _ = proj.shape
    c = w0.shape[1]
    chunk = SCAN_CHUNK
    n_tot = tt // tb
    n_lat = t_lat // tb
    hb = tb // SUBLANES
    n_h = tt // SUBLANES
    li = np.arange(SCAN_LANES) // RWKV_HEAD
    ones_bd = jnp.asarray(li[:, None] == li[None, :], BF16)
    ti = np.arange(tb)
    same = (ti[:, None] // chunk) == (ti[None, :] // chunk)
    cmask = jnp.asarray(np.stack([same & (ti[None, :] <= ti[:, None]), same & (ti[None, :] >= ti[:, None])]), BF16)

    def main(cb, width):
        blk = cb * LANES // width
        return pl.BlockSpec((1, tb, width), lambda b, j: (b, j, blk))

    def prev(cb, width):
        blk = cb * LANES // width
        return pl.BlockSpec((1, SUBLANES, width), lambda b, j: (b, jnp.maximum(j * hb - 1, 0), blk))

    def nxt(cb, width):
        blk = cb * LANES // width
        return pl.BlockSpec((1, SUBLANES, width), lambda b, j: (b, jnp.minimum((j + 1) * hb, n_h - 1), blk))

    def const(shape):
        nd = len(shape)
        return pl.BlockSpec(shape, lambda b, j: (0,) * nd)

    in_specs = []
    for name, width in (("r", c), ("k", c), ("v", c), ("lora", LORA_PAD)):
        in_specs += [prev(col[name], width), main(col[name], width), nxt(col[name], width)]
    in_specs += [const((3, c)), const((3, c)), const((3, c)), const((3, LORA_PAD)),
                 const((2, c)), const((2, DECAY_LORA, c)), const((2, c)), const((2, ICLR_LORA, c)),
                 const((GATE_LORA, c)), const((1, c)), const((1, c)), const((1, c)),
                 const((SCAN_LANES, SCAN_LANES)), const((2, tb, tb))]
    one = pl.BlockSpec((1, tb, c), lambda b, j: (b, j, 0))
    two = pl.BlockSpec((2, 1, tb, c), lambda b, j: (0, b, j, 0))
    dec = pl.BlockSpec((2, 1, tb // chunk, 1, c), lambda b, j: (0, b, j, 0, 0))
    s1 = jax.ShapeDtypeStruct((bsz, tt, c), F32)
    s1h = jax.ShapeDtypeStruct((bsz, tt, c), BF16)
    s2h = jax.ShapeDtypeStruct((2, bsz, tt, c), BF16)
    sdec = jax.ShapeDtypeStruct((2, bsz, tt // chunk, 1, c), F32)
    args = [proj] * 12 + [shift_rkv[:, :c], shift_rkv[:, c:2 * c], shift_rkv[:, 2 * c:], shift_lora,
                          w0, w_up.astype(BF16), a0, a_up.astype(BF16), g_up.astype(BF16),
                          k_k.reshape(1, c), k_a.reshape(1, c), r_k.reshape(1, c), ones_bd, cmask]
    return pl.pallas_call(
        functools.partial(_feat_kernel, n_lat=n_lat, n_tot=n_tot, chunk=chunk),
        grid=(bsz, n_tot),
        in_specs=in_specs,
        out_specs=[one, one, one, two, two, two, two, dec],
        out_shape=[s1h, s1, s1, s2h, s2h, s2h, s2h, sdec],
        compiler_params=_cparams(("arbitrary", "arbitrary")),
        name="rwkv_features",
    )(*args)


def _scan_kernel(gm_ref, bdm_ref, hm_ref, eye_ref, *refs, chunk, nsub):
    ins = refs[:12]
    y_refs = refs[12:14]
    s_ref = refs[14]
    heads = SCAN_LANES // RWKV_HEAD

    @pl.when(pl.program_id(2) == 0)
    def _():
        s_ref[...] = jnp.zeros_like(s_ref)

    bdm = bdm_ref[...]
    bdm_h = bdm.astype(BF16)
    eye = eye_ref[...]
    hms = [hm_ref[h] for h in range(heads)]

    def rows_bd(m):
        return jnp.concatenate([m * hms[h] for h in range(heads)], axis=0)

    def blocks_bd(m):
        return jnp.concatenate([m] * heads, axis=0) * bdm_h

    steps = chunk.bit_length() - 2
    units = [(d, q) for d in range(2) for q in range(nsub)]
    n_u = len(units)

    def lanes(q):
        return slice(q * SCAN_LANES, (q + 1) * SCAN_LANES)

    kt = [ins[6 * d + 2][0, 0, :, lanes(q)] for d, q in units]
    bt = [ins[6 * d + 3][0, 0, :, lanes(q)] for d, q in units]
    v = [ins[6 * d + 4][0, :, lanes(q)] for d, q in units]
    x = [jnp.concatenate([ins[6 * d][0, 0, :, lanes(q)], ins[6 * d + 1][0, 0, :, lanes(q)]], axis=0)
         for d, q in units]
    g = [_dot_nt(x[i], jnp.concatenate([rows_bd(kt[i]), rows_bd(bt[i])], axis=0)) * gm_ref[units[i][0]]
         for i in range(n_u)]
    s_old = [s_ref[i] for i in range(n_u)]
    xs = [_dot_nt(x[i], s_old[i].astype(BF16)) for i in range(n_u)]
    vbd = [rows_bd(v[i]) for i in range(n_u)]
    rhs = [xs[i][:chunk] + _dot(g[i][:chunk, :SCAN_LANES].astype(BF16), vbd[i]) for i in range(n_u)]
    t = [eye - g[i][:chunk, SCAN_LANES:] for i in range(n_u)]
    pw = [g[i][:chunk, SCAN_LANES:].astype(BF16) for i in range(n_u)]
    for _ in range(steps):
        pw = [_dot(pw[i], blocks_bd(pw[i])).astype(BF16) for i in range(n_u)]
        t = [t[i] + _dot(t[i].astype(BF16), blocks_bd(pw[i])) for i in range(n_u)]
    u = [(-_dot(t[i].astype(BF16), rows_bd(rhs[i].astype(BF16)))).astype(BF16) for i in range(n_u)]
    for i, (d, q) in enumerate(units):
        y_refs[d][0, :, lanes(q)] = xs[i][chunk:] + _dot(
            g[i][chunk:].astype(BF16), jnp.concatenate([vbd[i], rows_bd(u[i])], axis=0))
    for i, (d, q) in enumerate(units):
        upd = _dot_tn(jnp.concatenate([v[i], u[i]], axis=0), jnp.concatenate([kt[i], bt[i]], axis=0))
        s_ref[i] = (s_old[i] + upd) * ins[6 * d + 5][0, 0, 0, :, lanes(q)] * bdm


def _scan(v, at, rt, kt, bt, dec, t_lat, lane_block):
    bsz, tt, c = v.shape
    chunk = SCAN_CHUNK
    heads = SCAN_LANES // RWKV_HEAD
    n_tot = tt // chunk
    n_lat = t_lat // chunk
    n_ctx = n_tot - n_lat
    nsub = lane_block // SCAN_LANES

    ti = np.arange(chunk)
    before = [ti[None, :] < ti[:, None], ti[None, :] > ti[:, None]]
    gmask = np.stack([np.concatenate([np.tile(before[d], (1, 2 * heads)),
                                      np.tile(before[d] | np.eye(chunk, dtype=bool), (1, 2 * heads))], axis=0)
                      for d in range(2)]).astype(np.float32)
    li = np.arange(SCAN_LANES) // RWKV_HEAD
    bdm = (li[:, None] == li[None, :]).astype(np.float32)
    hmask = jnp.asarray((li[None, None, :] == np.arange(heads)[:, None, None]), BF16)
    eye = np.tile(np.eye(chunk, dtype=np.float32), (1, heads))

    def chunk_index(d, s):
        return jnp.where(s < n_ctx, n_lat + s, s - n_ctx) if d == 0 else n_tot - 1 - s

    def const(shape):
        nd = len(shape)
        return pl.BlockSpec(shape, lambda b, h, s: (0,) * nd)

    in_specs = [const(gmask.shape), const(bdm.shape), const(hmask.shape), const(eye.shape)]
    args = [jnp.asarray(gmask), jnp.asarray(bdm), hmask, jnp.asarray(eye)]
    out_specs = []
    for d in range(2):
        two = pl.BlockSpec((1, 1, chunk, lane_block), lambda b, h, s, d=d: (d, b, chunk_index(d, s), h))
        one = pl.BlockSpec((1, chunk, lane_block), lambda b, h, s, d=d: (b, chunk_index(d, s), h))
        in_specs += [two, two, two, two, one,
                     pl.BlockSpec((1, 1, 1, 1, lane_block), lambda b, h, s, d=d: (d, b, chunk_index(d, s), 0, h))]
        args += [at, rt, kt, bt, v, dec]
        out_specs.append(one)
    ys = jax.ShapeDtypeStruct((bsz, tt, c), F32)
    return pl.pallas_call(
        functools.partial(_scan_kernel, chunk=chunk, nsub=nsub),
        grid=(bsz, c // lane_block, n_tot),
        in_specs=in_specs,
        out_specs=out_specs,
        out_shape=[ys, ys],
        scratch_shapes=[pltpu.VMEM((2 * nsub, SCAN_LANES, SCAN_LANES), F32)],
        compiler_params=_cparams(("arbitrary", "arbitrary", "arbitrary")),
        name="rwkv_scan",
    )(*args)


def _rout_kernel(yf, yb, bonus, g, gng, gnb, ones_ref, o_ref):
    ones_bd = ones_ref[...]
    inv_n = 1.0 / RWKV_HEAD
    y = yf[0] + yb[0]
    mu = _head_sum(y, ones_bd) * inv_n
    yc = y - mu
    var = _head_sum(yc * yc, ones_bd) * inv_n
    yn = yc * lax.rsqrt(var + GN_EPS) * gng[...] + gnb[...]
    o_ref[0] = ((yn + bonus[0]) * g[0]).astype(o_ref.dtype)


def _rwkv_out(yf, yb, bonus, g, gn_g, gn_b, t_lat, tb):
    bsz, _, c = g.shape
    li = np.arange(SCAN_LANES) // RWKV_HEAD
    ones_bd = jnp.asarray(li[:, None] == li[None, :], BF16)
    one = pl.BlockSpec((1, tb, c), lambda b, j: (b, j, 0))

    def const(shape):
        return pl.BlockSpec(shape, lambda b, j: (0, 0))

    return pl.pallas_call(
        _rout_kernel,
        grid=(bsz, t_lat // tb),
        in_specs=[one, one, one, one, const((1, c)), const((1, c)), const((SCAN_LANES, SCAN_LANES))],
        out_specs=one,
        out_shape=jax.ShapeDtypeStruct((bsz, t_lat, c), BF16),
        compiler_params=_cparams(("arbitrary", "arbitrary")),
        name="rwkv_out",
    )(yf, yb, bonus, g, gn_g.reshape(1, c), gn_b.reshape(1, c), ones_bd)


def _rope(x, c, s):
    return x * c + pltpu.roll(x, ATT_HEAD // 2, 1) * s


def _attn_kernel(sink_ref, q_ref, km_ref, k0_ref, kp_ref, vm_ref, v0_ref, vp_ref, kc_ref, vc_ref,
                 c0_ref, s0_ref, cm_ref, sm_ref, cp_ref, sp_ref, o_ref, *, n_blk):
    n = pl.program_id(1)
    blk = ATT_HEAD
    scale = ATT_HEAD ** -0.5
    kvh = range(ATT_KV_HEADS)
    c0, s0 = c0_ref[...], s0_ref[...]
    cm, sm = cm_ref[...], sm_ref[...]
    cp, sp = cp_ref[...], sp_ref[...]

    def head(ref, i):
        return ref[0, :, i * ATT_HEAD:(i + 1) * ATT_HEAD]

    qq = [jnp.concatenate([_rope(head(q_ref, g * ATT_GROUPS + h), c0, s0) for h in range(ATT_GROUPS)],
                          axis=0).astype(BF16) for g in kvh]
    kw = [jnp.concatenate([_rope(head(km_ref, g), cm, sm), _rope(head(k0_ref, g), c0, s0),
                           _rope(head(kp_ref, g), cp, sp)], axis=0).astype(BF16) for g in kvh]
    vw = [jnp.concatenate([head(vm_ref, g), head(v0_ref, g), head(vp_ref, g)], axis=0).astype(BF16) for g in kvh]
    kc = [head(kc_ref, g).astype(BF16) for g in kvh]
    vc = [head(vc_ref, g).astype(BF16) for g in kvh]
    ii = lax.broadcasted_iota(jnp.int32, (ATT_GROUPS * blk, 3 * blk), 0) % blk
    jj = lax.broadcasted_iota(jnp.int32, (ATT_GROUPS * blk, 3 * blk), 1)
    in_prev = jnp.logical_and(jnp.logical_and(jj < blk, jj >= ii), n > 0)
    in_self = jnp.logical_and(jj >= blk, jj < 2 * blk)
    in_next = jnp.logical_and(jnp.logical_and(jj >= 2 * blk, jj - 2 * blk <= ii), n < n_blk - 1)
    valid = jnp.logical_or(jnp.logical_or(in_prev, in_self), in_next)
    s_w = [jnp.where(valid, _dot_nt(qq[g], kw[g]) * scale, NEG_INF) for g in kvh]
    s_c = [_dot_nt(qq[g], kc[g]) * scale for g in kvh]
    s_s = [jnp.concatenate([jnp.full((blk, 1), sink_ref[g * ATT_GROUPS + h], F32) for h in range(ATT_GROUPS)],
                           axis=0) for g in kvh]
    m = [jnp.maximum(jnp.maximum(jnp.max(s_w[g], axis=-1, keepdims=True),
                                 jnp.max(s_c[g], axis=-1, keepdims=True)), s_s[g]) for g in kvh]
    p_w = [jnp.exp(s_w[g] - m[g]) for g in kvh]
    p_c = [jnp.exp(s_c[g] - m[g]) for g in kvh]
    den = [jnp.sum(p_w[g], axis=-1, keepdims=True) + jnp.sum(p_c[g], axis=-1, keepdims=True)
           + jnp.exp(s_s[g] - m[g]) for g in kvh]
    o = [(_dot(p_w[g].astype(BF16), vw[g]) + _dot(p_c[g].astype(BF16), vc[g])) / den[g] for g in kvh]
    o_ref[0] = jnp.concatenate([o[g][h * blk:(h + 1) * blk] for g in kvh for h in range(ATT_GROUPS)],
                               axis=1).astype(o_ref.dtype)


def _attention(proj, col, sink, cos2, sin2, t_lat):
    bsz, tt, _ = proj.shape
    blk = ATT_HEAD
    n_blk = t_lat // blk
    l = tt - t_lat
    qw = ATT_KV_HEADS * ATT_GROUPS * ATT_HEAD
    kw = ATT_KV_HEADS * ATT_HEAD
    q_blk = col["q"] * LANES // qw

    def kv(name, off):
        cb = col[name] * LANES // kw
        return pl.BlockSpec((1, blk, kw), lambda b, n: (b, jnp.clip(n + off, 0, n_blk - 1), cb))

    def ctx(name):
        cb = col[name] * LANES // kw
        return pl.BlockSpec((1, l, kw), lambda b, n: (b, t_lat // l, cb))

    def tab(off):
        return pl.BlockSpec((blk, ATT_HEAD), lambda b, n: (jnp.clip(n + off, 0, n_blk - 1), 0))

    in_specs = [pl.BlockSpec(memory_space=pltpu.SMEM),
                pl.BlockSpec((1, blk, qw), lambda b, n: (b, n, q_blk)),
                kv("ak", -1), kv("ak", 0), kv("ak", 1), kv("av", -1), kv("av", 0), kv("av", 1),
                ctx("ak"), ctx("av"), tab(0), tab(0), tab(-1), tab(-1), tab(1), tab(1)]
    return pl.pallas_call(
        functools.partial(_attn_kernel, n_blk=n_blk),
        grid=(bsz, n_blk),
        in_specs=in_specs,
        out_specs=pl.BlockSpec((1, blk, qw), lambda b, n: (b, n, 0)),
        out_shape=jax.ShapeDtypeStruct((bsz, t_lat, qw), BF16),
        compiler_params=_cparams(("arbitrary", "arbitrary")),
        name="window_attention",
    )(sink, proj, proj, proj, proj, proj, proj, proj, proj, proj, cos2, sin2, cos2, sin2, cos2, sin2)


def _merge_kernel(o1_ref, o2_ref, w1_ref, w2_ref, g1_ref, g2_ref, o_ref):
    y1 = _dot(o1_ref[0], w1_ref[...])
    y2 = _dot(o2_ref[0], w2_ref[...])
    o_ref[0] = (jax.nn.sigmoid(g1_ref[0]) * y1 + jax.nn.sigmoid(g2_ref[0]) * y2).astype(o_ref.dtype)


def _merge(o_rwkv, o_att, w1, w2, proj, col, tm, tn):
    bsz, t, c = o_rwkv.shape
    n = w1.shape[1]
    gr = col["gate_r"] * LANES // tn
    ga = col["gate_a"] * LANES // tn
    return pl.pallas_call(
        _merge_kernel,
        grid=(bsz, t // tm, n // tn),
        in_specs=[pl.BlockSpec((1, tm, c), lambda b, i, j: (b, i, 0)),
                  pl.BlockSpec((1, tm, c), lambda b, i, j: (b, i, 0)),
                  pl.BlockSpec((c, tn), lambda b, i, j: (0, j)),
                  pl.BlockSpec((c, tn), lambda b, i, j: (0, j)),
                  pl.BlockSpec((1, tm, tn), lambda b, i, j: (b, i, gr + j)),
                  pl.BlockSpec((1, tm, tn), lambda b, i, j: (b, i, ga + j))],
        out_specs=pl.BlockSpec((1, tm, tn), lambda b, i, j: (b, i, j)),
        out_shape=jax.ShapeDtypeStruct((bsz, t, n), BF16),
        compiler_params=_cparams(("arbitrary", "arbitrary", "arbitrary")),
        name="gated_merge",
    )(o_rwkv, o_att, w1, w2, proj, proj)


def _ln_kernel(x_ref, y_ref, gt_ref, g_ref, b_ref, *rest, with_mod):
    bi = pl.program_id(0)
    gt = gt_ref[pl.ds(bi, 1), :]
    z = DEEPNORM_ALPHA * x_ref[0] + gt * y_ref[0]
    mu = jnp.mean(z, axis=-1, keepdims=True)
    zc = z - mu
    var = jnp.mean(zc * zc, axis=-1, keepdims=True)
    out = zc * lax.rsqrt(var + LN_EPS) * g_ref[...] + b_ref[...]
    if with_mod:
        sh_ref, sc_ref, o_ref, h_ref = rest
        o_ref[0] = out
        h_ref[0] = (out * (1.0 + sc_ref[pl.ds(bi, 1), :]) + sh_ref[pl.ds(bi, 1), :]).astype(BF16)
    else:
        (o_ref,) = rest
        o_ref[0] = out


def _res_ln(x, y, mod, gate_col, g, b, tb, mod_cols=None):
    bsz, t, d = x.shape
    blk = pl.BlockSpec((1, tb, d), lambda bi, j: (bi, j, 0))
    vec = pl.BlockSpec((1, d), lambda bi, j: (0, 0))

    def modspec(cb):
        return pl.BlockSpec((MOD_ROWS, d), lambda bi, j: (0, cb))

    in_specs = [blk, blk, modspec(gate_col), vec, vec]
    args = [x, y, mod, g.reshape(1, d), b.reshape(1, d)]
    out_specs = [blk]
    out_shape = [jax.ShapeDtypeStruct((bsz, t, d), F32)]
    if mod_cols is not None:
        in_specs += [modspec(mod_cols[0]), modspec(mod_cols[1])]
        args += [mod, mod]
        out_specs.append(blk)
        out_shape.append(jax.ShapeDtypeStruct((bsz, t, d), BF16))
    return pl.pallas_call(
        functools.partial(_ln_kernel, with_mod=mod_cols is not None),
        grid=(bsz, t // tb),
        in_specs=in_specs,
        out_specs=out_specs,
        out_shape=out_shape,
        compiler_params=_cparams(("arbitrary", "arbitrary")),
        name="residual_layernorm",
    )(*args)


def _ffn_up_kernel(a_ref, wg_ref, wu_ref, o_ref):
    a = a_ref[...]
    gg = _dot(a, wg_ref[...])
    uu = _dot(a, wu_ref[...])
    o_ref[...] = (gg * jax.nn.sigmoid(gg) * uu).astype(o_ref.dtype)


def _ffn_up(a, wg, wu, tm, tf):
    m, k = a.shape
    f = wg.shape[1]
    return pl.pallas_call(
        _ffn_up_kernel,
        grid=(m // tm, f // tf),
        in_specs=[pl.BlockSpec((tm, k), lambda i, j: (i, 0)),
                  pl.BlockSpec((k, tf), lambda i, j: (0, j)),
                  pl.BlockSpec((k, tf), lambda i, j: (0, j))],
        out_specs=pl.BlockSpec((tm, tf), lambda i, j: (i, j)),
        out_shape=jax.ShapeDtypeStruct((m, f), BF16),
        compiler_params=_cparams(("arbitrary", "arbitrary")),
        name="swiglu_up",
    )(a, wg, wu)


def _rope_tables(t):
    rows = t // GRID_W
    row = jnp.broadcast_to(jnp.arange(rows, dtype=F32)[:, None], (rows, GRID_W)).reshape(t)
    colp = jnp.broadcast_to(jnp.arange(GRID_W, dtype=F32)[None, :], (rows, GRID_W)).reshape(t)
    axis_dim = ATT_HEAD // 2
    inv = ROPE_BASE ** (-jnp.arange(0, axis_dim, 2, dtype=F32) / axis_dim)
    ang = jnp.concatenate([row[:, None] * inv, colp[:, None] * inv], -1)
    cos, sin = jnp.cos(ang), jnp.sin(ang)
    return jnp.concatenate([cos, cos], -1), jnp.concatenate([-sin, sin], -1)


def _block(x, c, ctx, c_ctx, w_ada, b_ada, w_in, rwkv_shift, rwkv_w0, rwkv_w_up, rwkv_a0, rwkv_a_up,
           rwkv_g_up, rwkv_k_k, rwkv_k_a, rwkv_r_k, rwkv_gn_g, rwkv_gn_b, attn_sink, w_rwkv_o, w_att_o,
           w_out, ln1_g, ln1_b, w_ff_gate, w_ff_up, w_ff_down, ln2_g, ln2_b, tiles):
    bsz, t, d = x.shape
    l = ctx.shape[1]
    tt = t + l
    crw = rwkv_w0.shape[1]
    d_att = ATT_KV_HEADS * ATT_GROUPS * ATT_HEAD
    d_kv = ATT_KV_HEADS * ATT_HEAD
    n_lora = 2 * DECAY_LORA + 2 * ICLR_LORA + GATE_LORA
    n_rw = 3 * crw + n_lora

    cc = jnp.zeros((MOD_ROWS, d), F32).at[:bsz].set(c).at[bsz].set(c_ctx)
    mod = _ada(cc, w_ada, b_ada)

    o_q = n_rw
    o_k = o_q + d_att
    o_v = o_k + d_kv
    o_g = o_v + d_kv
    pad = LORA_PAD - n_lora
    w_perm = jnp.concatenate([w_in[:, o_g:], w_in[:, :3 * crw], w_in[:, o_q:o_g],
                              w_in[:, 3 * crw:n_rw], jnp.zeros((d, pad), F32)], axis=1).astype(BF16)
    col = {}
    off = 0
    for name, width in (("gate_r", d), ("gate_a", d), ("r", crw), ("k", crw), ("v", crw),
                        ("q", d_att), ("ak", d_kv), ("av", d_kv), ("lora", LORA_PAD)):
        col[name] = off // LANES
        off += width
    shift_rkv = rwkv_shift[:, :3 * crw]
    shift_lora = jnp.concatenate([rwkv_shift[:, 3 * crw:], jnp.zeros((3, pad), F32)], axis=1)

    h = _prep(x, ctx, mod)
    proj = _mm_wstat(h.reshape(bsz * tt, d), w_perm, tiles["proj_tm"], tiles["proj_tn"], F32, "in_proj")
    proj = proj.reshape(bsz, tt, -1)

    v, g, bonus, at, rt, kt, bt, dec = _features(proj, col, shift_rkv, shift_lora, rwkv_w0, rwkv_w_up, rwkv_a0,
                                                 rwkv_a_up, rwkv_g_up, rwkv_k_k, rwkv_k_a, rwkv_r_k.reshape(-1),
                                                 t, tiles["feat_tb"])
    yf, yb = _scan(v, at, rt, kt, bt, dec, t, tiles["scan_lanes"])
    o_rwkv = _rwkv_out(yf, yb, bonus, g, rwkv_gn_g, rwkv_gn_b, t, tiles["feat_tb"])

    cos2, sin2 = _rope_tables(t)
    o_att = _attention(proj, col, attn_sink, cos2, sin2, t)

    ym = _merge(o_rwkv, o_att, w_rwkv_o.astype(BF16), w_att_o.astype(BF16), proj, col,
                tiles["merge_tm"], tiles["merge_tn"])
    yo = _mm_astat(ym.reshape(bsz * t, d), w_out.astype(BF16), tiles["out_tm"], tiles["out_tn"], F32, "out_proj")
    x1, h2 = _res_ln(x, yo.reshape(bsz, t, d), mod, 2, ln1_g, ln1_b, tiles["ln_tb"], mod_cols=(3, 4))

    f = w_ff_gate.shape[1]
    fp = -(-f // tiles["ffn_pad"]) * tiles["ffn_pad"]
    wg = jnp.pad(w_ff_gate, ((0, 0), (0, fp - f))).astype(BF16)
    wu = jnp.pad(w_ff_up, ((0, 0), (0, fp - f))).astype(BF16)
    wd = jnp.pad(w_ff_down, ((0, fp - f), (0, 0))).astype(BF16)
    u = _ffn_up(h2.reshape(bsz * t, d), wg, wu, tiles["ffn_tm"], tiles["ffn_tf"])
    ff = _mm_ktiled(u, wd, tiles["down_tm"], tiles["down_tn"], fp // tiles["down_ksplit"], F32, "swiglu_down")
    (out,) = _res_ln(x1, ff.reshape(bsz, t, d), mod, 5, ln2_g, ln2_b, tiles["ln_tb"])
    return out


_TILES = dict(proj_tm=512, proj_tn=1024, feat_tb=128, scan_lanes=1024, merge_tm=1024, merge_tn=512, out_tm=1024, out_tn=1024,
              ln_tb=256, ffn_pad=1024, ffn_tm=1024, ffn_tf=512, down_tm=2048, down_tn=1024, down_ksplit=8)


def kernel(x, c, ctx, c_ctx, w_ada, b_ada, w_in, rwkv_shift, rwkv_w0, rwkv_w_up, rwkv_a0, rwkv_a_up, rwkv_g_up, rwkv_k_k, rwkv_k_a, rwkv_r_k, rwkv_gn_g, rwkv_gn_b, attn_sink, w_rwkv_o, w_att_o, w_out, ln1_g, ln1_b, w_ff_gate, w_ff_up, w_ff_down, ln2_g, ln2_b):
    assert w_ada.shape[0] == DEPTH
    return _block(x, c, ctx, c_ctx, w_ada[0], b_ada[0], w_in[0], rwkv_shift[0], rwkv_w0[0], rwkv_w_up[0],
                  rwkv_a0[0], rwkv_a_up[0], rwkv_g_up[0], rwkv_k_k[0], rwkv_k_a[0], rwkv_r_k[0], rwkv_gn_g[0],
                  rwkv_gn_b[0], attn_sink[0], w_rwkv_o[0], w_att_o[0], w_out[0], ln1_g[0], ln1_b[0],
                  w_ff_gate[0], w_ff_up[0], w_ff_down[0], ln2_g[0], ln2_b[0], _TILES)
```

```python
import functools

import jax
import jax.numpy as jnp
import numpy as np
from jax import lax
from jax.experimental import pallas as pl
from jax.experimental.pallas import tpu as pltpu

F32 = jnp.float32
BF16 = jnp.bfloat16
HIGHEST = lax.Precision.HIGHEST

RWKV_HEAD = 64
DECAY_LORA = 96
ICLR_LORA = 96
GATE_LORA = 256
GN_EPS = 64e-5
ATT_HEAD = 128
ATT_KV_HEADS = 4
ATT_GROUPS = 4
GRID_W = 64
ROPE_BASE = 10000.0
LN_EPS = 1e-5
DEPTH = 1
DEEPNORM_ALPHA = (2 * DEPTH) ** 0.25
NEG_INF = -1e30

LANES = 128
SUBLANES = 8
VMEM_LIMIT = 56 * 1024 * 1024
MOD_ROWS = 8
SCAN_CHUNK = 64
SCAN_LANES = 256
LORA_PAD = 1024


def _cparams(sem):
    return pltpu.CompilerParams(dimension_semantics=sem, vmem_limit_bytes=VMEM_LIMIT)


def _dot(a, b, precision=None):
    return jnp.dot(a, b, preferred_element_type=F32, precision=precision)


def _dot_nt(a, b, precision=None):
    return lax.dot_general(a, b, (((1,), (1,)), ((), ())), preferred_element_type=F32, precision=precision)


def _dot_tn(a, b, precision=None):
    return lax.dot_general(a, b, (((0,), (0,)), ((), ())), preferred_element_type=F32, precision=precision)


def _split2(x):
    hi = x.astype(BF16)
    return hi, (x - hi.astype(F32)).astype(BF16)


def _head_sum(x, ones_bd):
    rows = x.shape[0]
    hi, lo = _split2(x)
    both = jnp.concatenate([hi, lo], axis=0)
    out = []
    for s in range(x.shape[1] // SCAN_LANES):
        part = _dot(both[:, s * SCAN_LANES:(s + 1) * SCAN_LANES], ones_bd)
        out.append(part[:rows] + part[rows:])
    return jnp.concatenate(out, axis=1)


def _ada_kernel(c_ref, w_ref, b_ref, o_ref):
    a = c_ref[...]
    a = a * jax.nn.sigmoid(a)
    o_ref[...] = _dot(a, w_ref[...], HIGHEST) + b_ref[...]


def _ada(cc, w_ada, b_ada):
    d, n = w_ada.shape
    tn = 512
    return pl.pallas_call(
        _ada_kernel,
        grid=(n // tn,),
        in_specs=[pl.BlockSpec((MOD_ROWS, d), lambda j: (0, 0)),
                  pl.BlockSpec((d, tn), lambda j: (0, j)),
                  pl.BlockSpec((1, tn), lambda j: (0, j))],
        out_specs=pl.BlockSpec((MOD_ROWS, tn), lambda j: (0, j)),
        out_shape=jax.ShapeDtypeStruct((MOD_ROWS, n), F32),
        compiler_params=_cparams(("arbitrary",)),
        name="ada",
    )(cc, w_ada, b_ada.reshape(1, n))


def _prep_kernel(x_ref, ctx_ref, sh_ref, sc_ref, o_ref, *, n_lat, ctx_row):
    b = pl.program_id(0)
    j = pl.program_id(1)
    is_ctx = j >= n_lat
    row = jnp.where(is_ctx, ctx_row, b)
    sh = sh_ref[pl.ds(row, 1), :]
    sc = sc_ref[pl.ds(row, 1), :]
    xin = jnp.where(is_ctx, ctx_ref[0], x_ref[0])
    o_ref[0] = (xin * (1.0 + sc) + sh).astype(BF16)


def _prep(x, ctx, mod):
    bsz, t, d = x.shape
    l = ctx.shape[1]
    tb = l
    n_lat = t // tb
    return pl.pallas_call(
        functools.partial(_prep_kernel, n_lat=n_lat, ctx_row=bsz),
        grid=(bsz, n_lat + 1),
        in_specs=[pl.BlockSpec((1, tb, d), lambda b, j: (b, jnp.minimum(j, n_lat - 1), 0)),
                  pl.BlockSpec((1, l, d), lambda b, j: (b, 0, 0)),
                  pl.BlockSpec((MOD_ROWS, d), lambda b, j: (0, 0)),
                  pl.BlockSpec((MOD_ROWS, d), lambda b, j: (0, 1))],
        out_specs=pl.BlockSpec((1, tb, d), lambda b, j: (b, j, 0)),
        out_shape=jax.ShapeDtypeStruct((bsz, t + l, d), BF16),
        compiler_params=_cparams(("arbitrary", "arbitrary")),
        name="prep",
    )(x, ctx, mod, mod)


def _mm_kernel(a_ref, w_ref, o_ref):
    o_ref[...] = _dot(a_ref[...], w_ref[...]).astype(o_ref.dtype)


def _mm_wstat(a, w, tm, tn, out_dtype, name):
    m, k = a.shape
    n = w.shape[1]
    return pl.pallas_call(
        _mm_kernel,
        grid=(n // tn, m // tm),
        in_specs=[pl.BlockSpec((tm, k), lambda j, i: (i, 0)),
                  pl.BlockSpec((k, tn), lambda j, i: (0, j))],
        out_specs=pl.BlockSpec((tm, tn), lambda j, i: (i, j)),
        out_shape=jax.ShapeDtypeStruct((m, n), out_dtype),
        compiler_params=_cparams(("arbitrary", "arbitrary")),
        name=name,
    )(a, w)


def _mm_astat(a, w, tm, tn, out_dtype, name):
    m, k = a.shape
    n = w.shape[1]
    return pl.pallas_call(
        _mm_kernel,
        grid=(m // tm, n // tn),
        in_specs=[pl.BlockSpec((tm, k), lambda i, j: (i, 0)),
                  pl.BlockSpec((k, tn), lambda i, j: (0, j))],
        out_specs=pl.BlockSpec((tm, tn), lambda i, j: (i, j)),
        out_shape=jax.ShapeDtypeStruct((m, n), out_dtype),
        compiler_params=_cparams(("arbitrary", "arbitrary")),
        name=name,
    )(a, w)


def _mm_acc_kernel(a_ref, w_ref, o_ref, acc_ref):
    kk = pl.program_id(2)

    @pl.when(kk == 0)
    def _():
        acc_ref[...] = jnp.zeros_like(acc_ref)

    acc_ref[...] += _dot(a_ref[...], w_ref[...])

    @pl.when(kk == pl.num_programs(2) - 1)
    def _():
        o_ref[...] = acc_ref[...].astype(o_ref.dtype)


def _mm_ktiled(a, w, tm, tn, tk, out_dtype, name):
    m, k = a.shape
    n = w.shape[1]
    return pl.pallas_call(
        _mm_acc_kernel,
        grid=(m // tm, n // tn, k // tk),
        in_specs=[pl.BlockSpec((tm, tk), lambda i, j, q: (i, q)),
                  pl.BlockSpec((tk, tn), lambda i, j, q: (q, j))],
        out_specs=pl.BlockSpec((tm, tn), lambda i, j, q: (i, j)),
        out_shape=jax.ShapeDtypeStruct((m, n), out_dtype),
        scratch_shapes=[pltpu.VMEM((tm, tn), F32)],
        compiler_params=_cparams(("arbitrary", "arbitrary", "arbitrary")),
        name=name,
    )(a, w)


def _conv3(prev_ref, cur_ref, next_ref, w_ref, at_start, at_end):
    cur = cur_ref[0]
    rows = cur.shape[0]
    prev_row = jnp.where(at_start, 0.0, prev_ref[0][SUBLANES - 1:SUBLANES, :])
    next_row = jnp.where(at_end, 0.0, next_ref[0][0:1, :])
    up = jnp.concatenate([prev_row, cur[:rows - 1]], axis=0)
    down = jnp.concatenate([cur[1:], next_row], axis=0)
    w = w_ref[...]
    return up * w[0:1] + cur * w[1:2] + down * w[2:3]


def _feat_kernel(rp, rc, rn, kp, kc, kn, vp, vc, vn, lp, lc, ln_,
                 shr, shk, shv, shl, w0, wup, a0, aup, gup, kk_w, ka_w, rk_w, ones_ref, cm_ref,
                 v_o, g_o, bonus_o, at_o, rt_o, kt_o, bt_o, dec_o, *, n_lat, n_tot, chunk):
    j = pl.program_id(1)
    at_start = jnp.logical_or(j == 0, j == n_lat)
    at_end = jnp.logical_or(j == n_lat - 1, j == n_tot - 1)
    r = _conv3(rp, rc, rn, shr, at_start, at_end)
    k = _conv3(kp, kc, kn, shk, at_start, at_end)
    v = _conv3(vp, vc, vn, shv, at_start, at_end)
    lo = _conv3(lp, lc, ln_, shl, at_start, at_end)
    rows = r.shape[0]
    v_o[0] = v.astype(BF16)
    gd = lo[:, 2 * DECAY_LORA + 2 * ICLR_LORA:2 * DECAY_LORA + 2 * ICLR_LORA + GATE_LORA]
    ones_bd = ones_ref[...]
    g_o[0] = _dot(jax.nn.sigmoid(gd).astype(BF16), gup[...])
    kk = k * kk_w[...]
    kk = kk / jnp.maximum(jnp.sqrt(_head_sum(kk * kk, ones_bd)), 1e-12)
    k_sum = jnp.zeros_like(k)
    for d in range(2):
        wd = lo[:, d * DECAY_LORA:(d + 1) * DECAY_LORA]
        ad = lo[:, 2 * DECAY_LORA + d * ICLR_LORA:2 * DECAY_LORA + (d + 1) * ICLR_LORA]
        z = w0[d:d + 1, :] + _dot(jnp.tanh(wd).astype(BF16), wup[d])
        lw = jax.nn.sigmoid(z) * (-np.exp(-0.5))
        iclr = jax.nn.sigmoid(a0[d:d + 1, :] + _dot(ad.astype(BF16), aup[d]))
        k_dir = k * (1.0 + (iclr - 1.0) * ka_w[...])
        k_sum = k_sum + k_dir
        lw_hi, lw_lo = _split2(lw)
        cum = _dot(cm_ref[d], lw_hi) + _dot(cm_ref[d], lw_lo)
        p_inv = jnp.exp(-cum)
        at_o[d, 0] = (kk * jnp.exp(cum - lw)).astype(BF16)
        rt_o[d, 0] = (r * jnp.exp(cum)).astype(BF16)
        kt_o[d, 0] = (k_dir * p_inv).astype(BF16)
        bt_o[d, 0] = (kk * iclr * p_inv).astype(BF16)
        for q in range(rows // chunk):
            last = (q + 1) * chunk - 1 if d == 0 else q * chunk
            dec_o[d, 0, q] = jnp.exp(cum[last:last + 1, :])
    bonus_o[0] = _head_sum(r * k_sum * rk_w[...], ones_bd) * v


def _features(proj, col, shift_rkv, shift_lora, w0, w_up, a0, a_up, g_up, k_k, k_a, r_k, t_lat, tb):
    bsz, tt, _ = proj.shape
    c = w0.shape[1]
    chunk = SCAN_CHUNK
    n_tot = tt // tb
    n_lat = t_lat // tb
    hb = tb // SUBLANES
    n_h = tt // SUBLANES
    li = np.arange(SCAN_LANES) // RWKV_HEAD
    ones_bd = jnp.asarray(li[:, None] == li[None, :], BF16)
    ti = np.arange(tb)
    same = (ti[:, None] // chunk) == (ti[None, :] // chunk)
    cmask = jnp.asarray(np.stack([same & (ti[None, :] <= ti[:, None]), same & (ti[None, :] >= ti[:, None])]), BF16)

    def main(cb, width):
        blk = cb * LANES // width
        return pl.BlockSpec((1, tb, width), lambda b, j: (b, j, blk))

    def prev(cb, width):
        blk = cb * LANES // width
        return pl.BlockSpec((1, SUBLANES, width), lambda b, j: (b, jnp.maximum(j * hb - 1, 0), blk))

    def nxt(cb, width):
        blk = cb * LANES // width
        return pl.BlockSpec((1, SUBLANES, width), lambda b, j: (b, jnp.minimum((j + 1) * hb, n_h - 1), blk))

    def const(shape):
        nd = len(shape)
        return pl.BlockSpec(shape, lambda b, j: (0,) * nd)

    in_specs = []
    for name, width in (("r", c), ("k", c), ("v", c), ("lora", LORA_PAD)):
        in_specs += [prev(col[name], width), main(col[name], width), nxt(col[name], width)]
    in_specs += [const((3, c)), const((3, c)), const((3, c)), const((3, LORA_PAD)),
                 const((2, c)), const((2, DECAY_LORA, c)), const((2, c)), const((2, ICLR_LORA, c)),
                 const((GATE_LORA, c)), const((1, c)), const((1, c)), const((1, c)),
                 const((SCAN_LANES, SCAN_LANES)), const((2, tb, tb))]
    one = pl.BlockSpec((1, tb, c), lambda b, j: (b, j, 0))
    two = pl.BlockSpec((2, 1, tb, c), lambda b, j: (0, b, j, 0))
    dec = pl.BlockSpec((2, 1, tb // chunk, 1, c), lambda b, j: (0, b, j, 0, 0))
    s1 = jax.ShapeDtypeStruct((bsz, tt, c), F32)
    s1h = jax.ShapeDtypeStruct((bsz, tt, c), BF16)
    s2h = jax.ShapeDtypeStruct((2, bsz, tt, c), BF16)
    sdec = jax.ShapeDtypeStruct((2, bsz, tt // chunk, 1, c), F32)
    args = [proj] * 12 + [shift_rkv[:, :c], shift_rkv[:, c:2 * c], shift_rkv[:, 2 * c:], shift_lora,
                          w0, w_up.astype(BF16), a0, a_up.astype(BF16), g_up.astype(BF16),
                          k_k.reshape(1, c), k_a.reshape(1, c), r_k.reshape(1, c), ones_bd, cmask]
    return pl.pallas_call(
        functools.partial(_feat_kernel, n_lat=n_lat, n_tot=n_tot, chunk=chunk),
        grid=(bsz, n_tot),
        in_specs=in_specs,
        out_specs=[one, one, one, two, two, two, two, dec],
        out_shape=[s1h, s1, s1, s2h, s2h, s2h, s2h, sdec],
        compiler_params=_cparams(("arbitrary", "arbitrary")),
        name="rwkv_features",
    )(*args)


def _scan_kernel(gm_ref, bdm_ref, hm_ref, eye_ref, *refs, chunk, nsub):
    ins = refs[:12]
    y_refs = refs[12:14]
    s_ref = refs[14]
    heads = SCAN_LANES // RWKV_HEAD

    @pl.when(pl.program_id(2) == 0)
    def _():
        s_ref[...] = jnp.zeros_like(s_ref)

    bdm = bdm_ref[...]
    bdm_h = bdm.astype(BF16)
    eye = eye_ref[...]
    hms = [hm_ref[h] for h in range(heads)]

    def rows_bd(m):
        return jnp.concatenate([m * hms[h] for h in range(heads)], axis=0)

    def blocks_bd(m):
        return jnp.concatenate([m] * heads, axis=0) * bdm_h

    steps = chunk.bit_length() - 2
    units = [(d, q) for d in range(2) for q in range(nsub)]
    n_u = len(units)

    def lanes(q):
        return slice(q * SCAN_LANES, (q + 1) * SCAN_LANES)

    kt = [ins[6 * d + 2][0, 0, :, lanes(q)] for d, q in units]
    bt = [ins[6 * d + 3][0, 0, :, lanes(q)] for d, q in units]
    v = [ins[6 * d + 4][0, :, lanes(q)] for d, q in units]
    x = [jnp.concatenate([ins[6 * d][0, 0, :, lanes(q)], ins[6 * d + 1][0, 0, :, lanes(q)]], axis=0)
         for d, q in units]
    g = [_dot_nt(x[i], jnp.concatenate([rows_bd(kt[i]), rows_bd(bt[i])], axis=0)) * gm_ref[units[i][0]]
         for i in range(n_u)]
    s_old = [s_ref[i] for i in range(n_u)]
    xs = [_dot_nt(x[i], s_old[i].astype(BF16)) for i in range(n_u)]
    vbd = [rows_bd(v[i]) for i in range(n_u)]
    rhs = [xs[i][:chunk] + _dot(g[i][:chunk, :SCAN_LANES].astype(BF16), vbd[i]) for i in range(n_u)]
    t = [eye - g[i][:chunk, SCAN_LANES:] for i in range(n_u)]
    pw = [g[i][:chunk, SCAN_LANES:].astype(BF16) for i in range(n_u)]
    for _ in range(steps):
        pw = [_dot(pw[i], blocks_bd(pw[i])).astype(BF16) for i in range(n_u)]
        t = [t[i] + _dot(t[i].astype(BF16), blocks_bd(pw[i])) for i in range(n_u)]
    u = [(-_dot(t[i].astype(BF16), rows_bd(rhs[i].astype(BF16)))).astype(BF16) for i in range(n_u)]
    for i, (d, q) in enumerate(units):
        y_refs[d][0, :, lanes(q)] = xs[i][chunk:] + _dot(
            g[i][chunk:].astype(BF16), jnp.concatenate([vbd[i], rows_bd(u[i])], axis=0))
    for i, (d, q) in enumerate(units):
        upd = _dot_tn(jnp.concatenate([v[i], u[i]], axis=0), jnp.concatenate([kt[i], bt[i]], axis=0))
        s_ref[i] = (s_old[i] + upd) * ins[6 * d + 5][0, 0, 0, :, lanes(q)] * bdm


def _scan(v, at, rt, kt, bt, dec, t_lat, lane_block):
    bsz, tt, c = v.shape
    chunk = SCAN_CHUNK
    heads = SCAN_LANES // RWKV_HEAD
    n_tot = tt // chunk
    n_lat = t_lat // chunk
    n_ctx = n_tot - n_lat
    nsub = lane_block // SCAN_LANES

    ti = np.arange(chunk)
    before = [ti[None, :] < ti[:, None], ti[None, :] > ti[:, None]]
    gmask = np.stack([np.concatenate([np.tile(before[d], (1, 2 * heads)),
                                      np.tile(before[d] | np.eye(chunk, dtype=bool), (1, 2 * heads))], axis=0)
                      for d in range(2)]).astype(np.float32)
    li = np.arange(SCAN_LANES) // RWKV_HEAD
    bdm = (li[:, None] == li[None, :]).astype(np.float32)
    hmask = jnp.asarray((li[None, None, :] == np.arange(heads)[:, None, None]), BF16)
    eye = np.tile(np.eye(chunk, dtype=np.float32), (1, heads))

    def chunk_index(d, s):
        return jnp.where(s < n_ctx, n_lat + s, s - n_ctx) if d == 0 else n_tot - 1 - s

    def const(shape):
        nd = len(shape)
        return pl.BlockSpec(shape, lambda b, h, s: (0,) * nd)

    in_specs = [const(gmask.shape), const(bdm.shape), const(hmask.shape), const(eye.shape)]
    args = [jnp.asarray(gmask), jnp.asarray(bdm), hmask, jnp.asarray(eye)]
    out_specs = []
    for d in range(2):
        two = pl.BlockSpec((1, 1, chunk, lane_block), lambda b, h, s, d=d: (d, b, chunk_index(d, s), h))
        one = pl.BlockSpec((1, chunk, lane_block), lambda b, h, s, d=d: (b, chunk_index(d, s), h))
        in_specs += [two, two, two, two, one,
                     pl.BlockSpec((1, 1, 1, 1, lane_block), lambda b, h, s, d=d: (d, b, chunk_index(d, s), 0, h))]
        args += [at, rt, kt, bt, v, dec]
        out_specs.append(one)
    ys = jax.ShapeDtypeStruct((bsz, tt, c), F32)
    return pl.pallas_call(
        functools.partial(_scan_kernel, chunk=chunk, nsub=nsub),
        grid=(bsz, c // lane_block, n_tot),
        in_specs=in_specs,
        out_specs=out_specs,
        out_shape=[ys, ys],
        scratch_shapes=[pltpu.VMEM((2 * nsub, SCAN_LANES, SCAN_LANES), F32)],
        compiler_params=_cparams(("arbitrary", "arbitrary", "arbitrary")),
        name="rwkv_scan",
    )(*args)


def _rout_kernel(yf, yb, bonus, g, gng, gnb, ones_ref, o_ref):
    ones_bd = ones_ref[...]
    inv_n = 1.0 / RWKV_HEAD
    y = yf[0] + yb[0]
    mu = _head_sum(y, ones_bd) * inv_n
    yc = y - mu
    var = _head_sum(yc * yc, ones_bd) * inv_n
    yn = yc * lax.rsqrt(var + GN_EPS) * gng[...] + gnb[...]
    o_ref[0] = ((yn + bonus[0]) * g[0]).astype(o_ref.dtype)


def _rwkv_out(yf, yb, bonus, g, gn_g, gn_b, t_lat, tb):
    bsz, _, c = g.shape
    li = np.arange(SCAN_LANES) // RWKV_HEAD
    ones_bd = jnp.asarray(li[:, None] == li[None, :], BF16)
    one = pl.BlockSpec((1, tb, c), lambda b, j: (b, j, 0))

    def const(shape):
        return pl.BlockSpec(shape, lambda b, j: (0, 0))

    return pl.pallas_call(
        _rout_kernel,
        grid=(bsz, t_lat // tb),
        in_specs=[one, one, one, one, const((1, c)), const((1, c)), const((SCAN_LANES, SCAN_LANES))],
        out_specs=one,
        out_shape=jax.ShapeDtypeStruct((bsz, t_lat, c), BF16),
        compiler_params=_cparams(("arbitrary", "arbitrary")),
        name="rwkv_out",
    )(yf, yb, bonus, g, gn_g.reshape(1, c), gn_b.reshape(1, c), ones_bd)


def _rope(x, c, s):
    return x * c + pltpu.roll(x, ATT_HEAD // 2, 1) * s


def _attn_kernel(sink_ref, q_ref, km_ref, k0_ref, kp_ref, vm_ref, v0_ref, vp_ref, kc_ref, vc_ref,
                 c0_ref, s0_ref, cm_ref, sm_ref, cp_ref, sp_ref, o_ref, *, n_blk):
    n = pl.program_id(1)
    blk = ATT_HEAD
    scale = ATT_HEAD ** -0.5
    kvh = range(ATT_KV_HEADS)
    c0, s0 = c0_ref[...], s0_ref[...]
    cm, sm = cm_ref[...], sm_ref[...]
    cp, sp = cp_ref[...], sp_ref[...]

    def head(ref, i):
        return ref[0, :, i * ATT_HEAD:(i + 1) * ATT_HEAD]

    qq = [jnp.concatenate([_rope(head(q_ref, g * ATT_GROUPS + h), c0, s0) for h in range(ATT_GROUPS)],
                          axis=0).astype(BF16) for g in kvh]
    kw = [jnp.concatenate([_rope(head(km_ref, g), cm, sm), _rope(head(k0_ref, g), c0, s0),
                           _rope(head(kp_ref, g), cp, sp)], axis=0).astype(BF16) for g in kvh]
    vw = [jnp.concatenate([head(vm_ref, g), head(v0_ref, g), head(vp_ref, g)], axis=0).astype(BF16) for g in kvh]
    kc = [head(kc_ref, g).astype(BF16) for g in kvh]
    vc = [head(vc_ref, g).astype(BF16) for g in kvh]
    ii = lax.broadcasted_iota(jnp.int32, (ATT_GROUPS * blk, 3 * blk), 0) % blk
    jj = lax.broadcasted_iota(jnp.int32, (ATT_GROUPS * blk, 3 * blk), 1)
    in_prev = jnp.logical_and(jnp.logical_and(jj < blk, jj >= ii), n > 0)
    in_self = jnp.logical_and(jj >= blk, jj < 2 * blk)
    in_next = jnp.logical_and(jnp.logical_and(jj >= 2 * blk, jj - 2 * blk <= ii), n < n_blk - 1)
    valid = jnp.logical_or(jnp.logical_or(in_prev, in_self), in_next)
    s_w = [jnp.where(valid, _dot_nt(qq[g], kw[g]) * scale, NEG_INF) for g in kvh]
    s_c = [_dot_nt(qq[g], kc[g]) * scale for g in kvh]
    s_s = [jnp.concatenate([jnp.full((blk, 1), sink_ref[g * ATT_GROUPS + h], F32) for h in range(ATT_GROUPS)],
                           axis=0) for g in kvh]
    m = [jnp.maximum(jnp.maximum(jnp.max(s_w[g], axis=-1, keepdims=True),
                                 jnp.max(s_c[g], axis=-1, keepdims=True)), s_s[g]) for g in kvh]
    p_w = [jnp.exp(s_w[g] - m[g]) for g in kvh]
    p_c = [jnp.exp(s_c[g] - m[g]) for g in kvh]
    den = [jnp.sum(p_w[g], axis=-1, keepdims=True) + jnp.sum(p_c[g], axis=-1, keepdims=True)
           + jnp.exp(s_s[g] - m[g]) for g in kvh]
    o = [(_dot(p_w[g].astype(BF16), vw[g]) + _dot(p_c[g].astype(BF16), vc[g])) / den[g] for g in kvh]
    o_ref[0] = jnp.concatenate([o[g][h * blk:(h + 1) * blk] for g in kvh for h in range(ATT_GROUPS)],
                               axis=1).astype(o_ref.dtype)


def _attention(proj, col, sink, cos2, sin2, t_lat):
    bsz, tt, _ = proj.shape
    blk = ATT_HEAD
    n_blk = t_lat // blk
    l = tt - t_lat
    qw = ATT_KV_HEADS * ATT_GROUPS * ATT_HEAD
    kw = ATT_KV_HEADS * ATT_HEAD
    q_blk = col["q"] * LANES // qw

    def kv(name, off):
        cb = col[name] * LANES // kw
        return pl.BlockSpec((1, blk, kw), lambda b, n: (b, jnp.clip(n + off, 0, n_blk - 1), cb))

    def ctx(name):
        cb = col[name] * LANES // kw
        return pl.BlockSpec((1, l, kw), lambda b, n: (b, t_lat // l, cb))

    def tab(off):
        return pl.BlockSpec((blk, ATT_HEAD), lambda b, n: (jnp.clip(n + off, 0, n_blk - 1), 0))

    in_specs = [pl.BlockSpec(memory_space=pltpu.SMEM),
                pl.BlockSpec((1, blk, qw), lambda b, n: (b, n, q_blk)),
                kv("ak", -1), kv("ak", 0), kv("ak", 1), kv("av", -1), kv("av", 0), kv("av", 1),
                ctx("ak"), ctx("av"), tab(0), tab(0), tab(-1), tab(-1), tab(1), tab(1)]
    return pl.pallas_call(
        functools.partial(_attn_kernel, n_blk=n_blk),
        grid=(bsz, n_blk),
        in_specs=in_specs,
        out_specs=pl.BlockSpec((1, blk, qw), lambda b, n: (b, n, 0)),
        out_shape=jax.ShapeDtypeStruct((bsz, t_lat, qw), BF16),
        compiler_params=_cparams(("arbitrary", "arbitrary")),
        name="window_attention",
    )(sink, proj, proj, proj, proj, proj, proj, proj, proj, proj, cos2, sin2, cos2, sin2, cos2, sin2)


def _merge_kernel(o1_ref, o2_ref, w1_ref, w2_ref, g1_ref, g2_ref, o_ref):
    y1 = _dot(o1_ref[0], w1_ref[...])
    y2 = _dot(o2_ref[0], w2_ref[...])
    o_ref[0] = (jax.nn.sigmoid(g1_ref[0]) * y1 + jax.nn.sigmoid(g2_ref[0]) * y2).astype(o_ref.dtype)


def _merge(o_rwkv, o_att, w1, w2, proj, col, tm, tn):
    bsz, t, c = o_rwkv.shape
    n = w1.shape[1]
    gr = col["gate_r"] * LANES // tn
    ga = col["gate_a"] * LANES // tn
    return pl.pallas_call(
        _merge_kernel,
        grid=(bsz, t // tm, n // tn),
        in_specs=[pl.BlockSpec((1, tm, c), lambda b, i, j: (b, i, 0)),
                  pl.BlockSpec((1, tm, c), lambda b, i, j: (b, i, 0)),
                  pl.BlockSpec((c, tn), lambda b, i, j: (0, j)),
                  pl.BlockSpec((c, tn), lambda b, i, j: (0, j)),
                  pl.BlockSpec((1, tm, tn), lambda b, i, j: (b, i, gr + j)),
                  pl.BlockSpec((1, tm, tn), lambda b, i, j: (b, i, ga + j))],
        out_specs=pl.BlockSpec((1, tm, tn), lambda b, i, j: (b, i, j)),
        out_shape=jax.ShapeDtypeStruct((bsz, t, n), BF16),
        compiler_params=_cparams(("arbitrary", "arbitrary", "arbitrary")),
        name="gated_merge",
    )(o_rwkv, o_att, w1, w2, proj, proj)


def _ln_kernel(x_ref, y_ref, gt_ref, g_ref, b_ref, *rest, with_mod):
    bi = pl.program_id(0)
    gt = gt_ref[pl.ds(bi, 1), :]
    z = DEEPNORM_ALPHA * x_ref[0] + gt * y_ref[0]
    mu = jnp.mean(z, axis=-1, keepdims=True)
    zc = z - mu
    var = jnp.mean(zc * zc, axis=-1, keepdims=True)
    out = zc * lax.rsqrt(var + LN_EPS) * g_ref[...] + b_ref[...]
    if with_mod:
        sh_ref, sc_ref, o_ref, h_ref = rest
        o_ref[0] = out
        h_ref[0] = (out * (1.0 + sc_ref[pl.ds(bi, 1), :]) + sh_ref[pl.ds(bi, 1), :]).astype(BF16)
    else:
        (o_ref,) = rest
        o_ref[0] = out


def _res_ln(x, y, mod, gate_col, g, b, tb, mod_cols=None):
    bsz, t, d = x.shape
    blk = pl.BlockSpec((1, tb, d), lambda bi, j: (bi, j, 0))
    vec = pl.BlockSpec((1, d), lambda bi, j: (0, 0))

    def modspec(cb):
        return pl.BlockSpec((MOD_ROWS, d), lambda bi, j: (0, cb))

    in_specs = [blk, blk, modspec(gate_col), vec, vec]
    args = [x, y, mod, g.reshape(1, d), b.reshape(1, d)]
    out_specs = [blk]
    out_shape = [jax.ShapeDtypeStruct((bsz, t, d), F32)]
    if mod_cols is not None:
        in_specs += [modspec(mod_cols[0]), modspec(mod_cols[1])]
        args += [mod, mod]
        out_specs.append(blk)
        out_shape.append(jax.ShapeDtypeStruct((bsz, t, d), BF16))
    return pl.pallas_call(
        functools.partial(_ln_kernel, with_mod=mod_cols is not None),
        grid=(bsz, t // tb),
        in_specs=in_specs,
        out_specs=out_specs,
        out_shape=out_shape,
        compiler_params=_cparams(("arbitrary", "arbitrary")),
        name="residual_layernorm",
    )(*args)


def _ffn_up_kernel(a_ref, wg_ref, wu_ref, o_ref):
    a = a_ref[...]
    gg = _dot(a, wg_ref[...])
    uu = _dot(a, wu_ref[...])
    o_ref[...] = (gg * jax.nn.sigmoid(gg) * uu).astype(o_ref.dtype)


def _ffn_up(a, wg, wu, tm, tf):
    m, k = a.shape
    f = wg.shape[1]
    return pl.pallas_call(
        _ffn_up_kernel,
        grid=(m // tm, f // tf),
        in_specs=[pl.BlockSpec((tm, k), lambda i, j: (i, 0)),
                  pl.BlockSpec((k, tf), lambda i, j: (0, j)),
                  pl.BlockSpec((k, tf), lambda i, j: (0, j))],
        out_specs=pl.BlockSpec((tm, tf), lambda i, j: (i, j)),
        out_shape=jax.ShapeDtypeStruct((m, f), BF16),
        compiler_params=_cparams(("arbitrary", "arbitrary")),
        name="swiglu_up",
    )(a, wg, wu)


def _rope_tables(t):
    rows = t // GRID_W
    row = jnp.broadcast_to(jnp.arange(rows, dtype=F32)[:, None], (rows, GRID_W)).reshape(t)
    colp = jnp.broadcast_to(jnp.arange(GRID_W, dtype=F32)[None, :], (rows, GRID_W)).reshape(t)
    axis_dim = ATT_HEAD // 2
    inv = ROPE_BASE ** (-jnp.arange(0, axis_dim, 2, dtype=F32) / axis_dim)
    ang = jnp.concatenate([row[:, None] * inv, colp[:, None] * inv], -1)
    cos, sin = jnp.cos(ang), jnp.sin(ang)
    return jnp.concatenate([cos, cos], -1), jnp.concatenate([-sin, sin], -1)


def _block(x, c, ctx, c_ctx, w_ada, b_ada, w_in, rwkv_shift, rwkv_w0, rwkv_w_up, rwkv_a0, rwkv_a_up,
           rwkv_g_up, rwkv_k_k, rwkv_k_a, rwkv_r_k, rwkv_gn_g, rwkv_gn_b, attn_sink, w_rwkv_o, w_att_o,
           w_out, ln1_g, ln1_b, w_ff_gate, w_ff_up, w_ff_down, ln2_g, ln2_b, tiles):
    bsz, t, d = x.shape
    l = ctx.shape[1]
    tt = t + l
    crw = rwkv_w0.shape[1]
    d_att = ATT_KV_HEADS * ATT_GROUPS * ATT_HEAD
    d_kv = ATT_KV_HEADS * ATT_HEAD
    n_lora = 2 * DECAY_LORA + 2 * ICLR_LORA + GATE_LORA
    n_rw = 3 * crw + n_lora

    cc = jnp.zeros((MOD_ROWS, d), F32).at[:bsz].set(c).at[bsz].set(c_ctx)
    mod = _ada(cc, w_ada, b_ada)

    o_q = n_rw
    o_k = o_q + d_att
    o_v = o_k + d_kv
    o_g = o_v + d_kv
    pad = LORA_PAD - n_lora
    w_perm = jnp.concatenate([w_in[:, o_g:], w_in[:, :3 * crw], w_in[:, o_q:o_g],
                              w_in[:, 3 * crw:n_rw], jnp.zeros((d, pad), F32)], axis=1).astype(BF16)
    col = {}
    off = 0
    for name, width in (("gate_r", d), ("gate_a", d), ("r", crw), ("k", crw), ("v", crw),
                        ("q", d_att), ("ak", d_kv), ("av", d_kv), ("lora", LORA_PAD)):
        col[name] = off // LANES
        off += width
    shift_rkv = rwkv_shift[:, :3 * crw]
    shift_lora = jnp.concatenate([rwkv_shift[:, 3 * crw:], jnp.zeros((3, pad), F32)], axis=1)

    h = _prep(x, ctx, mod)
    proj = _mm_wstat(h.reshape(bsz * tt, d), w_perm, tiles["proj_tm"], tiles["proj_tn"], F32, "in_proj")
    proj = proj.reshape(bsz, tt, -1)

    v, g, bonus, at, rt, kt, bt, dec = _features(proj, col, shift_rkv, shift_lora, rwkv_w0, rwkv_w_up, rwkv_a0,
                                                 rwkv_a_up, rwkv_g_up, rwkv_k_k, rwkv_k_a, rwkv_r_k.reshape(-1),
                                                 t, tiles["feat_tb"])
    yf, yb = _scan(v, at, rt, kt, bt, dec, t, tiles["scan_lanes"])
    o_rwkv = _rwkv_out(yf, yb, bonus, g, rwkv_gn_g, rwkv_gn_b, t, tiles["feat_tb"])

    cos2, sin2 = _rope_tables(t)
    o_att = _attention(proj, col, attn_sink, cos2, sin2, t)

    ym = _merge(o_rwkv, o_att, w_rwkv_o.astype(BF16), w_att_o.astype(BF16), proj, col,
                tiles["merge_tm"], tiles["merge_tn"])
    yo = _mm_astat(ym.reshape(bsz * t, d), w_out.astype(BF16), tiles["out_tm"], tiles["out_tn"], F32, "out_proj")
    x1, h2 = _res_ln(x, yo.reshape(bsz, t, d), mod, 2, ln1_g, ln1_b, tiles["ln_tb"], mod_cols=(3, 4))

    u = _ffn_up(h2.reshape(bsz * t, d), w_ff_gate.astype(BF16), w_ff_up.astype(BF16),
                tiles["ffn_tm"], tiles["ffn_tf"])
    ff = _mm_astat(u, w_ff_down.astype(BF16), tiles["down_tm"], tiles["down_tn"], F32, "swiglu_down")
    (out,) = _res_ln(x1, ff.reshape(bsz, t, d), mod, 5, ln2_g, ln2_b, tiles["ln_tb"])
    return out


_TILES = dict(proj_tm=512, proj_tn=1024, feat_tb=128, scan_lanes=1024, merge_tm=1024, merge_tn=512, out_tm=1024, out_tn=1024,
              ln_tb=256, ffn_tm=2048, ffn_tf=256, down_tm=512, down_tn=512)


def kernel(x, c, ctx, c_ctx, w_ada, b_ada, w_in, rwkv_shift, rwkv_w0, rwkv_w_up, rwkv_a0, rwkv_a_up, rwkv_g_up, rwkv_k_k, rwkv_k_a, rwkv_r_k, rwkv_gn_g, rwkv_gn_b, attn_sink, w_rwkv_o, w_att_o, w_out, ln1_g, ln1_b, w_ff_gate, w_ff_up, w_ff_down, ln2_g, ln2_b):
    assert w_ada.shape[0] == DEPTH
    return _block(x, c, ctx, c_ctx, w_ada[0], b_ada[0], w_in[0], rwkv_shift[0], rwkv_w0[0], rwkv_w_up[0],
                  rwkv_a0[0], rwkv_a_up[0], rwkv_g_up[0], rwkv_k_k[0], rwkv_k_a[0], rwkv_r_k[0], rwkv_gn_g[0],
                  rwkv_gn_b[0], attn_sink[0], w_rwkv_o[0], w_att_o[0], w_out[0], ln1_g[0], ln1_b[0],
                  w_ff_gate[0], w_ff_up[0], w_ff_down[0], ln2_g[0], ln2_b[0], _TILES)
```

```python
import functools

import jax
import jax.numpy as jnp
import numpy as np
from jax import lax
from jax.experimental import pallas as pl
from jax.experimental.pallas import tpu as pltpu

F32 = jnp.float32
BF16 = jnp.bfloat16
HIGHEST = lax.Precision.HIGHEST

RWKV_HEAD = 64
DECAY_LORA = 96
ICLR_LORA = 96
GATE_LORA = 256
GN_EPS = 64e-5
ATT_HEAD = 128
ATT_KV_HEADS = 4
ATT_GROUPS = 4
GRID_W = 64
ROPE_BASE = 10000.0
LN_EPS = 1e-5
DEPTH = 1
DEEPNORM_ALPHA = (2 * DEPTH) ** 0.25
NEG_INF = -1e30

LANES = 128
SUBLANES = 8
VMEM_LIMIT = 56 * 1024 * 1024
MOD_ROWS = 8
SCAN_CHUNK = 64
SCAN_LANES = 256
SCAN_PLANES = 5
LORA_PAD = 1024


def _cparams(sem):
    return pltpu.CompilerParams(dimension_semantics=sem, vmem_limit_bytes=VMEM_LIMIT)


def _dot(a, b, precision=None):
    return jnp.dot(a, b, preferred_element_type=F32, precision=precision)


def _dot_nt(a, b, precision=None):
    return lax.dot_general(a, b, (((1,), (1,)), ((), ())), preferred_element_type=F32, precision=precision)


def _dot_tn(a, b, precision=None):
    return lax.dot_general(a, b, (((0,), (0,)), ((), ())), preferred_element_type=F32, precision=precision)


def _split2(x):
    hi = x.astype(BF16)
    return hi, (x - hi.astype(F32)).astype(BF16)


def _head_sum(x, ones_bd):
    rows = x.shape[0]
    hi, lo = _split2(x)
    both = jnp.concatenate([hi, lo], axis=0)
    out = []
    for s in range(x.shape[1] // SCAN_LANES):
        part = _dot(both[:, s * SCAN_LANES:(s + 1) * SCAN_LANES], ones_bd)
        out.append(part[:rows] + part[rows:])
    return jnp.concatenate(out, axis=1)


def _ada_kernel(c_ref, w_ref, b_ref, o_ref):
    a = c_ref[...]
    a = a * jax.nn.sigmoid(a)
    o_ref[...] = _dot(a, w_ref[...], HIGHEST) + b_ref[...]


def _ada(cc, w_ada, b_ada):
    d, n = w_ada.shape
    tn = 512
    return pl.pallas_call(
        _ada_kernel,
        grid=(n // tn,),
        in_specs=[pl.BlockSpec((MOD_ROWS, d), lambda j: (0, 0)),
                  pl.BlockSpec((d, tn), lambda j: (0, j)),
                  pl.BlockSpec((1, tn), lambda j: (0, j))],
        out_specs=pl.BlockSpec((MOD_ROWS, tn), lambda j: (0, j)),
        out_shape=jax.ShapeDtypeStruct((MOD_ROWS, n), F32),
        compiler_params=_cparams(("arbitrary",)),
        name="ada",
    )(cc, w_ada, b_ada.reshape(1, n))


def _prep_kernel(x_ref, ctx_ref, sh_ref, sc_ref, o_ref, *, n_lat, ctx_row):
    b = pl.program_id(0)
    j = pl.program_id(1)
    is_ctx = j >= n_lat
    row = jnp.where(is_ctx, ctx_row, b)
    sh = sh_ref[pl.ds(row, 1), :]
    sc = sc_ref[pl.ds(row, 1), :]
    xin = jnp.where(is_ctx, ctx_ref[0], x_ref[0])
    o_ref[0] = (xin * (1.0 + sc) + sh).astype(BF16)


def _prep(x, ctx, mod):
    bsz, t, d = x.shape
    l = ctx.shape[1]
    tb = l
    n_lat = t // tb
    return pl.pallas_call(
        functools.partial(_prep_kernel, n_lat=n_lat, ctx_row=bsz),
        grid=(bsz, n_lat + 1),
        in_specs=[pl.BlockSpec((1, tb, d), lambda b, j: (b, jnp.minimum(j, n_lat - 1), 0)),
                  pl.BlockSpec((1, l, d), lambda b, j: (b, 0, 0)),
                  pl.BlockSpec((MOD_ROWS, d), lambda b, j: (0, 0)),
                  pl.BlockSpec((MOD_ROWS, d), lambda b, j: (0, 1))],
        out_specs=pl.BlockSpec((1, tb, d), lambda b, j: (b, j, 0)),
        out_shape=jax.ShapeDtypeStruct((bsz, t + l, d), BF16),
        compiler_params=_cparams(("arbitrary", "arbitrary")),
        name="prep",
    )(x, ctx, mod, mod)


def _mm_kernel(a_ref, w_ref, o_ref):
    o_ref[...] = _dot(a_ref[...], w_ref[...]).astype(o_ref.dtype)


def _mm_wstat(a, w, tm, tn, out_dtype, name):
    m, k = a.shape
    n = w.shape[1]
    return pl.pallas_call(
        _mm_kernel,
        grid=(n // tn, m // tm),
        in_specs=[pl.BlockSpec((tm, k), lambda j, i: (i, 0)),
                  pl.BlockSpec((k, tn), lambda j, i: (0, j))],
        out_specs=pl.BlockSpec((tm, tn), lambda j, i: (i, j)),
        out_shape=jax.ShapeDtypeStruct((m, n), out_dtype),
        compiler_params=_cparams(("arbitrary", "arbitrary")),
        name=name,
    )(a, w)


def _mm_astat(a, w, tm, tn, out_dtype, name):
    m, k = a.shape
    n = w.shape[1]
    return pl.pallas_call(
        _mm_kernel,
        grid=(m // tm, n // tn),
        in_specs=[pl.BlockSpec((tm, k), lambda i, j: (i, 0)),
                  pl.BlockSpec((k, tn), lambda i, j: (0, j))],
        out_specs=pl.BlockSpec((tm, tn), lambda i, j: (i, j)),
        out_shape=jax.ShapeDtypeStruct((m, n), out_dtype),
        compiler_params=_cparams(("arbitrary", "arbitrary")),
        name=name,
    )(a, w)


def _mm_acc_kernel(a_ref, w_ref, o_ref, acc_ref):
    kk = pl.program_id(2)

    @pl.when(kk == 0)
    def _():
        acc_ref[...] = jnp.zeros_like(acc_ref)

    acc_ref[...] += _dot(a_ref[...], w_ref[...])

    @pl.when(kk == pl.num_programs(2) - 1)
    def _():
        o_ref[...] = acc_ref[...].astype(o_ref.dtype)


def _mm_ktiled(a, w, tm, tn, tk, out_dtype, name):
    m, k = a.shape
    n = w.shape[1]
    return pl.pallas_call(
        _mm_acc_kernel,
        grid=(m // tm, n // tn, k // tk),
        in_specs=[pl.BlockSpec((tm, tk), lambda i, j, q: (i, q)),
                  pl.BlockSpec((tk, tn), lambda i, j, q: (q, j))],
        out_specs=pl.BlockSpec((tm, tn), lambda i, j, q: (i, j)),
        out_shape=jax.ShapeDtypeStruct((m, n), out_dtype),
        scratch_shapes=[pltpu.VMEM((tm, tn), F32)],
        compiler_params=_cparams(("arbitrary", "arbitrary", "arbitrary")),
        name=name,
    )(a, w)


def _conv3(prev_ref, cur_ref, next_ref, w_ref, at_start, at_end):
    cur = cur_ref[0]
    rows = cur.shape[0]
    prev_row = jnp.where(at_start, 0.0, prev_ref[0][SUBLANES - 1:SUBLANES, :])
    next_row = jnp.where(at_end, 0.0, next_ref[0][0:1, :])
    up = jnp.concatenate([prev_row, cur[:rows - 1]], axis=0)
    down = jnp.concatenate([cur[1:], next_row], axis=0)
    w = w_ref[...]
    return up * w[0:1] + cur * w[1:2] + down * w[2:3]


def _feat_kernel(rp, rc, rn, kp, kc, kn, vp, vc, vn, lp, lc, ln_,
                 shr, shk, shv, shl, w0, wup, a0, aup, gup, kk_w, ka_w, rk_w, ones_ref, cm_ref,
                 g_o, bonus_o, ops_o, dec_o, *, n_lat, n_tot, chunk):
    j = pl.program_id(1)
    at_start = jnp.logical_or(j == 0, j == n_lat)
    at_end = jnp.logical_or(j == n_lat - 1, j == n_tot - 1)
    r = _conv3(rp, rc, rn, shr, at_start, at_end)
    k = _conv3(kp, kc, kn, shk, at_start, at_end)
    v = _conv3(vp, vc, vn, shv, at_start, at_end)
    lo = _conv3(lp, lc, ln_, shl, at_start, at_end)
    rows = r.shape[0]
    v_h = v.astype(BF16)
    gd = lo[:, 2 * DECAY_LORA + 2 * ICLR_LORA:2 * DECAY_LORA + 2 * ICLR_LORA + GATE_LORA]
    ones_bd = ones_ref[...]
    g_o[0] = _dot(jax.nn.sigmoid(gd).astype(BF16), gup[...])
    kk = k * kk_w[...]
    kk = kk / jnp.maximum(jnp.sqrt(_head_sum(kk * kk, ones_bd)), 1e-12)
    k_sum = jnp.zeros_like(k)
    for d in range(2):
        wd = lo[:, d * DECAY_LORA:(d + 1) * DECAY_LORA]
        ad = lo[:, 2 * DECAY_LORA + d * ICLR_LORA:2 * DECAY_LORA + (d + 1) * ICLR_LORA]
        z = w0[d:d + 1, :] + _dot(jnp.tanh(wd).astype(BF16), wup[d])
        lw = jax.nn.sigmoid(z) * (-np.exp(-0.5))
        iclr = jax.nn.sigmoid(a0[d:d + 1, :] + _dot(ad.astype(BF16), aup[d]))
        k_dir = k * (1.0 + (iclr - 1.0) * ka_w[...])
        k_sum = k_sum + k_dir
        lw_hi, lw_lo = _split2(lw)
        cum = _dot(cm_ref[d], lw_hi) + _dot(cm_ref[d], lw_lo)
        p_inv = jnp.exp(-cum)
        planes = [(kk * jnp.exp(cum - lw)).astype(BF16), (r * jnp.exp(cum)).astype(BF16),
                  (k_dir * p_inv).astype(BF16), (kk * iclr * p_inv).astype(BF16), v_h]
        for hg in range(r.shape[1] // SCAN_LANES):
            for p, plane in enumerate(planes):
                dst = (hg * SCAN_PLANES + p) * SCAN_LANES
                ops_o[d, 0, :, dst:dst + SCAN_LANES] = plane[:, hg * SCAN_LANES:(hg + 1) * SCAN_LANES]
        for q in range(rows // chunk):
            last = (q + 1) * chunk - 1 if d == 0 else q * chunk
            dec_o[d, 0, q] = jnp.exp(cum[last:last + 1, :])
    bonus_o[0] = _head_sum(r * k_sum * rk_w[...], ones_bd) * v


def _features(proj, col, shift_rkv, shift_lora, w0, w_up, a0, a_up, g_up, k_k, k_a, r_k, t_lat, tb):
    bsz, tt, _ = proj.shape
    c = w0.shape[1]
    chunk = SCAN_CHUNK
    n_tot = tt // tb
    n_lat = t_lat // tb
    hb = tb // SUBLANES
    n_h = tt // SUBLANES
    li = np.arange(SCAN_LANES) // RWKV_HEAD
    ones_bd = jnp.asarray(li[:, None] == li[None, :], BF16)
    ti = np.arange(tb)
    same = (ti[:, None] // chunk) == (ti[None, :] // chunk)
    cmask = jnp.asarray(np.stack([same & (ti[None, :] <= ti[:, None]), same & (ti[None, :] >= ti[:, None])]), BF16)

    def main(cb, width):
        blk = cb * LANES // width
        return pl.BlockSpec((1, tb, width), lambda b, j: (b, j, blk))

    def prev(cb, width):
        blk = cb * LANES // width
        return pl.BlockSpec((1, SUBLANES, width), lambda b, j: (b, jnp.maximum(j * hb - 1, 0), blk))

    def nxt(cb, width):
        blk = cb * LANES // width
        return pl.BlockSpec((1, SUBLANES, width), lambda b, j: (b, jnp.minimum((j + 1) * hb, n_h - 1), blk))

    def const(shape):
        nd = len(shape)
        return pl.BlockSpec(shape, lambda b, j: (0,) * nd)

    in_specs = []
    for name, width in (("r", c), ("k", c), ("v", c), ("lora", LORA_PAD)):
        in_specs += [prev(col[name], width), main(col[name], width), nxt(col[name], width)]
    in_specs += [const((3, c)), const((3, c)), const((3, c)), const((3, LORA_PAD)),
                 const((2, c)), const((2, DECAY_LORA, c)), const((2, c)), const((2, ICLR_LORA, c)),
                 const((GATE_LORA, c)), const((1, c)), const((1, c)), const((1, c)),
                 const((SCAN_LANES, SCAN_LANES)), const((2, tb, tb))]
    one = pl.BlockSpec((1, tb, c), lambda b, j: (b, j, 0))
    ops = pl.BlockSpec((2, 1, tb, SCAN_PLANES * c), lambda b, j: (0, b, j, 0))
    dec = pl.BlockSpec((2, 1, tb // chunk, 1, c), lambda b, j: (0, b, j, 0, 0))
    s1 = jax.ShapeDtypeStruct((bsz, tt, c), F32)
    sops = jax.ShapeDtypeStruct((2, bsz, tt, SCAN_PLANES * c), BF16)
    sdec = jax.ShapeDtypeStruct((2, bsz, tt // chunk, 1, c), F32)
    args = [proj] * 12 + [shift_rkv[:, :c], shift_rkv[:, c:2 * c], shift_rkv[:, 2 * c:], shift_lora,
                          w0, w_up.astype(BF16), a0, a_up.astype(BF16), g_up.astype(BF16),
                          k_k.reshape(1, c), k_a.reshape(1, c), r_k.reshape(1, c), ones_bd, cmask]
    return pl.pallas_call(
        functools.partial(_feat_kernel, n_lat=n_lat, n_tot=n_tot, chunk=chunk),
        grid=(bsz, n_tot),
        in_specs=in_specs,
        out_specs=[one, one, ops, dec],
        out_shape=[s1, s1, sops, sdec],
        compiler_params=_cparams(("arbitrary", "arbitrary")),
        name="rwkv_features",
    )(*args)


def _scan_kernel(gm_ref, bdm_ref, hm_ref, eye_ref, *refs, chunk, nsub):
    ops_refs = refs[0:4:2]
    dec_refs = refs[1:4:2]
    y_refs = refs[4:6]
    s_ref = refs[6]
    heads = SCAN_LANES // RWKV_HEAD

    @pl.when(pl.program_id(2) == 0)
    def _():
        s_ref[...] = jnp.zeros_like(s_ref)

    bdm = bdm_ref[...]
    bdm_h = bdm.astype(BF16)
    eye = eye_ref[...]
    hms = [hm_ref[h] for h in range(heads)]

    def rows_bd(m):
        return jnp.concatenate([m * hms[h] for h in range(heads)], axis=0)

    def blocks_bd(m):
        return jnp.concatenate([m] * heads, axis=0) * bdm_h

    steps = chunk.bit_length() - 2
    units = [(d, q) for d in range(2) for q in range(nsub)]
    n_u = len(units)

    def lanes(q):
        return slice(q * SCAN_LANES, (q + 1) * SCAN_LANES)

    def plane(d, q, p):
        return ops_refs[d][0, 0, :, lanes(q * SCAN_PLANES + p)]

    kt = [plane(d, q, 2) for d, q in units]
    bt = [plane(d, q, 3) for d, q in units]
    v = [plane(d, q, 4) for d, q in units]
    x = [jnp.concatenate([plane(d, q, 0), plane(d, q, 1)], axis=0) for d, q in units]
    g = [_dot_nt(x[i], jnp.concatenate([rows_bd(kt[i]), rows_bd(bt[i])], axis=0)) * gm_ref[units[i][0]]
         for i in range(n_u)]
    s_old = [s_ref[i] for i in range(n_u)]
    xs = [_dot_nt(x[i], s_old[i].astype(BF16)) for i in range(n_u)]
    gv = [_dot(g[i][:, :SCAN_LANES].astype(BF16), rows_bd(v[i])) for i in range(n_u)]
    rhs = [xs[i][:chunk] + gv[i][:chunk] for i in range(n_u)]
    t = [eye - g[i][:chunk, SCAN_LANES:] for i in range(n_u)]
    lm = [g[i][:chunk, SCAN_LANES:].astype(BF16) for i in range(n_u)]
    pw = [_dot(lm[i], blocks_bd(lm[i])).astype(BF16) for i in range(n_u)]
    for _ in range(steps - 1):
        both = [_dot(jnp.concatenate([t[i].astype(BF16), pw[i]], axis=0), blocks_bd(pw[i])) for i in range(n_u)]
        t = [t[i] + both[i][:chunk] for i in range(n_u)]
        pw = [both[i][chunk:].astype(BF16) for i in range(n_u)]
    t = [t[i] + _dot(t[i].astype(BF16), blocks_bd(pw[i])) for i in range(n_u)]
    u = [(-_dot(t[i].astype(BF16), rows_bd(rhs[i].astype(BF16)))).astype(BF16) for i in range(n_u)]
    for i, (d, q) in enumerate(units):
        y_refs[d][0, :, lanes(q)] = (xs[i][chunk:] + gv[i][chunk:]
                                     + _dot(g[i][chunk:, SCAN_LANES:].astype(BF16), rows_bd(u[i])))
    for i, (d, q) in enumerate(units):
        upd = _dot_tn(jnp.concatenate([v[i], u[i]], axis=0), jnp.concatenate([kt[i], bt[i]], axis=0))
        s_ref[i] = (s_old[i] + upd) * dec_refs[d][0, 0, 0, :, lanes(q)] * bdm


def _scan(ops, dec, t_lat, lane_block):
    _, bsz, tt, c = ops.shape
    c //= SCAN_PLANES
    chunk = SCAN_CHUNK
    heads = SCAN_LANES // RWKV_HEAD
    n_tot = tt // chunk
    n_lat = t_lat // chunk
    n_ctx = n_tot - n_lat
    nsub = lane_block // SCAN_LANES

    ti = np.arange(chunk)
    before = [ti[None, :] < ti[:, None], ti[None, :] > ti[:, None]]
    gmask = np.stack([np.concatenate([np.tile(before[d], (1, 2 * heads)),
                                      np.tile(before[d] | np.eye(chunk, dtype=bool), (1, 2 * heads))], axis=0)
                      for d in range(2)]).astype(np.float32)
    li = np.arange(SCAN_LANES) // RWKV_HEAD
    bdm = (li[:, None] == li[None, :]).astype(np.float32)
    hmask = jnp.asarray((li[None, None, :] == np.arange(heads)[:, None, None]), BF16)
    eye = np.tile(np.eye(chunk, dtype=np.float32), (1, heads))

    def chunk_index(d, s):
        return jnp.where(s < n_ctx, n_lat + s, s - n_ctx) if d == 0 else n_tot - 1 - s

    def const(shape):
        nd = len(shape)
        return pl.BlockSpec(shape, lambda b, h, s: (0,) * nd)

    in_specs = [const(gmask.shape), const(bdm.shape), const(hmask.shape), const(eye.shape)]
    args = [jnp.asarray(gmask), jnp.asarray(bdm), hmask, jnp.asarray(eye)]
    out_specs = []
    for d in range(2):
        in_specs += [pl.BlockSpec((1, 1, chunk, SCAN_PLANES * lane_block),
                                  lambda b, h, s, d=d: (d, b, chunk_index(d, s), h)),
                     pl.BlockSpec((1, 1, 1, 1, lane_block), lambda b, h, s, d=d: (d, b, chunk_index(d, s), 0, h))]
        args += [ops, dec]
        out_specs.append(pl.BlockSpec((1, chunk, lane_block), lambda b, h, s, d=d: (b, chunk_index(d, s), h)))
    ys = jax.ShapeDtypeStruct((bsz, tt, c), F32)
    return pl.pallas_call(
        functools.partial(_scan_kernel, chunk=chunk, nsub=nsub),
        grid=(bsz, c // lane_block, n_tot),
        in_specs=in_specs,
        out_specs=out_specs,
        out_shape=[ys, ys],
        scratch_shapes=[pltpu.VMEM((2 * nsub, SCAN_LANES, SCAN_LANES), F32)],
        compiler_params=_cparams(("arbitrary", "arbitrary", "arbitrary")),
        name="rwkv_scan",
    )(*args)


def _rout_kernel(yf, yb, bonus, g, gng, gnb, ones_ref, o_ref):
    ones_bd = ones_ref[...]
    inv_n = 1.0 / RWKV_HEAD
    y = yf[0] + yb[0]
    mu = _head_sum(y, ones_bd) * inv_n
    yc = y - mu
    var = _head_sum(yc * yc, ones_bd) * inv_n
    yn = yc * lax.rsqrt(var + GN_EPS) * gng[...] + gnb[...]
    o_ref[0] = ((yn + bonus[0]) * g[0]).astype(o_ref.dtype)


def _rwkv_out(yf, yb, bonus, g, gn_g, gn_b, t_lat, tb):
    bsz, _, c = g.shape
    li = np.arange(SCAN_LANES) // RWKV_HEAD
    ones_bd = jnp.asarray(li[:, None] == li[None, :], BF16)
    one = pl.BlockSpec((1, tb, c), lambda b, j: (b, j, 0))

    def const(shape):
        return pl.BlockSpec(shape, lambda b, j: (0, 0))

    return pl.pallas_call(
        _rout_kernel,
        grid=(bsz, t_lat // tb),
        in_specs=[one, one, one, one, const((1, c)), const((1, c)), const((SCAN_LANES, SCAN_LANES))],
        out_specs=one,
        out_shape=jax.ShapeDtypeStruct((bsz, t_lat, c), BF16),
        compiler_params=_cparams(("arbitrary", "arbitrary")),
        name="rwkv_out",
    )(yf, yb, bonus, g, gn_g.reshape(1, c), gn_b.reshape(1, c), ones_bd)


def _rope(x, c, s):
    return x * c + pltpu.roll(x, ATT_HEAD // 2, 1) * s


def _attn_kernel(sink_ref, q_ref, km_ref, k0_ref, kp_ref, vm_ref, v0_ref, vp_ref, kc_ref, vc_ref,
                 c0_ref, s0_ref, cm_ref, sm_ref, cp_ref, sp_ref, o_ref, *, n_blk):
    n = pl.program_id(1)
    blk = ATT_HEAD
    scale = ATT_HEAD ** -0.5
    kvh = range(ATT_KV_HEADS)
    c0, s0 = c0_ref[...], s0_ref[...]
    cm, sm = cm_ref[...], sm_ref[...]
    cp, sp = cp_ref[...], sp_ref[...]

    def head(ref, i):
        return ref[0, :, i * ATT_HEAD:(i + 1) * ATT_HEAD]

    qq = [jnp.concatenate([_rope(head(q_ref, g * ATT_GROUPS + h), c0, s0) for h in range(ATT_GROUPS)],
                          axis=0).astype(BF16) for g in kvh]
    kw = [jnp.concatenate([_rope(head(km_ref, g), cm, sm), _rope(head(k0_ref, g), c0, s0),
                           _rope(head(kp_ref, g), cp, sp)], axis=0).astype(BF16) for g in kvh]
    vw = [jnp.concatenate([head(vm_ref, g), head(v0_ref, g), head(vp_ref, g)], axis=0).astype(BF16) for g in kvh]
    kc = [head(kc_ref, g).astype(BF16) for g in kvh]
    vc = [head(vc_ref, g).astype(BF16) for g in kvh]
    ii = lax.broadcasted_iota(jnp.int32, (ATT_GROUPS * blk, 3 * blk), 0) % blk
    jj = lax.broadcasted_iota(jnp.int32, (ATT_GROUPS * blk, 3 * blk), 1)
    in_prev = jnp.logical_and(jnp.logical_and(jj < blk, jj >= ii), n > 0)
    in_self = jnp.logical_and(jj >= blk, jj < 2 * blk)
    in_next = jnp.logical_and(jnp.logical_and(jj >= 2 * blk, jj - 2 * blk <= ii), n < n_blk - 1)
    valid = jnp.logical_or(jnp.logical_or(in_prev, in_self), in_next)
    s_w = [jnp.where(valid, _dot_nt(qq[g], kw[g]) * scale, NEG_INF) for g in kvh]
    s_c = [_dot_nt(qq[g], kc[g]) * scale for g in kvh]
    s_s = [jnp.concatenate([jnp.full((blk, 1), sink_ref[g * ATT_GROUPS + h], F32) for h in range(ATT_GROUPS)],
                           axis=0) for g in kvh]
    m = [jnp.maximum(jnp.maximum(jnp.max(s_w[g], axis=-1, keepdims=True),
                                 jnp.max(s_c[g], axis=-1, keepdims=True)), s_s[g]) for g in kvh]
    p_w = [jnp.exp(s_w[g] - m[g]) for g in kvh]
    p_c = [jnp.exp(s_c[g] - m[g]) for g in kvh]
    den = [jnp.sum(p_w[g], axis=-1, keepdims=True) + jnp.sum(p_c[g], axis=-1, keepdims=True)
           + jnp.exp(s_s[g] - m[g]) for g in kvh]
    o = [(_dot(p_w[g].astype(BF16), vw[g]) + _dot(p_c[g].astype(BF16), vc[g])) / den[g] for g in kvh]
    o_ref[0] = jnp.concatenate([o[g][h * blk:(h + 1) * blk] for g in kvh for h in range(ATT_GROUPS)],
                               axis=1).astype(o_ref.dtype)


def _attention(proj, col, sink, cos2, sin2, t_lat):
    bsz, tt, _ = proj.shape
    blk = ATT_HEAD
    n_blk = t_lat // blk
    l = tt - t_lat
    qw = ATT_KV_HEADS * ATT_GROUPS * ATT_HEAD
    kw = ATT_KV_HEADS * ATT_HEAD
    q_blk = col["q"] * LANES // qw

    def kv(name, off):
        cb = col[name] * LANES // kw
        return pl.BlockSpec((1, blk, kw), lambda b, n: (b, jnp.clip(n + off, 0, n_blk - 1), cb))

    def ctx(name):
        cb = col[name] * LANES // kw
        return pl.BlockSpec((1, l, kw), lambda b, n: (b, t_lat // l, cb))

    def tab(off):
        return pl.BlockSpec((blk, ATT_HEAD), lambda b, n: (jnp.clip(n + off, 0, n_blk - 1), 0))

    in_specs = [pl.BlockSpec(memory_space=pltpu.SMEM),
                pl.BlockSpec((1, blk, qw), lambda b, n: (b, n, q_blk)),
                kv("ak", -1), kv("ak", 0), kv("ak", 1), kv("av", -1), kv("av", 0), kv("av", 1),
                ctx("ak"), ctx("av"), tab(0), tab(0), tab(-1), tab(-1), tab(1), tab(1)]
    return pl.pallas_call(
        functools.partial(_attn_kernel, n_blk=n_blk),
        grid=(bsz, n_blk),
        in_specs=in_specs,
        out_specs=pl.BlockSpec((1, blk, qw), lambda b, n: (b, n, 0)),
        out_shape=jax.ShapeDtypeStruct((bsz, t_lat, qw), BF16),
        compiler_params=_cparams(("arbitrary", "arbitrary")),
        name="window_attention",
    )(sink, proj, proj, proj, proj, proj, proj, proj, proj, proj, cos2, sin2, cos2, sin2, cos2, sin2)


def _merge_kernel(o1_ref, o2_ref, w1_ref, w2_ref, g1_ref, g2_ref, o_ref):
    y1 = _dot(o1_ref[0], w1_ref[...])
    y2 = _dot(o2_ref[0], w2_ref[...])
    o_ref[0] = (jax.nn.sigmoid(g1_ref[0]) * y1 + jax.nn.sigmoid(g2_ref[0]) * y2).astype(o_ref.dtype)


def _merge(o_rwkv, o_att, w1, w2, proj, col, tm, tn):
    bsz, t, c = o_rwkv.shape
    n = w1.shape[1]
    gr = col["gate_r"] * LANES // tn
    ga = col["gate_a"] * LANES // tn
    return pl.pallas_call(
        _merge_kernel,
        grid=(bsz, t // tm, n // tn),
        in_specs=[pl.BlockSpec((1, tm, c), lambda b, i, j: (b, i, 0)),
                  pl.BlockSpec((1, tm, c), lambda b, i, j: (b, i, 0)),
                  pl.BlockSpec((c, tn), lambda b, i, j: (0, j)),
                  pl.BlockSpec((c, tn), lambda b, i, j: (0, j)),
                  pl.BlockSpec((1, tm, tn), lambda b, i, j: (b, i, gr + j)),
                  pl.BlockSpec((1, tm, tn), lambda b, i, j: (b, i, ga + j))],
        out_specs=pl.BlockSpec((1, tm, tn), lambda b, i, j: (b, i, j)),
        out_shape=jax.ShapeDtypeStruct((bsz, t, n), BF16),
        compiler_params=_cparams(("arbitrary", "arbitrary", "arbitrary")),
        name="gated_merge",
    )(o_rwkv, o_att, w1, w2, proj, proj)


def _ln_kernel(x_ref, y_ref, gt_ref, g_ref, b_ref, *rest, with_mod):
    bi = pl.program_id(0)
    gt = gt_ref[pl.ds(bi, 1), :]
    z = DEEPNORM_ALPHA * x_ref[0] + gt * y_ref[0]
    mu = jnp.mean(z, axis=-1, keepdims=True)
    zc = z - mu
    var = jnp.mean(zc * zc, axis=-1, keepdims=True)
    out = zc * lax.rsqrt(var + LN_EPS) * g_ref[...] + b_ref[...]
    if with_mod:
        sh_ref, sc_ref, o_ref, h_ref = rest
        o_ref[0] = out
        h_ref[0] = (out * (1.0 + sc_ref[pl.ds(bi, 1), :]) + sh_ref[pl.ds(bi, 1), :]).astype(BF16)
    else:
        (o_ref,) = rest
        o_ref[0] = out


def _res_ln(x, y, mod, gate_col, g, b, tb, mod_cols=None):
    bsz, t, d = x.shape
    blk = pl.BlockSpec((1, tb, d), lambda bi, j: (bi, j, 0))
    vec = pl.BlockSpec((1, d), lambda bi, j: (0, 0))

    def modspec(cb):
        return pl.BlockSpec((MOD_ROWS, d), lambda bi, j: (0, cb))

    in_specs = [blk, blk, modspec(gate_col), vec, vec]
    args = [x, y, mod, g.reshape(1, d), b.reshape(1, d)]
    out_specs = [blk]
    out_shape = [jax.ShapeDtypeStruct((bsz, t, d), F32)]
    if mod_cols is not None:
        in_specs += [modspec(mod_cols[0]), modspec(mod_cols[1])]
        args += [mod, mod]
        out_specs.append(blk)
        out_shape.append(jax.ShapeDtypeStruct((bsz, t, d), BF16))
    return pl.pallas_call(
        functools.partial(_ln_kernel, with_mod=mod_cols is not None),
        grid=(bsz, t // tb),
        in_specs=in_specs,
        out_specs=out_specs,
        out_shape=out_shape,
        compiler_params=_cparams(("arbitrary", "arbitrary")),
        name="residual_layernorm",
    )(*args)


def _ffn_up_kernel(a_ref, wg_ref, wu_ref, o_ref):
    a = a_ref[...]
    gg = _dot(a, wg_ref[...])
    uu = _dot(a, wu_ref[...])
    o_ref[...] = (gg * jax.nn.sigmoid(gg) * uu).astype(o_ref.dtype)


def _ffn_up(a, wg, wu, tm, tf):
    m, k = a.shape
    f = wg.shape[1]
    return pl.pallas_call(
        _ffn_up_kernel,
        grid=(m // tm, f // tf),
        in_specs=[pl.BlockSpec((tm, k), lambda i, j: (i, 0)),
                  pl.BlockSpec((k, tf), lambda i, j: (0, j)),
                  pl.BlockSpec((k, tf), lambda i, j: (0, j))],
        out_specs=pl.BlockSpec((tm, tf), lambda i, j: (i, j)),
        out_shape=jax.ShapeDtypeStruct((m, f), BF16),
        compiler_params=_cparams(("arbitrary", "arbitrary")),
        name="swiglu_up",
    )(a, wg, wu)


def _rope_tables(t):
    rows = t // GRID_W
    row = jnp.broadcast_to(jnp.arange(rows, dtype=F32)[:, None], (rows, GRID_W)).reshape(t)
    colp = jnp.broadcast_to(jnp.arange(GRID_W, dtype=F32)[None, :], (rows, GRID_W)).reshape(t)
    axis_dim = ATT_HEAD // 2
    inv = ROPE_BASE ** (-jnp.arange(0, axis_dim, 2, dtype=F32) / axis_dim)
    ang = jnp.concatenate([row[:, None] * inv, colp[:, None] * inv], -1)
    cos, sin = jnp.cos(ang), jnp.sin(ang)
    return jnp.concatenate([cos, cos], -1), jnp.concatenate([-sin, sin], -1)


def _block(x, c, ctx, c_ctx, w_ada, b_ada, w_in, rwkv_shift, rwkv_w0, rwkv_w_up, rwkv_a0, rwkv_a_up,
           rwkv_g_up, rwkv_k_k, rwkv_k_a, rwkv_r_k, rwkv_gn_g, rwkv_gn_b, attn_sink, w_rwkv_o, w_att_o,
           w_out, ln1_g, ln1_b, w_ff_gate, w_ff_up, w_ff_down, ln2_g, ln2_b, tiles):
    bsz, t, d = x.shape
    l = ctx.shape[1]
    tt = t + l
    crw = rwkv_w0.shape[1]
    d_att = ATT_KV_HEADS * ATT_GROUPS * ATT_HEAD
    d_kv = ATT_KV_HEADS * ATT_HEAD
    n_lora = 2 * DECAY_LORA + 2 * ICLR_LORA + GATE_LORA
    n_rw = 3 * crw + n_lora

    cc = jnp.zeros((MOD_ROWS, d), F32).at[:bsz].set(c).at[bsz].set(c_ctx)
    mod = _ada(cc, w_ada, b_ada)

    o_q = n_rw
    o_k = o_q + d_att
    o_v = o_k + d_kv
    o_g = o_v + d_kv
    pad = LORA_PAD - n_lora
    w_perm = jnp.concatenate([w_in[:, o_g:], w_in[:, :3 * crw], w_in[:, o_q:o_g],
                              w_in[:, 3 * crw:n_rw], jnp.zeros((d, pad), F32)], axis=1).astype(BF16)
    col = {}
    off = 0
    for name, width in (("gate_r", d), ("gate_a", d), ("r", crw), ("k", crw), ("v", crw),
                        ("q", d_att), ("ak", d_kv), ("av", d_kv), ("lora", LORA_PAD)):
        col[name] = off // LANES
        off += width
    shift_rkv = rwkv_shift[:, :3 * crw]
    shift_lora = jnp.concatenate([rwkv_shift[:, 3 * crw:], jnp.zeros((3, pad), F32)], axis=1)

    h = _prep(x, ctx, mod)
    proj = _mm_wstat(h.reshape(bsz * tt, d), w_perm, tiles["proj_tm"], tiles["proj_tn"], F32, "in_proj")
    proj = proj.reshape(bsz, tt, -1)

    g, bonus, ops, dec = _features(proj, col, shift_rkv, shift_lora, rwkv_w0, rwkv_w_up, rwkv_a0, rwkv_a_up,
                                   rwkv_g_up, rwkv_k_k, rwkv_k_a, rwkv_r_k.reshape(-1), t, tiles["feat_tb"])
    yf, yb = _scan(ops, dec, t, tiles["scan_lanes"])
    o_rwkv = _rwkv_out(yf, yb, bonus, g, rwkv_gn_g, rwkv_gn_b, t, tiles["feat_tb"])

    cos2, sin2 = _rope_tables(t)
    o_att = _attention(proj, col, attn_sink, cos2, sin2, t)

    ym = _merge(o_rwkv, o_att, w_rwkv_o.astype(BF16), w_att_o.astype(BF16), proj, col,
                tiles["merge_tm"], tiles["merge_tn"])
    yo = _mm_astat(ym.reshape(bsz * t, d), w_out.astype(BF16), tiles["out_tm"], tiles["out_tn"], F32, "out_proj")
    x1, h2 = _res_ln(x, yo.reshape(bsz, t, d), mod, 2, ln1_g, ln1_b, tiles["ln_tb"], mod_cols=(3, 4))

    u = _ffn_up(h2.reshape(bsz * t, d), w_ff_gate.astype(BF16), w_ff_up.astype(BF16),
                tiles["ffn_tm"], tiles["ffn_tf"])
    ff = _mm_astat(u, w_ff_down.astype(BF16), tiles["down_tm"], tiles["down_tn"], F32, "swiglu_down")
    (out,) = _res_ln(x1, ff.reshape(bsz, t, d), mod, 5, ln2_g, ln2_b, tiles["ln_tb"])
    return out


_TILES = dict(proj_tm=512, proj_tn=1024, feat_tb=128, scan_lanes=1024, merge_tm=1024, merge_tn=512, out_tm=1024, out_tn=1024,
              ln_tb=256, ffn_tm=2048, ffn_tf=256, down_tm=512, down_tn=512)


def kernel(x, c, ctx, c_ctx, w_ada, b_ada, w_in, rwkv_shift, rwkv_w0, rwkv_w_up, rwkv_a0, rwkv_a_up, rwkv_g_up, rwkv_k_k, rwkv_k_a, rwkv_r_k, rwkv_gn_g, rwkv_gn_b, attn_sink, w_rwkv_o, w_att_o, w_out, ln1_g, ln1_b, w_ff_gate, w_ff_up, w_ff_down, ln2_g, ln2_b):
    assert w_ada.shape[0] == DEPTH
    return _block(x, c, ctx, c_ctx, w_ada[0], b_ada[0], w_in[0], rwkv_shift[0], rwkv_w0[0], rwkv_w_up[0],
                  rwkv_a0[0], rwkv_a_up[0], rwkv_g_up[0], rwkv_k_k[0], rwkv_k_a[0], rwkv_r_k[0], rwkv_gn_g[0],
                  rwkv_gn_b[0], attn_sink[0], w_rwkv_o[0], w_att_o[0], w_out[0], ln1_g[0], ln1_b[0],
                  w_ff_gate[0], w_ff_up[0], w_ff_down[0], ln2_g[0], ln2_b[0], _TILES)
```

```python
import functools

import jax
import jax.numpy as jnp
import numpy as np
from jax import lax
from jax.experimental import pallas as pl
from jax.experimental.pallas import tpu as pltpu

F32 = jnp.float32
BF16 = jnp.bfloat16
HIGHEST = lax.Precision.HIGHEST

RWKV_HEAD = 64
DECAY_LORA = 96
ICLR_LORA = 96
GATE_LORA = 256
GN_EPS = 64e-5
ATT_HEAD = 128
ATT_KV_HEADS = 4
ATT_GROUPS = 4
GRID_W = 64
ROPE_BASE = 10000.0
LN_EPS = 1e-5
DEPTH = 1
DEEPNORM_ALPHA = (2 * DEPTH) ** 0.25
NEG_INF = -1e30

LANES = 128
SUBLANES = 8
VMEM_LIMIT = 56 * 1024 * 1024
MOD_ROWS = 8
SCAN_CHUNK = 64
SCAN_LANES = 256
SCAN_PLANES = 5
LORA_PAD = 1024


def _cparams(sem):
    return pltpu.CompilerParams(dimension_semantics=sem, vmem_limit_bytes=VMEM_LIMIT)


def _dot(a, b, precision=None):
    return jnp.dot(a, b, preferred_element_type=F32, precision=precision)


def _dot_nt(a, b, precision=None):
    return lax.dot_general(a, b, (((1,), (1,)), ((), ())), preferred_element_type=F32, precision=precision)


def _dot_tn(a, b, precision=None):
    return lax.dot_general(a, b, (((0,), (0,)), ((), ())), preferred_element_type=F32, precision=precision)


def _split2(x):
    hi = x.astype(BF16)
    return hi, (x - hi.astype(F32)).astype(BF16)


def _head_sum(x, ones_bd):
    rows = x.shape[0]
    hi, lo = _split2(x)
    both = jnp.concatenate([hi, lo], axis=0)
    out = []
    for s in range(x.shape[1] // SCAN_LANES):
        part = _dot(both[:, s * SCAN_LANES:(s + 1) * SCAN_LANES], ones_bd)
        out.append(part[:rows] + part[rows:])
    return jnp.concatenate(out, axis=1)


def _ada_kernel(c_ref, w_ref, b_ref, o_ref):
    a = c_ref[...]
    a = a * jax.nn.sigmoid(a)
    a_hi, a_lo = _split2(a)
    w_hi, w_lo = _split2(w_ref[...])
    part = _dot(jnp.concatenate([a_hi, a_lo], axis=0), w_hi)
    o_ref[...] = part[:MOD_ROWS] + part[MOD_ROWS:] + _dot(a_hi, w_lo) + b_ref[...]


def _ada(cc, w_ada, b_ada):
    d, n = w_ada.shape
    tn = 512
    return pl.pallas_call(
        _ada_kernel,
        grid=(n // tn,),
        in_specs=[pl.BlockSpec((MOD_ROWS, d), lambda j: (0, 0)),
                  pl.BlockSpec((d, tn), lambda j: (0, j)),
                  pl.BlockSpec((1, tn), lambda j: (0, j))],
        out_specs=pl.BlockSpec((MOD_ROWS, tn), lambda j: (0, j)),
        out_shape=jax.ShapeDtypeStruct((MOD_ROWS, n), F32),
        compiler_params=_cparams(("arbitrary",)),
        name="ada",
    )(cc, w_ada, b_ada.reshape(1, n))


def _prep_kernel(x_ref, ctx_ref, sh_ref, sc_ref, o_ref, *, n_lat, ctx_row):
    b = pl.program_id(0)
    j = pl.program_id(1)
    is_ctx = j >= n_lat
    row = jnp.where(is_ctx, ctx_row, b)
    sh = sh_ref[pl.ds(row, 1), :]
    sc = sc_ref[pl.ds(row, 1), :]
    xin = jnp.where(is_ctx, ctx_ref[0], x_ref[0])
    o_ref[0] = (xin * (1.0 + sc) + sh).astype(BF16)


def _prep(x, ctx, mod):
    bsz, t, d = x.shape
    l = ctx.shape[1]
    tb = l
    n_lat = t // tb
    return pl.pallas_call(
        functools.partial(_prep_kernel, n_lat=n_lat, ctx_row=bsz),
        grid=(bsz, n_lat + 1),
        in_specs=[pl.BlockSpec((1, tb, d), lambda b, j: (b, jnp.minimum(j, n_lat - 1), 0)),
                  pl.BlockSpec((1, l, d), lambda b, j: (b, 0, 0)),
                  pl.BlockSpec((MOD_ROWS, d), lambda b, j: (0, 0)),
                  pl.BlockSpec((MOD_ROWS, d), lambda b, j: (0, 1))],
        out_specs=pl.BlockSpec((1, tb, d), lambda b, j: (b, j, 0)),
        out_shape=jax.ShapeDtypeStruct((bsz, t + l, d), BF16),
        compiler_params=_cparams(("arbitrary", "arbitrary")),
        name="prep",
    )(x, ctx, mod, mod)


def _mm_kernel(a_ref, w_ref, o_ref):
    o_ref[...] = _dot(a_ref[...], w_ref[...]).astype(o_ref.dtype)


def _mm_wstat(a, w, tm, tn, out_dtype, name):
    m, k = a.shape
    n = w.shape[1]
    return pl.pallas_call(
        _mm_kernel,
        grid=(pl.cdiv(n, tn), m // tm),
        in_specs=[pl.BlockSpec((tm, k), lambda j, i: (i, 0)),
                  pl.BlockSpec((k, tn), lambda j, i: (0, j))],
        out_specs=pl.BlockSpec((tm, tn), lambda j, i: (i, j)),
        out_shape=jax.ShapeDtypeStruct((m, n), out_dtype),
        compiler_params=_cparams(("arbitrary", "arbitrary")),
        name=name,
    )(a, w)


def _mm_rows_kernel(a_ref, w_ref, o_ref):
    o_ref[0] = _dot(a_ref[0], w_ref[...]).astype(o_ref.dtype)


def _mm_lead_rows(a, w, rows, tm, tn, out_dtype, name):
    bsz, _, k = a.shape
    n = w.shape[1]
    return pl.pallas_call(
        _mm_rows_kernel,
        grid=(n // tn, bsz, rows // tm),
        in_specs=[pl.BlockSpec((1, tm, k), lambda j, b, i: (b, i, 0)),
                  pl.BlockSpec((k, tn), lambda j, b, i: (0, j))],
        out_specs=pl.BlockSpec((1, tm, tn), lambda j, b, i: (b, i, j)),
        out_shape=jax.ShapeDtypeStruct((bsz, rows, n), out_dtype),
        compiler_params=_cparams(("arbitrary", "arbitrary", "arbitrary")),
        name=name,
    )(a, w)


def _mm_astat(a, w, tm, tn, out_dtype, name):
    m, k = a.shape
    n = w.shape[1]
    return pl.pallas_call(
        _mm_kernel,
        grid=(m // tm, n // tn),
        in_specs=[pl.BlockSpec((tm, k), lambda i, j: (i, 0)),
                  pl.BlockSpec((k, tn), lambda i, j: (0, j))],
        out_specs=pl.BlockSpec((tm, tn), lambda i, j: (i, j)),
        out_shape=jax.ShapeDtypeStruct((m, n), out_dtype),
        compiler_params=_cparams(("arbitrary", "arbitrary")),
        name=name,
    )(a, w)


def _mm_acc_kernel(a_ref, w_ref, o_ref, acc_ref):
    kk = pl.program_id(2)

    @pl.when(kk == 0)
    def _():
        acc_ref[...] = jnp.zeros_like(acc_ref)

    acc_ref[...] += _dot(a_ref[...], w_ref[...])

    @pl.when(kk == pl.num_programs(2) - 1)
    def _():
        o_ref[...] = acc_ref[...].astype(o_ref.dtype)


def _mm_ktiled(a, w, tm, tn, tk, out_dtype, name):
    m, k = a.shape
    n = w.shape[1]
    return pl.pallas_call(
        _mm_acc_kernel,
        grid=(m // tm, n // tn, k // tk),
        in_specs=[pl.BlockSpec((tm, tk), lambda i, j, q: (i, q)),
                  pl.BlockSpec((tk, tn), lambda i, j, q: (q, j))],
        out_specs=pl.BlockSpec((tm, tn), lambda i, j, q: (i, j)),
        out_shape=jax.ShapeDtypeStruct((m, n), out_dtype),
        scratch_shapes=[pltpu.VMEM((tm, tn), F32)],
        compiler_params=_cparams(("arbitrary", "arbitrary", "arbitrary")),
        name=name,
    )(a, w)


def _conv3(prev_ref, cur_ref, next_ref, w_ref, at_start, at_end):
    cur = cur_ref[0]
    rows = cur.shape[0]
    prev_row = jnp.where(at_start, 0.0, prev_ref[0][SUBLANES - 1:SUBLANES, :])
    next_row = jnp.where(at_end, 0.0, next_ref[0][0:1, :])
    up = jnp.concatenate([prev_row, cur[:rows - 1]], axis=0)
    down = jnp.concatenate([cur[1:], next_row], axis=0)
    w = w_ref[...]
    return up * w[0:1] + cur * w[1:2] + down * w[2:3]


def _feat_kernel(rp, rc, rn, kp, kc, kn, vp, vc, vn, lp, lc, ln_,
                 shr, shk, shv, shl, w0, wup, a0, aup, gup, kk_w, ka_w, rk_w, ones_ref, cm_ref,
                 g_o, bonus_o, ops_o, dec_o, *, n_lat, n_tot, chunk):
    j = pl.program_id(1)
    at_start = jnp.logical_or(j == 0, j == n_lat)
    at_end = jnp.logical_or(j == n_lat - 1, j == n_tot - 1)
    r = _conv3(rp, rc, rn, shr, at_start, at_end)
    k = _conv3(kp, kc, kn, shk, at_start, at_end)
    v = _conv3(vp, vc, vn, shv, at_start, at_end)
    lo = _conv3(lp, lc, ln_, shl, at_start, at_end)
    rows = r.shape[0]
    v_h = v.astype(BF16)
    gd = lo[:, 2 * DECAY_LORA + 2 * ICLR_LORA:2 * DECAY_LORA + 2 * ICLR_LORA + GATE_LORA]
    ones_bd = ones_ref[...]
    g_o[0] = _dot(jax.nn.sigmoid(gd).astype(BF16), gup[...])
    kk = k * kk_w[...]
    kk = kk / jnp.maximum(jnp.sqrt(_head_sum(kk * kk, ones_bd)), 1e-12)
    k_sum = jnp.zeros_like(k)
    for d in range(2):
        wd = lo[:, d * DECAY_LORA:(d + 1) * DECAY_LORA]
        ad = lo[:, 2 * DECAY_LORA + d * ICLR_LORA:2 * DECAY_LORA + (d + 1) * ICLR_LORA]
        z = w0[d:d + 1, :] + _dot(jnp.tanh(wd).astype(BF16), wup[d])
        lw = jax.nn.sigmoid(z) * (-np.exp(-0.5))
        iclr = jax.nn.sigmoid(a0[d:d + 1, :] + _dot(ad.astype(BF16), aup[d]))
        k_dir = k * (1.0 + (iclr - 1.0) * ka_w[...])
        k_sum = k_sum + k_dir
        lw_hi, lw_lo = _split2(lw)
        cum = _dot(cm_ref[d], lw_hi) + _dot(cm_ref[d], lw_lo)
        p_inv = jnp.exp(-cum)
        planes = [(kk * jnp.exp(cum - lw)).astype(BF16), (r * jnp.exp(cum)).astype(BF16),
                  (k_dir * p_inv).astype(BF16), (kk * iclr * p_inv).astype(BF16), v_h]
        for hg in range(r.shape[1] // SCAN_LANES):
            for p, plane in enumerate(planes):
                dst = (hg * SCAN_PLANES + p) * SCAN_LANES
                ops_o[d, 0, :, dst:dst + SCAN_LANES] = plane[:, hg * SCAN_LANES:(hg + 1) * SCAN_LANES]
        for q in range(rows // chunk):
            last = (q + 1) * chunk - 1 if d == 0 else q * chunk
            dec_o[d, 0, q] = jnp.exp(cum[last:last + 1, :])
    bonus_o[0] = _head_sum(r * k_sum * rk_w[...], ones_bd) * v


def _features(proj, col, shift_rkv, shift_lora, w0, w_up, a0, a_up, g_up, k_k, k_a, r_k, t_lat, tb):
    bsz, tt, _ = proj.shape
    c = w0.shape[1]
    chunk = SCAN_CHUNK
    n_tot = tt // tb
    n_lat = t_lat // tb
    hb = tb // SUBLANES
    n_h = tt // SUBLANES
    li = np.arange(SCAN_LANES) // RWKV_HEAD
    ones_bd = jnp.asarray(li[:, None] == li[None, :], BF16)
    ti = np.arange(tb)
    same = (ti[:, None] // chunk) == (ti[None, :] // chunk)
    cmask = jnp.asarray(np.stack([same & (ti[None, :] <= ti[:, None]), same & (ti[None, :] >= ti[:, None])]), BF16)

    def main(cb, width):
        blk = cb * LANES // width
        return pl.BlockSpec((1, tb, width), lambda b, j: (b, j, blk))

    def prev(cb, width):
        blk = cb * LANES // width
        return pl.BlockSpec((1, SUBLANES, width), lambda b, j: (b, jnp.maximum(j * hb - 1, 0), blk))

    def nxt(cb, width):
        blk = cb * LANES // width
        return pl.BlockSpec((1, SUBLANES, width), lambda b, j: (b, jnp.minimum((j + 1) * hb, n_h - 1), blk))

    def const(shape):
        nd = len(shape)
        return pl.BlockSpec(shape, lambda b, j: (0,) * nd)

    in_specs = []
    for name, width in (("r", c), ("k", c), ("v", c), ("lora", LORA_PAD)):
        in_specs += [prev(col[name], width), main(col[name], width), nxt(col[name], width)]
    in_specs += [const((3, c)), const((3, c)), const((3, c)), const((3, LORA_PAD)),
                 const((2, c)), const((2, DECAY_LORA, c)), const((2, c)), const((2, ICLR_LORA, c)),
                 const((GATE_LORA, c)), const((1, c)), const((1, c)), const((1, c)),
                 const((SCAN_LANES, SCAN_LANES)), const((2, tb, tb))]
    one = pl.BlockSpec((1, tb, c), lambda b, j: (b, j, 0))
    ops = pl.BlockSpec((2, 1, tb, SCAN_PLANES * c), lambda b, j: (0, b, j, 0))
    dec = pl.BlockSpec((2, 1, tb // chunk, 1, c), lambda b, j: (0, b, j, 0, 0))
    s1 = jax.ShapeDtypeStruct((bsz, tt, c), F32)
    sops = jax.ShapeDtypeStruct((2, bsz, tt, SCAN_PLANES * c), BF16)
    sdec = jax.ShapeDtypeStruct((2, bsz, tt // chunk, 1, c), F32)
    args = [proj] * 12 + [shift_rkv[:, :c], shift_rkv[:, c:2 * c], shift_rkv[:, 2 * c:], shift_lora,
                          w0, w_up.astype(BF16), a0, a_up.astype(BF16), g_up.astype(BF16),
                          k_k.reshape(1, c), k_a.reshape(1, c), r_k.reshape(1, c), ones_bd, cmask]
    return pl.pallas_call(
        functools.partial(_feat_kernel, n_lat=n_lat, n_tot=n_tot, chunk=chunk),
        grid=(bsz, n_tot),
        in_specs=in_specs,
        out_specs=[one, one, ops, dec],
        out_shape=[s1, s1, sops, sdec],
        compiler_params=_cparams(("arbitrary", "arbitrary")),
        name="rwkv_features",
    )(*args)


def _scan_kernel(gm_ref, bdm_ref, hm_ref, eye_ref, *refs, chunk, nsub):
    ops_refs = refs[0:4:2]
    dec_refs = refs[1:4:2]
    y_refs = refs[4:6]
    s_ref = refs[6]
    heads = SCAN_LANES // RWKV_HEAD

    @pl.when(pl.program_id(2) == 0)
    def _():
        s_ref[...] = jnp.zeros_like(s_ref)

    bdm = bdm_ref[...]
    bdm_h = bdm.astype(BF16)
    eye = eye_ref[...]
    hms = [hm_ref[h] for h in range(heads)]

    def rows_bd(m):
        return jnp.concatenate([m * hms[h] for h in range(heads)], axis=0)

    def blocks_bd(m):
        return jnp.concatenate([m] * heads, axis=0) * bdm_h

    steps = chunk.bit_length() - 2
    units = [(d, q) for d in range(2) for q in range(nsub)]
    n_u = len(units)

    def lanes(q):
        return slice(q * SCAN_LANES, (q + 1) * SCAN_LANES)

    def plane(d, q, p):
        return ops_refs[d][0, 0, :, lanes(q * SCAN_PLANES + p)]

    kt = [plane(d, q, 2) for d, q in units]
    bt = [plane(d, q, 3) for d, q in units]
    v = [plane(d, q, 4) for d, q in units]
    x = [jnp.concatenate([plane(d, q, 0), plane(d, q, 1)], axis=0) for d, q in units]
    g = [_dot_nt(x[i], jnp.concatenate([rows_bd(kt[i]), rows_bd(bt[i])], axis=0)) * gm_ref[units[i][0]]
         for i in range(n_u)]
    s_old = [s_ref[i] for i in range(n_u)]
    xs = [_dot_nt(x[i], s_old[i].astype(BF16)) for i in range(n_u)]
    gv = [_dot(g[i][:, :SCAN_LANES].astype(BF16), rows_bd(v[i])) for i in range(n_u)]
    rhs = [xs[i][:chunk] + gv[i][:chunk] for i in range(n_u)]
    t = [eye - g[i][:chunk, SCAN_LANES:] for i in range(n_u)]
    lm = [g[i][:chunk, SCAN_LANES:].astype(BF16) for i in range(n_u)]
    pw = [_dot(lm[i], blocks_bd(lm[i])).astype(BF16) for i in range(n_u)]
    for _ in range(steps - 1):
        both = [_dot(jnp.concatenate([t[i].astype(BF16), pw[i]], axis=0), blocks_bd(pw[i])) for i in range(n_u)]
        t = [t[i] + both[i][:chunk] for i in range(n_u)]
        pw = [both[i][chunk:].astype(BF16) for i in range(n_u)]
    t = [t[i] + _dot(t[i].astype(BF16), blocks_bd(pw[i])) for i in range(n_u)]
    u = [(-_dot(t[i].astype(BF16), rows_bd(rhs[i].astype(BF16)))).astype(BF16) for i in range(n_u)]
    for i, (d, q) in enumerate(units):
        y_refs[d][0, :, lanes(q)] = (xs[i][chunk:] + gv[i][chunk:]
                                     + _dot(g[i][chunk:, SCAN_LANES:].astype(BF16), rows_bd(u[i])))
    for i, (d, q) in enumerate(units):
        upd = _dot_tn(jnp.concatenate([v[i], u[i]], axis=0), jnp.concatenate([kt[i], bt[i]], axis=0))
        s_ref[i] = (s_old[i] + upd) * dec_refs[d][0, 0, 0, :, lanes(q)] * bdm


def _scan(ops, dec, t_lat, lane_block):
    _, bsz, tt, c = ops.shape
    c //= SCAN_PLANES
    chunk = SCAN_CHUNK
    heads = SCAN_LANES // RWKV_HEAD
    n_tot = tt // chunk
    n_lat = t_lat // chunk
    n_ctx = n_tot - n_lat
    nsub = lane_block // SCAN_LANES

    ti = np.arange(chunk)
    before = [ti[None, :] < ti[:, None], ti[None, :] > ti[:, None]]
    gmask = np.stack([np.concatenate([np.tile(before[d], (1, 2 * heads)),
                                      np.tile(before[d] | np.eye(chunk, dtype=bool), (1, 2 * heads))], axis=0)
                      for d in range(2)]).astype(np.float32)
    li = np.arange(SCAN_LANES) // RWKV_HEAD
    bdm = (li[:, None] == li[None, :]).astype(np.float32)
    hmask = jnp.asarray((li[None, None, :] == np.arange(heads)[:, None, None]), BF16)
    eye = np.tile(np.eye(chunk, dtype=np.float32), (1, heads))

    def chunk_index(d, s):
        return jnp.where(s < n_ctx, n_lat + s, s - n_ctx) if d == 0 else n_tot - 1 - s

    def const(shape):
        nd = len(shape)
        return pl.BlockSpec(shape, lambda b, h, s: (0,) * nd)

    in_specs = [const(gmask.shape), const(bdm.shape), const(hmask.shape), const(eye.shape)]
    args = [jnp.asarray(gmask), jnp.asarray(bdm), hmask, jnp.asarray(eye)]
    out_specs = []
    for d in range(2):
        in_specs += [pl.BlockSpec((1, 1, chunk, SCAN_PLANES * lane_block),
                                  lambda b, h, s, d=d: (d, b, chunk_index(d, s), h)),
                     pl.BlockSpec((1, 1, 1, 1, lane_block), lambda b, h, s, d=d: (d, b, chunk_index(d, s), 0, h))]
        args += [ops, dec]
        out_specs.append(pl.BlockSpec((1, chunk, lane_block), lambda b, h, s, d=d: (b, chunk_index(d, s), h)))
    ys = jax.ShapeDtypeStruct((bsz, tt, c), F32)
    return pl.pallas_call(
        functools.partial(_scan_kernel, chunk=chunk, nsub=nsub),
        grid=(bsz, c // lane_block, n_tot),
        in_specs=in_specs,
        out_specs=out_specs,
        out_shape=[ys, ys],
        scratch_shapes=[pltpu.VMEM((2 * nsub, SCAN_LANES, SCAN_LANES), F32)],
        compiler_params=_cparams(("arbitrary", "arbitrary", "arbitrary")),
        name="rwkv_scan",
    )(*args)


def _rout_kernel(yf, yb, bonus, g, gng, gnb, ones_ref, o_ref):
    ones_bd = ones_ref[...]
    inv_n = 1.0 / RWKV_HEAD
    y = yf[0] + yb[0]
    mu = _head_sum(y, ones_bd) * inv_n
    yc = y - mu
    var = _head_sum(yc * yc, ones_bd) * inv_n
    yn = yc * lax.rsqrt(var + GN_EPS) * gng[...] + gnb[...]
    o_ref[0] = ((yn + bonus[0]) * g[0]).astype(o_ref.dtype)


def _rwkv_out(yf, yb, bonus, g, gn_g, gn_b, t_lat, tb):
    bsz, _, c = g.shape
    li = np.arange(SCAN_LANES) // RWKV_HEAD
    ones_bd = jnp.asarray(li[:, None] == li[None, :], BF16)
    one = pl.BlockSpec((1, tb, c), lambda b, j: (b, j, 0))

    def const(shape):
        return pl.BlockSpec(shape, lambda b, j: (0, 0))

    return pl.pallas_call(
        _rout_kernel,
        grid=(bsz, t_lat // tb),
        in_specs=[one, one, one, one, const((1, c)), const((1, c)), const((SCAN_LANES, SCAN_LANES))],
        out_specs=one,
        out_shape=jax.ShapeDtypeStruct((bsz, t_lat, c), BF16),
        compiler_params=_cparams(("arbitrary", "arbitrary")),
        name="rwkv_out",
    )(yf, yb, bonus, g, gn_g.reshape(1, c), gn_b.reshape(1, c), ones_bd)


def _rope(x, c, s):
    return x * c + pltpu.roll(x, ATT_HEAD // 2, 1) * s


def _attn_kernel(sink_ref, q_ref, km_ref, k0_ref, kp_ref, vm_ref, v0_ref, vp_ref, kc_ref, vc_ref,
                 c0_ref, s0_ref, cm_ref, sm_ref, cp_ref, sp_ref, o_ref, *, n_blk):
    n = pl.program_id(1)
    blk = ATT_HEAD
    scale = ATT_HEAD ** -0.5
    kvh = range(ATT_KV_HEADS)
    c0, s0 = c0_ref[...], s0_ref[...]
    cm, sm = cm_ref[...], sm_ref[...]
    cp, sp = cp_ref[...], sp_ref[...]

    def head(ref, i):
        return ref[0, :, i * ATT_HEAD:(i + 1) * ATT_HEAD]

    qq = [jnp.concatenate([_rope(head(q_ref, g * ATT_GROUPS + h), c0, s0) for h in range(ATT_GROUPS)],
                          axis=0).astype(BF16) for g in kvh]
    kw = [jnp.concatenate([_rope(head(km_ref, g), cm, sm), _rope(head(k0_ref, g), c0, s0),
                           _rope(head(kp_ref, g), cp, sp)], axis=0).astype(BF16) for g in kvh]
    vw = [jnp.concatenate([head(vm_ref, g), head(v0_ref, g), head(vp_ref, g)], axis=0).astype(BF16) for g in kvh]
    kc = [head(kc_ref, g).astype(BF16) for g in kvh]
    vc = [head(vc_ref, g).astype(BF16) for g in kvh]
    ii = lax.broadcasted_iota(jnp.int32, (ATT_GROUPS * blk, 3 * blk), 0) % blk
    jj = lax.broadcasted_iota(jnp.int32, (ATT_GROUPS * blk, 3 * blk), 1)
    in_prev = jnp.logical_and(jnp.logical_and(jj < blk, jj >= ii), n > 0)
    in_self = jnp.logical_and(jj >= blk, jj < 2 * blk)
    in_next = jnp.logical_and(jnp.logical_and(jj >= 2 * blk, jj - 2 * blk <= ii), n < n_blk - 1)
    valid = jnp.logical_or(jnp.logical_or(in_prev, in_self), in_next)
    s_w = [jnp.where(valid, _dot_nt(qq[g], kw[g]) * scale, NEG_INF) for g in kvh]
    s_c = [_dot_nt(qq[g], kc[g]) * scale for g in kvh]
    s_s = [jnp.concatenate([jnp.full((blk, 1), sink_ref[g * ATT_GROUPS + h], F32) for h in range(ATT_GROUPS)],
                           axis=0) for g in kvh]
    m = [jnp.maximum(jnp.maximum(jnp.max(s_w[g], axis=-1, keepdims=True),
                                 jnp.max(s_c[g], axis=-1, keepdims=True)), s_s[g]) for g in kvh]
    p_w = [jnp.exp(s_w[g] - m[g]) for g in kvh]
    p_c = [jnp.exp(s_c[g] - m[g]) for g in kvh]
    den = [jnp.sum(p_w[g], axis=-1, keepdims=True) + jnp.sum(p_c[g], axis=-1, keepdims=True)
           + jnp.exp(s_s[g] - m[g]) for g in kvh]
    o = [(_dot(p_w[g].astype(BF16), vw[g]) + _dot(p_c[g].astype(BF16), vc[g])) / den[g] for g in kvh]
    o_ref[0] = jnp.concatenate([o[g][h * blk:(h + 1) * blk] for g in kvh for h in range(ATT_GROUPS)],
                               axis=1).astype(o_ref.dtype)


def _attention(proj, col, sink, cos2, sin2, t_lat):
    bsz, tt, _ = proj.shape
    blk = ATT_HEAD
    n_blk = t_lat // blk
    l = tt - t_lat
    qw = ATT_KV_HEADS * ATT_GROUPS * ATT_HEAD
    kw = ATT_KV_HEADS * ATT_HEAD
    q_blk = col["q"] * LANES // qw

    def kv(name, off):
        cb = col[name] * LANES // kw
        return pl.BlockSpec((1, blk, kw), lambda b, n: (b, jnp.clip(n + off, 0, n_blk - 1), cb))

    def ctx(name):
        cb = col[name] * LANES // kw
        return pl.BlockSpec((1, l, kw), lambda b, n: (b, t_lat // l, cb))

    def tab(off):
        return pl.BlockSpec((blk, ATT_HEAD), lambda b, n: (jnp.clip(n + off, 0, n_blk - 1), 0))

    in_specs = [pl.BlockSpec(memory_space=pltpu.SMEM),
                pl.BlockSpec((1, blk, qw), lambda b, n: (b, n, q_blk)),
                kv("ak", -1), kv("ak", 0), kv("ak", 1), kv("av", -1), kv("av", 0), kv("av", 1),
                ctx("ak"), ctx("av"), tab(0), tab(0), tab(-1), tab(-1), tab(1), tab(1)]
    return pl.pallas_call(
        functools.partial(_attn_kernel, n_blk=n_blk),
        grid=(bsz, n_blk),
        in_specs=in_specs,
        out_specs=pl.BlockSpec((1, blk, qw), lambda b, n: (b, n, 0)),
        out_shape=jax.ShapeDtypeStruct((bsz, t_lat, qw), BF16),
        compiler_params=_cparams(("arbitrary", "arbitrary")),
        name="window_attention",
    )(sink, proj, proj, proj, proj, proj, proj, proj, proj, proj, cos2, sin2, cos2, sin2, cos2, sin2)


def _merge_kernel(o1_ref, o2_ref, w1_ref, w2_ref, g1_ref, g2_ref, o_ref):
    y1 = _dot(o1_ref[0], w1_ref[...])
    y2 = _dot(o2_ref[0], w2_ref[...])
    o_ref[0] = (jax.nn.sigmoid(g1_ref[0].astype(F32)) * y1
                + jax.nn.sigmoid(g2_ref[0].astype(F32)) * y2).astype(o_ref.dtype)


def _merge(o_rwkv, o_att, w1, w2, gates, tm, tn):
    bsz, t, c = o_rwkv.shape
    n = w1.shape[1]
    gr = 0
    ga = n // tn
    return pl.pallas_call(
        _merge_kernel,
        grid=(bsz, t // tm, n // tn),
        in_specs=[pl.BlockSpec((1, tm, c), lambda b, i, j: (b, i, 0)),
                  pl.BlockSpec((1, tm, c), lambda b, i, j: (b, i, 0)),
                  pl.BlockSpec((c, tn), lambda b, i, j: (0, j)),
                  pl.BlockSpec((c, tn), lambda b, i, j: (0, j)),
                  pl.BlockSpec((1, tm, tn), lambda b, i, j: (b, i, gr + j)),
                  pl.BlockSpec((1, tm, tn), lambda b, i, j: (b, i, ga + j))],
        out_specs=pl.BlockSpec((1, tm, tn), lambda b, i, j: (b, i, j)),
        out_shape=jax.ShapeDtypeStruct((bsz, t, n), BF16),
        compiler_params=_cparams(("arbitrary", "arbitrary", "arbitrary")),
        name="gated_merge",
    )(o_rwkv, o_att, w1, w2, gates, gates)


def _ln_kernel(x_ref, y_ref, gt_ref, g_ref, b_ref, *rest, with_mod):
    bi = pl.program_id(0)
    gt = gt_ref[pl.ds(bi, 1), :]
    z = DEEPNORM_ALPHA * x_ref[0] + gt * y_ref[0]
    mu = jnp.mean(z, axis=-1, keepdims=True)
    zc = z - mu
    var = jnp.mean(zc * zc, axis=-1, keepdims=True)
    out = zc * lax.rsqrt(var + LN_EPS) * g_ref[...] + b_ref[...]
    if with_mod:
        sh_ref, sc_ref, o_ref, h_ref = rest
        o_ref[0] = out
        h_ref[0] = (out * (1.0 + sc_ref[pl.ds(bi, 1), :]) + sh_ref[pl.ds(bi, 1), :]).astype(BF16)
    else:
        (o_ref,) = rest
        o_ref[0] = out


def _res_ln(x, y, mod, gate_col, g, b, tb, mod_cols=None):
    bsz, t, d = x.shape
    blk = pl.BlockSpec((1, tb, d), lambda bi, j: (bi, j, 0))
    vec = pl.BlockSpec((1, d), lambda bi, j: (0, 0))

    def modspec(cb):
        return pl.BlockSpec((MOD_ROWS, d), lambda bi, j: (0, cb))

    in_specs = [blk, blk, modspec(gate_col), vec, vec]
    args = [x, y, mod, g.reshape(1, d), b.reshape(1, d)]
    out_specs = [blk]
    out_shape = [jax.ShapeDtypeStruct((bsz, t, d), F32)]
    if mod_cols is not None:
        in_specs += [modspec(mod_cols[0]), modspec(mod_cols[1])]
        args += [mod, mod]
        out_specs.append(blk)
        out_shape.append(jax.ShapeDtypeStruct((bsz, t, d), BF16))
    return pl.pallas_call(
        functools.partial(_ln_kernel, with_mod=mod_cols is not None),
        grid=(bsz, t // tb),
        in_specs=in_specs,
        out_specs=out_specs,
        out_shape=out_shape,
        compiler_params=_cparams(("arbitrary", "arbitrary")),
        name="residual_layernorm",
    )(*args)


def _ffn_up_kernel(a_ref, wg_ref, wu_ref, o_ref):
    a = a_ref[...]
    gg = _dot(a, wg_ref[...])
    uu = _dot(a, wu_ref[...])
    o_ref[...] = (gg * jax.nn.sigmoid(gg) * uu).astype(o_ref.dtype)


def _ffn_up(a, wg, wu, tm, tf):
    m, k = a.shape
    f = wg.shape[1]
    return pl.pallas_call(
        _ffn_up_kernel,
        grid=(m // tm, f // tf),
        in_specs=[pl.BlockSpec((tm, k), lambda i, j: (i, 0)),
                  pl.BlockSpec((k, tf), lambda i, j: (0, j)),
                  pl.BlockSpec((k, tf), lambda i, j: (0, j))],
        out_specs=pl.BlockSpec((tm, tf), lambda i, j: (i, j)),
        out_shape=jax.ShapeDtypeStruct((m, f), BF16),
        compiler_params=_cparams(("arbitrary", "arbitrary")),
        name="swiglu_up",
    )(a, wg, wu)


def _rope_tables(t):
    rows = t // GRID_W
    row = jnp.broadcast_to(jnp.arange(rows, dtype=F32)[:, None], (rows, GRID_W)).reshape(t)
    colp = jnp.broadcast_to(jnp.arange(GRID_W, dtype=F32)[None, :], (rows, GRID_W)).reshape(t)
    axis_dim = ATT_HEAD // 2
    inv = ROPE_BASE ** (-jnp.arange(0, axis_dim, 2, dtype=F32) / axis_dim)
    ang = jnp.concatenate([row[:, None] * inv, colp[:, None] * inv], -1)
    cos, sin = jnp.cos(ang), jnp.sin(ang)
    return jnp.concatenate([cos, cos], -1), jnp.concatenate([-sin, sin], -1)


def _block(x, c, ctx, c_ctx, w_ada, b_ada, w_in, rwkv_shift, rwkv_w0, rwkv_w_up, rwkv_a0, rwkv_a_up,
           rwkv_g_up, rwkv_k_k, rwkv_k_a, rwkv_r_k, rwkv_gn_g, rwkv_gn_b, attn_sink, w_rwkv_o, w_att_o,
           w_out, ln1_g, ln1_b, w_ff_gate, w_ff_up, w_ff_down, ln2_g, ln2_b, tiles):
    bsz, t, d = x.shape
    l = ctx.shape[1]
    tt = t + l
    crw = rwkv_w0.shape[1]
    d_att = ATT_KV_HEADS * ATT_GROUPS * ATT_HEAD
    d_kv = ATT_KV_HEADS * ATT_HEAD
    n_lora = 2 * DECAY_LORA + 2 * ICLR_LORA + GATE_LORA
    n_rw = 3 * crw + n_lora

    cc = jnp.zeros((MOD_ROWS, d), F32).at[:bsz].set(c).at[bsz].set(c_ctx)
    mod = _ada(cc, w_ada, b_ada)

    o_q = n_rw
    o_g = o_q + d_att + 2 * d_kv
    w_rest = jnp.concatenate([w_in[:, :3 * crw], w_in[:, o_q:o_g], w_in[:, 3 * crw:n_rw]], axis=1).astype(BF16)
    w_gates = w_in[:, o_g:].astype(BF16)
    col = {}
    off = 0
    for name, width in (("r", crw), ("k", crw), ("v", crw), ("q", d_att), ("ak", d_kv), ("av", d_kv),
                        ("lora", n_lora)):
        col[name] = off // LANES
        off += width
    shift_rkv = rwkv_shift[:, :3 * crw]
    shift_lora = jnp.concatenate([rwkv_shift[:, 3 * crw:], jnp.zeros((3, LORA_PAD - n_lora), F32)], axis=1)

    h = _prep(x, ctx, mod)
    proj = _mm_wstat(h.reshape(bsz * tt, d), w_rest, tiles["proj_tm"], tiles["proj_tn"], F32, "in_proj")
    proj = proj.reshape(bsz, tt, -1)
    gates = _mm_lead_rows(h, w_gates, t, tiles["gate_tm"], tiles["gate_tn"], BF16, "gate_proj")

    g, bonus, ops, dec = _features(proj, col, shift_rkv, shift_lora, rwkv_w0, rwkv_w_up, rwkv_a0, rwkv_a_up,
                                   rwkv_g_up, rwkv_k_k, rwkv_k_a, rwkv_r_k.reshape(-1), t, tiles["feat_tb"])
    yf, yb = _scan(ops, dec, t, tiles["scan_lanes"])
    o_rwkv = _rwkv_out(yf, yb, bonus, g, rwkv_gn_g, rwkv_gn_b, t, tiles["feat_tb"])

    cos2, sin2 = _rope_tables(t)
    o_att = _attention(proj, col, attn_sink, cos2, sin2, t)

    ym = _merge(o_rwkv, o_att, w_rwkv_o.astype(BF16), w_att_o.astype(BF16), gates,
                tiles["merge_tm"], tiles["merge_tn"])
    yo = _mm_astat(ym.reshape(bsz * t, d), w_out.astype(BF16), tiles["out_tm"], tiles["out_tn"], F32, "out_proj")
    x1, h2 = _res_ln(x, yo.reshape(bsz, t, d), mod, 2, ln1_g, ln1_b, tiles["ln_tb"], mod_cols=(3, 4))

    u = _ffn_up(h2.reshape(bsz * t, d), w_ff_gate.astype(BF16), w_ff_up.astype(BF16),
                tiles["ffn_tm"], tiles["ffn_tf"])
    ff = _mm_astat(u, w_ff_down.astype(BF16), tiles["down_tm"], tiles["down_tn"], F32, "swiglu_down")
    (out,) = _res_ln(x1, ff.reshape(bsz, t, d), mod, 5, ln2_g, ln2_b, tiles["ln_tb"])
    return out


_TILES = dict(proj_tm=512, proj_tn=1024, gate_tm=1024, gate_tn=1024, feat_tb=128, scan_lanes=1024, merge_tm=1024, merge_tn=512, out_tm=1024, out_tn=1024,
              ln_tb=256, ffn_tm=2048, ffn_tf=256, down_tm=512, down_tn=512)


def kernel(x, c, ctx, c_ctx, w_ada, b_ada, w_in, rwkv_shift, rwkv_w0, rwkv_w_up, rwkv_a0, rwkv_a_up, rwkv_g_up, rwkv_k_k, rwkv_k_a, rwkv_r_k, rwkv_gn_g, rwkv_gn_b, attn_sink, w_rwkv_o, w_att_o, w_out, ln1_g, ln1_b, w_ff_gate, w_ff_up, w_ff_down, ln2_g, ln2_b):
    assert w_ada.shape[0] == DEPTH
    return _block(x, c, ctx, c_ctx, w_ada[0], b_ada[0], w_in[0], rwkv_shift[0], rwkv_w0[0], rwkv_w_up[0],
                  rwkv_a0[0], rwkv_a_up[0], rwkv_g_up[0], rwkv_k_k[0], rwkv_k_a[0], rwkv_r_k[0], rwkv_gn_g[0],
                  rwkv_gn_b[0], attn_sink[0], w_rwkv_o[0], w_att_o[0], w_out[0], ln1_g[0], ln1_b[0],
                  w_ff_gate[0], w_ff_up[0], w_ff_down[0], ln2_g[0], ln2_b[0], _TILES)
```

```python
import functools

import jax
import jax.numpy as jnp
import numpy as np
from jax import lax
from jax.experimental import pallas as pl
from jax.experimental.pallas import tpu as pltpu

F32 = jnp.float32
BF16 = jnp.bfloat16
HIGHEST = lax.Precision.HIGHEST

RWKV_HEAD = 64
DECAY_LORA = 96
ICLR_LORA = 96
GATE_LORA = 256
GN_EPS = 64e-5
ATT_HEAD = 128
ATT_KV_HEADS = 4
ATT_GROUPS = 4
GRID_W = 64
ROPE_BASE = 10000.0
LN_EPS = 1e-5
DEPTH = 1
DEEPNORM_ALPHA = (2 * DEPTH) ** 0.25
NEG_INF = -1e30

LANES = 128
SUBLANES = 8
VMEM_LIMIT = 56 * 1024 * 1024
MOD_ROWS = 8
SCAN_CHUNK = 64
SCAN_LANES = 256
SCAN_PLANES = 5
LORA_PAD = 1024


def _cparams(sem):
    return pltpu.CompilerParams(dimension_semantics=sem, vmem_limit_bytes=VMEM_LIMIT)


def _dot(a, b, precision=None):
    return jnp.dot(a, b, preferred_element_type=F32, precision=precision)


def _dot_nt(a, b, precision=None):
    return lax.dot_general(a, b, (((1,), (1,)), ((), ())), preferred_element_type=F32, precision=precision)


def _dot_tn(a, b, precision=None):
    return lax.dot_general(a, b, (((0,), (0,)), ((), ())), preferred_element_type=F32, precision=precision)


def _split2(x):
    hi = x.astype(BF16)
    return hi, (x - hi.astype(F32)).astype(BF16)


def _head_sum(x, ones_bd):
    rows = x.shape[0]
    hi, lo = _split2(x)
    both = jnp.concatenate([hi, lo], axis=0)
    out = []
    for s in range(x.shape[1] // SCAN_LANES):
        part = _dot(both[:, s * SCAN_LANES:(s + 1) * SCAN_LANES], ones_bd)
        out.append(part[:rows] + part[rows:])
    return jnp.concatenate(out, axis=1)


def _ada_kernel(c_ref, w_ref, b_ref, o_ref):
    a = c_ref[...]
    a = a * jax.nn.sigmoid(a)
    a_hi, a_lo = _split2(a)
    w_hi, w_lo = _split2(w_ref[...])
    part = _dot(jnp.concatenate([a_hi, a_lo], axis=0), w_hi)
    o_ref[...] = part[:MOD_ROWS] + part[MOD_ROWS:] + _dot(a_hi, w_lo) + b_ref[...]


def _ada(cc, w_ada, b_ada):
    d, n = w_ada.shape
    tn = 512
    return pl.pallas_call(
        _ada_kernel,
        grid=(n // tn,),
        in_specs=[pl.BlockSpec((MOD_ROWS, d), lambda j: (0, 0)),
                  pl.BlockSpec((d, tn), lambda j: (0, j)),
                  pl.BlockSpec((1, tn), lambda j: (0, j))],
        out_specs=pl.BlockSpec((MOD_ROWS, tn), lambda j: (0, j)),
        out_shape=jax.ShapeDtypeStruct((MOD_ROWS, n), F32),
        compiler_params=_cparams(("arbitrary",)),
        name="ada",
    )(cc, w_ada, b_ada.reshape(1, n))


def _prep_kernel(x_ref, ctx_ref, sh_ref, sc_ref, o_ref, *, n_lat, ctx_row):
    b = pl.program_id(0)
    j = pl.program_id(1)
    is_ctx = j >= n_lat
    row = jnp.where(is_ctx, ctx_row, b)
    sh = sh_ref[pl.ds(row, 1), :]
    sc = sc_ref[pl.ds(row, 1), :]
    xin = jnp.where(is_ctx, ctx_ref[0], x_ref[0])
    o_ref[0] = (xin * (1.0 + sc) + sh).astype(BF16)


def _prep(x, ctx, mod):
    bsz, t, d = x.shape
    l = ctx.shape[1]
    tb = l
    n_lat = t // tb
    return pl.pallas_call(
        functools.partial(_prep_kernel, n_lat=n_lat, ctx_row=bsz),
        grid=(bsz, n_lat + 1),
        in_specs=[pl.BlockSpec((1, tb, d), lambda b, j: (b, jnp.minimum(j, n_lat - 1), 0)),
                  pl.BlockSpec((1, l, d), lambda b, j: (b, 0, 0)),
                  pl.BlockSpec((MOD_ROWS, d), lambda b, j: (0, 0)),
                  pl.BlockSpec((MOD_ROWS, d), lambda b, j: (0, 1))],
        out_specs=pl.BlockSpec((1, tb, d), lambda b, j: (b, j, 0)),
        out_shape=jax.ShapeDtypeStruct((bsz, t + l, d), BF16),
        compiler_params=_cparams(("arbitrary", "arbitrary")),
        name="prep",
    )(x, ctx, mod, mod)


def _mm_kernel(a_ref, w_ref, o_ref):
    o_ref[...] = _dot(a_ref[...], w_ref[...]).astype(o_ref.dtype)


def _mm_wstat(a, w, tm, tn, out_dtype, name):
    m, k = a.shape
    n = w.shape[1]
    return pl.pallas_call(
        _mm_kernel,
        grid=(pl.cdiv(n, tn), m // tm),
        in_specs=[pl.BlockSpec((tm, k), lambda j, i: (i, 0)),
                  pl.BlockSpec((k, tn), lambda j, i: (0, j))],
        out_specs=pl.BlockSpec((tm, tn), lambda j, i: (i, j)),
        out_shape=jax.ShapeDtypeStruct((m, n), out_dtype),
        compiler_params=_cparams(("arbitrary", "arbitrary")),
        name=name,
    )(a, w)


def _mm_rows_kernel(a_ref, w_ref, o_ref):
    o_ref[0] = _dot(a_ref[0], w_ref[...]).astype(o_ref.dtype)


def _mm_lead_rows(a, w, rows, tm, tn, out_dtype, name):
    bsz, _, k = a.shape
    n = w.shape[1]
    return pl.pallas_call(
        _mm_rows_kernel,
        grid=(n // tn, bsz, rows // tm),
        in_specs=[pl.BlockSpec((1, tm, k), lambda j, b, i: (b, i, 0)),
                  pl.BlockSpec((k, tn), lambda j, b, i: (0, j))],
        out_specs=pl.BlockSpec((1, tm, tn), lambda j, b, i: (b, i, j)),
        out_shape=jax.ShapeDtypeStruct((bsz, rows, n), out_dtype),
        compiler_params=_cparams(("arbitrary", "arbitrary", "arbitrary")),
        name=name,
    )(a, w)


def _mm_astat(a, w, tm, tn, out_dtype, name):
    m, k = a.shape
    n = w.shape[1]
    return pl.pallas_call(
        _mm_kernel,
        grid=(m // tm, n // tn),
        in_specs=[pl.BlockSpec((tm, k), lambda i, j: (i, 0)),
                  pl.BlockSpec((k, tn), lambda i, j: (0, j))],
        out_specs=pl.BlockSpec((tm, tn), lambda i, j: (i, j)),
        out_shape=jax.ShapeDtypeStruct((m, n), out_dtype),
        compiler_params=_cparams(("arbitrary", "arbitrary")),
        name=name,
    )(a, w)


def _mm_acc_kernel(a_ref, w_ref, o_ref, acc_ref):
    kk = pl.program_id(2)

    @pl.when(kk == 0)
    def _():
        acc_ref[...] = jnp.zeros_like(acc_ref)

    acc_ref[...] += _dot(a_ref[...], w_ref[...])

    @pl.when(kk == pl.num_programs(2) - 1)
    def _():
        o_ref[...] = acc_ref[...].astype(o_ref.dtype)


def _mm_ktiled(a, w, tm, tn, tk, out_dtype, name):
    m, k = a.shape
    n = w.shape[1]
    return pl.pallas_call(
        _mm_acc_kernel,
        grid=(m // tm, n // tn, k // tk),
        in_specs=[pl.BlockSpec((tm, tk), lambda i, j, q: (i, q)),
                  pl.BlockSpec((tk, tn), lambda i, j, q: (q, j))],
        out_specs=pl.BlockSpec((tm, tn), lambda i, j, q: (i, j)),
        out_shape=jax.ShapeDtypeStruct((m, n), out_dtype),
        scratch_shapes=[pltpu.VMEM((tm, tn), F32)],
        compiler_params=_cparams(("arbitrary", "arbitrary", "arbitrary")),
        name=name,
    )(a, w)


def _conv3(prev_ref, cur_ref, next_ref, w_ref, at_start, at_end):
    cur = cur_ref[0]
    rows = cur.shape[0]
    prev_row = jnp.where(at_start, 0.0, prev_ref[0][SUBLANES - 1:SUBLANES, :])
    next_row = jnp.where(at_end, 0.0, next_ref[0][0:1, :])
    up = jnp.concatenate([prev_row, cur[:rows - 1]], axis=0)
    down = jnp.concatenate([cur[1:], next_row], axis=0)
    w = w_ref[...]
    return up * w[0:1] + cur * w[1:2] + down * w[2:3]


def _feat_kernel(rp, rc, rn, kp, kc, kn, vp, vc, vn, lp, lc, ln_,
                 shr, shk, shv, shl, w0, wup, a0, aup, gup, kk_w, ka_w, rk_w, ones_ref, cm_ref,
                 g_o, bonus_o, ops_o, dec_o, *, n_lat, n_tot, chunk):
    j = pl.program_id(1)
    at_start = jnp.logical_or(j == 0, j == n_lat)
    at_end = jnp.logical_or(j == n_lat - 1, j == n_tot - 1)
    r = _conv3(rp, rc, rn, shr, at_start, at_end)
    k = _conv3(kp, kc, kn, shk, at_start, at_end)
    v = _conv3(vp, vc, vn, shv, at_start, at_end)
    lo = _conv3(lp, lc, ln_, shl, at_start, at_end)
    rows = r.shape[0]
    v_h = v.astype(BF16)
    gd = lo[:, 2 * DECAY_LORA + 2 * ICLR_LORA:2 * DECAY_LORA + 2 * ICLR_LORA + GATE_LORA]
    ones_bd = ones_ref[...]
    g_o[0] = _dot(jax.nn.sigmoid(gd).astype(BF16), gup[...])
    kk = k * kk_w[...]
    kk = kk / jnp.maximum(jnp.sqrt(_head_sum(kk * kk, ones_bd)), 1e-12)
    k_sum = jnp.zeros_like(k)
    for d in range(2):
        wd = lo[:, d * DECAY_LORA:(d + 1) * DECAY_LORA]
        ad = lo[:, 2 * DECAY_LORA + d * ICLR_LORA:2 * DECAY_LORA + (d + 1) * ICLR_LORA]
        z = w0[d:d + 1, :] + _dot(jnp.tanh(wd).astype(BF16), wup[d])
        lw = jax.nn.sigmoid(z) * (-np.exp(-0.5))
        iclr = jax.nn.sigmoid(a0[d:d + 1, :] + _dot(ad.astype(BF16), aup[d]))
        k_dir = k * (1.0 + (iclr - 1.0) * ka_w[...])
        k_sum = k_sum + k_dir
        lw_hi, lw_lo = _split2(lw)
        cum = _dot(cm_ref[d], lw_hi) + _dot(cm_ref[d], lw_lo)
        p_inv = jnp.exp(-cum)
        planes = [(kk * jnp.exp(cum - lw)).astype(BF16), (r * jnp.exp(cum)).astype(BF16),
                  (k_dir * p_inv).astype(BF16), (kk * iclr * p_inv).astype(BF16), v_h]
        for hg in range(r.shape[1] // SCAN_LANES):
            for p, plane in enumerate(planes):
                dst = (hg * SCAN_PLANES + p) * SCAN_LANES
                ops_o[d, 0, :, dst:dst + SCAN_LANES] = plane[:, hg * SCAN_LANES:(hg + 1) * SCAN_LANES]
        for q in range(rows // chunk):
            last = (q + 1) * chunk - 1 if d == 0 else q * chunk
            dec_o[d, 0, q] = jnp.exp(cum[last:last + 1, :])
    bonus_o[0] = _head_sum(r * k_sum * rk_w[...], ones_bd) * v


def _features(proj, col, shift_rkv, shift_lora, w0, w_up, a0, a_up, g_up, k_k, k_a, r_k, t_lat, tb):
    bsz, tt, _ = proj.shape
    c = w0.shape[1]
    chunk = SCAN_CHUNK
    n_tot = tt // tb
    n_lat = t_lat // tb
    hb = tb // SUBLANES
    n_h = tt // SUBLANES
    li = np.arange(SCAN_LANES) // RWKV_HEAD
    ones_bd = jnp.asarray(li[:, None] == li[None, :], BF16)
    ti = np.arange(tb)
    same = (ti[:, None] // chunk) == (ti[None, :] // chunk)
    cmask = jnp.asarray(np.stack([same & (ti[None, :] <= ti[:, None]), same & (ti[None, :] >= ti[:, None])]), BF16)

    def main(cb, width):
        blk = cb * LANES // width
        return pl.BlockSpec((1, tb, width), lambda b, j: (b, j, blk))

    def prev(cb, width):
        blk = cb * LANES // width
        return pl.BlockSpec((1, SUBLANES, width), lambda b, j: (b, jnp.maximum(j * hb - 1, 0), blk))

    def nxt(cb, width):
        blk = cb * LANES // width
        return pl.BlockSpec((1, SUBLANES, width), lambda b, j: (b, jnp.minimum((j + 1) * hb, n_h - 1), blk))

    def const(shape):
        nd = len(shape)
        return pl.BlockSpec(shape, lambda b, j: (0,) * nd)

    in_specs = []
    for name, width in (("r", c), ("k", c), ("v", c), ("lora", LORA_PAD)):
        in_specs += [prev(col[name], width), main(col[name], width), nxt(col[name], width)]
    in_specs += [const((3, c)), const((3, c)), const((3, c)), const((3, LORA_PAD)),
                 const((2, c)), const((2, DECAY_LORA, c)), const((2, c)), const((2, ICLR_LORA, c)),
                 const((GATE_LORA, c)), const((1, c)), const((1, c)), const((1, c)),
                 const((SCAN_LANES, SCAN_LANES)), const((2, tb, tb))]
    one = pl.BlockSpec((1, tb, c), lambda b, j: (b, j, 0))
    ops = pl.BlockSpec((2, 1, tb, SCAN_PLANES * c), lambda b, j: (0, b, j, 0))
    dec = pl.BlockSpec((2, 1, tb // chunk, 1, c), lambda b, j: (0, b, j, 0, 0))
    s1 = jax.ShapeDtypeStruct((bsz, tt, c), F32)
    sops = jax.ShapeDtypeStruct((2, bsz, tt, SCAN_PLANES * c), BF16)
    sdec = jax.ShapeDtypeStruct((2, bsz, tt // chunk, 1, c), F32)
    args = [proj] * 12 + [shift_rkv[:, :c], shift_rkv[:, c:2 * c], shift_rkv[:, 2 * c:], shift_lora,
                          w0, w_up.astype(BF16), a0, a_up.astype(BF16), g_up.astype(BF16),
                          k_k.reshape(1, c), k_a.reshape(1, c), r_k.reshape(1, c), ones_bd, cmask]
    return pl.pallas_call(
        functools.partial(_feat_kernel, n_lat=n_lat, n_tot=n_tot, chunk=chunk),
        grid=(bsz, n_tot),
        in_specs=in_specs,
        out_specs=[one, one, ops, dec],
        out_shape=[s1, s1, sops, sdec],
        compiler_params=_cparams(("arbitrary", "arbitrary")),
        name="rwkv_features",
    )(*args)


def _scan_kernel(gm_ref, bdm_ref, hm_ref, eye_ref, *refs, chunk, nsub):
    ops_refs = refs[0:4:2]
    dec_refs = refs[1:4:2]
    y_refs = refs[4:6]
    s_ref = refs[6]
    heads = SCAN_LANES // RWKV_HEAD

    @pl.when(pl.program_id(2) == 0)
    def _():
        s_ref[...] = jnp.zeros_like(s_ref)

    bdm = bdm_ref[...]
    bdm_h = bdm.astype(BF16)
    eye = eye_ref[...]
    hms = [hm_ref[h] for h in range(heads)]

    def rows_bd(m):
        return jnp.concatenate([m * hms[h] for h in range(heads)], axis=0)

    def blocks_bd(m):
        return jnp.concatenate([m] * heads, axis=0) * bdm_h

    steps = chunk.bit_length() - 2
    units = [(d, q) for d in range(2) for q in range(nsub)]
    n_u = len(units)

    def lanes(q):
        return slice(q * SCAN_LANES, (q + 1) * SCAN_LANES)

    def plane(d, q, p):
        return ops_refs[d][0, 0, :, lanes(q * SCAN_PLANES + p)]

    kt = [plane(d, q, 2) for d, q in units]
    bt = [plane(d, q, 3) for d, q in units]
    v = [plane(d, q, 4) for d, q in units]
    x = [jnp.concatenate([plane(d, q, 0), plane(d, q, 1)], axis=0) for d, q in units]
    g = [_dot_nt(x[i], jnp.concatenate([rows_bd(kt[i]), rows_bd(bt[i])], axis=0)) * gm_ref[units[i][0]]
         for i in range(n_u)]
    s_old = [s_ref[i] for i in range(n_u)]
    xs = [_dot_nt(x[i], s_old[i].astype(BF16)) for i in range(n_u)]
    gv = [_dot(g[i][:, :SCAN_LANES].astype(BF16), rows_bd(v[i])) for i in range(n_u)]
    rhs = [xs[i][:chunk] + gv[i][:chunk] for i in range(n_u)]
    t = [eye - g[i][:chunk, SCAN_LANES:] for i in range(n_u)]
    lm = [g[i][:chunk, SCAN_LANES:].astype(BF16) for i in range(n_u)]
    pw = [_dot(lm[i], blocks_bd(lm[i])).astype(BF16) for i in range(n_u)]
    for _ in range(steps - 1):
        both = [_dot(jnp.concatenate([t[i].astype(BF16), pw[i]], axis=0), blocks_bd(pw[i])) for i in range(n_u)]
        t = [t[i] + both[i][:chunk] for i in range(n_u)]
        pw = [both[i][chunk:].astype(BF16) for i in range(n_u)]
    t = [t[i] + _dot(t[i].astype(BF16), blocks_bd(pw[i])) for i in range(n_u)]
    u = [(-_dot(t[i].astype(BF16), rows_bd(rhs[i].astype(BF16)))).astype(BF16) for i in range(n_u)]
    for i, (d, q) in enumerate(units):
        y_refs[d][0, :, lanes(q)] = (xs[i][chunk:] + gv[i][chunk:]
                                     + _dot(g[i][chunk:, SCAN_LANES:].astype(BF16), rows_bd(u[i])))
    for i, (d, q) in enumerate(units):
        upd = _dot_tn(jnp.concatenate([v[i], u[i]], axis=0), jnp.concatenate([kt[i], bt[i]], axis=0))
        s_ref[i] = (s_old[i] + upd) * dec_refs[d][0, 0, 0, :, lanes(q)] * bdm


def _scan(ops, dec, t_lat, lane_block):
    _, bsz, tt, c = ops.shape
    c //= SCAN_PLANES
    chunk = SCAN_CHUNK
    heads = SCAN_LANES // RWKV_HEAD
    n_tot = tt // chunk
    n_lat = t_lat // chunk
    n_ctx = n_tot - n_lat
    nsub = lane_block // SCAN_LANES

    ti = np.arange(chunk)
    before = [ti[None, :] < ti[:, None], ti[None, :] > ti[:, None]]
    gmask = np.stack([np.concatenate([np.tile(before[d], (1, 2 * heads)),
                                      np.tile(before[d] | np.eye(chunk, dtype=bool), (1, 2 * heads))], axis=0)
                      for d in range(2)]).astype(np.float32)
    li = np.arange(SCAN_LANES) // RWKV_HEAD
    bdm = (li[:, None] == li[None, :]).astype(np.float32)
    hmask = jnp.asarray((li[None, None, :] == np.arange(heads)[:, None, None]), BF16)
    eye = np.tile(np.eye(chunk, dtype=np.float32), (1, heads))

    def chunk_index(d, s):
        return jnp.where(s < n_ctx, n_lat + s, s - n_ctx) if d == 0 else n_tot - 1 - s

    def const(shape):
        nd = len(shape)
        return pl.BlockSpec(shape, lambda b, h, s: (0,) * nd)

    in_specs = [const(gmask.shape), const(bdm.shape), const(hmask.shape), const(eye.shape)]
    args = [jnp.asarray(gmask), jnp.asarray(bdm), hmask, jnp.asarray(eye)]
    out_specs = []
    for d in range(2):
        in_specs += [pl.BlockSpec((1, 1, chunk, SCAN_PLANES * lane_block),
                                  lambda b, h, s, d=d: (d, b, chunk_index(d, s), h)),
                     pl.BlockSpec((1, 1, 1, 1, lane_block), lambda b, h, s, d=d: (d, b, chunk_index(d, s), 0, h))]
        args += [ops, dec]
        out_specs.append(pl.BlockSpec((1, chunk, lane_block), lambda b, h, s, d=d: (b, chunk_index(d, s), h)))
    ys = jax.ShapeDtypeStruct((bsz, tt, c), F32)
    return pl.pallas_call(
        functools.partial(_scan_kernel, chunk=chunk, nsub=nsub),
        grid=(bsz, c // lane_block, n_tot),
        in_specs=in_specs,
        out_specs=out_specs,
        out_shape=[ys, ys],
        scratch_shapes=[pltpu.VMEM((2 * nsub, SCAN_LANES, SCAN_LANES), F32)],
        compiler_params=_cparams(("arbitrary", "arbitrary", "arbitrary")),
        name="rwkv_scan",
    )(*args)


def _rout_kernel(yf, yb, bonus, g, gng, gnb, ones_ref, o_ref):
    ones_bd = ones_ref[...]
    inv_n = 1.0 / RWKV_HEAD
    y = yf[0] + yb[0]
    mu = _head_sum(y, ones_bd) * inv_n
    yc = y - mu
    var = _head_sum(yc * yc, ones_bd) * inv_n
    yn = yc * lax.rsqrt(var + GN_EPS) * gng[...] + gnb[...]
    o_ref[0] = ((yn + bonus[0]) * g[0]).astype(o_ref.dtype)


def _rwkv_out(yf, yb, bonus, g, gn_g, gn_b, t_lat, tb):
    bsz, _, c = g.shape
    li = np.arange(SCAN_LANES) // RWKV_HEAD
    ones_bd = jnp.asarray(li[:, None] == li[None, :], BF16)
    one = pl.BlockSpec((1, tb, c), lambda b, j: (b, j, 0))

    def const(shape):
        return pl.BlockSpec(shape, lambda b, j: (0, 0))

    return pl.pallas_call(
        _rout_kernel,
        grid=(bsz, t_lat // tb),
        in_specs=[one, one, one, one, const((1, c)), const((1, c)), const((SCAN_LANES, SCAN_LANES))],
        out_specs=one,
        out_shape=jax.ShapeDtypeStruct((bsz, t_lat, c), BF16),
        compiler_params=_cparams(("arbitrary", "arbitrary")),
        name="rwkv_out",
    )(yf, yb, bonus, g, gn_g.reshape(1, c), gn_b.reshape(1, c), ones_bd)


def _rope(x, c, s):
    return x * c + pltpu.roll(x, ATT_HEAD // 2, 1) * s


def _attn_kernel(sink_ref, q_ref, km_ref, k0_ref, kp_ref, vm_ref, v0_ref, vp_ref, kc_ref, vc_ref,
                 c0_ref, s0_ref, cm_ref, sm_ref, cp_ref, sp_ref, o_ref, *, n_blk):
    n = pl.program_id(1)
    blk = ATT_HEAD
    scale = ATT_HEAD ** -0.5
    kvh = range(ATT_KV_HEADS)
    c0, s0 = c0_ref[...], s0_ref[...]
    cm, sm = cm_ref[...], sm_ref[...]
    cp, sp = cp_ref[...], sp_ref[...]

    def head(ref, i):
        return ref[0, :, i * ATT_HEAD:(i + 1) * ATT_HEAD]

    qq = [jnp.concatenate([_rope(head(q_ref, g * ATT_GROUPS + h), c0, s0) for h in range(ATT_GROUPS)],
                          axis=0).astype(BF16) for g in kvh]
    kw = [jnp.concatenate([_rope(head(km_ref, g), cm, sm), _rope(head(k0_ref, g), c0, s0),
                           _rope(head(kp_ref, g), cp, sp)], axis=0).astype(BF16) for g in kvh]
    vw = [jnp.concatenate([head(vm_ref, g), head(v0_ref, g), head(vp_ref, g)], axis=0).astype(BF16) for g in kvh]
    kc = [head(kc_ref, g).astype(BF16) for g in kvh]
    vc = [head(vc_ref, g).astype(BF16) for g in kvh]
    ii = lax.broadcasted_iota(jnp.int32, (ATT_GROUPS * blk, 3 * blk), 0) % blk
    jj = lax.broadcasted_iota(jnp.int32, (ATT_GROUPS * blk, 3 * blk), 1)
    in_prev = jnp.logical_and(jnp.logical_and(jj < blk, jj >= ii), n > 0)
    in_self = jnp.logical_and(jj >= blk, jj < 2 * blk)
    in_next = jnp.logical_and(jnp.logical_and(jj >= 2 * blk, jj - 2 * blk <= ii), n < n_blk - 1)
    valid = jnp.logical_or(jnp.logical_or(in_prev, in_self), in_next)
    s_w = [jnp.where(valid, _dot_nt(qq[g], kw[g]) * scale, NEG_INF) for g in kvh]
    s_c = [_dot_nt(qq[g], kc[g]) * scale for g in kvh]
    s_s = [jnp.concatenate([jnp.full((blk, 1), sink_ref[g * ATT_GROUPS + h], F32) for h in range(ATT_GROUPS)],
                           axis=0) for g in kvh]
    m = [jnp.maximum(jnp.maximum(jnp.max(s_w[g], axis=-1, keepdims=True),
                                 jnp.max(s_c[g], axis=-1, keepdims=True)), s_s[g]) for g in kvh]
    p_w = [jnp.exp(s_w[g] - m[g]) for g in kvh]
    p_c = [jnp.exp(s_c[g] - m[g]) for g in kvh]
    den = [jnp.sum(p_w[g], axis=-1, keepdims=True) + jnp.sum(p_c[g], axis=-1, keepdims=True)
           + jnp.exp(s_s[g] - m[g]) for g in kvh]
    o = [(_dot(p_w[g].astype(BF16), vw[g]) + _dot(p_c[g].astype(BF16), vc[g])) / den[g] for g in kvh]
    o_ref[0] = jnp.concatenate([o[g][h * blk:(h + 1) * blk] for g in kvh for h in range(ATT_GROUPS)],
                               axis=1).astype(o_ref.dtype)


def _attention(proj, col, sink, cos2, sin2, t_lat):
    bsz, tt, _ = proj.shape
    blk = ATT_HEAD
    n_blk = t_lat // blk
    l = tt - t_lat
    qw = ATT_KV_HEADS * ATT_GROUPS * ATT_HEAD
    kw = ATT_KV_HEADS * ATT_HEAD
    q_blk = col["q"] * LANES // qw

    def kv(name, off):
        cb = col[name] * LANES // kw
        return pl.BlockSpec((1, blk, kw), lambda b, n: (b, jnp.clip(n + off, 0, n_blk - 1), cb))

    def ctx(name):
        cb = col[name] * LANES // kw
        return pl.BlockSpec((1, l, kw), lambda b, n: (b, t_lat // l, cb))

    def tab(off):
        return pl.BlockSpec((blk, ATT_HEAD), lambda b, n: (jnp.clip(n + off, 0, n_blk - 1), 0))

    in_specs = [pl.BlockSpec(memory_space=pltpu.SMEM),
                pl.BlockSpec((1, blk, qw), lambda b, n: (b, n, q_blk)),
                kv("ak", -1), kv("ak", 0), kv("ak", 1), kv("av", -1), kv("av", 0), kv("av", 1),
                ctx("ak"), ctx("av"), tab(0), tab(0), tab(-1), tab(-1), tab(1), tab(1)]
    return pl.pallas_call(
        functools.partial(_attn_kernel, n_blk=n_blk),
        grid=(bsz, n_blk),
        in_specs=in_specs,
        out_specs=pl.BlockSpec((1, blk, qw), lambda b, n: (b, n, 0)),
        out_shape=jax.ShapeDtypeStruct((bsz, t_lat, qw), BF16),
        compiler_params=_cparams(("arbitrary", "arbitrary")),
        name="window_attention",
    )(sink, proj, proj, proj, proj, proj, proj, proj, proj, proj, cos2, sin2, cos2, sin2, cos2, sin2)


def _merge_kernel(o1_ref, o2_ref, w1_ref, w2_ref, g1_ref, g2_ref, o_ref):
    y1 = _dot(o1_ref[0], w1_ref[...])
    y2 = _dot(o2_ref[0], w2_ref[...])
    o_ref[0] = (jax.nn.sigmoid(g1_ref[0].astype(F32)) * y1
                + jax.nn.sigmoid(g2_ref[0].astype(F32)) * y2).astype(o_ref.dtype)


def _merge(o_rwkv, o_att, w1, w2, gates, tm, tn):
    bsz, t, c = o_rwkv.shape
    n = w1.shape[1]
    gr = 0
    ga = n // tn
    return pl.pallas_call(
        _merge_kernel,
        grid=(bsz, t // tm, n // tn),
        in_specs=[pl.BlockSpec((1, tm, c), lambda b, i, j: (b, i, 0)),
                  pl.BlockSpec((1, tm, c), lambda b, i, j: (b, i, 0)),
                  pl.BlockSpec((c, tn), lambda b, i, j: (0, j)),
                  pl.BlockSpec((c, tn), lambda b, i, j: (0, j)),
                  pl.BlockSpec((1, tm, tn), lambda b, i, j: (b, i, gr + j)),
                  pl.BlockSpec((1, tm, tn), lambda b, i, j: (b, i, ga + j))],
        out_specs=pl.BlockSpec((1, tm, tn), lambda b, i, j: (b, i, j)),
        out_shape=jax.ShapeDtypeStruct((bsz, t, n), BF16),
        compiler_params=_cparams(("arbitrary", "arbitrary", "arbitrary")),
        name="gated_merge",
    )(o_rwkv, o_att, w1, w2, gates, gates)


def _ln_kernel(x_ref, y_ref, gt_ref, g_ref, b_ref, *rest, with_mod):
    bi = pl.program_id(0)
    gt = gt_ref[pl.ds(bi, 1), :]
    z = DEEPNORM_ALPHA * x_ref[0] + gt * y_ref[0]
    mu = jnp.mean(z, axis=-1, keepdims=True)
    zc = z - mu
    var = jnp.mean(zc * zc, axis=-1, keepdims=True)
    out = zc * lax.rsqrt(var + LN_EPS) * g_ref[...] + b_ref[...]
    if with_mod:
        sh_ref, sc_ref, o_ref, h_ref = rest
        o_ref[0] = out
        h_ref[0] = (out * (1.0 + sc_ref[pl.ds(bi, 1), :]) + sh_ref[pl.ds(bi, 1), :]).astype(BF16)
    else:
        (o_ref,) = rest
        o_ref[0] = out


def _res_ln(x, y, mod, gate_col, g, b, tb, mod_cols=None):
    bsz, t, d = x.shape
    blk = pl.BlockSpec((1, tb, d), lambda bi, j: (bi, j, 0))
    vec = pl.BlockSpec((1, d), lambda bi, j: (0, 0))

    def modspec(cb):
        return pl.BlockSpec((MOD_ROWS, d), lambda bi, j: (0, cb))

    in_specs = [blk, blk, modspec(gate_col), vec, vec]
    args = [x, y, mod, g.reshape(1, d), b.reshape(1, d)]
    out_specs = [blk]
    out_shape = [jax.ShapeDtypeStruct((bsz, t, d), F32)]
    if mod_cols is not None:
        in_specs += [modspec(mod_cols[0]), modspec(mod_cols[1])]
        args += [mod, mod]
        out_specs.append(blk)
        out_shape.append(jax.ShapeDtypeStruct((bsz, t, d), BF16))
    return pl.pallas_call(
        functools.partial(_ln_kernel, with_mod=mod_cols is not None),
        grid=(bsz, t // tb),
        in_specs=in_specs,
        out_specs=out_specs,
        out_shape=out_shape,
        compiler_params=_cparams(("arbitrary", "arbitrary")),
        name="residual_layernorm",
    )(*args)


def _ffn_up_kernel(a_ref, wg_ref, wu_ref, o_ref):
    a = a_ref[...]
    gg = _dot(a, wg_ref[...].astype(BF16))
    uu = _dot(a, wu_ref[...].astype(BF16))
    o_ref[...] = (gg * jax.nn.sigmoid(gg) * uu).astype(o_ref.dtype)


def _ffn_up(a, wg, wu, tm, tf):
    m, k = a.shape
    f = wg.shape[1]
    return pl.pallas_call(
        _ffn_up_kernel,
        grid=(m // tm, f // tf),
        in_specs=[pl.BlockSpec((tm, k), lambda i, j: (i, 0)),
                  pl.BlockSpec((k, tf), lambda i, j: (0, j)),
                  pl.BlockSpec((k, tf), lambda i, j: (0, j))],
        out_specs=pl.BlockSpec((tm, tf), lambda i, j: (i, j)),
        out_shape=jax.ShapeDtypeStruct((m, f), BF16),
        compiler_params=_cparams(("arbitrary", "arbitrary")),
        name="swiglu_up",
    )(a, wg, wu)


def _rope_tables(t):
    rows = t // GRID_W
    row = jnp.broadcast_to(jnp.arange(rows, dtype=F32)[:, None], (rows, GRID_W)).reshape(t)
    colp = jnp.broadcast_to(jnp.arange(GRID_W, dtype=F32)[None, :], (rows, GRID_W)).reshape(t)
    axis_dim = ATT_HEAD // 2
    inv = ROPE_BASE ** (-jnp.arange(0, axis_dim, 2, dtype=F32) / axis_dim)
    ang = jnp.concatenate([row[:, None] * inv, colp[:, None] * inv], -1)
    cos, sin = jnp.cos(ang), jnp.sin(ang)
    return jnp.concatenate([cos, cos], -1), jnp.concatenate([-sin, sin], -1)


def _block(x, c, ctx, c_ctx, w_ada, b_ada, w_in, rwkv_shift, rwkv_w0, rwkv_w_up, rwkv_a0, rwkv_a_up,
           rwkv_g_up, rwkv_k_k, rwkv_k_a, rwkv_r_k, rwkv_gn_g, rwkv_gn_b, attn_sink, w_rwkv_o, w_att_o,
           w_out, ln1_g, ln1_b, w_ff_gate, w_ff_up, w_ff_down, ln2_g, ln2_b, tiles):
    bsz, t, d = x.shape
    l = ctx.shape[1]
    tt = t + l
    crw = rwkv_w0.shape[1]
    d_att = ATT_KV_HEADS * ATT_GROUPS * ATT_HEAD
    d_kv = ATT_KV_HEADS * ATT_HEAD
    n_lora = 2 * DECAY_LORA + 2 * ICLR_LORA + GATE_LORA
    n_rw = 3 * crw + n_lora

    cc = jnp.zeros((MOD_ROWS, d), F32).at[:bsz].set(c).at[bsz].set(c_ctx)
    mod = _ada(cc, w_ada, b_ada)

    o_q = n_rw
    o_g = o_q + d_att + 2 * d_kv
    w_in16 = w_in.astype(BF16)
    w_rest = jnp.concatenate([w_in16[:, :3 * crw], w_in16[:, o_q:o_g], w_in16[:, 3 * crw:n_rw]], axis=1)
    w_gates = w_in16[:, o_g:]
    col = {}
    off = 0
    for name, width in (("r", crw), ("k", crw), ("v", crw), ("q", d_att), ("ak", d_kv), ("av", d_kv),
                        ("lora", n_lora)):
        col[name] = off // LANES
        off += width
    shift_rkv = rwkv_shift[:, :3 * crw]
    shift_lora = jnp.concatenate([rwkv_shift[:, 3 * crw:], jnp.zeros((3, LORA_PAD - n_lora), F32)], axis=1)

    h = _prep(x, ctx, mod)
    proj = _mm_wstat(h.reshape(bsz * tt, d), w_rest, tiles["proj_tm"], tiles["proj_tn"], F32, "in_proj")
    proj = proj.reshape(bsz, tt, -1)
    gates = _mm_lead_rows(h, w_gates, t, tiles["gate_tm"], tiles["gate_tn"], BF16, "gate_proj")

    g, bonus, ops, dec = _features(proj, col, shift_rkv, shift_lora, rwkv_w0, rwkv_w_up, rwkv_a0, rwkv_a_up,
                                   rwkv_g_up, rwkv_k_k, rwkv_k_a, rwkv_r_k.reshape(-1), t, tiles["feat_tb"])
    yf, yb = _scan(ops, dec, t, tiles["scan_lanes"])
    o_rwkv = _rwkv_out(yf, yb, bonus, g, rwkv_gn_g, rwkv_gn_b, t, tiles["feat_tb"])

    cos2, sin2 = _rope_tables(t)
    o_att = _attention(proj, col, attn_sink, cos2, sin2, t)

    ym = _merge(o_rwkv, o_att, w_rwkv_o.astype(BF16), w_att_o.astype(BF16), gates,
                tiles["merge_tm"], tiles["merge_tn"])
    yo = _mm_astat(ym.reshape(bsz * t, d), w_out.astype(BF16), tiles["out_tm"], tiles["out_tn"], F32, "out_proj")
    x1, h2 = _res_ln(x, yo.reshape(bsz, t, d), mod, 2, ln1_g, ln1_b, tiles["ln_tb"], mod_cols=(3, 4))

    u = _ffn_up(h2.reshape(bsz * t, d), w_ff_gate, w_ff_up, tiles["ffn_tm"], tiles["ffn_tf"])
    ff = _mm_astat(u, w_ff_down.astype(BF16), tiles["down_tm"], tiles["down_tn"], F32, "swiglu_down")
    (out,) = _res_ln(x1, ff.reshape(bsz, t, d), mod, 5, ln2_g, ln2_b, tiles["ln_tb"])
    return out


_TILES = dict(proj_tm=512, proj_tn=1024, gate_tm=1024, gate_tn=1024, feat_tb=128, scan_lanes=2048, merge_tm=1024, merge_tn=512, out_tm=1024, out_tn=1024,
              ln_tb=256, ffn_tm=2048, ffn_tf=256, down_tm=512, down_tn=512)


def kernel(x, c, ctx, c_ctx, w_ada, b_ada, w_in, rwkv_shift, rwkv_w0, rwkv_w_up, rwkv_a0, rwkv_a_up, rwkv_g_up, rwkv_k_k, rwkv_k_a, rwkv_r_k, rwkv_gn_g, rwkv_gn_b, attn_sink, w_rwkv_o, w_att_o, w_out, ln1_g, ln1_b, w_ff_gate, w_ff_up, w_ff_down, ln2_g, ln2_b):
    assert w_ada.shape[0] == DEPTH
    return _block(x, c, ctx, c_ctx, w_ada[0], b_ada[0], w_in[0], rwkv_shift[0], rwkv_w0[0], rwkv_w_up[0],
                  rwkv_a0[0], rwkv_a_up[0], rwkv_g_up[0], rwkv_k_k[0], rwkv_k_a[0], rwkv_r_k[0], rwkv_gn_g[0],
                  rwkv_gn_b[0], attn_sink[0], w_rwkv_o[0], w_att_o[0], w_out[0], ln1_g[0], ln1_b[0],
                  w_ff_gate[0], w_ff_up[0], w_ff_down[0], ln2_g[0], ln2_b[0], _TILES)
```

```python
import functools

import jax
import jax.numpy as jnp
import numpy as np
from jax import lax
from jax.experimental import pallas as pl
from jax.experimental.pallas import tpu as pltpu

F32 = jnp.float32
BF16 = jnp.bfloat16
HIGHEST = lax.Precision.HIGHEST

RWKV_HEAD = 64
DECAY_LORA = 96
ICLR_LORA = 96
GATE_LORA = 256
GN_EPS = 64e-5
ATT_HEAD = 128
ATT_KV_HEADS = 4
ATT_GROUPS = 4
GRID_W = 64
ROPE_BASE = 10000.0
LN_EPS = 1e-5
DEPTH = 1
DEEPNORM_ALPHA = (2 * DEPTH) ** 0.25
NEG_INF = -1e30

LANES = 128
SUBLANES = 8
VMEM_LIMIT = 56 * 1024 * 1024
MOD_ROWS = 8
SCAN_CHUNK = 64
SCAN_LANES = 256
SCAN_PLANES = 5
LORA_PAD = 1024


def _cparams(sem):
    return pltpu.CompilerParams(dimension_semantics=sem, vmem_limit_bytes=VMEM_LIMIT)


def _dot(a, b, precision=None):
    return jnp.dot(a, b, preferred_element_type=F32, precision=precision)


def _dot_nt(a, b, precision=None):
    return lax.dot_general(a, b, (((1,), (1,)), ((), ())), preferred_element_type=F32, precision=precision)


def _dot_tn(a, b, precision=None):
    return lax.dot_general(a, b, (((0,), (0,)), ((), ())), preferred_element_type=F32, precision=precision)


def _split2(x):
    hi = x.astype(BF16)
    return hi, (x - hi.astype(F32)).astype(BF16)


def _head_sum(x, ones_bd):
    rows = x.shape[0]
    hi, lo = _split2(x)
    both = jnp.concatenate([hi, lo], axis=0)
    out = []
    for s in range(x.shape[1] // SCAN_LANES):
        part = _dot(both[:, s * SCAN_LANES:(s + 1) * SCAN_LANES], ones_bd)
        out.append(part[:rows] + part[rows:])
    return jnp.concatenate(out, axis=1)


def _ada_kernel(c_ref, w_ref, b_ref, o_ref):
    a = c_ref[...]
    a = a * jax.nn.sigmoid(a)
    a_hi, a_lo = _split2(a)
    w_hi, w_lo = _split2(w_ref[...])
    part = _dot(jnp.concatenate([a_hi, a_lo], axis=0), w_hi)
    o_ref[...] = part[:MOD_ROWS] + part[MOD_ROWS:] + _dot(a_hi, w_lo) + b_ref[...]


def _ada(cc, w_ada, b_ada):
    d, n = w_ada.shape
    tn = 512
    return pl.pallas_call(
        _ada_kernel,
        grid=(n // tn,),
        in_specs=[pl.BlockSpec((MOD_ROWS, d), lambda j: (0, 0)),
                  pl.BlockSpec((d, tn), lambda j: (0, j)),
                  pl.BlockSpec((1, tn), lambda j: (0, j))],
        out_specs=pl.BlockSpec((MOD_ROWS, tn), lambda j: (0, j)),
        out_shape=jax.ShapeDtypeStruct((MOD_ROWS, n), F32),
        compiler_params=_cparams(("arbitrary",)),
        name="ada",
    )(cc, w_ada, b_ada.reshape(1, n))


def _prep_kernel(x_ref, ctx_ref, sh_ref, sc_ref, o_ref, *, n_lat, ctx_row):
    b = pl.program_id(0)
    j = pl.program_id(1)
    is_ctx = j >= n_lat
    row = jnp.where(is_ctx, ctx_row, b)
    sh = sh_ref[pl.ds(row, 1), :]
    sc = sc_ref[pl.ds(row, 1), :]
    xin = jnp.where(is_ctx, ctx_ref[0], x_ref[0])
    o_ref[0] = (xin * (1.0 + sc) + sh).astype(BF16)


def _prep(x, ctx, mod):
    bsz, t, d = x.shape
    l = ctx.shape[1]
    tb = l
    n_lat = t // tb
    return pl.pallas_call(
        functools.partial(_prep_kernel, n_lat=n_lat, ctx_row=bsz),
        grid=(bsz, n_lat + 1),
        in_specs=[pl.BlockSpec((1, tb, d), lambda b, j: (b, jnp.minimum(j, n_lat - 1), 0)),
                  pl.BlockSpec((1, l, d), lambda b, j: (b, 0, 0)),
                  pl.BlockSpec((MOD_ROWS, d), lambda b, j: (0, 0)),
                  pl.BlockSpec((MOD_ROWS, d), lambda b, j: (0, 1))],
        out_specs=pl.BlockSpec((1, tb, d), lambda b, j: (b, j, 0)),
        out_shape=jax.ShapeDtypeStruct((bsz, t + l, d), BF16),
        compiler_params=_cparams(("arbitrary", "arbitrary")),
        name="prep",
    )(x, ctx, mod, mod)


def _cast_kernel(w_ref, o_ref):
    o_ref[...] = w_ref[...].astype(o_ref.dtype)


def _stage_columns(w, segments, name):
    rows = w.shape[0]
    starts, src = [], []
    off = 0
    for start, width in segments:
        assert start % LANES == 0 and width % LANES == 0
        starts.append(off // LANES)
        src.append(start // LANES)
        off += width

    def src_block(j):
        blk = j - starts[0] + src[0]
        for s0, b0 in zip(starts[1:], src[1:]):
            blk = jnp.where(j >= s0, j - s0 + b0, blk)
        return blk

    return pl.pallas_call(
        _cast_kernel,
        grid=(off // LANES,),
        in_specs=[pl.BlockSpec((rows, LANES), lambda j: (0, src_block(j)))],
        out_specs=pl.BlockSpec((rows, LANES), lambda j: (0, j)),
        out_shape=jax.ShapeDtypeStruct((rows, off), BF16),
        compiler_params=_cparams(("arbitrary",)),
        name=name,
    )(w)


def _mm_kernel(a_ref, w_ref, o_ref):
    o_ref[...] = _dot(a_ref[...], w_ref[...]).astype(o_ref.dtype)


def _mm_wstat(a, w, tm, tn, out_dtype, name):
    m, k = a.shape
    n = w.shape[1]
    return pl.pallas_call(
        _mm_kernel,
        grid=(pl.cdiv(n, tn), m // tm),
        in_specs=[pl.BlockSpec((tm, k), lambda j, i: (i, 0)),
                  pl.BlockSpec((k, tn), lambda j, i: (0, j))],
        out_specs=pl.BlockSpec((tm, tn), lambda j, i: (i, j)),
        out_shape=jax.ShapeDtypeStruct((m, n), out_dtype),
        compiler_params=_cparams(("arbitrary", "arbitrary")),
        name=name,
    )(a, w)


def _mm_rows_kernel(a_ref, w_ref, o_ref):
    o_ref[0] = _dot(a_ref[0], w_ref[...]).astype(o_ref.dtype)


def _mm_lead_rows(a, w, rows, tm, tn, out_dtype, name):
    bsz, _, k = a.shape
    n = w.shape[1]
    return pl.pallas_call(
        _mm_rows_kernel,
        grid=(n // tn, bsz, rows // tm),
        in_specs=[pl.BlockSpec((1, tm, k), lambda j, b, i: (b, i, 0)),
                  pl.BlockSpec((k, tn), lambda j, b, i: (0, j))],
        out_specs=pl.BlockSpec((1, tm, tn), lambda j, b, i: (b, i, j)),
        out_shape=jax.ShapeDtypeStruct((bsz, rows, n), out_dtype),
        compiler_params=_cparams(("arbitrary", "arbitrary", "arbitrary")),
        name=name,
    )(a, w)


def _mm_astat(a, w, tm, tn, out_dtype, name):
    m, k = a.shape
    n = w.shape[1]
    return pl.pallas_call(
        _mm_kernel,
        grid=(m // tm, n // tn),
        in_specs=[pl.BlockSpec((tm, k), lambda i, j: (i, 0)),
                  pl.BlockSpec((k, tn), lambda i, j: (0, j))],
        out_specs=pl.BlockSpec((tm, tn), lambda i, j: (i, j)),
        out_shape=jax.ShapeDtypeStruct((m, n), out_dtype),
        compiler_params=_cparams(("arbitrary", "arbitrary")),
        name=name,
    )(a, w)


def _mm_acc_kernel(a_ref, w_ref, o_ref, acc_ref):
    kk = pl.program_id(2)

    @pl.when(kk == 0)
    def _():
        acc_ref[...] = jnp.zeros_like(acc_ref)

    acc_ref[...] += _dot(a_ref[...], w_ref[...])

    @pl.when(kk == pl.num_programs(2) - 1)
    def _():
        o_ref[...] = acc_ref[...].astype(o_ref.dtype)


def _mm_ktiled(a, w, tm, tn, tk, out_dtype, name):
    m, k = a.shape
    n = w.shape[1]
    return pl.pallas_call(
        _mm_acc_kernel,
        grid=(m // tm, n // tn, k // tk),
        in_specs=[pl.BlockSpec((tm, tk), lambda i, j, q: (i, q)),
                  pl.BlockSpec((tk, tn), lambda i, j, q: (q, j))],
        out_specs=pl.BlockSpec((tm, tn), lambda i, j, q: (i, j)),
        out_shape=jax.ShapeDtypeStruct((m, n), out_dtype),
        scratch_shapes=[pltpu.VMEM((tm, tn), F32)],
        compiler_params=_cparams(("arbitrary", "arbitrary", "arbitrary")),
        name=name,
    )(a, w)


def _conv3(prev_ref, cur_ref, next_ref, w_ref, at_start, at_end):
    cur = cur_ref[0]
    rows = cur.shape[0]
    prev_row = jnp.where(at_start, 0.0, prev_ref[0][SUBLANES - 1:SUBLANES, :])
    next_row = jnp.where(at_end, 0.0, next_ref[0][0:1, :])
    up = jnp.concatenate([prev_row, cur[:rows - 1]], axis=0)
    down = jnp.concatenate([cur[1:], next_row], axis=0)
    w = w_ref[...]
    return up * w[0:1] + cur * w[1:2] + down * w[2:3]


def _feat_kernel(rp, rc, rn, kp, kc, kn, vp, vc, vn, lp, lc, ln_,
                 shr, shk, shv, shl, w0, wup, a0, aup, gup, kk_w, ka_w, rk_w, ones_ref, cm_ref,
                 g_o, bonus_o, ops_o, dec_o, *, n_lat, n_tot, chunk):
    j = pl.program_id(1)
    at_start = jnp.logical_or(j == 0, j == n_lat)
    at_end = jnp.logical_or(j == n_lat - 1, j == n_tot - 1)
    r = _conv3(rp, rc, rn, shr, at_start, at_end)
    k = _conv3(kp, kc, kn, shk, at_start, at_end)
    v = _conv3(vp, vc, vn, shv, at_start, at_end)
    lo = _conv3(lp, lc, ln_, shl, at_start, at_end)
    rows = r.shape[0]
    v_h = v.astype(BF16)
    gd = lo[:, 2 * DECAY_LORA + 2 * ICLR_LORA:2 * DECAY_LORA + 2 * ICLR_LORA + GATE_LORA]
    ones_bd = ones_ref[...]
    g_o[0] = _dot(jax.nn.sigmoid(gd).astype(BF16), gup[...])
    kk = k * kk_w[...]
    kk = kk / jnp.maximum(jnp.sqrt(_head_sum(kk * kk, ones_bd)), 1e-12)
    k_sum = jnp.zeros_like(k)
    for d in range(2):
        wd = lo[:, d * DECAY_LORA:(d + 1) * DECAY_LORA]
        ad = lo[:, 2 * DECAY_LORA + d * ICLR_LORA:2 * DECAY_LORA + (d + 1) * ICLR_LORA]
        z = w0[d:d + 1, :] + _dot(jnp.tanh(wd).astype(BF16), wup[d])
        lw = jax.nn.sigmoid(z) * (-np.exp(-0.5))
        iclr = jax.nn.sigmoid(a0[d:d + 1, :] + _dot(ad.astype(BF16), aup[d]))
        k_dir = k * (1.0 + (iclr - 1.0) * ka_w[...])
        k_sum = k_sum + k_dir
        lw_hi, lw_lo = _split2(lw)
        cum = _dot(cm_ref[d], lw_hi) + _dot(cm_ref[d], lw_lo)
        p_inv = jnp.exp(-cum)
        planes = [(kk * jnp.exp(cum - lw)).astype(BF16), (r * jnp.exp(cum)).astype(BF16),
                  (k_dir * p_inv).astype(BF16), (kk * iclr * p_inv).astype(BF16), v_h]
        for hg in range(r.shape[1] // SCAN_LANES):
            for p, plane in enumerate(planes):
                dst = (hg * SCAN_PLANES + p) * SCAN_LANES
                ops_o[d, 0, :, dst:dst + SCAN_LANES] = plane[:, hg * SCAN_LANES:(hg + 1) * SCAN_LANES]
        for q in range(rows // chunk):
            last = (q + 1) * chunk - 1 if d == 0 else q * chunk
            dec_o[d, 0, q] = jnp.exp(cum[last:last + 1, :])
    bonus_o[0] = _head_sum(r * k_sum * rk_w[...], ones_bd) * v


def _features(proj, col, shift_rkv, shift_lora, w0, w_up, a0, a_up, g_up, k_k, k_a, r_k, t_lat, tb):
    bsz, tt, _ = proj.shape
    c = w0.shape[1]
    chunk = SCAN_CHUNK
    n_tot = tt // tb
    n_lat = t_lat // tb
    hb = tb // SUBLANES
    n_h = tt // SUBLANES
    li = np.arange(SCAN_LANES) // RWKV_HEAD
    ones_bd = jnp.asarray(li[:, None] == li[None, :], BF16)
    ti = np.arange(tb)
    same = (ti[:, None] // chunk) == (ti[None, :] // chunk)
    cmask = jnp.asarray(np.stack([same & (ti[None, :] <= ti[:, None]), same & (ti[None, :] >= ti[:, None])]), BF16)

    def main(cb, width):
        blk = cb * LANES // width
        return pl.BlockSpec((1, tb, width), lambda b, j: (b, j, blk))

    def prev(cb, width):
        blk = cb * LANES // width
        return pl.BlockSpec((1, SUBLANES, width), lambda b, j: (b, jnp.maximum(j * hb - 1, 0), blk))

    def nxt(cb, width):
        blk = cb * LANES // width
        return pl.BlockSpec((1, SUBLANES, width), lambda b, j: (b, jnp.minimum((j + 1) * hb, n_h - 1), blk))

    def const(shape):
        nd = len(shape)
        return pl.BlockSpec(shape, lambda b, j: (0,) * nd)

    in_specs = []
    for name, width in (("r", c), ("k", c), ("v", c), ("lora", LORA_PAD)):
        in_specs += [prev(col[name], width), main(col[name], width), nxt(col[name], width)]
    in_specs += [const((3, c)), const((3, c)), const((3, c)), const((3, LORA_PAD)),
                 const((2, c)), const((2, DECAY_LORA, c)), const((2, c)), const((2, ICLR_LORA, c)),
                 const((GATE_LORA, c)), const((1, c)), const((1, c)), const((1, c)),
                 const((SCAN_LANES, SCAN_LANES)), const((2, tb, tb))]
    one = pl.BlockSpec((1, tb, c), lambda b, j: (b, j, 0))
    ops = pl.BlockSpec((2, 1, tb, SCAN_PLANES * c), lambda b, j: (0, b, j, 0))
    dec = pl.BlockSpec((2, 1, tb // chunk, 1, c), lambda b, j: (0, b, j, 0, 0))
    s1 = jax.ShapeDtypeStruct((bsz, tt, c), F32)
    sops = jax.ShapeDtypeStruct((2, bsz, tt, SCAN_PLANES * c), BF16)
    sdec = jax.ShapeDtypeStruct((2, bsz, tt // chunk, 1, c), F32)
    args = [proj] * 12 + [shift_rkv[:, :c], shift_rkv[:, c:2 * c], shift_rkv[:, 2 * c:], shift_lora,
                          w0, w_up.astype(BF16), a0, a_up.astype(BF16), g_up.astype(BF16),
                          k_k.reshape(1, c), k_a.reshape(1, c), r_k.reshape(1, c), ones_bd, cmask]
    return pl.pallas_call(
        functools.partial(_feat_kernel, n_lat=n_lat, n_tot=n_tot, chunk=chunk),
        grid=(bsz, n_tot),
        in_specs=in_specs,
        out_specs=[one, one, ops, dec],
        out_shape=[s1, s1, sops, sdec],
        compiler_params=_cparams(("arbitrary", "arbitrary")),
        name="rwkv_features",
    )(*args)


def _scan_kernel(gm_ref, bdm_ref, hm_ref, eye_ref, *refs, chunk, nsub):
    ops_refs = refs[0:4:2]
    dec_refs = refs[1:4:2]
    y_refs = refs[4:6]
    s_ref = refs[6]
    heads = SCAN_LANES // RWKV_HEAD

    @pl.when(pl.program_id(2) == 0)
    def _():
        s_ref[...] = jnp.zeros_like(s_ref)

    bdm = bdm_ref[...]
    bdm_h = bdm.astype(BF16)
    eye = eye_ref[...]
    hms = [hm_ref[h] for h in range(heads)]

    def rows_bd(m):
        return jnp.concatenate([m * hms[h] for h in range(heads)], axis=0)

    def blocks_bd(m):
        return jnp.concatenate([m] * heads, axis=0) * bdm_h

    steps = chunk.bit_length() - 2
    units = [(d, q) for d in range(2) for q in range(nsub)]
    n_u = len(units)

    def lanes(q):
        return slice(q * SCAN_LANES, (q + 1) * SCAN_LANES)

    def plane(d, q, p):
        return ops_refs[d][0, 0, :, lanes(q * SCAN_PLANES + p)]

    kt = [plane(d, q, 2) for d, q in units]
    bt = [plane(d, q, 3) for d, q in units]
    v = [plane(d, q, 4) for d, q in units]
    x = [jnp.concatenate([plane(d, q, 0), plane(d, q, 1)], axis=0) for d, q in units]
    g = [_dot_nt(x[i], jnp.concatenate([rows_bd(kt[i]), rows_bd(bt[i])], axis=0)) * gm_ref[units[i][0]]
         for i in range(n_u)]
    s_old = [s_ref[i] for i in range(n_u)]
    xs = [_dot_nt(x[i], s_old[i].astype(BF16)) for i in range(n_u)]
    gv = [_dot(g[i][:, :SCAN_LANES].astype(BF16), rows_bd(v[i])) for i in range(n_u)]
    rhs = [xs[i][:chunk] + gv[i][:chunk] for i in range(n_u)]
    t = [eye - g[i][:chunk, SCAN_LANES:] for i in range(n_u)]
    lm = [g[i][:chunk, SCAN_LANES:].astype(BF16) for i in range(n_u)]
    pw = [_dot(lm[i], blocks_bd(lm[i])).astype(BF16) for i in range(n_u)]
    for _ in range(steps - 1):
        both = [_dot(jnp.concatenate([t[i].astype(BF16), pw[i]], axis=0), blocks_bd(pw[i])) for i in range(n_u)]
        t = [t[i] + both[i][:chunk] for i in range(n_u)]
        pw = [both[i][chunk:].astype(BF16) for i in range(n_u)]
    t = [t[i] + _dot(t[i].astype(BF16), blocks_bd(pw[i])) for i in range(n_u)]
    u = [(-_dot(t[i].astype(BF16), rows_bd(rhs[i].astype(BF16)))).astype(BF16) for i in range(n_u)]
    for i, (d, q) in enumerate(units):
        y_refs[d][0, :, lanes(q)] = (xs[i][chunk:] + gv[i][chunk:]
                                     + _dot(g[i][chunk:, SCAN_LANES:].astype(BF16), rows_bd(u[i])))
    for i, (d, q) in enumerate(units):
        upd = _dot_tn(jnp.concatenate([v[i], u[i]], axis=0), jnp.concatenate([kt[i], bt[i]], axis=0))
        s_ref[i] = (s_old[i] + upd) * dec_refs[d][0, 0, 0, :, lanes(q)] * bdm


def _scan(ops, dec, t_lat, lane_block):
    _, bsz, tt, c = ops.shape
    c //= SCAN_PLANES
    chunk = SCAN_CHUNK
    heads = SCAN_LANES // RWKV_HEAD
    n_tot = tt // chunk
    n_lat = t_lat // chunk
    n_ctx = n_tot - n_lat
    nsub = lane_block // SCAN_LANES

    ti = np.arange(chunk)
    before = [ti[None, :] < ti[:, None], ti[None, :] > ti[:, None]]
    gmask = np.stack([np.concatenate([np.tile(before[d], (1, 2 * heads)),
                                      np.tile(before[d] | np.eye(chunk, dtype=bool), (1, 2 * heads))], axis=0)
                      for d in range(2)]).astype(np.float32)
    li = np.arange(SCAN_LANES) // RWKV_HEAD
    bdm = (li[:, None] == li[None, :]).astype(np.float32)
    hmask = jnp.asarray((li[None, None, :] == np.arange(heads)[:, None, None]), BF16)
    eye = np.tile(np.eye(chunk, dtype=np.float32), (1, heads))

    def chunk_index(d, s):
        return jnp.where(s < n_ctx, n_lat + s, s - n_ctx) if d == 0 else n_tot - 1 - s

    def const(shape):
        nd = len(shape)
        return pl.BlockSpec(shape, lambda b, h, s: (0,) * nd)

    in_specs = [const(gmask.shape), const(bdm.shape), const(hmask.shape), const(eye.shape)]
    args = [jnp.asarray(gmask), jnp.asarray(bdm), hmask, jnp.asarray(eye)]
    out_specs = []
    for d in range(2):
        in_specs += [pl.BlockSpec((1, 1, chunk, SCAN_PLANES * lane_block),
                                  lambda b, h, s, d=d: (d, b, chunk_index(d, s), h)),
                     pl.BlockSpec((1, 1, 1, 1, lane_block), lambda b, h, s, d=d: (d, b, chunk_index(d, s), 0, h))]
        args += [ops, dec]
        out_specs.append(pl.BlockSpec((1, chunk, lane_block), lambda b, h, s, d=d: (b, chunk_index(d, s), h)))
    ys = jax.ShapeDtypeStruct((bsz, tt, c), F32)
    return pl.pallas_call(
        functools.partial(_scan_kernel, chunk=chunk, nsub=nsub),
        grid=(bsz, c // lane_block, n_tot),
        in_specs=in_specs,
        out_specs=out_specs,
        out_shape=[ys, ys],
        scratch_shapes=[pltpu.VMEM((2 * nsub, SCAN_LANES, SCAN_LANES), F32)],
        compiler_params=_cparams(("arbitrary", "arbitrary", "arbitrary")),
        name="rwkv_scan",
    )(*args)


def _rout_kernel(yf, yb, bonus, g, gng, gnb, ones_ref, o_ref):
    ones_bd = ones_ref[...]
    inv_n = 1.0 / RWKV_HEAD
    y = yf[0] + yb[0]
    mu = _head_sum(y, ones_bd) * inv_n
    yc = y - mu
    var = _head_sum(yc * yc, ones_bd) * inv_n
    yn = yc * lax.rsqrt(var + GN_EPS) * gng[...] + gnb[...]
    o_ref[0] = ((yn + bonus[0]) * g[0]).astype(o_ref.dtype)


def _rwkv_out(yf, yb, bonus, g, gn_g, gn_b, t_lat, tb):
    bsz, _, c = g.shape
    li = np.arange(SCAN_LANES) // RWKV_HEAD
    ones_bd = jnp.asarray(li[:, None] == li[None, :], BF16)
    one = pl.BlockSpec((1, tb, c), lambda b, j: (b, j, 0))

    def const(shape):
        return pl.BlockSpec(shape, lambda b, j: (0, 0))

    return pl.pallas_call(
        _rout_kernel,
        grid=(bsz, t_lat // tb),
        in_specs=[one, one, one, one, const((1, c)), const((1, c)), const((SCAN_LANES, SCAN_LANES))],
        out_specs=one,
        out_shape=jax.ShapeDtypeStruct((bsz, t_lat, c), BF16),
        compiler_params=_cparams(("arbitrary", "arbitrary")),
        name="rwkv_out",
    )(yf, yb, bonus, g, gn_g.reshape(1, c), gn_b.reshape(1, c), ones_bd)


def _rope(x, c, s):
    return x * c + pltpu.roll(x, ATT_HEAD // 2, 1) * s


def _attn_kernel(sink_ref, q_ref, km_ref, k0_ref, kp_ref, vm_ref, v0_ref, vp_ref, kc_ref, vc_ref,
                 c0_ref, s0_ref, cm_ref, sm_ref, cp_ref, sp_ref, o_ref, *, n_blk):
    n = pl.program_id(1)
    blk = ATT_HEAD
    scale = ATT_HEAD ** -0.5
    kvh = range(ATT_KV_HEADS)
    c0, s0 = c0_ref[...], s0_ref[...]
    cm, sm = cm_ref[...], sm_ref[...]
    cp, sp = cp_ref[...], sp_ref[...]

    def head(ref, i):
        return ref[0, :, i * ATT_HEAD:(i + 1) * ATT_HEAD]

    qq = [jnp.concatenate([_rope(head(q_ref, g * ATT_GROUPS + h), c0, s0) for h in range(ATT_GROUPS)],
                          axis=0).astype(BF16) for g in kvh]
    kw = [jnp.concatenate([_rope(head(km_ref, g), cm, sm), _rope(head(k0_ref, g), c0, s0),
                           _rope(head(kp_ref, g), cp, sp)], axis=0).astype(BF16) for g in kvh]
    vw = [jnp.concatenate([head(vm_ref, g), head(v0_ref, g), head(vp_ref, g)], axis=0).astype(BF16) for g in kvh]
    kc = [head(kc_ref, g).astype(BF16) for g in kvh]
    vc = [head(vc_ref, g).astype(BF16) for g in kvh]
    ii = lax.broadcasted_iota(jnp.int32, (ATT_GROUPS * blk, 3 * blk), 0) % blk
    jj = lax.broadcasted_iota(jnp.int32, (ATT_GROUPS * blk, 3 * blk), 1)
    in_prev = jnp.logical_and(jnp.logical_and(jj < blk, jj >= ii), n > 0)
    in_self = jnp.logical_and(jj >= blk, jj < 2 * blk)
    in_next = jnp.logical_and(jnp.logical_and(jj >= 2 * blk, jj - 2 * blk <= ii), n < n_blk - 1)
    valid = jnp.logical_or(jnp.logical_or(in_prev, in_self), in_next)
    s_w = [jnp.where(valid, _dot_nt(qq[g], kw[g]) * scale, NEG_INF) for g in kvh]
    s_c = [_dot_nt(qq[g], kc[g]) * scale for g in kvh]
    s_s = [jnp.concatenate([jnp.full((blk, 1), sink_ref[g * ATT_GROUPS + h], F32) for h in range(ATT_GROUPS)],
                           axis=0) for g in kvh]
    m = [jnp.maximum(jnp.maximum(jnp.max(s_w[g], axis=-1, keepdims=True),
                                 jnp.max(s_c[g], axis=-1, keepdims=True)), s_s[g]) for g in kvh]
    p_w = [jnp.exp(s_w[g] - m[g]) for g in kvh]
    p_c = [jnp.exp(s_c[g] - m[g]) for g in kvh]
    den = [jnp.sum(p_w[g], axis=-1, keepdims=True) + jnp.sum(p_c[g], axis=-1, keepdims=True)
           + jnp.exp(s_s[g] - m[g]) for g in kvh]
    o = [(_dot(p_w[g].astype(BF16), vw[g]) + _dot(p_c[g].astype(BF16), vc[g])) / den[g] for g in kvh]
    o_ref[0] = jnp.concatenate([o[g][h * blk:(h + 1) * blk] for g in kvh for h in range(ATT_GROUPS)],
                               axis=1).astype(o_ref.dtype)


def _attention(proj, col, sink, cos2, sin2, t_lat):
    bsz, tt, _ = proj.shape
    blk = ATT_HEAD
    n_blk = t_lat // blk
    l = tt - t_lat
    qw = ATT_KV_HEADS * ATT_GROUPS * ATT_HEAD
    kw = ATT_KV_HEADS * ATT_HEAD
    q_blk = col["q"] * LANES // qw

    def kv(name, off):
        cb = col[name] * LANES // kw
        return pl.BlockSpec((1, blk, kw), lambda b, n: (b, jnp.clip(n + off, 0, n_blk - 1), cb))

    def ctx(name):
        cb = col[name] * LANES // kw
        return pl.BlockSpec((1, l, kw), lambda b, n: (b, t_lat // l, cb))

    def tab(off):
        return pl.BlockSpec((blk, ATT_HEAD), lambda b, n: (jnp.clip(n + off, 0, n_blk - 1), 0))

    in_specs = [pl.BlockSpec(memory_space=pltpu.SMEM),
                pl.BlockSpec((1, blk, qw), lambda b, n: (b, n, q_blk)),
                kv("ak", -1), kv("ak", 0), kv("ak", 1), kv("av", -1), kv("av", 0), kv("av", 1),
                ctx("ak"), ctx("av"), tab(0), tab(0), tab(-1), tab(-1), tab(1), tab(1)]
    return pl.pallas_call(
        functools.partial(_attn_kernel, n_blk=n_blk),
        grid=(bsz, n_blk),
        in_specs=in_specs,
        out_specs=pl.BlockSpec((1, blk, qw), lambda b, n: (b, n, 0)),
        out_shape=jax.ShapeDtypeStruct((bsz, t_lat, qw), BF16),
        compiler_params=_cparams(("arbitrary", "arbitrary")),
        name="window_attention",
    )(sink, proj, proj, proj, proj, proj, proj, proj, proj, proj, cos2, sin2, cos2, sin2, cos2, sin2)


def _merge_kernel(o1_ref, o2_ref, w1_ref, w2_ref, g1_ref, g2_ref, o_ref):
    y1 = _dot(o1_ref[0], w1_ref[...])
    y2 = _dot(o2_ref[0], w2_ref[...])
    o_ref[0] = (jax.nn.sigmoid(g1_ref[0].astype(F32)) * y1
                + jax.nn.sigmoid(g2_ref[0].astype(F32)) * y2).astype(o_ref.dtype)


def _merge(o_rwkv, o_att, w1, w2, gates, tm, tn):
    bsz, t, c = o_rwkv.shape
    n = w1.shape[1]
    gr = 0
    ga = n // tn
    return pl.pallas_call(
        _merge_kernel,
        grid=(bsz, t // tm, n // tn),
        in_specs=[pl.BlockSpec((1, tm, c), lambda b, i, j: (b, i, 0)),
                  pl.BlockSpec((1, tm, c), lambda b, i, j: (b, i, 0)),
                  pl.BlockSpec((c, tn), lambda b, i, j: (0, j)),
                  pl.BlockSpec((c, tn), lambda b, i, j: (0, j)),
                  pl.BlockSpec((1, tm, tn), lambda b, i, j: (b, i, gr + j)),
                  pl.BlockSpec((1, tm, tn), lambda b, i, j: (b, i, ga + j))],
        out_specs=pl.BlockSpec((1, tm, tn), lambda b, i, j: (b, i, j)),
        out_shape=jax.ShapeDtypeStruct((bsz, t, n), BF16),
        compiler_params=_cparams(("arbitrary", "arbitrary", "arbitrary")),
        name="gated_merge",
    )(o_rwkv, o_att, w1, w2, gates, gates)


def _ln_kernel(x_ref, y_ref, gt_ref, g_ref, b_ref, *rest, with_mod):
    bi = pl.program_id(0)
    gt = gt_ref[pl.ds(bi, 1), :]
    z = DEEPNORM_ALPHA * x_ref[0] + gt * y_ref[0]
    mu = jnp.mean(z, axis=-1, keepdims=True)
    zc = z - mu
    var = jnp.mean(zc * zc, axis=-1, keepdims=True)
    out = zc * lax.rsqrt(var + LN_EPS) * g_ref[...] + b_ref[...]
    if with_mod:
        sh_ref, sc_ref, o_ref, h_ref = rest
        o_ref[0] = out
        h_ref[0] = (out * (1.0 + sc_ref[pl.ds(bi, 1), :]) + sh_ref[pl.ds(bi, 1), :]).astype(BF16)
    else:
        (o_ref,) = rest
        o_ref[0] = out


def _res_ln(x, y, mod, gate_col, g, b, tb, mod_cols=None):
    bsz, t, d = x.shape
    blk = pl.BlockSpec((1, tb, d), lambda bi, j: (bi, j, 0))
    vec = pl.BlockSpec((1, d), lambda bi, j: (0, 0))

    def modspec(cb):
        return pl.BlockSpec((MOD_ROWS, d), lambda bi, j: (0, cb))

    in_specs = [blk, blk, modspec(gate_col), vec, vec]
    args = [x, y, mod, g.reshape(1, d), b.reshape(1, d)]
    out_specs = [blk]
    out_shape = [jax.ShapeDtypeStruct((bsz, t, d), F32)]
    if mod_cols is not None:
        in_specs += [modspec(mod_cols[0]), modspec(mod_cols[1])]
        args += [mod, mod]
        out_specs.append(blk)
        out_shape.append(jax.ShapeDtypeStruct((bsz, t, d), BF16))
    return pl.pallas_call(
        functools.partial(_ln_kernel, with_mod=mod_cols is not None),
        grid=(bsz, t // tb),
        in_specs=in_specs,
        out_specs=out_specs,
        out_shape=out_shape,
        compiler_params=_cparams(("arbitrary", "arbitrary")),
        name="residual_layernorm",
    )(*args)


def _ffn_up_kernel(a_ref, wg_ref, wu_ref, o_ref):
    a = a_ref[...]
    gg = _dot(a, wg_ref[...].astype(BF16))
    uu = _dot(a, wu_ref[...].astype(BF16))
    o_ref[...] = (gg * jax.nn.sigmoid(gg) * uu).astype(o_ref.dtype)


def _ffn_up(a, wg, wu, tm, tf):
    m, k = a.shape
    f = wg.shape[1]
    return pl.pallas_call(
        _ffn_up_kernel,
        grid=(m // tm, f // tf),
        in_specs=[pl.BlockSpec((tm, k), lambda i, j: (i, 0)),
                  pl.BlockSpec((k, tf), lambda i, j: (0, j)),
                  pl.BlockSpec((k, tf), lambda i, j: (0, j))],
        out_specs=pl.BlockSpec((tm, tf), lambda i, j: (i, j)),
        out_shape=jax.ShapeDtypeStruct((m, f), BF16),
        compiler_params=_cparams(("arbitrary", "arbitrary")),
        name="swiglu_up",
    )(a, wg, wu)


def _rope_tables(t):
    rows = t // GRID_W
    row = jnp.broadcast_to(jnp.arange(rows, dtype=F32)[:, None], (rows, GRID_W)).reshape(t)
    colp = jnp.broadcast_to(jnp.arange(GRID_W, dtype=F32)[None, :], (rows, GRID_W)).reshape(t)
    axis_dim = ATT_HEAD // 2
    inv = ROPE_BASE ** (-jnp.arange(0, axis_dim, 2, dtype=F32) / axis_dim)
    ang = jnp.concatenate([row[:, None] * inv, colp[:, None] * inv], -1)
    cos, sin = jnp.cos(ang), jnp.sin(ang)
    return jnp.concatenate([cos, cos], -1), jnp.concatenate([-sin, sin], -1)


def _block(x, c, ctx, c_ctx, w_ada, b_ada, w_in, rwkv_shift, rwkv_w0, rwkv_w_up, rwkv_a0, rwkv_a_up,
           rwkv_g_up, rwkv_k_k, rwkv_k_a, rwkv_r_k, rwkv_gn_g, rwkv_gn_b, attn_sink, w_rwkv_o, w_att_o,
           w_out, ln1_g, ln1_b, w_ff_gate, w_ff_up, w_ff_down, ln2_g, ln2_b, tiles):
    bsz, t, d = x.shape
    l = ctx.shape[1]
    tt = t + l
    crw = rwkv_w0.shape[1]
    d_att = ATT_KV_HEADS * ATT_GROUPS * ATT_HEAD
    d_kv = ATT_KV_HEADS * ATT_HEAD
    n_lora = 2 * DECAY_LORA + 2 * ICLR_LORA + GATE_LORA
    n_rw = 3 * crw + n_lora

    cc = jnp.zeros((MOD_ROWS, d), F32).at[:bsz].set(c).at[bsz].set(c_ctx)
    mod = _ada(cc, w_ada, b_ada)

    o_q = n_rw
    o_g = o_q + d_att + 2 * d_kv
    w_rest = _stage_columns(w_in, [(0, 3 * crw), (o_q, o_g - o_q), (3 * crw, n_lora)], "stage_w_rest")
    w_gates = _stage_columns(w_in, [(o_g, w_in.shape[1] - o_g)], "stage_w_gates")
    col = {}
    off = 0
    for name, width in (("r", crw), ("k", crw), ("v", crw), ("q", d_att), ("ak", d_kv), ("av", d_kv),
                        ("lora", n_lora)):
        col[name] = off // LANES
        off += width
    shift_rkv = rwkv_shift[:, :3 * crw]
    shift_lora = jnp.concatenate([rwkv_shift[:, 3 * crw:], jnp.zeros((3, LORA_PAD - n_lora), F32)], axis=1)

    h = _prep(x, ctx, mod)
    proj = _mm_wstat(h.reshape(bsz * tt, d), w_rest, tiles["proj_tm"], tiles["proj_tn"], F32, "in_proj")
    proj = proj.reshape(bsz, tt, -1)
    gates = _mm_lead_rows(h, w_gates, t, tiles["gate_tm"], tiles["gate_tn"], BF16, "gate_proj")

    g, bonus, ops, dec = _features(proj, col, shift_rkv, shift_lora, rwkv_w0, rwkv_w_up, rwkv_a0, rwkv_a_up,
                                   rwkv_g_up, rwkv_k_k, rwkv_k_a, rwkv_r_k.reshape(-1), t, tiles["feat_tb"])
    yf, yb = _scan(ops, dec, t, tiles["scan_lanes"])
    o_rwkv = _rwkv_out(yf, yb, bonus, g, rwkv_gn_g, rwkv_gn_b, t, tiles["feat_tb"])

    cos2, sin2 = _rope_tables(t)
    o_att = _attention(proj, col, attn_sink, cos2, sin2, t)

    ym = _merge(o_rwkv, o_att, w_rwkv_o.astype(BF16), w_att_o.astype(BF16), gates,
                tiles["merge_tm"], tiles["merge_tn"])
    yo = _mm_astat(ym.reshape(bsz * t, d), w_out.astype(BF16), tiles["out_tm"], tiles["out_tn"], F32, "out_proj")
    x1, h2 = _res_ln(x, yo.reshape(bsz, t, d), mod, 2, ln1_g, ln1_b, tiles["ln_tb"], mod_cols=(3, 4))

    u = _ffn_up(h2.reshape(bsz * t, d), w_ff_gate, w_ff_up, tiles["ffn_tm"], tiles["ffn_tf"])
    ff = _mm_astat(u, w_ff_down.astype(BF16), tiles["down_tm"], tiles["down_tn"], F32, "swiglu_down")
    (out,) = _res_ln(x1, ff.reshape(bsz, t, d), mod, 5, ln2_g, ln2_b, tiles["ln_tb"])
    return out


_TILES = dict(proj_tm=512, proj_tn=1024, gate_tm=1024, gate_tn=1024, feat_tb=128, scan_lanes=2048, merge_tm=1024, merge_tn=1024, out_tm=1024, out_tn=1024,
              ln_tb=256, ffn_tm=2048, ffn_tf=256, down_tm=512, down_tn=512)


def kernel(x, c, ctx, c_ctx, w_ada, b_ada, w_in, rwkv_shift, rwkv_w0, rwkv_w_up, rwkv_a0, rwkv_a_up, rwkv_g_up, rwkv_k_k, rwkv_k_a, rwkv_r_k, rwkv_gn_g, rwkv_gn_b, attn_sink, w_rwkv_o, w_att_o, w_out, ln1_g, ln1_b, w_ff_gate, w_ff_up, w_ff_down, ln2_g, ln2_b):
    assert w_ada.shape[0] == DEPTH
    return _block(x, c, ctx, c_ctx, w_ada[0], b_ada[0], w_in[0], rwkv_shift[0], rwkv_w0[0], rwkv_w_up[0],
                  rwkv_a0[0], rwkv_a_up[0], rwkv_g_up[0], rwkv_k_k[0], rwkv_k_a[0], rwkv_r_k[0], rwkv_gn_g[0],
                  rwkv_gn_b[0], attn_sink[0], w_rwkv_o[0], w_att_o[0], w_out[0], ln1_g[0], ln1_b[0],
                  w_ff_gate[0], w_ff_up[0], w_ff_down[0], ln2_g[0], ln2_b[0], _TILES)
```

```python
import functools

import jax
import jax.numpy as jnp
import numpy as np
from jax import lax
from jax.experimental import pallas as pl
from jax.experimental.pallas import tpu as pltpu

F32 = jnp.float32
BF16 = jnp.bfloat16
HIGHEST = lax.Precision.HIGHEST

RWKV_HEAD = 64
DECAY_LORA = 96
ICLR_LORA = 96
GATE_LORA = 256
GN_EPS = 64e-5
ATT_HEAD = 128
ATT_KV_HEADS = 4
ATT_GROUPS = 4
GRID_W = 64
ROPE_BASE = 10000.0
LN_EPS = 1e-5
DEPTH = 1
DEEPNORM_ALPHA = (2 * DEPTH) ** 0.25
NEG_INF = -1e30

LANES = 128
SUBLANES = 8
VMEM_LIMIT = 56 * 1024 * 1024
MOD_ROWS = 8
SCAN_CHUNK = 64
SCAN_LANES = 256
SCAN_PLANES = 5
LORA_PAD = 1024


def _cparams(sem):
    return pltpu.CompilerParams(dimension_semantics=sem, vmem_limit_bytes=VMEM_LIMIT)


def _dot(a, b, precision=None):
    return jnp.dot(a, b, preferred_element_type=F32, precision=precision)


def _dot_nt(a, b, precision=None):
    return lax.dot_general(a, b, (((1,), (1,)), ((), ())), preferred_element_type=F32, precision=precision)


def _dot_tn(a, b, precision=None):
    return lax.dot_general(a, b, (((0,), (0,)), ((), ())), preferred_element_type=F32, precision=precision)


def _split2(x):
    hi = x.astype(BF16)
    return hi, (x - hi.astype(F32)).astype(BF16)


def _head_sum(x, ones_bd):
    rows = x.shape[0]
    hi, lo = _split2(x)
    both = jnp.concatenate([hi, lo], axis=0)
    out = []
    for s in range(x.shape[1] // SCAN_LANES):
        part = _dot(both[:, s * SCAN_LANES:(s + 1) * SCAN_LANES], ones_bd)
        out.append(part[:rows] + part[rows:])
    return jnp.concatenate(out, axis=1)


def _ada_kernel(c_ref, w_ref, b_ref, o_ref):
    a = c_ref[...]
    a = a * jax.nn.sigmoid(a)
    a_hi, a_lo = _split2(a)
    w_hi, w_lo = _split2(w_ref[...])
    part = _dot(jnp.concatenate([a_hi, a_lo], axis=0), w_hi)
    o_ref[...] = part[:MOD_ROWS] + part[MOD_ROWS:] + _dot(a_hi, w_lo) + b_ref[...]


def _ada(cc, w_ada, b_ada):
    d, n = w_ada.shape
    tn = 512
    return pl.pallas_call(
        _ada_kernel,
        grid=(n // tn,),
        in_specs=[pl.BlockSpec((MOD_ROWS, d), lambda j: (0, 0)),
                  pl.BlockSpec((d, tn), lambda j: (0, j)),
                  pl.BlockSpec((1, tn), lambda j: (0, j))],
        out_specs=pl.BlockSpec((MOD_ROWS, tn), lambda j: (0, j)),
        out_shape=jax.ShapeDtypeStruct((MOD_ROWS, n), F32),
        compiler_params=_cparams(("arbitrary",)),
        name="ada",
    )(cc, w_ada, b_ada.reshape(1, n))


def _prep_kernel(x_ref, ctx_ref, sh_ref, sc_ref, o_ref, *, n_lat, ctx_row):
    b = pl.program_id(0)
    j = pl.program_id(1)
    is_ctx = j >= n_lat
    row = jnp.where(is_ctx, ctx_row, b)
    sh = sh_ref[pl.ds(row, 1), :]
    sc = sc_ref[pl.ds(row, 1), :]
    xin = jnp.where(is_ctx, ctx_ref[0], x_ref[0])
    o_ref[0] = (xin * (1.0 + sc) + sh).astype(BF16)


def _prep(x, ctx, mod):
    bsz, t, d = x.shape
    l = ctx.shape[1]
    tb = l
    n_lat = t // tb
    return pl.pallas_call(
        functools.partial(_prep_kernel, n_lat=n_lat, ctx_row=bsz),
        grid=(bsz, n_lat + 1),
        in_specs=[pl.BlockSpec((1, tb, d), lambda b, j: (b, jnp.minimum(j, n_lat - 1), 0)),
                  pl.BlockSpec((1, l, d), lambda b, j: (b, 0, 0)),
                  pl.BlockSpec((MOD_ROWS, d), lambda b, j: (0, 0)),
                  pl.BlockSpec((MOD_ROWS, d), lambda b, j: (0, 1))],
        out_specs=pl.BlockSpec((1, tb, d), lambda b, j: (b, j, 0)),
        out_shape=jax.ShapeDtypeStruct((bsz, t + l, d), BF16),
        compiler_params=_cparams(("arbitrary", "arbitrary")),
        name="prep",
    )(x, ctx, mod, mod)


def _cast_kernel(w_ref, o_ref):
    o_ref[...] = w_ref[...].astype(o_ref.dtype)


def _stage_columns(w, segments, name):
    rows = w.shape[0]
    starts, src = [], []
    off = 0
    for start, width in segments:
        assert start % LANES == 0 and width % LANES == 0
        starts.append(off // LANES)
        src.append(start // LANES)
        off += width

    def src_block(j):
        blk = j - starts[0] + src[0]
        for s0, b0 in zip(starts[1:], src[1:]):
            blk = jnp.where(j >= s0, j - s0 + b0, blk)
        return blk

    return pl.pallas_call(
        _cast_kernel,
        grid=(off // LANES,),
        in_specs=[pl.BlockSpec((rows, LANES), lambda j: (0, src_block(j)))],
        out_specs=pl.BlockSpec((rows, LANES), lambda j: (0, j)),
        out_shape=jax.ShapeDtypeStruct((rows, off), BF16),
        compiler_params=_cparams(("arbitrary",)),
        name=name,
    )(w)


def _mm_kernel(a_ref, w_ref, o_ref):
    o_ref[...] = _dot(a_ref[...], w_ref[...]).astype(o_ref.dtype)


def _mm_wstat(a, w, tm, tn, out_dtype, name):
    m, k = a.shape
    n = w.shape[1]
    return pl.pallas_call(
        _mm_kernel,
        grid=(pl.cdiv(n, tn), m // tm),
        in_specs=[pl.BlockSpec((tm, k), lambda j, i: (i, 0)),
                  pl.BlockSpec((k, tn), lambda j, i: (0, j))],
        out_specs=pl.BlockSpec((tm, tn), lambda j, i: (i, j)),
        out_shape=jax.ShapeDtypeStruct((m, n), out_dtype),
        compiler_params=_cparams(("arbitrary", "arbitrary")),
        name=name,
    )(a, w)


def _mm_rows_kernel(a_ref, w_ref, o_ref):
    o_ref[0] = _dot(a_ref[0], w_ref[...]).astype(o_ref.dtype)


def _mm_lead_rows(a, w, rows, tm, tn, out_dtype, name):
    bsz, _, k = a.shape
    n = w.shape[1]
    return pl.pallas_call(
        _mm_rows_kernel,
        grid=(n // tn, bsz, rows // tm),
        in_specs=[pl.BlockSpec((1, tm, k), lambda j, b, i: (b, i, 0)),
                  pl.BlockSpec((k, tn), lambda j, b, i: (0, j))],
        out_specs=pl.BlockSpec((1, tm, tn), lambda j, b, i: (b, i, j)),
        out_shape=jax.ShapeDtypeStruct((bsz, rows, n), out_dtype),
        compiler_params=_cparams(("arbitrary", "arbitrary", "arbitrary")),
        name=name,
    )(a, w)


def _mm_astat(a, w, tm, tn, out_dtype, name):
    m, k = a.shape
    n = w.shape[1]
    return pl.pallas_call(
        _mm_kernel,
        grid=(m // tm, n // tn),
        in_specs=[pl.BlockSpec((tm, k), lambda i, j: (i, 0)),
                  pl.BlockSpec((k, tn), lambda i, j: (0, j))],
        out_specs=pl.BlockSpec((tm, tn), lambda i, j: (i, j)),
        out_shape=jax.ShapeDtypeStruct((m, n), out_dtype),
        compiler_params=_cparams(("arbitrary", "arbitrary")),
        name=name,
    )(a, w)


def _mm_acc_kernel(a_ref, w_ref, o_ref, acc_ref):
    kk = pl.program_id(2)

    @pl.when(kk == 0)
    def _():
        acc_ref[...] = jnp.zeros_like(acc_ref)

    acc_ref[...] += _dot(a_ref[...], w_ref[...])

    @pl.when(kk == pl.num_programs(2) - 1)
    def _():
        o_ref[...] = acc_ref[...].astype(o_ref.dtype)


def _mm_ktiled(a, w, tm, tn, tk, out_dtype, name):
    m, k = a.shape
    n = w.shape[1]
    return pl.pallas_call(
        _mm_acc_kernel,
        grid=(m // tm, n // tn, k // tk),
        in_specs=[pl.BlockSpec((tm, tk), lambda i, j, q: (i, q)),
                  pl.BlockSpec((tk, tn), lambda i, j, q: (q, j))],
        out_specs=pl.BlockSpec((tm, tn), lambda i, j, q: (i, j)),
        out_shape=jax.ShapeDtypeStruct((m, n), out_dtype),
        scratch_shapes=[pltpu.VMEM((tm, tn), F32)],
        compiler_params=_cparams(("arbitrary", "arbitrary", "arbitrary")),
        name=name,
    )(a, w)


def _conv3(prev_ref, cur_ref, next_ref, w_ref, at_start, at_end):
    cur = cur_ref[0]
    rows = cur.shape[0]
    prev_row = jnp.where(at_start, 0.0, prev_ref[0][SUBLANES - 1:SUBLANES, :])
    next_row = jnp.where(at_end, 0.0, next_ref[0][0:1, :])
    up = jnp.concatenate([prev_row, cur[:rows - 1]], axis=0)
    down = jnp.concatenate([cur[1:], next_row], axis=0)
    w = w_ref[...]
    return up * w[0:1] + cur * w[1:2] + down * w[2:3]


def _feat_kernel(rp, rc, rn, kp, kc, kn, vp, vc, vn, lp, lc, ln_,
                 shr, shk, shv, shl, w0, wup, a0, aup, gup, kk_w, ka_w, rk_w, ones_ref, cm_ref,
                 g_o, bonus_o, ops_o, dec_o, *, n_lat, n_tot, chunk):
    j = pl.program_id(1)
    at_start = jnp.logical_or(j == 0, j == n_lat)
    at_end = jnp.logical_or(j == n_lat - 1, j == n_tot - 1)
    r = _conv3(rp, rc, rn, shr, at_start, at_end)
    k = _conv3(kp, kc, kn, shk, at_start, at_end)
    v = _conv3(vp, vc, vn, shv, at_start, at_end)
    lo = _conv3(lp, lc, ln_, shl, at_start, at_end)
    rows = r.shape[0]
    v_h = v.astype(BF16)
    gd = lo[:, 2 * DECAY_LORA + 2 * ICLR_LORA:2 * DECAY_LORA + 2 * ICLR_LORA + GATE_LORA]
    ones_bd = ones_ref[...]
    g_o[0] = _dot(jax.nn.sigmoid(gd).astype(BF16), gup[...])
    kk = k * kk_w[...]
    kk = kk / jnp.maximum(jnp.sqrt(_head_sum(kk * kk, ones_bd)), 1e-12)
    k_sum = jnp.zeros_like(k)
    for d in range(2):
        wd = lo[:, d * DECAY_LORA:(d + 1) * DECAY_LORA]
        ad = lo[:, 2 * DECAY_LORA + d * ICLR_LORA:2 * DECAY_LORA + (d + 1) * ICLR_LORA]
        z = w0[d:d + 1, :] + _dot(jnp.tanh(wd).astype(BF16), wup[d])
        lw = jax.nn.sigmoid(z) * (-np.exp(-0.5))
        iclr = jax.nn.sigmoid(a0[d:d + 1, :] + _dot(ad.astype(BF16), aup[d]))
        k_dir = k * (1.0 + (iclr - 1.0) * ka_w[...])
        k_sum = k_sum + k_dir
        lw_hi, lw_lo = _split2(lw)
        cum = _dot(cm_ref[d], lw_hi) + _dot(cm_ref[d], lw_lo)
        p_inv = jnp.exp(-cum)
        planes = [(kk * jnp.exp(cum - lw)).astype(BF16), (r * jnp.exp(cum)).astype(BF16),
                  (k_dir * p_inv).astype(BF16), (kk * iclr * p_inv).astype(BF16), v_h]
        for hg in range(r.shape[1] // SCAN_LANES):
            for p, plane in enumerate(planes):
                dst = (hg * SCAN_PLANES + p) * SCAN_LANES
                ops_o[d, 0, :, dst:dst + SCAN_LANES] = plane[:, hg * SCAN_LANES:(hg + 1) * SCAN_LANES]
        for q in range(rows // chunk):
            last = (q + 1) * chunk - 1 if d == 0 else q * chunk
            dec_o[d, 0, q] = jnp.exp(cum[last:last + 1, :])
    bonus_o[0] = _head_sum(r * k_sum * rk_w[...], ones_bd) * v


def _features(proj, col, shift_rkv, shift_lora, w0, w_up, a0, a_up, g_up, k_k, k_a, r_k, t_lat, tb):
    bsz, tt, _ = proj.shape
    c = w0.shape[1]
    chunk = SCAN_CHUNK
    n_tot = tt // tb
    n_lat = t_lat // tb
    hb = tb // SUBLANES
    n_h = tt // SUBLANES
    li = np.arange(SCAN_LANES) // RWKV_HEAD
    ones_bd = jnp.asarray(li[:, None] == li[None, :], BF16)
    ti = np.arange(tb)
    same = (ti[:, None] // chunk) == (ti[None, :] // chunk)
    cmask = jnp.asarray(np.stack([same & (ti[None, :] <= ti[:, None]), same & (ti[None, :] >= ti[:, None])]), BF16)

    def main(cb, width):
        blk = cb * LANES // width
        return pl.BlockSpec((1, tb, width), lambda b, j: (b, j, blk))

    def prev(cb, width):
        blk = cb * LANES // width
        return pl.BlockSpec((1, SUBLANES, width), lambda b, j: (b, jnp.maximum(j * hb - 1, 0), blk))

    def nxt(cb, width):
        blk = cb * LANES // width
        return pl.BlockSpec((1, SUBLANES, width), lambda b, j: (b, jnp.minimum((j + 1) * hb, n_h - 1), blk))

    def const(shape):
        nd = len(shape)
        return pl.BlockSpec(shape, lambda b, j: (0,) * nd)

    in_specs = []
    for name, width in (("r", c), ("k", c), ("v", c), ("lora", LORA_PAD)):
        in_specs += [prev(col[name], width), main(col[name], width), nxt(col[name], width)]
    in_specs += [const((3, c)), const((3, c)), const((3, c)), const((3, LORA_PAD)),
                 const((2, c)), const((2, DECAY_LORA, c)), const((2, c)), const((2, ICLR_LORA, c)),
                 const((GATE_LORA, c)), const((1, c)), const((1, c)), const((1, c)),
                 const((SCAN_LANES, SCAN_LANES)), const((2, tb, tb))]
    one = pl.BlockSpec((1, tb, c), lambda b, j: (b, j, 0))
    ops = pl.BlockSpec((2, 1, tb, SCAN_PLANES * c), lambda b, j: (0, b, j, 0))
    dec = pl.BlockSpec((2, 1, tb // chunk, 1, c), lambda b, j: (0, b, j, 0, 0))
    s1 = jax.ShapeDtypeStruct((bsz, tt, c), F32)
    sops = jax.ShapeDtypeStruct((2, bsz, tt, SCAN_PLANES * c), BF16)
    sdec = jax.ShapeDtypeStruct((2, bsz, tt // chunk, 1, c), F32)
    args = [proj] * 12 + [shift_rkv[:, :c], shift_rkv[:, c:2 * c], shift_rkv[:, 2 * c:], shift_lora,
                          w0, w_up.astype(BF16), a0, a_up.astype(BF16), g_up.astype(BF16),
                          k_k.reshape(1, c), k_a.reshape(1, c), r_k.reshape(1, c), ones_bd, cmask]
    return pl.pallas_call(
        functools.partial(_feat_kernel, n_lat=n_lat, n_tot=n_tot, chunk=chunk),
        grid=(bsz, n_tot),
        in_specs=in_specs,
        out_specs=[one, one, ops, dec],
        out_shape=[s1, s1, sops, sdec],
        compiler_params=_cparams(("arbitrary", "arbitrary")),
        name="rwkv_features",
    )(*args)


def _scan_kernel(gm_ref, bdm_ref, hm_ref, eye_ref, *refs, chunk, nsub):
    ops_refs = refs[0:4:2]
    dec_refs = refs[1:4:2]
    y_refs = refs[4:6]
    s_ref = refs[6]
    heads = SCAN_LANES // RWKV_HEAD

    @pl.when(pl.program_id(2) == 0)
    def _():
        s_ref[...] = jnp.zeros_like(s_ref)

    bdm = bdm_ref[...]
    bdm_h = bdm.astype(BF16)
    eye = eye_ref[...]
    hms = [hm_ref[h] for h in range(heads)]

    def rows_bd(m):
        return jnp.concatenate([m * hms[h] for h in range(heads)], axis=0)

    def blocks_bd(m):
        return jnp.concatenate([m] * heads, axis=0) * bdm_h

    steps = chunk.bit_length() - 2
    units = [(d, q) for d in range(2) for q in range(nsub)]
    n_u = len(units)

    def lanes(q):
        return slice(q * SCAN_LANES, (q + 1) * SCAN_LANES)

    def plane(d, q, p):
        return ops_refs[d][0, 0, :, lanes(q * SCAN_PLANES + p)]

    kt = [plane(d, q, 2) for d, q in units]
    bt = [plane(d, q, 3) for d, q in units]
    v = [plane(d, q, 4) for d, q in units]
    x = [jnp.concatenate([plane(d, q, 0), plane(d, q, 1)], axis=0) for d, q in units]
    g = [_dot_nt(x[i], jnp.concatenate([rows_bd(kt[i]), rows_bd(bt[i])], axis=0)) * gm_ref[units[i][0]]
         for i in range(n_u)]
    s_old = [s_ref[i] for i in range(n_u)]
    xs = [_dot_nt(x[i], s_old[i].astype(BF16)) for i in range(n_u)]
    gv = [_dot(g[i][:, :SCAN_LANES].astype(BF16), rows_bd(v[i])) for i in range(n_u)]
    rhs = [xs[i][:chunk] + gv[i][:chunk] for i in range(n_u)]
    t = [eye - g[i][:chunk, SCAN_LANES:] for i in range(n_u)]
    lm = [g[i][:chunk, SCAN_LANES:].astype(BF16) for i in range(n_u)]
    pw = [_dot(lm[i], blocks_bd(lm[i])).astype(BF16) for i in range(n_u)]
    for _ in range(steps - 1):
        both = [_dot(jnp.concatenate([t[i].astype(BF16), pw[i]], axis=0), blocks_bd(pw[i])) for i in range(n_u)]
        t = [t[i] + both[i][:chunk] for i in range(n_u)]
        pw = [both[i][chunk:].astype(BF16) for i in range(n_u)]
    t = [t[i] + _dot(t[i].astype(BF16), blocks_bd(pw[i])) for i in range(n_u)]
    u = [(-_dot(t[i].astype(BF16), rows_bd(rhs[i].astype(BF16)))).astype(BF16) for i in range(n_u)]
    for i, (d, q) in enumerate(units):
        y_refs[d][0, :, lanes(q)] = (xs[i][chunk:] + gv[i][chunk:]
                                     + _dot(g[i][chunk:, SCAN_LANES:].astype(BF16), rows_bd(u[i])))
    for i, (d, q) in enumerate(units):
        upd = _dot_tn(jnp.concatenate([v[i], u[i]], axis=0), jnp.concatenate([kt[i], bt[i]], axis=0))
        s_ref[i] = (s_old[i] + upd) * dec_refs[d][0, 0, 0, :, lanes(q)] * bdm


def _scan(ops, dec, t_lat, lane_block):
    _, bsz, tt, c = ops.shape
    c //= SCAN_PLANES
    chunk = SCAN_CHUNK
    heads = SCAN_LANES // RWKV_HEAD
    n_tot = tt // chunk
    n_lat = t_lat // chunk
    n_ctx = n_tot - n_lat
    nsub = lane_block // SCAN_LANES

    ti = np.arange(chunk)
    before = [ti[None, :] < ti[:, None], ti[None, :] > ti[:, None]]
    gmask = np.stack([np.concatenate([np.tile(before[d], (1, 2 * heads)),
                                      np.tile(before[d] | np.eye(chunk, dtype=bool), (1, 2 * heads))], axis=0)
                      for d in range(2)]).astype(np.float32)
    li = np.arange(SCAN_LANES) // RWKV_HEAD
    bdm = (li[:, None] == li[None, :]).astype(np.float32)
    hmask = jnp.asarray((li[None, None, :] == np.arange(heads)[:, None, None]), BF16)
    eye = np.tile(np.eye(chunk, dtype=np.float32), (1, heads))

    def chunk_index(d, s):
        return jnp.where(s < n_ctx, n_lat + s, s - n_ctx) if d == 0 else n_tot - 1 - s

    def const(shape):
        nd = len(shape)
        return pl.BlockSpec(shape, lambda b, h, s: (0,) * nd)

    in_specs = [const(gmask.shape), const(bdm.shape), const(hmask.shape), const(eye.shape)]
    args = [jnp.asarray(gmask), jnp.asarray(bdm), hmask, jnp.asarray(eye)]
    out_specs = []
    for d in range(2):
        in_specs += [pl.BlockSpec((1, 1, chunk, SCAN_PLANES * lane_block),
                                  lambda b, h, s, d=d: (d, b, chunk_index(d, s), h)),
                     pl.BlockSpec((1, 1, 1, 1, lane_block), lambda b, h, s, d=d: (d, b, chunk_index(d, s), 0, h))]
        args += [ops, dec]
        out_specs.append(pl.BlockSpec((1, chunk, lane_block), lambda b, h, s, d=d: (b, chunk_index(d, s), h)))
    ys = jax.ShapeDtypeStruct((bsz, tt, c), F32)
    return pl.pallas_call(
        functools.partial(_scan_kernel, chunk=chunk, nsub=nsub),
        grid=(bsz, c // lane_block, n_tot),
        in_specs=in_specs,
        out_specs=out_specs,
        out_shape=[ys, ys],
        scratch_shapes=[pltpu.VMEM((2 * nsub, SCAN_LANES, SCAN_LANES), F32)],
        compiler_params=_cparams(("arbitrary", "arbitrary", "arbitrary")),
        name="rwkv_scan",
    )(*args)


def _rout_kernel(yf, yb, bonus, g, gng, gnb, ones_ref, o_ref):
    ones_bd = ones_ref[...]
    inv_n = 1.0 / RWKV_HEAD
    y = yf[0] + yb[0]
    mu = _head_sum(y, ones_bd) * inv_n
    yc = y - mu
    var = _head_sum(yc * yc, ones_bd) * inv_n
    yn = yc * lax.rsqrt(var + GN_EPS) * gng[...] + gnb[...]
    o_ref[0] = ((yn + bonus[0]) * g[0]).astype(o_ref.dtype)


def _rwkv_out(yf, yb, bonus, g, gn_g, gn_b, t_lat, tb):
    bsz, _, c = g.shape
    li = np.arange(SCAN_LANES) // RWKV_HEAD
    ones_bd = jnp.asarray(li[:, None] == li[None, :], BF16)
    one = pl.BlockSpec((1, tb, c), lambda b, j: (b, j, 0))

    def const(shape):
        return pl.BlockSpec(shape, lambda b, j: (0, 0))

    return pl.pallas_call(
        _rout_kernel,
        grid=(bsz, t_lat // tb),
        in_specs=[one, one, one, one, const((1, c)), const((1, c)), const((SCAN_LANES, SCAN_LANES))],
        out_specs=one,
        out_shape=jax.ShapeDtypeStruct((bsz, t_lat, c), BF16),
        compiler_params=_cparams(("arbitrary", "arbitrary")),
        name="rwkv_out",
    )(yf, yb, bonus, g, gn_g.reshape(1, c), gn_b.reshape(1, c), ones_bd)


def _rope(x, c, s):
    return x * c + pltpu.roll(x, ATT_HEAD // 2, 1) * s


def _attn_kernel(sink_ref, q_ref, km_ref, k0_ref, kp_ref, vm_ref, v0_ref, vp_ref, kc_ref, vc_ref,
                 c0_ref, s0_ref, cm_ref, sm_ref, cp_ref, sp_ref, o_ref, *, n_blk):
    n = pl.program_id(1)
    blk = ATT_HEAD
    scale = ATT_HEAD ** -0.5
    kvh = range(ATT_KV_HEADS)
    c0, s0 = c0_ref[...], s0_ref[...]
    cm, sm = cm_ref[...], sm_ref[...]
    cp, sp = cp_ref[...], sp_ref[...]

    def head(ref, i):
        return ref[0, :, i * ATT_HEAD:(i + 1) * ATT_HEAD]

    qq = [jnp.concatenate([_rope(head(q_ref, g * ATT_GROUPS + h), c0, s0) for h in range(ATT_GROUPS)],
                          axis=0).astype(BF16) for g in kvh]
    kw = [jnp.concatenate([_rope(head(km_ref, g), cm, sm), _rope(head(k0_ref, g), c0, s0),
                           _rope(head(kp_ref, g), cp, sp)], axis=0).astype(BF16) for g in kvh]
    vw = [jnp.concatenate([head(vm_ref, g), head(v0_ref, g), head(vp_ref, g)], axis=0).astype(BF16) for g in kvh]
    kc = [head(kc_ref, g).astype(BF16) for g in kvh]
    vc = [head(vc_ref, g).astype(BF16) for g in kvh]
    ii = lax.broadcasted_iota(jnp.int32, (ATT_GROUPS * blk, 3 * blk), 0) % blk
    jj = lax.broadcasted_iota(jnp.int32, (ATT_GROUPS * blk, 3 * blk), 1)
    in_prev = jnp.logical_and(jnp.logical_and(jj < blk, jj >= ii), n > 0)
    in_self = jnp.logical_and(jj >= blk, jj < 2 * blk)
    in_next = jnp.logical_and(jnp.logical_and(jj >= 2 * blk, jj - 2 * blk <= ii), n < n_blk - 1)
    valid = jnp.logical_or(jnp.logical_or(in_prev, in_self), in_next)
    s_w = [jnp.where(valid, _dot_nt(qq[g], kw[g]) * scale, NEG_INF) for g in kvh]
    s_c = [_dot_nt(qq[g], kc[g]) * scale for g in kvh]
    s_s = [jnp.concatenate([jnp.full((blk, 1), sink_ref[g * ATT_GROUPS + h], F32) for h in range(ATT_GROUPS)],
                           axis=0) for g in kvh]
    m = [jnp.maximum(jnp.maximum(jnp.max(s_w[g], axis=-1, keepdims=True),
                                 jnp.max(s_c[g], axis=-1, keepdims=True)), s_s[g]) for g in kvh]
    p_w = [jnp.exp(s_w[g] - m[g]) for g in kvh]
    p_c = [jnp.exp(s_c[g] - m[g]) for g in kvh]
    den = [jnp.sum(p_w[g], axis=-1, keepdims=True) + jnp.sum(p_c[g], axis=-1, keepdims=True)
           + jnp.exp(s_s[g] - m[g]) for g in kvh]
    o = [(_dot(p_w[g].astype(BF16), vw[g]) + _dot(p_c[g].astype(BF16), vc[g])) / den[g] for g in kvh]
    o_ref[0] = jnp.concatenate([o[g][h * blk:(h + 1) * blk] for g in kvh for h in range(ATT_GROUPS)],
                               axis=1).astype(o_ref.dtype)


def _attention(proj, col, sink, cos2, sin2, t_lat):
    bsz, tt, _ = proj.shape
    blk = ATT_HEAD
    n_blk = t_lat // blk
    l = tt - t_lat
    qw = ATT_KV_HEADS * ATT_GROUPS * ATT_HEAD
    kw = ATT_KV_HEADS * ATT_HEAD
    q_blk = col["q"] * LANES // qw

    def kv(name, off):
        cb = col[name] * LANES // kw
        return pl.BlockSpec((1, blk, kw), lambda b, n: (b, jnp.clip(n + off, 0, n_blk - 1), cb))

    def ctx(name):
        cb = col[name] * LANES // kw
        return pl.BlockSpec((1, l, kw), lambda b, n: (b, t_lat // l, cb))

    def tab(off):
        return pl.BlockSpec((blk, ATT_HEAD), lambda b, n: (jnp.clip(n + off, 0, n_blk - 1), 0))

    in_specs = [pl.BlockSpec(memory_space=pltpu.SMEM),
                pl.BlockSpec((1, blk, qw), lambda b, n: (b, n, q_blk)),
                kv("ak", -1), kv("ak", 0), kv("ak", 1), kv("av", -1), kv("av", 0), kv("av", 1),
                ctx("ak"), ctx("av"), tab(0), tab(0), tab(-1), tab(-1), tab(1), tab(1)]
    return pl.pallas_call(
        functools.partial(_attn_kernel, n_blk=n_blk),
        grid=(bsz, n_blk),
        in_specs=in_specs,
        out_specs=pl.BlockSpec((1, blk, qw), lambda b, n: (b, n, 0)),
        out_shape=jax.ShapeDtypeStruct((bsz, t_lat, qw), BF16),
        compiler_params=_cparams(("arbitrary", "arbitrary")),
        name="window_attention",
    )(sink, proj, proj, proj, proj, proj, proj, proj, proj, proj, cos2, sin2, cos2, sin2, cos2, sin2)


def _merge_kernel(o1_ref, o2_ref, w1_ref, w2_ref, g1_ref, g2_ref, o_ref):
    y1 = _dot(o1_ref[0], w1_ref[...])
    y2 = _dot(o2_ref[0], w2_ref[...])
    o_ref[0] = (jax.nn.sigmoid(g1_ref[0].astype(F32)) * y1
                + jax.nn.sigmoid(g2_ref[0].astype(F32)) * y2).astype(o_ref.dtype)


def _merge(o_rwkv, o_att, w1, w2, gates, tm, tn):
    bsz, t, c = o_rwkv.shape
    n = w1.shape[1]
    gr = 0
    ga = n // tn
    return pl.pallas_call(
        _merge_kernel,
        grid=(bsz, t // tm, n // tn),
        in_specs=[pl.BlockSpec((1, tm, c), lambda b, i, j: (b, i, 0)),
                  pl.BlockSpec((1, tm, c), lambda b, i, j: (b, i, 0)),
                  pl.BlockSpec((c, tn), lambda b, i, j: (0, j)),
                  pl.BlockSpec((c, tn), lambda b, i, j: (0, j)),
                  pl.BlockSpec((1, tm, tn), lambda b, i, j: (b, i, gr + j)),
                  pl.BlockSpec((1, tm, tn), lambda b, i, j: (b, i, ga + j))],
        out_specs=pl.BlockSpec((1, tm, tn), lambda b, i, j: (b, i, j)),
        out_shape=jax.ShapeDtypeStruct((bsz, t, n), BF16),
        compiler_params=_cparams(("arbitrary", "arbitrary", "arbitrary")),
        name="gated_merge",
    )(o_rwkv, o_att, w1, w2, gates, gates)


def _mm_resid_kernel(a_ref, w_ref, x_ref, gt_ref, o_ref, *, tiles_per_batch):
    bi = pl.program_id(0) // tiles_per_batch
    y = _dot(a_ref[...], w_ref[...])
    o_ref[...] = DEEPNORM_ALPHA * x_ref[...] + gt_ref[pl.ds(bi, 1), :] * y


def _mm_resid(a, w, x, mod, gate_col, rows_per_batch, tm, tn, name):
    m, k = a.shape
    n = w.shape[1]
    gcb = gate_col * (n // tn)
    return pl.pallas_call(
        functools.partial(_mm_resid_kernel, tiles_per_batch=rows_per_batch // tm),
        grid=(m // tm, n // tn),
        in_specs=[pl.BlockSpec((tm, k), lambda i, j: (i, 0)),
                  pl.BlockSpec((k, tn), lambda i, j: (0, j)),
                  pl.BlockSpec((tm, tn), lambda i, j: (i, j)),
                  pl.BlockSpec((MOD_ROWS, tn), lambda i, j: (0, gcb + j))],
        out_specs=pl.BlockSpec((tm, tn), lambda i, j: (i, j)),
        out_shape=jax.ShapeDtypeStruct((m, n), F32),
        compiler_params=_cparams(("arbitrary", "arbitrary")),
        name=name,
    )(a, w, x, mod)


def _ln_kernel(z_ref, g_ref, b_ref, *rest, with_mod):
    bi = pl.program_id(0)
    z = z_ref[0]
    mu = jnp.mean(z, axis=-1, keepdims=True)
    zc = z - mu
    var = jnp.mean(zc * zc, axis=-1, keepdims=True)
    out = zc * lax.rsqrt(var + LN_EPS) * g_ref[...] + b_ref[...]
    if with_mod:
        sh_ref, sc_ref, o_ref, h_ref = rest
        o_ref[0] = out
        h_ref[0] = (out * (1.0 + sc_ref[pl.ds(bi, 1), :]) + sh_ref[pl.ds(bi, 1), :]).astype(BF16)
    else:
        (o_ref,) = rest
        o_ref[0] = out


def _res_ln(z, mod, g, b, tb, mod_cols=None):
    bsz, t, d = z.shape
    blk = pl.BlockSpec((1, tb, d), lambda bi, j: (bi, j, 0))
    vec = pl.BlockSpec((1, d), lambda bi, j: (0, 0))

    def modspec(cb):
        return pl.BlockSpec((MOD_ROWS, d), lambda bi, j: (0, cb))

    in_specs = [blk, vec, vec]
    args = [z, g.reshape(1, d), b.reshape(1, d)]
    out_specs = [blk]
    out_shape = [jax.ShapeDtypeStruct((bsz, t, d), F32)]
    if mod_cols is not None:
        in_specs += [modspec(mod_cols[0]), modspec(mod_cols[1])]
        args += [mod, mod]
        out_specs.append(blk)
        out_shape.append(jax.ShapeDtypeStruct((bsz, t, d), BF16))
    return pl.pallas_call(
        functools.partial(_ln_kernel, with_mod=mod_cols is not None),
        grid=(bsz, t // tb),
        in_specs=in_specs,
        out_specs=out_specs,
        out_shape=out_shape,
        compiler_params=_cparams(("arbitrary", "arbitrary")),
        name="residual_layernorm",
    )(*args)


def _ffn_up_kernel(a_ref, wg_ref, wu_ref, o_ref):
    a = a_ref[...]
    gg = _dot(a, wg_ref[...].astype(BF16))
    uu = _dot(a, wu_ref[...].astype(BF16))
    o_ref[...] = (gg * jax.nn.sigmoid(gg) * uu).astype(o_ref.dtype)


def _ffn_up(a, wg, wu, tm, tf):
    m, k = a.shape
    f = wg.shape[1]
    return pl.pallas_call(
        _ffn_up_kernel,
        grid=(m // tm, f // tf),
        in_specs=[pl.BlockSpec((tm, k), lambda i, j: (i, 0)),
                  pl.BlockSpec((k, tf), lambda i, j: (0, j)),
                  pl.BlockSpec((k, tf), lambda i, j: (0, j))],
        out_specs=pl.BlockSpec((tm, tf), lambda i, j: (i, j)),
        out_shape=jax.ShapeDtypeStruct((m, f), BF16),
        compiler_params=_cparams(("arbitrary", "arbitrary")),
        name="swiglu_up",
    )(a, wg, wu)


def _rope_tables(t):
    rows = t // GRID_W
    row = jnp.broadcast_to(jnp.arange(rows, dtype=F32)[:, None], (rows, GRID_W)).reshape(t)
    colp = jnp.broadcast_to(jnp.arange(GRID_W, dtype=F32)[None, :], (rows, GRID_W)).reshape(t)
    axis_dim = ATT_HEAD // 2
    inv = ROPE_BASE ** (-jnp.arange(0, axis_dim, 2, dtype=F32) / axis_dim)
    ang = jnp.concatenate([row[:, None] * inv, colp[:, None] * inv], -1)
    cos, sin = jnp.cos(ang), jnp.sin(ang)
    return jnp.concatenate([cos, cos], -1), jnp.concatenate([-sin, sin], -1)


def _block(x, c, ctx, c_ctx, w_ada, b_ada, w_in, rwkv_shift, rwkv_w0, rwkv_w_up, rwkv_a0, rwkv_a_up,
           rwkv_g_up, rwkv_k_k, rwkv_k_a, rwkv_r_k, rwkv_gn_g, rwkv_gn_b, attn_sink, w_rwkv_o, w_att_o,
           w_out, ln1_g, ln1_b, w_ff_gate, w_ff_up, w_ff_down, ln2_g, ln2_b, tiles):
    bsz, t, d = x.shape
    l = ctx.shape[1]
    tt = t + l
    crw = rwkv_w0.shape[1]
    d_att = ATT_KV_HEADS * ATT_GROUPS * ATT_HEAD
    d_kv = ATT_KV_HEADS * ATT_HEAD
    n_lora = 2 * DECAY_LORA + 2 * ICLR_LORA + GATE_LORA
    n_rw = 3 * crw + n_lora

    cc = jnp.zeros((MOD_ROWS, d), F32).at[:bsz].set(c).at[bsz].set(c_ctx)
    mod = _ada(cc, w_ada, b_ada)

    o_q = n_rw
    o_g = o_q + d_att + 2 * d_kv
    w_rest = _stage_columns(w_in, [(0, 3 * crw), (o_q, o_g - o_q), (3 * crw, n_lora)], "stage_w_rest")
    w_gates = _stage_columns(w_in, [(o_g, w_in.shape[1] - o_g)], "stage_w_gates")
    col = {}
    off = 0
    for name, width in (("r", crw), ("k", crw), ("v", crw), ("q", d_att), ("ak", d_kv), ("av", d_kv),
                        ("lora", n_lora)):
        col[name] = off // LANES
        off += width
    shift_rkv = rwkv_shift[:, :3 * crw]
    shift_lora = jnp.concatenate([rwkv_shift[:, 3 * crw:], jnp.zeros((3, LORA_PAD - n_lora), F32)], axis=1)

    h = _prep(x, ctx, mod)
    proj = _mm_wstat(h.reshape(bsz * tt, d), w_rest, tiles["proj_tm"], tiles["proj_tn"], F32, "in_proj")
    proj = proj.reshape(bsz, tt, -1)
    gates = _mm_lead_rows(h, w_gates, t, tiles["gate_tm"], tiles["gate_tn"], BF16, "gate_proj")

    g, bonus, ops, dec = _features(proj, col, shift_rkv, shift_lora, rwkv_w0, rwkv_w_up, rwkv_a0, rwkv_a_up,
                                   rwkv_g_up, rwkv_k_k, rwkv_k_a, rwkv_r_k.reshape(-1), t, tiles["feat_tb"])
    yf, yb = _scan(ops, dec, t, tiles["scan_lanes"])
    o_rwkv = _rwkv_out(yf, yb, bonus, g, rwkv_gn_g, rwkv_gn_b, t, tiles["feat_tb"])

    cos2, sin2 = _rope_tables(t)
    o_att = _attention(proj, col, attn_sink, cos2, sin2, t)

    ym = _merge(o_rwkv, o_att, w_rwkv_o.astype(BF16), w_att_o.astype(BF16), gates,
                tiles["merge_tm"], tiles["merge_tn"])
    z1 = _mm_resid(ym.reshape(bsz * t, d), w_out.astype(BF16), x.reshape(bsz * t, d), mod, 2, t,
                   tiles["out_tm"], tiles["out_tn"], "out_proj")
    x1, h2 = _res_ln(z1.reshape(bsz, t, d), mod, ln1_g, ln1_b, tiles["ln_tb"], mod_cols=(3, 4))

    u = _ffn_up(h2.reshape(bsz * t, d), w_ff_gate, w_ff_up, tiles["ffn_tm"], tiles["ffn_tf"])
    z2 = _mm_resid(u, w_ff_down.astype(BF16), x1.reshape(bsz * t, d), mod, 5, t,
                   tiles["down_tm"], tiles["down_tn"], "swiglu_down")
    (out,) = _res_ln(z2.reshape(bsz, t, d), mod, ln2_g, ln2_b, tiles["ln_tb"])
    return out


_TILES = dict(proj_tm=512, proj_tn=1024, gate_tm=1024, gate_tn=1024, feat_tb=128, scan_lanes=2048, merge_tm=1024, merge_tn=1024, out_tm=1024, out_tn=1024,
              ln_tb=512, ffn_tm=2048, ffn_tf=256, down_tm=512, down_tn=512)


def kernel(x, c, ctx, c_ctx, w_ada, b_ada, w_in, rwkv_shift, rwkv_w0, rwkv_w_up, rwkv_a0, rwkv_a_up, rwkv_g_up, rwkv_k_k, rwkv_k_a, rwkv_r_k, rwkv_gn_g, rwkv_gn_b, attn_sink, w_rwkv_o, w_att_o, w_out, ln1_g, ln1_b, w_ff_gate, w_ff_up, w_ff_down, ln2_g, ln2_b):
    assert w_ada.shape[0] == DEPTH
    return _block(x, c, ctx, c_ctx, w_ada[0], b_ada[0], w_in[0], rwkv_shift[0], rwkv_w0[0], rwkv_w_up[0],
                  rwkv_a0[0], rwkv_a_up[0], rwkv_g_up[0], rwkv_k_k[0], rwkv_k_a[0], rwkv_r_k[0], rwkv_gn_g[0],
                  rwkv_gn_b[0], attn_sink[0], w_rwkv_o[0], w_att_o[0], w_out[0], ln1_g[0], ln1_b[0],
                  w_ff_gate[0], w_ff_up[0], w_ff_down[0], ln2_g[0], ln2_b[0], _TILES)
```

```python
import functools

import jax
import jax.numpy as jnp
import numpy as np
from jax import lax
from jax.experimental import pallas as pl
from jax.experimental.pallas import tpu as pltpu

F32 = jnp.float32
BF16 = jnp.bfloat16
HIGHEST = lax.Precision.HIGHEST

RWKV_HEAD = 64
DECAY_LORA = 96
ICLR_LORA = 96
GATE_LORA = 256
GN_EPS = 64e-5
ATT_HEAD = 128
ATT_KV_HEADS = 4
ATT_GROUPS = 4
GRID_W = 64
ROPE_BASE = 10000.0
LN_EPS = 1e-5
DEPTH = 1
DEEPNORM_ALPHA = (2 * DEPTH) ** 0.25
NEG_INF = -1e30

LANES = 128
SUBLANES = 8
VMEM_LIMIT = 56 * 1024 * 1024
MOD_ROWS = 8
SCAN_CHUNK = 64
SCAN_LANES = 256
SCAN_PLANES = 5
LORA_PAD = 1024


def _cparams(sem):
    return pltpu.CompilerParams(dimension_semantics=sem, vmem_limit_bytes=VMEM_LIMIT)


def _dot(a, b, precision=None):
    return jnp.dot(a, b, preferred_element_type=F32, precision=precision)


def _dot_nt(a, b, precision=None):
    return lax.dot_general(a, b, (((1,), (1,)), ((), ())), preferred_element_type=F32, precision=precision)


def _dot_tn(a, b, precision=None):
    return lax.dot_general(a, b, (((0,), (0,)), ((), ())), preferred_element_type=F32, precision=precision)


def _split2(x):
    hi = x.astype(BF16)
    return hi, (x - hi.astype(F32)).astype(BF16)


def _head_sum(x, ones_bd):
    rows = x.shape[0]
    both = jnp.concatenate(_split2(x), axis=0)
    out = []
    for s in range(x.shape[1] // SCAN_LANES):
        part = _dot(both[:, s * SCAN_LANES:(s + 1) * SCAN_LANES], ones_bd)
        out.append(part[:rows] + part[rows:])
    return jnp.concatenate(out, axis=1)


def _ada_kernel(c_ref, w_ref, b_ref, o_ref):
    a = c_ref[...]
    a = a * jax.nn.sigmoid(a)
    a_hi, a_lo = _split2(a)
    w_hi, w_lo = _split2(w_ref[...])
    part = _dot(jnp.concatenate([a_hi, a_lo], axis=0), w_hi)
    o_ref[...] = part[:MOD_ROWS] + part[MOD_ROWS:] + _dot(a_hi, w_lo) + b_ref[...]


def _ada(cc, w_ada, b_ada):
    d, n = w_ada.shape
    tn = 512
    return pl.pallas_call(
        _ada_kernel,
        grid=(n // tn,),
        in_specs=[pl.BlockSpec((MOD_ROWS, d), lambda j: (0, 0)),
                  pl.BlockSpec((d, tn), lambda j: (0, j)),
                  pl.BlockSpec((1, tn), lambda j: (0, j))],
        out_specs=pl.BlockSpec((MOD_ROWS, tn), lambda j: (0, j)),
        out_shape=jax.ShapeDtypeStruct((MOD_ROWS, n), F32),
        compiler_params=_cparams(("arbitrary",)),
        name="ada",
    )(cc, w_ada, b_ada.reshape(1, n))


def _prep_kernel(x_ref, ctx_ref, sh_ref, sc_ref, o_ref, *, n_lat, ctx_row):
    b = pl.program_id(0)
    j = pl.program_id(1)
    is_ctx = j >= n_lat
    row = jnp.where(is_ctx, ctx_row, b)
    sh = sh_ref[pl.ds(row, 1), :]
    sc = sc_ref[pl.ds(row, 1), :]
    xin = jnp.where(is_ctx, ctx_ref[0], x_ref[0])
    o_ref[0] = (xin * (1.0 + sc) + sh).astype(BF16)


def _prep(x, ctx, mod):
    bsz, t, d = x.shape
    l = ctx.shape[1]
    tb = l
    n_lat = t // tb
    return pl.pallas_call(
        functools.partial(_prep_kernel, n_lat=n_lat, ctx_row=bsz),
        grid=(bsz, n_lat + 1),
        in_specs=[pl.BlockSpec((1, tb, d), lambda b, j: (b, jnp.minimum(j, n_lat - 1), 0)),
                  pl.BlockSpec((1, l, d), lambda b, j: (b, 0, 0)),
                  pl.BlockSpec((MOD_ROWS, d), lambda b, j: (0, 0)),
                  pl.BlockSpec((MOD_ROWS, d), lambda b, j: (0, 1))],
        out_specs=pl.BlockSpec((1, tb, d), lambda b, j: (b, j, 0)),
        out_shape=jax.ShapeDtypeStruct((bsz, t + l, d), BF16),
        compiler_params=_cparams(("arbitrary", "arbitrary")),
        name="prep",
    )(x, ctx, mod, mod)


def _cast_kernel(w_ref, o_ref):
    o_ref[...] = w_ref[...].astype(o_ref.dtype)


def _stage_columns(w, segments, name):
    rows = w.shape[0]
    starts, src = [], []
    off = 0
    for start, width in segments:
        assert start % LANES == 0 and width % LANES == 0
        starts.append(off // LANES)
        src.append(start // LANES)
        off += width

    def src_block(j):
        blk = j - starts[0] + src[0]
        for s0, b0 in zip(starts[1:], src[1:]):
            blk = jnp.where(j >= s0, j - s0 + b0, blk)
        return blk

    return pl.pallas_call(
        _cast_kernel,
        grid=(off // LANES,),
        in_specs=[pl.BlockSpec((rows, LANES), lambda j: (0, src_block(j)))],
        out_specs=pl.BlockSpec((rows, LANES), lambda j: (0, j)),
        out_shape=jax.ShapeDtypeStruct((rows, off), BF16),
        compiler_params=_cparams(("arbitrary",)),
        name=name,
    )(w)


def _mm_kernel(a_ref, w_ref, *refs, n_side):
    o_ref = refs[n_side]
    if len(o_ref.shape) == 3:
        o_ref[0] = _dot(a_ref[0], w_ref[...]).astype(o_ref.dtype)
    else:
        o_ref[...] = _dot(a_ref[...], w_ref[...]).astype(o_ref.dtype)
    for s_in, s_out in zip(refs[:n_side], refs[n_side + 1:]):
        s_out[...] = s_in[...].astype(s_out.dtype)


def _side_specs(side, n_steps, step_of):
    in_specs, out_specs, out_shape = [], [], []
    for arr, rows in side:
        nblk = arr.shape[0] // rows
        assert arr.shape[0] % rows == 0 and nblk <= n_steps
        spec = pl.BlockSpec((rows, arr.shape[1]), lambda *g, nblk=nblk: ((step_of(*g) * nblk) // n_steps, 0))
        in_specs.append(spec)
        out_specs.append(spec)
        out_shape.append(jax.ShapeDtypeStruct(arr.shape, BF16))
    return in_specs, out_specs, out_shape


def _mm_wstat(a, w, tm, tn, out_dtype, name, side=()):
    m, k = a.shape
    n = w.shape[1]
    nj, ni = pl.cdiv(n, tn), m // tm
    s_in, s_out, s_shape = _side_specs(side, nj * ni, lambda j, i: j * ni + i)
    return pl.pallas_call(
        functools.partial(_mm_kernel, n_side=len(side)),
        grid=(nj, ni),
        in_specs=[pl.BlockSpec((tm, k), lambda j, i: (i, 0)),
                  pl.BlockSpec((k, tn), lambda j, i: (0, j))] + s_in,
        out_specs=[pl.BlockSpec((tm, tn), lambda j, i: (i, j))] + s_out,
        out_shape=[jax.ShapeDtypeStruct((m, n), out_dtype)] + s_shape,
        compiler_params=_cparams(("arbitrary", "arbitrary")),
        name=name,
    )(a, w, *[arr for arr, _ in side])


def _mm_lead_rows(a, w, rows, tm, tn, out_dtype, name, side=()):
    bsz, _, k = a.shape
    n = w.shape[1]
    nj, ni = n // tn, rows // tm
    s_in, s_out, s_shape = _side_specs(side, nj * bsz * ni, lambda j, b, i: (j * bsz + b) * ni + i)
    return pl.pallas_call(
        functools.partial(_mm_kernel, n_side=len(side)),
        grid=(nj, bsz, ni),
        in_specs=[pl.BlockSpec((1, tm, k), lambda j, b, i: (b, i, 0)),
                  pl.BlockSpec((k, tn), lambda j, b, i: (0, j))] + s_in,
        out_specs=[pl.BlockSpec((1, tm, tn), lambda j, b, i: (b, i, j))] + s_out,
        out_shape=[jax.ShapeDtypeStruct((bsz, rows, n), out_dtype)] + s_shape,
        compiler_params=_cparams(("arbitrary", "arbitrary", "arbitrary")),
        name=name,
    )(a, w, *[arr for arr, _ in side])


def _mm_astat(a, w, tm, tn, out_dtype, name):
    m, k = a.shape
    n = w.shape[1]
    return pl.pallas_call(
        _mm_kernel,
        grid=(m // tm, n // tn),
        in_specs=[pl.BlockSpec((tm, k), lambda i, j: (i, 0)),
                  pl.BlockSpec((k, tn), lambda i, j: (0, j))],
        out_specs=pl.BlockSpec((tm, tn), lambda i, j: (i, j)),
        out_shape=jax.ShapeDtypeStruct((m, n), out_dtype),
        compiler_params=_cparams(("arbitrary", "arbitrary")),
        name=name,
    )(a, w)


def _mm_acc_kernel(a_ref, w_ref, o_ref, acc_ref):
    kk = pl.program_id(2)

    @pl.when(kk == 0)
    def _():
        acc_ref[...] = jnp.zeros_like(acc_ref)

    acc_ref[...] += _dot(a_ref[...], w_ref[...])

    @pl.when(kk == pl.num_programs(2) - 1)
    def _():
        o_ref[...] = acc_ref[...].astype(o_ref.dtype)


def _mm_ktiled(a, w, tm, tn, tk, out_dtype, name):
    m, k = a.shape
    n = w.shape[1]
    return pl.pallas_call(
        _mm_acc_kernel,
        grid=(m // tm, n // tn, k // tk),
        in_specs=[pl.BlockSpec((tm, tk), lambda i, j, q: (i, q)),
                  pl.BlockSpec((tk, tn), lambda i, j, q: (q, j))],
        out_specs=pl.BlockSpec((tm, tn), lambda i, j, q: (i, j)),
        out_shape=jax.ShapeDtypeStruct((m, n), out_dtype),
        scratch_shapes=[pltpu.VMEM((tm, tn), F32)],
        compiler_params=_cparams(("arbitrary", "arbitrary", "arbitrary")),
        name=name,
    )(a, w)


def _conv3(prev_ref, cur_ref, next_ref, w_ref, at_start, at_end):
    cur = cur_ref[0]
    rows = cur.shape[0]
    prev_row = jnp.where(at_start, 0.0, prev_ref[0][SUBLANES - 1:SUBLANES, :])
    next_row = jnp.where(at_end, 0.0, next_ref[0][0:1, :])
    up = jnp.concatenate([prev_row, cur[:rows - 1]], axis=0)
    down = jnp.concatenate([cur[1:], next_row], axis=0)
    w = w_ref[...]
    return up * w[0:1] + cur * w[1:2] + down * w[2:3]


def _feat_kernel(rp, rc, rn, kp, kc, kn, vp, vc, vn, lp, lc, ln_,
                 shr, shk, shv, shl, w0, wup, a0, aup, gup, kk_w, ka_w, rk_w, ones_ref, cm_ref,
                 g_o, bonus_o, ops_o, dec_o, *, n_lat, n_tot, chunk):
    j = pl.program_id(1)
    at_start = jnp.logical_or(j == 0, j == n_lat)
    at_end = jnp.logical_or(j == n_lat - 1, j == n_tot - 1)
    r = _conv3(rp, rc, rn, shr, at_start, at_end)
    k = _conv3(kp, kc, kn, shk, at_start, at_end)
    v = _conv3(vp, vc, vn, shv, at_start, at_end)
    lo = _conv3(lp, lc, ln_, shl, at_start, at_end)
    rows = r.shape[0]
    v_h = v.astype(BF16)
    gd = lo[:, 2 * DECAY_LORA + 2 * ICLR_LORA:2 * DECAY_LORA + 2 * ICLR_LORA + GATE_LORA]
    ones_bd = ones_ref[...]
    g_o[0] = _dot(jax.nn.sigmoid(gd).astype(BF16), gup[...])
    kk = k * kk_w[...]
    kk = kk * lax.rsqrt(jnp.maximum(_head_sum(kk * kk, ones_bd), 1e-24))
    k_sum = jnp.zeros_like(k)
    for d in range(2):
        wd = lo[:, d * DECAY_LORA:(d + 1) * DECAY_LORA]
        ad = lo[:, 2 * DECAY_LORA + d * ICLR_LORA:2 * DECAY_LORA + (d + 1) * ICLR_LORA]
        z = w0[d:d + 1, :] + _dot(jnp.tanh(wd).astype(BF16), wup[d])
        lw = jax.nn.sigmoid(z) * (-np.exp(-0.5))
        iclr = jax.nn.sigmoid(a0[d:d + 1, :] + _dot(ad.astype(BF16), aup[d]))
        k_dir = k * (1.0 + (iclr - 1.0) * ka_w[...])
        k_sum = k_sum + k_dir
        lw_hi, lw_lo = _split2(lw)
        cum = _dot(cm_ref[d], lw_hi) + _dot(cm_ref[d], lw_lo)
        p_inv = jnp.exp(-cum)
        planes = [(kk * jnp.exp(cum - lw)).astype(BF16), (r * jnp.exp(cum)).astype(BF16),
                  (k_dir * p_inv).astype(BF16), (kk * iclr * p_inv).astype(BF16), v_h]
        for hg in range(r.shape[1] // SCAN_LANES):
            for p, plane in enumerate(planes):
                dst = (hg * SCAN_PLANES + p) * SCAN_LANES
                ops_o[d, 0, :, dst:dst + SCAN_LANES] = plane[:, hg * SCAN_LANES:(hg + 1) * SCAN_LANES]
        for q in range(rows // chunk):
            last = (q + 1) * chunk - 1 if d == 0 else q * chunk
            dec_o[d, 0, q] = jnp.exp(cum[last:last + 1, :])
    bonus_o[0] = _head_sum(r * k_sum * rk_w[...], ones_bd) * v


def _features(proj, col, shift_rkv, shift_lora, w0, w_up, a0, a_up, g_up, k_k, k_a, r_k, t_lat, tb):
    bsz, tt, _ = proj.shape
    c = w0.shape[1]
    chunk = SCAN_CHUNK
    n_tot = tt // tb
    n_lat = t_lat // tb
    hb = tb // SUBLANES
    n_h = tt // SUBLANES
    li = np.arange(SCAN_LANES) // RWKV_HEAD
    ones_bd = jnp.asarray(li[:, None] == li[None, :], BF16)
    ti = np.arange(tb)
    same = (ti[:, None] // chunk) == (ti[None, :] // chunk)
    cmask = jnp.asarray(np.stack([same & (ti[None, :] <= ti[:, None]), same & (ti[None, :] >= ti[:, None])]), BF16)

    def main(cb, width):
        blk = cb * LANES // width
        return pl.BlockSpec((1, tb, width), lambda b, j: (b, j, blk))

    def prev(cb, width):
        blk = cb * LANES // width
        return pl.BlockSpec((1, SUBLANES, width), lambda b, j: (b, jnp.maximum(j * hb - 1, 0), blk))

    def nxt(cb, width):
        blk = cb * LANES // width
        return pl.BlockSpec((1, SUBLANES, width), lambda b, j: (b, jnp.minimum((j + 1) * hb, n_h - 1), blk))

    def const(shape):
        nd = len(shape)
        return pl.BlockSpec(shape, lambda b, j: (0,) * nd)

    in_specs = []
    for name, width in (("r", c), ("k", c), ("v", c), ("lora", LORA_PAD)):
        in_specs += [prev(col[name], width), main(col[name], width), nxt(col[name], width)]
    in_specs += [const((3, c)), const((3, c)), const((3, c)), const((3, LORA_PAD)),
                 const((2, c)), const((2, DECAY_LORA, c)), const((2, c)), const((2, ICLR_LORA, c)),
                 const((GATE_LORA, c)), const((1, c)), const((1, c)), const((1, c)),
                 const((SCAN_LANES, SCAN_LANES)), const((2, tb, tb))]
    one = pl.BlockSpec((1, tb, c), lambda b, j: (b, j, 0))
    ops = pl.BlockSpec((2, 1, tb, SCAN_PLANES * c), lambda b, j: (0, b, j, 0))
    dec = pl.BlockSpec((2, 1, tb // chunk, 1, c), lambda b, j: (0, b, j, 0, 0))
    s1 = jax.ShapeDtypeStruct((bsz, tt, c), F32)
    sops = jax.ShapeDtypeStruct((2, bsz, tt, SCAN_PLANES * c), BF16)
    sdec = jax.ShapeDtypeStruct((2, bsz, tt // chunk, 1, c), F32)
    args = [proj] * 12 + [shift_rkv[:, :c], shift_rkv[:, c:2 * c], shift_rkv[:, 2 * c:], shift_lora,
                          w0, w_up.astype(BF16), a0, a_up.astype(BF16), g_up.astype(BF16),
                          k_k.reshape(1, c), k_a.reshape(1, c), r_k.reshape(1, c), ones_bd, cmask]
    return pl.pallas_call(
        functools.partial(_feat_kernel, n_lat=n_lat, n_tot=n_tot, chunk=chunk),
        grid=(bsz, n_tot),
        in_specs=in_specs,
        out_specs=[one, one, ops, dec],
        out_shape=[s1, s1, sops, sdec],
        compiler_params=_cparams(("arbitrary", "arbitrary")),
        name="rwkv_features",
    )(*args)


def _scan_kernel(gm_ref, bdm_ref, hm_ref, eye_ref, *refs, chunk, nsub):
    ops_refs = refs[0:4:2]
    dec_refs = refs[1:4:2]
    y_refs = refs[4:6]
    s_ref = refs[6]
    heads = SCAN_LANES // RWKV_HEAD

    @pl.when(pl.program_id(2) == 0)
    def _():
        s_ref[...] = jnp.zeros_like(s_ref)

    bdm = bdm_ref[...]
    bdm_h = bdm.astype(BF16)
    eye = eye_ref[...]
    hms = [hm_ref[h] for h in range(heads)]

    def rows_bd(m):
        return jnp.concatenate([m * hms[h] for h in range(heads)], axis=0)

    def blocks_bd(m):
        return jnp.concatenate([m] * heads, axis=0) * bdm_h

    steps = chunk.bit_length() - 2
    units = [(d, q) for d in range(2) for q in range(nsub)]
    n_u = len(units)

    def lanes(q):
        return slice(q * SCAN_LANES, (q + 1) * SCAN_LANES)

    def plane(d, q, p):
        return ops_refs[d][0, 0, :, lanes(q * SCAN_PLANES + p)]

    kt = [plane(d, q, 2) for d, q in units]
    bt = [plane(d, q, 3) for d, q in units]
    v = [plane(d, q, 4) for d, q in units]
    x = [jnp.concatenate([plane(d, q, 0), plane(d, q, 1)], axis=0) for d, q in units]
    g = [_dot_nt(x[i], jnp.concatenate([rows_bd(kt[i]), rows_bd(bt[i])], axis=0)) * gm_ref[units[i][0]]
         for i in range(n_u)]
    s_old = [s_ref[i] for i in range(n_u)]
    xs = [_dot_nt(x[i], s_old[i].astype(BF16)) for i in range(n_u)]
    gv = [_dot(g[i][:, :SCAN_LANES].astype(BF16), rows_bd(v[i])) for i in range(n_u)]
    rhs = [xs[i][:chunk] + gv[i][:chunk] for i in range(n_u)]
    t = [eye - g[i][:chunk, SCAN_LANES:] for i in range(n_u)]
    lm = [g[i][:chunk, SCAN_LANES:].astype(BF16) for i in range(n_u)]
    pw = [_dot(lm[i], blocks_bd(lm[i])).astype(BF16) for i in range(n_u)]
    for _ in range(steps - 1):
        both = [_dot(jnp.concatenate([t[i].astype(BF16), pw[i]], axis=0), blocks_bd(pw[i])) for i in range(n_u)]
        t = [t[i] + both[i][:chunk] for i in range(n_u)]
        pw = [both[i][chunk:].astype(BF16) for i in range(n_u)]
    t = [t[i] + _dot(t[i].astype(BF16), blocks_bd(pw[i])) for i in range(n_u)]
    u = [(-_dot(t[i].astype(BF16), rows_bd(rhs[i].astype(BF16)))).astype(BF16) for i in range(n_u)]
    for i, (d, q) in enumerate(units):
        y_refs[d][0, :, lanes(q)] = (xs[i][chunk:] + gv[i][chunk:]
                                     + _dot(g[i][chunk:, SCAN_LANES:].astype(BF16), rows_bd(u[i])))
    for i, (d, q) in enumerate(units):
        upd = _dot_tn(jnp.concatenate([v[i], u[i]], axis=0), jnp.concatenate([kt[i], bt[i]], axis=0))
        s_ref[i] = (s_old[i] + upd) * dec_refs[d][0, 0, 0, :, lanes(q)] * bdm


def _scan(ops, dec, t_lat, lane_block):
    _, bsz, tt, c = ops.shape
    c //= SCAN_PLANES
    chunk = SCAN_CHUNK
    heads = SCAN_LANES // RWKV_HEAD
    n_tot = tt // chunk
    n_lat = t_lat // chunk
    n_ctx = n_tot - n_lat
    nsub = lane_block // SCAN_LANES

    ti = np.arange(chunk)
    before = [ti[None, :] < ti[:, None], ti[None, :] > ti[:, None]]
    gmask = np.stack([np.concatenate([np.tile(before[d], (1, 2 * heads)),
                                      np.tile(before[d] | np.eye(chunk, dtype=bool), (1, 2 * heads))], axis=0)
                      for d in range(2)]).astype(np.float32)
    li = np.arange(SCAN_LANES) // RWKV_HEAD
    bdm = (li[:, None] == li[None, :]).astype(np.float32)
    hmask = jnp.asarray((li[None, None, :] == np.arange(heads)[:, None, None]), BF16)
    eye = np.tile(np.eye(chunk, dtype=np.float32), (1, heads))

    def chunk_index(d, s):
        return jnp.where(s < n_ctx, n_lat + s, s - n_ctx) if d == 0 else n_tot - 1 - s

    def const(shape):
        nd = len(shape)
        return pl.BlockSpec(shape, lambda b, h, s: (0,) * nd)

    in_specs = [const(gmask.shape), const(bdm.shape), const(hmask.shape), const(eye.shape)]
    args = [jnp.asarray(gmask), jnp.asarray(bdm), hmask, jnp.asarray(eye)]
    out_specs = []
    for d in range(2):
        in_specs += [pl.BlockSpec((1, 1, chunk, SCAN_PLANES * lane_block),
                                  lambda b, h, s, d=d: (d, b, chunk_index(d, s), h)),
                     pl.BlockSpec((1, 1, 1, 1, lane_block), lambda b, h, s, d=d: (d, b, chunk_index(d, s), 0, h))]
        args += [ops, dec]
        out_specs.append(pl.BlockSpec((1, chunk, lane_block), lambda b, h, s, d=d: (b, chunk_index(d, s), h)))
    ys = jax.ShapeDtypeStruct((bsz, tt, c), F32)
    return pl.pallas_call(
        functools.partial(_scan_kernel, chunk=chunk, nsub=nsub),
        grid=(bsz, c // lane_block, n_tot),
        in_specs=in_specs,
        out_specs=out_specs,
        out_shape=[ys, ys],
        scratch_shapes=[pltpu.VMEM((2 * nsub, SCAN_LANES, SCAN_LANES), F32)],
        compiler_params=_cparams(("arbitrary", "arbitrary", "arbitrary")),
        name="rwkv_scan",
    )(*args)


def _rout_kernel(yf, yb, bonus, g, gng, gnb, ones_ref, o_ref):
    ones_bd = ones_ref[...]
    inv_n = 1.0 / RWKV_HEAD
    y = yf[0] + yb[0]
    mu = _head_sum(y, ones_bd) * inv_n
    yc = y - mu
    var = _head_sum(yc * yc, ones_bd) * inv_n
    yn = yc * lax.rsqrt(var + GN_EPS) * gng[...] + gnb[...]
    o_ref[0] = ((yn + bonus[0]) * g[0]).astype(o_ref.dtype)


def _rwkv_out(yf, yb, bonus, g, gn_g, gn_b, t_lat, tb):
    bsz, _, c = g.shape
    li = np.arange(SCAN_LANES) // RWKV_HEAD
    ones_bd = jnp.asarray(li[:, None] == li[None, :], BF16)
    one = pl.BlockSpec((1, tb, c), lambda b, j: (b, j, 0))

    def const(shape):
        return pl.BlockSpec(shape, lambda b, j: (0, 0))

    return pl.pallas_call(
        _rout_kernel,
        grid=(bsz, t_lat // tb),
        in_specs=[one, one, one, one, const((1, c)), const((1, c)), const((SCAN_LANES, SCAN_LANES))],
        out_specs=one,
        out_shape=jax.ShapeDtypeStruct((bsz, t_lat, c), BF16),
        compiler_params=_cparams(("arbitrary", "arbitrary")),
        name="rwkv_out",
    )(yf, yb, bonus, g, gn_g.reshape(1, c), gn_b.reshape(1, c), ones_bd)


def _rope(x, c, s):
    return x * c + pltpu.roll(x, ATT_HEAD // 2, 1) * s


def _attn_kernel(sink_ref, q_ref, km_ref, k0_ref, kp_ref, vm_ref, v0_ref, vp_ref, kc_ref, vc_ref,
                 c0_ref, s0_ref, cm_ref, sm_ref, cp_ref, sp_ref, o_ref, *, n_blk):
    n = pl.program_id(1)
    blk = ATT_HEAD
    scale = ATT_HEAD ** -0.5
    kvh = range(ATT_KV_HEADS)
    c0, s0 = c0_ref[...], s0_ref[...]
    cm, sm = cm_ref[...], sm_ref[...]
    cp, sp = cp_ref[...], sp_ref[...]

    def head(ref, i):
        return ref[0, :, i * ATT_HEAD:(i + 1) * ATT_HEAD]

    qq = [jnp.concatenate([_rope(head(q_ref, g * ATT_GROUPS + h), c0, s0) for h in range(ATT_GROUPS)],
                          axis=0).astype(BF16) for g in kvh]
    kw = [jnp.concatenate([_rope(head(km_ref, g), cm, sm), _rope(head(k0_ref, g), c0, s0),
                           _rope(head(kp_ref, g), cp, sp)], axis=0).astype(BF16) for g in kvh]
    vw = [jnp.concatenate([head(vm_ref, g), head(v0_ref, g), head(vp_ref, g)], axis=0).astype(BF16) for g in kvh]
    kc = [head(kc_ref, g).astype(BF16) for g in kvh]
    vc = [head(vc_ref, g).astype(BF16) for g in kvh]
    ii = lax.broadcasted_iota(jnp.int32, (ATT_GROUPS * blk, 3 * blk), 0) % blk
    jj = lax.broadcasted_iota(jnp.int32, (ATT_GROUPS * blk, 3 * blk), 1)
    in_prev = jnp.logical_and(jnp.logical_and(jj < blk, jj >= ii), n > 0)
    in_self = jnp.logical_and(jj >= blk, jj < 2 * blk)
    in_next = jnp.logical_and(jnp.logical_and(jj >= 2 * blk, jj - 2 * blk <= ii), n < n_blk - 1)
    valid = jnp.logical_or(jnp.logical_or(in_prev, in_self), in_next)
    s_w = [jnp.where(valid, _dot_nt(qq[g], kw[g]) * scale, NEG_INF) for g in kvh]
    s_c = [_dot_nt(qq[g], kc[g]) * scale for g in kvh]
    s_s = [jnp.concatenate([jnp.full((blk, 1), sink_ref[g * ATT_GROUPS + h], F32) for h in range(ATT_GROUPS)],
                           axis=0) for g in kvh]
    m = [jnp.maximum(jnp.maximum(jnp.max(s_w[g], axis=-1, keepdims=True),
                                 jnp.max(s_c[g], axis=-1, keepdims=True)), s_s[g]) for g in kvh]
    p_w = [jnp.exp(s_w[g] - m[g]) for g in kvh]
    p_c = [jnp.exp(s_c[g] - m[g]) for g in kvh]
    den = [jnp.sum(p_w[g], axis=-1, keepdims=True) + jnp.sum(p_c[g], axis=-1, keepdims=True)
           + jnp.exp(s_s[g] - m[g]) for g in kvh]
    o = [(_dot(p_w[g].astype(BF16), vw[g]) + _dot(p_c[g].astype(BF16), vc[g])) / den[g] for g in kvh]
    o_ref[0] = jnp.concatenate([o[g][h * blk:(h + 1) * blk] for g in kvh for h in range(ATT_GROUPS)],
                               axis=1).astype(o_ref.dtype)


def _attention(proj, col, sink, cos2, sin2, t_lat):
    bsz, tt, _ = proj.shape
    blk = ATT_HEAD
    n_blk = t_lat // blk
    l = tt - t_lat
    qw = ATT_KV_HEADS * ATT_GROUPS * ATT_HEAD
    kw = ATT_KV_HEADS * ATT_HEAD
    q_blk = col["q"] * LANES // qw

    def kv(name, off):
        cb = col[name] * LANES // kw
        return pl.BlockSpec((1, blk, kw), lambda b, n: (b, jnp.clip(n + off, 0, n_blk - 1), cb))

    def ctx(name):
        cb = col[name] * LANES // kw
        return pl.BlockSpec((1, l, kw), lambda b, n: (b, t_lat // l, cb))

    def tab(off):
        return pl.BlockSpec((blk, ATT_HEAD), lambda b, n: (jnp.clip(n + off, 0, n_blk - 1), 0))

    in_specs = [pl.BlockSpec(memory_space=pltpu.SMEM),
                pl.BlockSpec((1, blk, qw), lambda b, n: (b, n, q_blk)),
                kv("ak", -1), kv("ak", 0), kv("ak", 1), kv("av", -1), kv("av", 0), kv("av", 1),
                ctx("ak"), ctx("av"), tab(0), tab(0), tab(-1), tab(-1), tab(1), tab(1)]
    return pl.pallas_call(
        functools.partial(_attn_kernel, n_blk=n_blk),
        grid=(bsz, n_blk),
        in_specs=in_specs,
        out_specs=pl.BlockSpec((1, blk, qw), lambda b, n: (b, n, 0)),
        out_shape=jax.ShapeDtypeStruct((bsz, t_lat, qw), BF16),
        compiler_params=_cparams(("arbitrary", "arbitrary")),
        name="window_attention",
    )(sink, proj, proj, proj, proj, proj, proj, proj, proj, proj, cos2, sin2, cos2, sin2, cos2, sin2)


def _merge_kernel(o1_ref, o2_ref, w1_ref, w2_ref, g1_ref, g2_ref, o_ref):
    y1 = _dot(o1_ref[0], w1_ref[...])
    y2 = _dot(o2_ref[0], w2_ref[...])
    o_ref[0] = (jax.nn.sigmoid(g1_ref[0].astype(F32)) * y1
                + jax.nn.sigmoid(g2_ref[0].astype(F32)) * y2).astype(o_ref.dtype)


def _merge(o_rwkv, o_att, w1, w2, gates, tm, tn):
    bsz, t, c = o_rwkv.shape
    n = w1.shape[1]
    gr = 0
    ga = n // tn
    return pl.pallas_call(
        _merge_kernel,
        grid=(bsz, t // tm, n // tn),
        in_specs=[pl.BlockSpec((1, tm, c), lambda b, i, j: (b, i, 0)),
                  pl.BlockSpec((1, tm, c), lambda b, i, j: (b, i, 0)),
                  pl.BlockSpec((c, tn), lambda b, i, j: (0, j)),
                  pl.BlockSpec((c, tn), lambda b, i, j: (0, j)),
                  pl.BlockSpec((1, tm, tn), lambda b, i, j: (b, i, gr + j)),
                  pl.BlockSpec((1, tm, tn), lambda b, i, j: (b, i, ga + j))],
        out_specs=pl.BlockSpec((1, tm, tn), lambda b, i, j: (b, i, j)),
        out_shape=jax.ShapeDtypeStruct((bsz, t, n), BF16),
        compiler_params=_cparams(("arbitrary", "arbitrary", "arbitrary")),
        name="gated_merge",
    )(o_rwkv, o_att, w1, w2, gates, gates)


def _mm_resid_kernel(a_ref, w_ref, x_ref, gt_ref, o_ref, *, tiles_per_batch):
    bi = pl.program_id(0) // tiles_per_batch
    y = _dot(a_ref[...], w_ref[...])
    o_ref[...] = DEEPNORM_ALPHA * x_ref[...] + gt_ref[pl.ds(bi, 1), :] * y


def _mm_resid(a, w, x, mod, gate_col, rows_per_batch, tm, tn, name):
    m, k = a.shape
    n = w.shape[1]
    gcb = gate_col * (n // tn)
    return pl.pallas_call(
        functools.partial(_mm_resid_kernel, tiles_per_batch=rows_per_batch // tm),
        grid=(m // tm, n // tn),
        in_specs=[pl.BlockSpec((tm, k), lambda i, j: (i, 0)),
                  pl.BlockSpec((k, tn), lambda i, j: (0, j)),
                  pl.BlockSpec((tm, tn), lambda i, j: (i, j)),
                  pl.BlockSpec((MOD_ROWS, tn), lambda i, j: (0, gcb + j))],
        out_specs=pl.BlockSpec((tm, tn), lambda i, j: (i, j)),
        out_shape=jax.ShapeDtypeStruct((m, n), F32),
        compiler_params=_cparams(("arbitrary", "arbitrary")),
        name=name,
    )(a, w, x, mod)


def _ln_kernel(z_ref, g_ref, b_ref, *rest, with_mod):
    bi = pl.program_id(0)
    z = z_ref[0]
    mu = jnp.mean(z, axis=-1, keepdims=True)
    zc = z - mu
    var = jnp.mean(zc * zc, axis=-1, keepdims=True)
    out = zc * lax.rsqrt(var + LN_EPS) * g_ref[...] + b_ref[...]
    if with_mod:
        sh_ref, sc_ref, o_ref, h_ref = rest
        o_ref[0] = out
        h_ref[0] = (out * (1.0 + sc_ref[pl.ds(bi, 1), :]) + sh_ref[pl.ds(bi, 1), :]).astype(BF16)
    else:
        (o_ref,) = rest
        o_ref[0] = out


def _res_ln(z, mod, g, b, tb, mod_cols=None):
    bsz, t, d = z.shape
    blk = pl.BlockSpec((1, tb, d), lambda bi, j: (bi, j, 0))
    vec = pl.BlockSpec((1, d), lambda bi, j: (0, 0))

    def modspec(cb):
        return pl.BlockSpec((MOD_ROWS, d), lambda bi, j: (0, cb))

    in_specs = [blk, vec, vec]
    args = [z, g.reshape(1, d), b.reshape(1, d)]
    out_specs = [blk]
    out_shape = [jax.ShapeDtypeStruct((bsz, t, d), F32)]
    if mod_cols is not None:
        in_specs += [modspec(mod_cols[0]), modspec(mod_cols[1])]
        args += [mod, mod]
        out_specs.append(blk)
        out_shape.append(jax.ShapeDtypeStruct((bsz, t, d), BF16))
    return pl.pallas_call(
        functools.partial(_ln_kernel, with_mod=mod_cols is not None),
        grid=(bsz, t // tb),
        in_specs=in_specs,
        out_specs=out_specs,
        out_shape=out_shape,
        compiler_params=_cparams(("arbitrary", "arbitrary")),
        name="residual_layernorm",
    )(*args)


def _ffn_up_kernel(a_ref, wg_ref, wu_ref, o_ref):
    a = a_ref[...]
    gg = _dot(a, wg_ref[...].astype(BF16))
    uu = _dot(a, wu_ref[...].astype(BF16))
    o_ref[...] = (gg * jax.nn.sigmoid(gg) * uu).astype(o_ref.dtype)


def _ffn_up(a, wg, wu, tm, tf):
    m, k = a.shape
    f = wg.shape[1]
    return pl.pallas_call(
        _ffn_up_kernel,
        grid=(m // tm, f // tf),
        in_specs=[pl.BlockSpec((tm, k), lambda i, j: (i, 0)),
                  pl.BlockSpec((k, tf), lambda i, j: (0, j)),
                  pl.BlockSpec((k, tf), lambda i, j: (0, j))],
        out_specs=pl.BlockSpec((tm, tf), lambda i, j: (i, j)),
        out_shape=jax.ShapeDtypeStruct((m, f), BF16),
        compiler_params=_cparams(("arbitrary", "arbitrary")),
        name="swiglu_up",
    )(a, wg, wu)


def _rope_tables(t):
    rows = t // GRID_W
    row = jnp.broadcast_to(jnp.arange(rows, dtype=F32)[:, None], (rows, GRID_W)).reshape(t)
    colp = jnp.broadcast_to(jnp.arange(GRID_W, dtype=F32)[None, :], (rows, GRID_W)).reshape(t)
    axis_dim = ATT_HEAD // 2
    inv = ROPE_BASE ** (-jnp.arange(0, axis_dim, 2, dtype=F32) / axis_dim)
    ang = jnp.concatenate([row[:, None] * inv, colp[:, None] * inv], -1)
    cos, sin = jnp.cos(ang), jnp.sin(ang)
    return jnp.concatenate([cos, cos], -1), jnp.concatenate([-sin, sin], -1)


def _block(x, c, ctx, c_ctx, w_ada, b_ada, w_in, rwkv_shift, rwkv_w0, rwkv_w_up, rwkv_a0, rwkv_a_up,
           rwkv_g_up, rwkv_k_k, rwkv_k_a, rwkv_r_k, rwkv_gn_g, rwkv_gn_b, attn_sink, w_rwkv_o, w_att_o,
           w_out, ln1_g, ln1_b, w_ff_gate, w_ff_up, w_ff_down, ln2_g, ln2_b, tiles):
    bsz, t, d = x.shape
    l = ctx.shape[1]
    tt = t + l
    crw = rwkv_w0.shape[1]
    d_att = ATT_KV_HEADS * ATT_GROUPS * ATT_HEAD
    d_kv = ATT_KV_HEADS * ATT_HEAD
    n_lora = 2 * DECAY_LORA + 2 * ICLR_LORA + GATE_LORA
    n_rw = 3 * crw + n_lora

    cc = jnp.zeros((MOD_ROWS, d), F32).at[:bsz].set(c).at[bsz].set(c_ctx)
    mod = _ada(cc, w_ada, b_ada)

    o_q = n_rw
    o_g = o_q + d_att + 2 * d_kv
    w_rest = _stage_columns(w_in, [(0, 3 * crw), (o_q, o_g - o_q), (3 * crw, n_lora)], "stage_w_rest")
    w_gates = _stage_columns(w_in, [(o_g, w_in.shape[1] - o_g)], "stage_w_gates")
    col = {}
    off = 0
    for name, width in (("r", crw), ("k", crw), ("v", crw), ("q", d_att), ("ak", d_kv), ("av", d_kv),
                        ("lora", n_lora)):
        col[name] = off // LANES
        off += width
    shift_rkv = rwkv_shift[:, :3 * crw]
    shift_lora = jnp.concatenate([rwkv_shift[:, 3 * crw:], jnp.zeros((3, LORA_PAD - n_lora), F32)], axis=1)

    h = _prep(x, ctx, mod)
    proj, w_out16, w_ro16, w_ao16 = _mm_wstat(
        h.reshape(bsz * tt, d), w_rest, tiles["proj_tm"], tiles["proj_tn"], F32, "in_proj",
        side=[(w_out, tiles["side_rows"]), (w_rwkv_o, tiles["side_rows"]), (w_att_o, tiles["side_rows"])])
    proj = proj.reshape(bsz, tt, -1)
    gates, w_down16 = _mm_lead_rows(h, w_gates, t, tiles["gate_tm"], tiles["gate_tn"], BF16, "gate_proj",
                                    side=[(w_ff_down, tiles["side_rows_down"])])

    g, bonus, ops, dec = _features(proj, col, shift_rkv, shift_lora, rwkv_w0, rwkv_w_up, rwkv_a0, rwkv_a_up,
                                   rwkv_g_up, rwkv_k_k, rwkv_k_a, rwkv_r_k.reshape(-1), t, tiles["feat_tb"])
    yf, yb = _scan(ops, dec, t, tiles["scan_lanes"])
    o_rwkv = _rwkv_out(yf, yb, bonus, g, rwkv_gn_g, rwkv_gn_b, t, tiles["feat_tb"])

    cos2, sin2 = _rope_tables(t)
    o_att = _attention(proj, col, attn_sink, cos2, sin2, t)

    ym = _merge(o_rwkv, o_att, w_ro16, w_ao16, gates, tiles["merge_tm"], tiles["merge_tn"])
    z1 = _mm_resid(ym.reshape(bsz * t, d), w_out16, x.reshape(bsz * t, d), mod, 2, t,
                   tiles["out_tm"], tiles["out_tn"], "out_proj")
    x1, h2 = _res_ln(z1.reshape(bsz, t, d), mod, ln1_g, ln1_b, tiles["ln_tb"], mod_cols=(3, 4))

    u = _ffn_up(h2.reshape(bsz * t, d), w_ff_gate, w_ff_up, tiles["ffn_tm"], tiles["ffn_tf"])
    z2 = _mm_resid(u, w_down16, x1.reshape(bsz * t, d), mod, 5, t,
                   tiles["down_tm"], tiles["down_tn"], "swiglu_down")
    (out,) = _res_ln(z2.reshape(bsz, t, d), mod, ln2_g, ln2_b, tiles["ln_tb"])
    return out


_TILES = dict(proj_tm=512, proj_tn=1024, gate_tm=1024, gate_tn=1024, side_rows=128, side_rows_down=256, feat_tb=128, scan_lanes=2048, merge_tm=1024, merge_tn=1024, out_tm=1024, out_tn=1024,
              ln_tb=512, ffn_tm=2048, ffn_tf=256, down_tm=512, down_tn=512)


def kernel(x, c, ctx, c_ctx, w_ada, b_ada, w_in, rwkv_shift, rwkv_w0, rwkv_w_up, rwkv_a0, rwkv_a_up, rwkv_g_up, rwkv_k_k, rwkv_k_a, rwkv_r_k, rwkv_gn_g, rwkv_gn_b, attn_sink, w_rwkv_o, w_att_o, w_out, ln1_g, ln1_b, w_ff_gate, w_ff_up, w_ff_down, ln2_g, ln2_b):
    assert w_ada.shape[0] == DEPTH
    return _block(x, c, ctx, c_ctx, w_ada[0], b_ada[0], w_in[0], rwkv_shift[0], rwkv_w0[0], rwkv_w_up[0],
                  rwkv_a0[0], rwkv_a_up[0], rwkv_g_up[0], rwkv_k_k[0], rwkv_k_a[0], rwkv_r_k[0], rwkv_gn_g[0],
                  rwkv_gn_b[0], attn_sink[0], w_rwkv_o[0], w_att_o[0], w_out[0], ln1_g[0], ln1_b[0],
                  w_ff_gate[0], w_ff_up[0], w_ff_down[0], ln2_g[0], ln2_b[0], _TILES)
```

```python
import functools

import jax
import jax.numpy as jnp
import numpy as np
from jax import lax
from jax.experimental import pallas as pl
from jax.experimental.pallas import tpu as pltpu

F32 = jnp.float32
BF16 = jnp.bfloat16

RWKV_HEAD = 64
DECAY_LORA = 96
ICLR_LORA = 96
GATE_LORA = 256
GN_EPS = 64e-5
ATT_HEAD = 128
ATT_KV_HEADS = 4
ATT_GROUPS = 4
GRID_W = 64
ROPE_BASE = 10000.0
LN_EPS = 1e-5
DEPTH = 1
DEEPNORM_ALPHA = (2 * DEPTH) ** 0.25
NEG_INF = -1e30

LANES = 128
SUBLANES = 8
VMEM_LIMIT = 56 * 1024 * 1024
MOD_ROWS = 8
SCAN_CHUNK = 64
SCAN_LANES = 256
SCAN_PLANES = 5
LORA_PAD = 1024


def _cparams(sem):
    return pltpu.CompilerParams(dimension_semantics=sem, vmem_limit_bytes=VMEM_LIMIT)


def _dot(a, b, precision=None):
    return jnp.dot(a, b, preferred_element_type=F32, precision=precision)


def _dot_nt(a, b, precision=None):
    return lax.dot_general(a, b, (((1,), (1,)), ((), ())), preferred_element_type=F32, precision=precision)


def _dot_tn(a, b, precision=None):
    return lax.dot_general(a, b, (((0,), (0,)), ((), ())), preferred_element_type=F32, precision=precision)


def _split2(x):
    hi = x.astype(BF16)
    return hi, (x - hi.astype(F32)).astype(BF16)


def _head_sum(x, ones_bd):
    rows = x.shape[0]
    both = jnp.concatenate(_split2(x), axis=0)
    out = []
    for s in range(x.shape[1] // SCAN_LANES):
        part = _dot(both[:, s * SCAN_LANES:(s + 1) * SCAN_LANES], ones_bd)
        out.append(part[:rows] + part[rows:])
    return jnp.concatenate(out, axis=1)


def _ada_kernel(c_ref, w_ref, b_ref, o_ref):
    a = c_ref[...]
    a = a * jax.nn.sigmoid(a)
    a_hi, a_lo = _split2(a)
    w_hi, w_lo = _split2(w_ref[...])
    part = _dot(jnp.concatenate([a_hi, a_lo], axis=0), w_hi)
    o_ref[...] = part[:MOD_ROWS] + part[MOD_ROWS:] + _dot(a_hi, w_lo) + b_ref[...]


def _ada(cc, w_ada, b_ada):
    d, n = w_ada.shape
    tn = 512
    return pl.pallas_call(
        _ada_kernel,
        grid=(n // tn,),
        in_specs=[pl.BlockSpec((MOD_ROWS, d), lambda j: (0, 0)),
                  pl.BlockSpec((d, tn), lambda j: (0, j)),
                  pl.BlockSpec((1, tn), lambda j: (0, j))],
        out_specs=pl.BlockSpec((MOD_ROWS, tn), lambda j: (0, j)),
        out_shape=jax.ShapeDtypeStruct((MOD_ROWS, n), F32),
        compiler_params=_cparams(("arbitrary",)),
        name="ada",
    )(cc, w_ada, b_ada.reshape(1, n))


def _prep_kernel(x_ref, ctx_ref, sh_ref, sc_ref, o_ref, *, n_lat, ctx_row):
    b = pl.program_id(0)
    j = pl.program_id(1)
    is_ctx = j >= n_lat
    row = jnp.where(is_ctx, ctx_row, b)
    sh = sh_ref[pl.ds(row, 1), :]
    sc = sc_ref[pl.ds(row, 1), :]
    xin = jnp.where(is_ctx, ctx_ref[0], x_ref[0])
    o_ref[0] = (xin * (1.0 + sc) + sh).astype(BF16)


def _prep(x, ctx, mod):
    bsz, t, d = x.shape
    l = ctx.shape[1]
    tb = l
    n_lat = t // tb
    return pl.pallas_call(
        functools.partial(_prep_kernel, n_lat=n_lat, ctx_row=bsz),
        grid=(bsz, n_lat + 1),
        in_specs=[pl.BlockSpec((1, tb, d), lambda b, j: (b, jnp.minimum(j, n_lat - 1), 0)),
                  pl.BlockSpec((1, l, d), lambda b, j: (b, 0, 0)),
                  pl.BlockSpec((MOD_ROWS, d), lambda b, j: (0, 0)),
                  pl.BlockSpec((MOD_ROWS, d), lambda b, j: (0, 1))],
        out_specs=pl.BlockSpec((1, tb, d), lambda b, j: (b, j, 0)),
        out_shape=jax.ShapeDtypeStruct((bsz, t + l, d), BF16),
        compiler_params=_cparams(("arbitrary", "arbitrary")),
        name="prep",
    )(x, ctx, mod, mod)


def _cast_kernel(w_ref, o_ref):
    o_ref[...] = w_ref[...].astype(o_ref.dtype)


def _stage_columns(w, segments, name):
    rows = w.shape[0]
    starts, src = [], []
    off = 0
    for start, width in segments:
        assert start % LANES == 0 and width % LANES == 0
        starts.append(off // LANES)
        src.append(start // LANES)
        off += width

    def src_block(j):
        blk = j - starts[0] + src[0]
        for s0, b0 in zip(starts[1:], src[1:]):
            blk = jnp.where(j >= s0, j - s0 + b0, blk)
        return blk

    return pl.pallas_call(
        _cast_kernel,
        grid=(off // LANES,),
        in_specs=[pl.BlockSpec((rows, LANES), lambda j: (0, src_block(j)))],
        out_specs=pl.BlockSpec((rows, LANES), lambda j: (0, j)),
        out_shape=jax.ShapeDtypeStruct((rows, off), BF16),
        compiler_params=_cparams(("arbitrary",)),
        name=name,
    )(w)


def _mm_kernel(a_ref, w_ref, *refs, side_blocks, strides, n_steps):
    n_side = len(side_blocks)
    o_ref = refs[n_side]
    if len(o_ref.shape) == 3:
        o_ref[0] = _dot(a_ref[0], w_ref[...]).astype(o_ref.dtype)
    else:
        o_ref[...] = _dot(a_ref[...], w_ref[...]).astype(o_ref.dtype)
    step = sum(pl.program_id(ax) * st for ax, st in enumerate(strides))
    for s_in, s_out, nblk in zip(refs[:n_side], refs[n_side + 1:], side_blocks):
        fresh = jnp.logical_or(step == 0, (step * nblk) // n_steps != ((step - 1) * nblk) // n_steps)

        @pl.when(fresh)
        def _(s_in=s_in, s_out=s_out):
            s_out[...] = s_in[...].astype(s_out.dtype)


def _side_specs(side, n_steps, step_of):
    in_specs, out_specs, out_shape = [], [], []
    for arr, rows in side:
        nblk = arr.shape[0] // rows
        assert arr.shape[0] % rows == 0 and nblk <= n_steps
        spec = pl.BlockSpec((rows, arr.shape[1]), lambda *g, nblk=nblk: ((step_of(*g) * nblk) // n_steps, 0))
        in_specs.append(spec)
        out_specs.append(spec)
        out_shape.append(jax.ShapeDtypeStruct(arr.shape, BF16))
    return in_specs, out_specs, out_shape


def _mm_wstat(a, w, tm, tn, out_dtype, name, side=()):
    m, k = a.shape
    n = w.shape[1]
    nj, ni = pl.cdiv(n, tn), m // tm
    s_in, s_out, s_shape = _side_specs(side, nj * ni, lambda j, i: j * ni + i)
    return pl.pallas_call(
        functools.partial(_mm_kernel, side_blocks=tuple(arr.shape[0] // r for arr, r in side),
                          strides=(ni, 1), n_steps=nj * ni),
        grid=(nj, ni),
        in_specs=[pl.BlockSpec((tm, k), lambda j, i: (i, 0)),
                  pl.BlockSpec((k, tn), lambda j, i: (0, j))] + s_in,
        out_specs=[pl.BlockSpec((tm, tn), lambda j, i: (i, j))] + s_out,
        out_shape=[jax.ShapeDtypeStruct((m, n), out_dtype)] + s_shape,
        compiler_params=_cparams(("arbitrary", "arbitrary")),
        name=name,
    )(a, w, *[arr for arr, _ in side])


def _mm_lead_rows(a, w, rows, tm, tn, out_dtype, name, side=()):
    bsz, _, k = a.shape
    n = w.shape[1]
    nj, ni = n // tn, rows // tm
    s_in, s_out, s_shape = _side_specs(side, nj * bsz * ni, lambda j, b, i: (j * bsz + b) * ni + i)
    return pl.pallas_call(
        functools.partial(_mm_kernel, side_blocks=tuple(arr.shape[0] // r for arr, r in side),
                          strides=(bsz * ni, ni, 1), n_steps=nj * bsz * ni),
        grid=(nj, bsz, ni),
        in_specs=[pl.BlockSpec((1, tm, k), lambda j, b, i: (b, i, 0)),
                  pl.BlockSpec((k, tn), lambda j, b, i: (0, j))] + s_in,
        out_specs=[pl.BlockSpec((1, tm, tn), lambda j, b, i: (b, i, j))] + s_out,
        out_shape=[jax.ShapeDtypeStruct((bsz, rows, n), out_dtype)] + s_shape,
        compiler_params=_cparams(("arbitrary", "arbitrary", "arbitrary")),
        name=name,
    )(a, w, *[arr for arr, _ in side])


def _conv3(prev_ref, cur_ref, next_ref, w_ref, at_start, at_end):
    cur = cur_ref[0]
    rows = cur.shape[0]
    prev_row = jnp.where(at_start, 0.0, prev_ref[0][SUBLANES - 1:SUBLANES, :])
    next_row = jnp.where(at_end, 0.0, next_ref[0][0:1, :])
    up = jnp.concatenate([prev_row, cur[:rows - 1]], axis=0)
    down = jnp.concatenate([cur[1:], next_row], axis=0)
    w = w_ref[...]
    return up * w[0:1] + cur * w[1:2] + down * w[2:3]


def _feat_kernel(rp, rc, rn, kp, kc, kn, vp, vc, vn, lp, lc, ln_,
                 shr, shk, shv, shl, w0, wup, a0, aup, gup, kk_w, ka_w, rk_w, ones_ref, cm_ref,
                 g_o, bonus_o, ops_o, dec_o, *, n_lat, n_tot, chunk):
    j = pl.program_id(1)
    at_start = jnp.logical_or(j == 0, j == n_lat)
    at_end = jnp.logical_or(j == n_lat - 1, j == n_tot - 1)
    r = _conv3(rp, rc, rn, shr, at_start, at_end)
    k = _conv3(kp, kc, kn, shk, at_start, at_end)
    v = _conv3(vp, vc, vn, shv, at_start, at_end)
    lo = _conv3(lp, lc, ln_, shl, at_start, at_end)
    rows = r.shape[0]
    v_h = v.astype(BF16)
    gd = lo[:, 2 * DECAY_LORA + 2 * ICLR_LORA:2 * DECAY_LORA + 2 * ICLR_LORA + GATE_LORA]
    ones_bd = ones_ref[...]
    g_o[0] = _dot(jax.nn.sigmoid(gd).astype(BF16), gup[...])
    kk = k * kk_w[...]
    kk = kk * lax.rsqrt(jnp.maximum(_head_sum(kk * kk, ones_bd), 1e-24))
    k_sum = jnp.zeros_like(k)
    for d in range(2):
        wd = lo[:, d * DECAY_LORA:(d + 1) * DECAY_LORA]
        ad = lo[:, 2 * DECAY_LORA + d * ICLR_LORA:2 * DECAY_LORA + (d + 1) * ICLR_LORA]
        z = w0[d:d + 1, :] + _dot(jnp.tanh(wd).astype(BF16), wup[d])
        lw = jax.nn.sigmoid(z) * (-np.exp(-0.5))
        iclr = jax.nn.sigmoid(a0[d:d + 1, :] + _dot(ad.astype(BF16), aup[d]))
        k_dir = k * (1.0 + (iclr - 1.0) * ka_w[...])
        k_sum = k_sum + k_dir
        lw_hi, lw_lo = _split2(lw)
        cum = _dot(cm_ref[d], lw_hi) + _dot(cm_ref[d], lw_lo)
        p_inv = jnp.exp(-cum)
        planes = [(kk * jnp.exp(cum - lw)).astype(BF16), (r * jnp.exp(cum)).astype(BF16),
                  (k_dir * p_inv).astype(BF16), (kk * iclr * p_inv).astype(BF16), v_h]
        for hg in range(r.shape[1] // SCAN_LANES):
            for p, plane in enumerate(planes):
                dst = (hg * SCAN_PLANES + p) * SCAN_LANES
                ops_o[d, 0, :, dst:dst + SCAN_LANES] = plane[:, hg * SCAN_LANES:(hg + 1) * SCAN_LANES]
        for q in range(rows // chunk):
            last = (q + 1) * chunk - 1 if d == 0 else q * chunk
            dec_o[d, 0, q] = jnp.exp(cum[last:last + 1, :])
    bonus_o[0] = _head_sum(r * k_sum * rk_w[...], ones_bd) * v


def _features(proj, col, shift_rkv, shift_lora, w0, w_up, a0, a_up, g_up, k_k, k_a, r_k, t_lat, tb):
    bsz, tt, _ = proj.shape
    c = w0.shape[1]
    chunk = SCAN_CHUNK
    n_tot = tt // tb
    n_lat = t_lat // tb
    hb = tb // SUBLANES
    n_h = tt // SUBLANES
    li = np.arange(SCAN_LANES) // RWKV_HEAD
    ones_bd = jnp.asarray(li[:, None] == li[None, :], BF16)
    ti = np.arange(tb)
    same = (ti[:, None] // chunk) == (ti[None, :] // chunk)
    cmask = jnp.asarray(np.stack([same & (ti[None, :] <= ti[:, None]), same & (ti[None, :] >= ti[:, None])]), BF16)

    def main(cb, width):
        blk = cb * LANES // width
        return pl.BlockSpec((1, tb, width), lambda b, j: (b, j, blk))

    def prev(cb, width):
        blk = cb * LANES // width
        return pl.BlockSpec((1, SUBLANES, width), lambda b, j: (b, jnp.maximum(j * hb - 1, 0), blk))

    def nxt(cb, width):
        blk = cb * LANES // width
        return pl.BlockSpec((1, SUBLANES, width), lambda b, j: (b, jnp.minimum((j + 1) * hb, n_h - 1), blk))

    def const(shape):
        nd = len(shape)
        return pl.BlockSpec(shape, lambda b, j: (0,) * nd)

    in_specs = []
    for name, width in (("r", c), ("k", c), ("v", c), ("lora", LORA_PAD)):
        in_specs += [prev(col[name], width), main(col[name], width), nxt(col[name], width)]
    in_specs += [const((3, c)), const((3, c)), const((3, c)), const((3, LORA_PAD)),
                 const((2, c)), const((2, DECAY_LORA, c)), const((2, c)), const((2, ICLR_LORA, c)),
                 const((GATE_LORA, c)), const((1, c)), const((1, c)), const((1, c)),
                 const((SCAN_LANES, SCAN_LANES)), const((2, tb, tb))]
    one = pl.BlockSpec((1, tb, c), lambda b, j: (b, j, 0))
    ops = pl.BlockSpec((2, 1, tb, SCAN_PLANES * c), lambda b, j: (0, b, j, 0))
    dec = pl.BlockSpec((2, 1, tb // chunk, 1, c), lambda b, j: (0, b, j, 0, 0))
    s1 = jax.ShapeDtypeStruct((bsz, tt, c), F32)
    sops = jax.ShapeDtypeStruct((2, bsz, tt, SCAN_PLANES * c), BF16)
    sdec = jax.ShapeDtypeStruct((2, bsz, tt // chunk, 1, c), F32)
    args = [proj] * 12 + [shift_rkv[:, :c], shift_rkv[:, c:2 * c], shift_rkv[:, 2 * c:], shift_lora,
                          w0, w_up.astype(BF16), a0, a_up.astype(BF16), g_up.astype(BF16),
                          k_k.reshape(1, c), k_a.reshape(1, c), r_k.reshape(1, c), ones_bd, cmask]
    return pl.pallas_call(
        functools.partial(_feat_kernel, n_lat=n_lat, n_tot=n_tot, chunk=chunk),
        grid=(bsz, n_tot),
        in_specs=in_specs,
        out_specs=[one, one, ops, dec],
        out_shape=[s1, s1, sops, sdec],
        compiler_params=_cparams(("arbitrary", "arbitrary")),
        name="rwkv_features",
    )(*args)


def _scan_kernel(gm_ref, bdm_ref, hm_ref, eye_ref, *refs, chunk, nsub):
    ops_refs = refs[0:4:2]
    dec_refs = refs[1:4:2]
    y_refs = refs[4:6]
    s_ref = refs[6]
    heads = SCAN_LANES // RWKV_HEAD

    @pl.when(pl.program_id(2) == 0)
    def _():
        s_ref[...] = jnp.zeros_like(s_ref)

    bdm = bdm_ref[...]
    bdm_h = bdm.astype(BF16)
    eye = eye_ref[...]
    hms = [hm_ref[h] for h in range(heads)]

    def rows_bd(m):
        return jnp.concatenate([m * hms[h] for h in range(heads)], axis=0)

    def blocks_bd(m):
        return jnp.concatenate([m] * heads, axis=0) * bdm_h

    steps = chunk.bit_length() - 2
    units = [(d, q) for d in range(2) for q in range(nsub)]
    n_u = len(units)

    def lanes(q):
        return slice(q * SCAN_LANES, (q + 1) * SCAN_LANES)

    def plane(d, q, p):
        return ops_refs[d][0, 0, :, lanes(q * SCAN_PLANES + p)]

    kt = [plane(d, q, 2) for d, q in units]
    bt = [plane(d, q, 3) for d, q in units]
    v = [plane(d, q, 4) for d, q in units]
    x = [jnp.concatenate([plane(d, q, 0), plane(d, q, 1)], axis=0) for d, q in units]
    g = [_dot_nt(x[i], jnp.concatenate([rows_bd(kt[i]), rows_bd(bt[i])], axis=0)) * gm_ref[units[i][0]]
         for i in range(n_u)]
    s_old = [s_ref[i] for i in range(n_u)]
    xs = [_dot_nt(x[i], s_old[i].astype(BF16)) for i in range(n_u)]
    gv = [_dot(g[i][:, :SCAN_LANES].astype(BF16), rows_bd(v[i])) for i in range(n_u)]
    rhs = [xs[i][:chunk] + gv[i][:chunk] for i in range(n_u)]
    t = [eye - g[i][:chunk, SCAN_LANES:] for i in range(n_u)]
    lm = [g[i][:chunk, SCAN_LANES:].astype(BF16) for i in range(n_u)]
    pw = [_dot(lm[i], blocks_bd(lm[i])).astype(BF16) for i in range(n_u)]
    for _ in range(steps - 1):
        both = [_dot(jnp.concatenate([t[i].astype(BF16), pw[i]], axis=0), blocks_bd(pw[i])) for i in range(n_u)]
        t = [t[i] + both[i][:chunk] for i in range(n_u)]
        pw = [both[i][chunk:].astype(BF16) for i in range(n_u)]
    t = [t[i] + _dot(t[i].astype(BF16), blocks_bd(pw[i])) for i in range(n_u)]
    u = [(-_dot(t[i].astype(BF16), rows_bd(rhs[i].astype(BF16)))).astype(BF16) for i in range(n_u)]
    for i, (d, q) in enumerate(units):
        y_refs[d][0, :, lanes(q)] = (xs[i][chunk:] + gv[i][chunk:]
                                     + _dot(g[i][chunk:, SCAN_LANES:].astype(BF16), rows_bd(u[i])))
    for i, (d, q) in enumerate(units):
        upd = _dot_tn(jnp.concatenate([v[i], u[i]], axis=0), jnp.concatenate([kt[i], bt[i]], axis=0))
        s_ref[i] = (s_old[i] + upd) * dec_refs[d][0, 0, 0, :, lanes(q)] * bdm


def _scan(ops, dec, t_lat, lane_block):
    _, bsz, tt, c = ops.shape
    c //= SCAN_PLANES
    chunk = SCAN_CHUNK
    heads = SCAN_LANES // RWKV_HEAD
    n_tot = tt // chunk
    n_lat = t_lat // chunk
    n_ctx = n_tot - n_lat
    nsub = lane_block // SCAN_LANES

    ti = np.arange(chunk)
    before = [ti[None, :] < ti[:, None], ti[None, :] > ti[:, None]]
    gmask = np.stack([np.concatenate([np.tile(before[d], (1, 2 * heads)),
                                      np.tile(before[d] | np.eye(chunk, dtype=bool), (1, 2 * heads))], axis=0)
                      for d in range(2)]).astype(np.float32)
    li = np.arange(SCAN_LANES) // RWKV_HEAD
    bdm = (li[:, None] == li[None, :]).astype(np.float32)
    hmask = jnp.asarray((li[None, None, :] == np.arange(heads)[:, None, None]), BF16)
    eye = np.tile(np.eye(chunk, dtype=np.float32), (1, heads))

    def chunk_index(d, s):
        return jnp.where(s < n_ctx, n_lat + s, s - n_ctx) if d == 0 else n_tot - 1 - s

    def const(shape):
        nd = len(shape)
        return pl.BlockSpec(shape, lambda b, h, s: (0,) * nd)

    in_specs = [const(gmask.shape), const(bdm.shape), const(hmask.shape), const(eye.shape)]
    args = [jnp.asarray(gmask), jnp.asarray(bdm), hmask, jnp.asarray(eye)]
    out_specs = []
    for d in range(2):
        in_specs += [pl.BlockSpec((1, 1, chunk, SCAN_PLANES * lane_block),
                                  lambda b, h, s, d=d: (d, b, chunk_index(d, s), h)),
                     pl.BlockSpec((1, 1, 1, 1, lane_block), lambda b, h, s, d=d: (d, b, chunk_index(d, s), 0, h))]
        args += [ops, dec]
        out_specs.append(pl.BlockSpec((1, chunk, lane_block), lambda b, h, s, d=d: (b, chunk_index(d, s), h)))
    ys = jax.ShapeDtypeStruct((bsz, tt, c), F32)
    return pl.pallas_call(
        functools.partial(_scan_kernel, chunk=chunk, nsub=nsub),
        grid=(bsz, c // lane_block, n_tot),
        in_specs=in_specs,
        out_specs=out_specs,
        out_shape=[ys, ys],
        scratch_shapes=[pltpu.VMEM((2 * nsub, SCAN_LANES, SCAN_LANES), F32)],
        compiler_params=_cparams(("arbitrary", "arbitrary", "arbitrary")),
        name="rwkv_scan",
    )(*args)


def _rout_kernel(yf, yb, bonus, g, gng, gnb, ones_ref, o_ref):
    ones_bd = ones_ref[...]
    inv_n = 1.0 / RWKV_HEAD
    y = yf[0] + yb[0]
    mu = _head_sum(y, ones_bd) * inv_n
    yc = y - mu
    var = _head_sum(yc * yc, ones_bd) * inv_n
    yn = yc * lax.rsqrt(var + GN_EPS) * gng[...] + gnb[...]
    o_ref[0] = ((yn + bonus[0]) * g[0]).astype(o_ref.dtype)


def _rwkv_out(yf, yb, bonus, g, gn_g, gn_b, t_lat, tb):
    bsz, _, c = g.shape
    li = np.arange(SCAN_LANES) // RWKV_HEAD
    ones_bd = jnp.asarray(li[:, None] == li[None, :], BF16)
    one = pl.BlockSpec((1, tb, c), lambda b, j: (b, j, 0))

    def const(shape):
        return pl.BlockSpec(shape, lambda b, j: (0, 0))

    return pl.pallas_call(
        _rout_kernel,
        grid=(bsz, t_lat // tb),
        in_specs=[one, one, one, one, const((1, c)), const((1, c)), const((SCAN_LANES, SCAN_LANES))],
        out_specs=one,
        out_shape=jax.ShapeDtypeStruct((bsz, t_lat, c), BF16),
        compiler_params=_cparams(("arbitrary", "arbitrary")),
        name="rwkv_out",
    )(yf, yb, bonus, g, gn_g.reshape(1, c), gn_b.reshape(1, c), ones_bd)


def _rope(x, c, s):
    return x * c + pltpu.roll(x, ATT_HEAD // 2, 1) * s


def _attn_kernel(sink_ref, q_ref, km_ref, k0_ref, kp_ref, vm_ref, v0_ref, vp_ref, kc_ref, vc_ref,
                 c0_ref, s0_ref, cm_ref, sm_ref, cp_ref, sp_ref, o_ref, *, n_blk):
    n = pl.program_id(1)
    blk = ATT_HEAD
    scale = ATT_HEAD ** -0.5
    kvh = range(ATT_KV_HEADS)
    c0, s0 = c0_ref[...], s0_ref[...]
    cm, sm = cm_ref[...], sm_ref[...]
    cp, sp = cp_ref[...], sp_ref[...]

    def head(ref, i):
        return ref[0, :, i * ATT_HEAD:(i + 1) * ATT_HEAD]

    qq = [jnp.concatenate([_rope(head(q_ref, g * ATT_GROUPS + h), c0, s0) for h in range(ATT_GROUPS)],
                          axis=0).astype(BF16) for g in kvh]
    kw = [jnp.concatenate([_rope(head(km_ref, g), cm, sm), _rope(head(k0_ref, g), c0, s0),
                           _rope(head(kp_ref, g), cp, sp)], axis=0).astype(BF16) for g in kvh]
    vw = [jnp.concatenate([head(vm_ref, g), head(v0_ref, g), head(vp_ref, g)], axis=0).astype(BF16) for g in kvh]
    kc = [head(kc_ref, g).astype(BF16) for g in kvh]
    vc = [head(vc_ref, g).astype(BF16) for g in kvh]
    ii = lax.broadcasted_iota(jnp.int32, (ATT_GROUPS * blk, 3 * blk), 0) % blk
    jj = lax.broadcasted_iota(jnp.int32, (ATT_GROUPS * blk, 3 * blk), 1)
    in_prev = jnp.logical_and(jnp.logical_and(jj < blk, jj >= ii), n > 0)
    in_self = jnp.logical_and(jj >= blk, jj < 2 * blk)
    in_next = jnp.logical_and(jnp.logical_and(jj >= 2 * blk, jj - 2 * blk <= ii), n < n_blk - 1)
    valid = jnp.logical_or(jnp.logical_or(in_prev, in_self), in_next)
    s_w = [jnp.where(valid, _dot_nt(qq[g], kw[g]) * scale, NEG_INF) for g in kvh]
    s_c = [_dot_nt(qq[g], kc[g]) * scale for g in kvh]
    s_s = [jnp.concatenate([jnp.full((blk, 1), sink_ref[g * ATT_GROUPS + h], F32) for h in range(ATT_GROUPS)],
                           axis=0) for g in kvh]
    m = [jnp.maximum(jnp.maximum(jnp.max(s_w[g], axis=-1, keepdims=True),
                                 jnp.max(s_c[g], axis=-1, keepdims=True)), s_s[g]) for g in kvh]
    p_w = [jnp.exp(s_w[g] - m[g]) for g in kvh]
    p_c = [jnp.exp(s_c[g] - m[g]) for g in kvh]
    den = [jnp.sum(p_w[g], axis=-1, keepdims=True) + jnp.sum(p_c[g], axis=-1, keepdims=True)
           + jnp.exp(s_s[g] - m[g]) for g in kvh]
    o = [(_dot(p_w[g].astype(BF16), vw[g]) + _dot(p_c[g].astype(BF16), vc[g])) / den[g] for g in kvh]
    o_ref[0] = jnp.concatenate([o[g][h * blk:(h + 1) * blk] for g in kvh for h in range(ATT_GROUPS)],
                               axis=1).astype(o_ref.dtype)


def _attention(proj, col, sink, cos2, sin2, t_lat):
    bsz, tt, _ = proj.shape
    blk = ATT_HEAD
    n_blk = t_lat // blk
    l = tt - t_lat
    qw = ATT_KV_HEADS * ATT_GROUPS * ATT_HEAD
    kw = ATT_KV_HEADS * ATT_HEAD
    q_blk = col["q"] * LANES // qw

    def kv(name, off):
        cb = col[name] * LANES // kw
        return pl.BlockSpec((1, blk, kw), lambda b, n: (b, jnp.clip(n + off, 0, n_blk - 1), cb))

    def ctx(name):
        cb = col[name] * LANES // kw
        return pl.BlockSpec((1, l, kw), lambda b, n: (b, t_lat // l, cb))

    def tab(off):
        return pl.BlockSpec((blk, ATT_HEAD), lambda b, n: (jnp.clip(n + off, 0, n_blk - 1), 0))

    in_specs = [pl.BlockSpec(memory_space=pltpu.SMEM),
                pl.BlockSpec((1, blk, qw), lambda b, n: (b, n, q_blk)),
                kv("ak", -1), kv("ak", 0), kv("ak", 1), kv("av", -1), kv("av", 0), kv("av", 1),
                ctx("ak"), ctx("av"), tab(0), tab(0), tab(-1), tab(-1), tab(1), tab(1)]
    return pl.pallas_call(
        functools.partial(_attn_kernel, n_blk=n_blk),
        grid=(bsz, n_blk),
        in_specs=in_specs,
        out_specs=pl.BlockSpec((1, blk, qw), lambda b, n: (b, n, 0)),
        out_shape=jax.ShapeDtypeStruct((bsz, t_lat, qw), BF16),
        compiler_params=_cparams(("arbitrary", "arbitrary")),
        name="window_attention",
    )(sink, proj, proj, proj, proj, proj, proj, proj, proj, proj, cos2, sin2, cos2, sin2, cos2, sin2)


def _merge_kernel(o1_ref, o2_ref, w1_ref, w2_ref, g1_ref, g2_ref, o_ref):
    y1 = _dot(o1_ref[0], w1_ref[...])
    y2 = _dot(o2_ref[0], w2_ref[...])
    o_ref[0] = (jax.nn.sigmoid(g1_ref[0].astype(F32)) * y1
                + jax.nn.sigmoid(g2_ref[0].astype(F32)) * y2).astype(o_ref.dtype)


def _merge(o_rwkv, o_att, w1, w2, gates, tm, tn):
    bsz, t, c = o_rwkv.shape
    n = w1.shape[1]
    gr = 0
    ga = n // tn
    return pl.pallas_call(
        _merge_kernel,
        grid=(bsz, t // tm, n // tn),
        in_specs=[pl.BlockSpec((1, tm, c), lambda b, i, j: (b, i, 0)),
                  pl.BlockSpec((1, tm, c), lambda b, i, j: (b, i, 0)),
                  pl.BlockSpec((c, tn), lambda b, i, j: (0, j)),
                  pl.BlockSpec((c, tn), lambda b, i, j: (0, j)),
                  pl.BlockSpec((1, tm, tn), lambda b, i, j: (b, i, gr + j)),
                  pl.BlockSpec((1, tm, tn), lambda b, i, j: (b, i, ga + j))],
        out_specs=pl.BlockSpec((1, tm, tn), lambda b, i, j: (b, i, j)),
        out_shape=jax.ShapeDtypeStruct((bsz, t, n), BF16),
        compiler_params=_cparams(("arbitrary", "arbitrary", "arbitrary")),
        name="gated_merge",
    )(o_rwkv, o_att, w1, w2, gates, gates)


def _mm_resid_kernel(a_ref, w_ref, x_ref, gt_ref, o_ref, *, tiles_per_batch):
    bi = pl.program_id(0) // tiles_per_batch
    y = _dot(a_ref[...], w_ref[...])
    o_ref[...] = DEEPNORM_ALPHA * x_ref[...] + gt_ref[pl.ds(bi, 1), :] * y


def _mm_resid(a, w, x, mod, gate_col, rows_per_batch, tm, tn, name):
    m, k = a.shape
    n = w.shape[1]
    gcb = gate_col * (n // tn)
    return pl.pallas_call(
        functools.partial(_mm_resid_kernel, tiles_per_batch=rows_per_batch // tm),
        grid=(m // tm, n // tn),
        in_specs=[pl.BlockSpec((tm, k), lambda i, j: (i, 0)),
                  pl.BlockSpec((k, tn), lambda i, j: (0, j)),
                  pl.BlockSpec((tm, tn), lambda i, j: (i, j)),
                  pl.BlockSpec((MOD_ROWS, tn), lambda i, j: (0, gcb + j))],
        out_specs=pl.BlockSpec((tm, tn), lambda i, j: (i, j)),
        out_shape=jax.ShapeDtypeStruct((m, n), F32),
        compiler_params=_cparams(("arbitrary", "arbitrary")),
        name=name,
    )(a, w, x, mod)


def _ln_kernel(z_ref, g_ref, b_ref, *rest, with_mod):
    bi = pl.program_id(0)
    z = z_ref[0]
    mu = jnp.mean(z, axis=-1, keepdims=True)
    zc = z - mu
    var = jnp.mean(zc * zc, axis=-1, keepdims=True)
    out = zc * lax.rsqrt(var + LN_EPS) * g_ref[...] + b_ref[...]
    if with_mod:
        sh_ref, sc_ref, o_ref, h_ref = rest
        o_ref[0] = out
        h_ref[0] = (out * (1.0 + sc_ref[pl.ds(bi, 1), :]) + sh_ref[pl.ds(bi, 1), :]).astype(BF16)
    else:
        (o_ref,) = rest
        o_ref[0] = out


def _res_ln(z, mod, g, b, tb, mod_cols=None):
    bsz, t, d = z.shape
    blk = pl.BlockSpec((1, tb, d), lambda bi, j: (bi, j, 0))
    vec = pl.BlockSpec((1, d), lambda bi, j: (0, 0))

    def modspec(cb):
        return pl.BlockSpec((MOD_ROWS, d), lambda bi, j: (0, cb))

    in_specs = [blk, vec, vec]
    args = [z, g.reshape(1, d), b.reshape(1, d)]
    out_specs = [blk]
    out_shape = [jax.ShapeDtypeStruct((bsz, t, d), F32)]
    if mod_cols is not None:
        in_specs += [modspec(mod_cols[0]), modspec(mod_cols[1])]
        args += [mod, mod]
        out_specs.append(blk)
        out_shape.append(jax.ShapeDtypeStruct((bsz, t, d), BF16))
    return pl.pallas_call(
        functools.partial(_ln_kernel, with_mod=mod_cols is not None),
        grid=(bsz, t // tb),
        in_specs=in_specs,
        out_specs=out_specs,
        out_shape=out_shape,
        compiler_params=_cparams(("arbitrary", "arbitrary")),
        name="residual_layernorm",
    )(*args)


def _ffn_up_kernel(a_ref, wg_ref, wu_ref, o_ref):
    a = a_ref[...]
    gg = _dot(a, wg_ref[...].astype(BF16))
    uu = _dot(a, wu_ref[...].astype(BF16))
    o_ref[...] = (gg * jax.nn.sigmoid(gg) * uu).astype(o_ref.dtype)


def _ffn_up(a, wg, wu, tm, tf):
    m, k = a.shape
    f = wg.shape[1]
    return pl.pallas_call(
        _ffn_up_kernel,
        grid=(m // tm, f // tf),
        in_specs=[pl.BlockSpec((tm, k), lambda i, j: (i, 0)),
                  pl.BlockSpec((k, tf), lambda i, j: (0, j)),
                  pl.BlockSpec((k, tf), lambda i, j: (0, j))],
        out_specs=pl.BlockSpec((tm, tf), lambda i, j: (i, j)),
        out_shape=jax.ShapeDtypeStruct((m, f), BF16),
        compiler_params=_cparams(("arbitrary", "arbitrary")),
        name="swiglu_up",
    )(a, wg, wu)


def _rope_tables(t):
    rows = t // GRID_W
    row = jnp.broadcast_to(jnp.arange(rows, dtype=F32)[:, None], (rows, GRID_W)).reshape(t)
    colp = jnp.broadcast_to(jnp.arange(GRID_W, dtype=F32)[None, :], (rows, GRID_W)).reshape(t)
    axis_dim = ATT_HEAD // 2
    inv = ROPE_BASE ** (-jnp.arange(0, axis_dim, 2, dtype=F32) / axis_dim)
    ang = jnp.concatenate([row[:, None] * inv, colp[:, None] * inv], -1)
    cos, sin = jnp.cos(ang), jnp.sin(ang)
    return jnp.concatenate([cos, cos], -1), jnp.concatenate([-sin, sin], -1)


def _block(x, c, ctx, c_ctx, w_ada, b_ada, w_in, rwkv_shift, rwkv_w0, rwkv_w_up, rwkv_a0, rwkv_a_up,
           rwkv_g_up, rwkv_k_k, rwkv_k_a, rwkv_r_k, rwkv_gn_g, rwkv_gn_b, attn_sink, w_rwkv_o, w_att_o,
           w_out, ln1_g, ln1_b, w_ff_gate, w_ff_up, w_ff_down, ln2_g, ln2_b, tiles):
    bsz, t, d = x.shape
    l = ctx.shape[1]
    tt = t + l
    crw = rwkv_w0.shape[1]
    d_att = ATT_KV_HEADS * ATT_GROUPS * ATT_HEAD
    d_kv = ATT_KV_HEADS * ATT_HEAD
    n_lora = 2 * DECAY_LORA + 2 * ICLR_LORA + GATE_LORA
    n_rw = 3 * crw + n_lora

    cc = jnp.zeros((MOD_ROWS, d), F32).at[:bsz].set(c).at[bsz].set(c_ctx)
    mod = _ada(cc, w_ada, b_ada)

    o_q = n_rw
    o_g = o_q + d_att + 2 * d_kv
    w_rest = _stage_columns(w_in, [(0, 3 * crw), (o_q, o_g - o_q), (3 * crw, n_lora)], "stage_w_rest")
    w_gates = _stage_columns(w_in, [(o_g, w_in.shape[1] - o_g)], "stage_w_gates")
    col = {}
    off = 0
    for name, width in (("r", crw), ("k", crw), ("v", crw), ("q", d_att), ("ak", d_kv), ("av", d_kv),
                        ("lora", n_lora)):
        col[name] = off // LANES
        off += width
    shift_rkv = rwkv_shift[:, :3 * crw]
    shift_lora = jnp.concatenate([rwkv_shift[:, 3 * crw:], jnp.zeros((3, LORA_PAD - n_lora), F32)], axis=1)

    h = _prep(x, ctx, mod)
    proj, w_out16, w_ro16, w_ao16 = _mm_wstat(
        h.reshape(bsz * tt, d), w_rest, tiles["proj_tm"], tiles["proj_tn"], F32, "in_proj",
        side=[(w_out, tiles["side_rows"]), (w_rwkv_o, tiles["side_rows"]), (w_att_o, tiles["side_rows"])])
    proj = proj.reshape(bsz, tt, -1)
    gates, w_down16 = _mm_lead_rows(h, w_gates, t, tiles["gate_tm"], tiles["gate_tn"], BF16, "gate_proj",
                                    side=[(w_ff_down, tiles["side_rows_down"])])

    g, bonus, ops, dec = _features(proj, col, shift_rkv, shift_lora, rwkv_w0, rwkv_w_up, rwkv_a0, rwkv_a_up,
                                   rwkv_g_up, rwkv_k_k, rwkv_k_a, rwkv_r_k.reshape(-1), t, tiles["feat_tb"])
    yf, yb = _scan(ops, dec, t, tiles["scan_lanes"])
    o_rwkv = _rwkv_out(yf, yb, bonus, g, rwkv_gn_g, rwkv_gn_b, t, tiles["feat_tb"])

    cos2, sin2 = _rope_tables(t)
    o_att = _attention(proj, col, attn_sink, cos2, sin2, t)

    ym = _merge(o_rwkv, o_att, w_ro16, w_ao16, gates, tiles["merge_tm"], tiles["merge_tn"])
    z1 = _mm_resid(ym.reshape(bsz * t, d), w_out16, x.reshape(bsz * t, d), mod, 2, t,
                   tiles["out_tm"], tiles["out_tn"], "out_proj")
    x1, h2 = _res_ln(z1.reshape(bsz, t, d), mod, ln1_g, ln1_b, tiles["ln_tb"], mod_cols=(3, 4))

    u = _ffn_up(h2.reshape(bsz * t, d), w_ff_gate, w_ff_up, tiles["ffn_tm"], tiles["ffn_tf"])
    z2 = _mm_resid(u, w_down16, x1.reshape(bsz * t, d), mod, 5, t,
                   tiles["down_tm"], tiles["down_tn"], "swiglu_down")
    (out,) = _res_ln(z2.reshape(bsz, t, d), mod, ln2_g, ln2_b, tiles["ln_tb"])
    return out


_TILES = dict(proj_tm=512, proj_tn=1024, gate_tm=1024, gate_tn=1024, side_rows=128, side_rows_down=256, feat_tb=128, scan_lanes=2048, merge_tm=1024, merge_tn=1024, out_tm=1024, out_tn=1024,
              ln_tb=512, ffn_tm=2048, ffn_tf=256, down_tm=512, down_tn=512)


def kernel(x, c, ctx, c_ctx, w_ada, b_ada, w_in, rwkv_shift, rwkv_w0, rwkv_w_up, rwkv_a0, rwkv_a_up, rwkv_g_up, rwkv_k_k, rwkv_k_a, rwkv_r_k, rwkv_gn_g, rwkv_gn_b, attn_sink, w_rwkv_o, w_att_o, w_out, ln1_g, ln1_b, w_ff_gate, w_ff_up, w_ff_down, ln2_g, ln2_b):
    assert w_ada.shape[0] == DEPTH
    return _block(x, c, ctx, c_ctx, w_ada[0], b_ada[0], w_in[0], rwkv_shift[0], rwkv_w0[0], rwkv_w_up[0],
                  rwkv_a0[0], rwkv_a_up[0], rwkv_g_up[0], rwkv_k_k[0], rwkv_k_a[0], rwkv_r_k[0], rwkv_gn_g[0],
                  rwkv_gn_b[0], attn_sink[0], w_rwkv_o[0], w_att_o[0], w_out[0], ln1_g[0], ln1_b[0],
                  w_ff_gate[0], w_ff_up[0], w_ff_down[0], ln2_g[0], ln2_b[0], _TILES)
```

```python
import functools

import jax
import jax.numpy as jnp
import numpy as np
from jax import lax
from jax.experimental import pallas as pl
from jax.experimental.pallas import tpu as pltpu

F32 = jnp.float32
BF16 = jnp.bfloat16

RWKV_HEAD = 64
DECAY_LORA = 96
ICLR_LORA = 96
GATE_LORA = 256
GN_EPS = 64e-5
ATT_HEAD = 128
ATT_KV_HEADS = 4
ATT_GROUPS = 4
GRID_W = 64
ROPE_BASE = 10000.0
LN_EPS = 1e-5
DEPTH = 1
DEEPNORM_ALPHA = (2 * DEPTH) ** 0.25
NEG_INF = -1e30

LANES = 128
SUBLANES = 8
VMEM_LIMIT = 56 * 1024 * 1024
MOD_ROWS = 8
SCAN_CHUNK = 64
SCAN_LANES = 256
SCAN_PLANES = 5
LORA_PAD = 1024


def _cparams(sem):
    return pltpu.CompilerParams(dimension_semantics=sem, vmem_limit_bytes=VMEM_LIMIT)


def _dot(a, b):
    return jnp.dot(a, b, preferred_element_type=F32)


def _dot_nt(a, b):
    return lax.dot_general(a, b, (((1,), (1,)), ((), ())), preferred_element_type=F32)


def _dot_tn(a, b):
    return lax.dot_general(a, b, (((0,), (0,)), ((), ())), preferred_element_type=F32)


def _split2(x):
    hi = x.astype(BF16)
    return hi, (x - hi.astype(F32)).astype(BF16)


def _head_sum(x, ones_bd):
    rows = x.shape[0]
    both = jnp.concatenate(_split2(x), axis=0)
    out = []
    for s in range(x.shape[1] // SCAN_LANES):
        part = _dot(both[:, s * SCAN_LANES:(s + 1) * SCAN_LANES], ones_bd)
        out.append(part[:rows] + part[rows:])
    return jnp.concatenate(out, axis=1)


def _ada_kernel(c_ref, w_ref, b_ref, o_ref):
    a = c_ref[...]
    a = a * jax.nn.sigmoid(a)
    a_hi, a_lo = _split2(a)
    w_hi, w_lo = _split2(w_ref[...])
    part = _dot(jnp.concatenate([a_hi, a_lo], axis=0), w_hi)
    o_ref[...] = part[:MOD_ROWS] + part[MOD_ROWS:] + _dot(a_hi, w_lo) + b_ref[...]


def _ada(cc, w_ada, b_ada, tn):
    d, n = w_ada.shape
    return pl.pallas_call(
        _ada_kernel,
        grid=(n // tn,),
        in_specs=[pl.BlockSpec((MOD_ROWS, d), lambda j: (0, 0)),
                  pl.BlockSpec((d, tn), lambda j: (0, j)),
                  pl.BlockSpec((1, tn), lambda j: (0, j))],
        out_specs=pl.BlockSpec((MOD_ROWS, tn), lambda j: (0, j)),
        out_shape=jax.ShapeDtypeStruct((MOD_ROWS, n), F32),
        compiler_params=_cparams(("arbitrary",)),
        name="ada",
    )(cc, w_ada, b_ada.reshape(1, n))


def _prep_kernel(x_ref, ctx_ref, sh_ref, sc_ref, o_ref, *, n_lat, ctx_row):
    b = pl.program_id(0)
    j = pl.program_id(1)
    is_ctx = j >= n_lat
    row = jnp.where(is_ctx, ctx_row, b)
    sh = sh_ref[pl.ds(row, 1), :]
    sc = sc_ref[pl.ds(row, 1), :]
    xin = jnp.where(is_ctx, ctx_ref[0], x_ref[0])
    o_ref[0] = (xin * (1.0 + sc) + sh).astype(BF16)


def _prep(x, ctx, mod):
    bsz, t, d = x.shape
    l = ctx.shape[1]
    tb = l
    n_lat = t // tb
    return pl.pallas_call(
        functools.partial(_prep_kernel, n_lat=n_lat, ctx_row=bsz),
        grid=(bsz, n_lat + 1),
        in_specs=[pl.BlockSpec((1, tb, d), lambda b, j: (b, jnp.minimum(j, n_lat - 1), 0)),
                  pl.BlockSpec((1, l, d), lambda b, j: (b, 0, 0)),
                  pl.BlockSpec((MOD_ROWS, d), lambda b, j: (0, 0)),
                  pl.BlockSpec((MOD_ROWS, d), lambda b, j: (0, 1))],
        out_specs=pl.BlockSpec((1, tb, d), lambda b, j: (b, j, 0)),
        out_shape=jax.ShapeDtypeStruct((bsz, t + l, d), BF16),
        compiler_params=_cparams(("arbitrary", "arbitrary")),
        name="prep",
    )(x, ctx, mod, mod)


def _cast_kernel(w_ref, o_ref):
    o_ref[...] = w_ref[...].astype(o_ref.dtype)


def _stage_columns(w, segments, name):
    rows = w.shape[0]
    starts, src = [], []
    off = 0
    for start, width in segments:
        assert start % LANES == 0 and width % LANES == 0
        starts.append(off // LANES)
        src.append(start // LANES)
        off += width

    def src_block(j):
        blk = j - starts[0] + src[0]
        for s0, b0 in zip(starts[1:], src[1:]):
            blk = jnp.where(j >= s0, j - s0 + b0, blk)
        return blk

    return pl.pallas_call(
        _cast_kernel,
        grid=(off // LANES,),
        in_specs=[pl.BlockSpec((rows, LANES), lambda j: (0, src_block(j)))],
        out_specs=pl.BlockSpec((rows, LANES), lambda j: (0, j)),
        out_shape=jax.ShapeDtypeStruct((rows, off), BF16),
        compiler_params=_cparams(("arbitrary",)),
        name=name,
    )(w)


def _mm_kernel(a_ref, w_ref, *refs, side_blocks, strides, n_steps):
    n_side = len(side_blocks)
    o_ref = refs[n_side]
    if len(o_ref.shape) == 3:
        o_ref[0] = _dot(a_ref[0], w_ref[...]).astype(o_ref.dtype)
    else:
        o_ref[...] = _dot(a_ref[...], w_ref[...]).astype(o_ref.dtype)
    step = sum(pl.program_id(ax) * st for ax, st in enumerate(strides))
    for s_in, s_out, nblk in zip(refs[:n_side], refs[n_side + 1:], side_blocks):
        fresh = jnp.logical_or(step == 0, (step * nblk) // n_steps != ((step - 1) * nblk) // n_steps)

        @pl.when(fresh)
        def _(s_in=s_in, s_out=s_out):
            s_out[...] = s_in[...].astype(s_out.dtype)


def _side_specs(side, n_steps, step_of):
    in_specs, out_specs, out_shape = [], [], []
    for arr, rows in side:
        nblk = arr.shape[0] // rows
        assert arr.shape[0] % rows == 0 and nblk <= n_steps
        spec = pl.BlockSpec((rows, arr.shape[1]), lambda *g, nblk=nblk: ((step_of(*g) * nblk) // n_steps, 0))
        in_specs.append(spec)
        out_specs.append(spec)
        out_shape.append(jax.ShapeDtypeStruct(arr.shape, BF16))
    return in_specs, out_specs, out_shape


def _mm_wstat(a, w, tm, tn, out_dtype, name, side=()):
    m, k = a.shape
    n = w.shape[1]
    nj, ni = pl.cdiv(n, tn), m // tm
    s_in, s_out, s_shape = _side_specs(side, nj * ni, lambda j, i: j * ni + i)
    return pl.pallas_call(
        functools.partial(_mm_kernel, side_blocks=tuple(arr.shape[0] // r for arr, r in side),
                          strides=(ni, 1), n_steps=nj * ni),
        grid=(nj, ni),
        in_specs=[pl.BlockSpec((tm, k), lambda j, i: (i, 0)),
                  pl.BlockSpec((k, tn), lambda j, i: (0, j))] + s_in,
        out_specs=[pl.BlockSpec((tm, tn), lambda j, i: (i, j))] + s_out,
        out_shape=[jax.ShapeDtypeStruct((m, n), out_dtype)] + s_shape,
        compiler_params=_cparams(("arbitrary", "arbitrary")),
        name=name,
    )(a, w, *[arr for arr, _ in side])


def _mm_lead_rows(a, w, rows, tm, tn, out_dtype, name, side=()):
    bsz, _, k = a.shape
    n = w.shape[1]
    nj, ni = n // tn, rows // tm
    s_in, s_out, s_shape = _side_specs(side, nj * bsz * ni, lambda j, b, i: (j * bsz + b) * ni + i)
    return pl.pallas_call(
        functools.partial(_mm_kernel, side_blocks=tuple(arr.shape[0] // r for arr, r in side),
                          strides=(bsz * ni, ni, 1), n_steps=nj * bsz * ni),
        grid=(nj, bsz, ni),
        in_specs=[pl.BlockSpec((1, tm, k), lambda j, b, i: (b, i, 0)),
                  pl.BlockSpec((k, tn), lambda j, b, i: (0, j))] + s_in,
        out_specs=[pl.BlockSpec((1, tm, tn), lambda j, b, i: (b, i, j))] + s_out,
        out_shape=[jax.ShapeDtypeStruct((bsz, rows, n), out_dtype)] + s_shape,
        compiler_params=_cparams(("arbitrary", "arbitrary", "arbitrary")),
        name=name,
    )(a, w, *[arr for arr, _ in side])


def _conv3(prev_ref, cur_ref, next_ref, w_ref, at_start, at_end):
    cur = cur_ref[0]
    rows = cur.shape[0]
    prev_row = jnp.where(at_start, 0.0, prev_ref[0][SUBLANES - 1:SUBLANES, :])
    next_row = jnp.where(at_end, 0.0, next_ref[0][0:1, :])
    up = jnp.concatenate([prev_row, cur[:rows - 1]], axis=0)
    down = jnp.concatenate([cur[1:], next_row], axis=0)
    w = w_ref[...]
    return up * w[0:1] + cur * w[1:2] + down * w[2:3]


def _feat_kernel(rp, rc, rn, kp, kc, kn, vp, vc, vn, lp, lc, ln_,
                 shr, shk, shv, shl, w0, wup, a0, aup, gup, kk_w, ka_w, rk_w, ones_ref, cm_ref,
                 g_o, bonus_o, ops_o, dec_o, *, n_lat, n_tot, chunk):
    j = pl.program_id(1)
    at_start = jnp.logical_or(j == 0, j == n_lat)
    at_end = jnp.logical_or(j == n_lat - 1, j == n_tot - 1)
    r = _conv3(rp, rc, rn, shr, at_start, at_end)
    k = _conv3(kp, kc, kn, shk, at_start, at_end)
    v = _conv3(vp, vc, vn, shv, at_start, at_end)
    lo = _conv3(lp, lc, ln_, shl, at_start, at_end)
    rows = r.shape[0]
    v_h = v.astype(BF16)
    gd = lo[:, 2 * DECAY_LORA + 2 * ICLR_LORA:2 * DECAY_LORA + 2 * ICLR_LORA + GATE_LORA]
    ones_bd = ones_ref[...]
    g_o[0] = _dot(jax.nn.sigmoid(gd).astype(BF16), gup[...])
    kk = k * kk_w[...]
    kk = kk * lax.rsqrt(jnp.maximum(_head_sum(kk * kk, ones_bd), 1e-24))
    k_sum = jnp.zeros_like(k)
    for d in range(2):
        wd = lo[:, d * DECAY_LORA:(d + 1) * DECAY_LORA]
        ad = lo[:, 2 * DECAY_LORA + d * ICLR_LORA:2 * DECAY_LORA + (d + 1) * ICLR_LORA]
        z = w0[d:d + 1, :] + _dot(jnp.tanh(wd).astype(BF16), wup[d])
        lw = jax.nn.sigmoid(z) * (-np.exp(-0.5))
        iclr = jax.nn.sigmoid(a0[d:d + 1, :] + _dot(ad.astype(BF16), aup[d]))
        k_dir = k * (1.0 + (iclr - 1.0) * ka_w[...])
        k_sum = k_sum + k_dir
        lw_hi, lw_lo = _split2(lw)
        cum = _dot(cm_ref[d], lw_hi) + _dot(cm_ref[d], lw_lo)
        p_inv = jnp.exp(-cum)
        planes = [(kk * jnp.exp(cum - lw)).astype(BF16), (r * jnp.exp(cum)).astype(BF16),
                  (k_dir * p_inv).astype(BF16), (kk * iclr * p_inv).astype(BF16), v_h]
        for hg in range(r.shape[1] // SCAN_LANES):
            for p, plane in enumerate(planes):
                dst = (hg * SCAN_PLANES + p) * SCAN_LANES
                ops_o[d, 0, :, dst:dst + SCAN_LANES] = plane[:, hg * SCAN_LANES:(hg + 1) * SCAN_LANES]
        for q in range(rows // chunk):
            last = (q + 1) * chunk - 1 if d == 0 else q * chunk
            dec_o[d, 0, q] = jnp.exp(cum[last:last + 1, :])
    bonus_o[0] = _head_sum(r * k_sum * rk_w[...], ones_bd) * v


def _features(proj, col, shift_rkv, shift_lora, w0, w_up, a0, a_up, g_up, k_k, k_a, r_k, t_lat, tb):
    bsz, tt, _ = proj.shape
    c = w0.shape[1]
    chunk = SCAN_CHUNK
    n_tot = tt // tb
    n_lat = t_lat // tb
    hb = tb // SUBLANES
    n_h = tt // SUBLANES
    li = np.arange(SCAN_LANES) // RWKV_HEAD
    ones_bd = jnp.asarray(li[:, None] == li[None, :], BF16)
    ti = np.arange(tb)
    same = (ti[:, None] // chunk) == (ti[None, :] // chunk)
    cmask = jnp.asarray(np.stack([same & (ti[None, :] <= ti[:, None]), same & (ti[None, :] >= ti[:, None])]), BF16)

    def main(cb, width):
        blk = cb * LANES // width
        return pl.BlockSpec((1, tb, width), lambda b, j: (b, j, blk))

    def prev(cb, width):
        blk = cb * LANES // width
        return pl.BlockSpec((1, SUBLANES, width), lambda b, j: (b, jnp.maximum(j * hb - 1, 0), blk))

    def nxt(cb, width):
        blk = cb * LANES // width
        return pl.BlockSpec((1, SUBLANES, width), lambda b, j: (b, jnp.minimum((j + 1) * hb, n_h - 1), blk))

    def const(shape):
        nd = len(shape)
        return pl.BlockSpec(shape, lambda b, j: (0,) * nd)

    in_specs = []
    for name, width in (("r", c), ("k", c), ("v", c), ("lora", LORA_PAD)):
        in_specs += [prev(col[name], width), main(col[name], width), nxt(col[name], width)]
    in_specs += [const((3, c)), const((3, c)), const((3, c)), const((3, LORA_PAD)),
                 const((2, c)), const((2, DECAY_LORA, c)), const((2, c)), const((2, ICLR_LORA, c)),
                 const((GATE_LORA, c)), const((1, c)), const((1, c)), const((1, c)),
                 const((SCAN_LANES, SCAN_LANES)), const((2, tb, tb))]
    one = pl.BlockSpec((1, tb, c), lambda b, j: (b, j, 0))
    ops = pl.BlockSpec((2, 1, tb, SCAN_PLANES * c), lambda b, j: (0, b, j, 0))
    dec = pl.BlockSpec((2, 1, tb // chunk, 1, c), lambda b, j: (0, b, j, 0, 0))
    s1 = jax.ShapeDtypeStruct((bsz, tt, c), F32)
    sops = jax.ShapeDtypeStruct((2, bsz, tt, SCAN_PLANES * c), BF16)
    sdec = jax.ShapeDtypeStruct((2, bsz, tt // chunk, 1, c), F32)
    args = [proj] * 12 + [shift_rkv[:, :c], shift_rkv[:, c:2 * c], shift_rkv[:, 2 * c:], shift_lora,
                          w0, w_up.astype(BF16), a0, a_up.astype(BF16), g_up.astype(BF16),
                          k_k.reshape(1, c), k_a.reshape(1, c), r_k.reshape(1, c), ones_bd, cmask]
    return pl.pallas_call(
        functools.partial(_feat_kernel, n_lat=n_lat, n_tot=n_tot, chunk=chunk),
        grid=(bsz, n_tot),
        in_specs=in_specs,
        out_specs=[one, one, ops, dec],
        out_shape=[s1, s1, sops, sdec],
        compiler_params=_cparams(("arbitrary", "arbitrary")),
        name="rwkv_features",
    )(*args)


def _scan_kernel(gm_ref, bdm_ref, hm_ref, eye_ref, *refs, chunk, nsub):
    ops_refs = refs[0:4:2]
    dec_refs = refs[1:4:2]
    y_refs = refs[4:6]
    s_ref = refs[6]
    heads = SCAN_LANES // RWKV_HEAD

    @pl.when(pl.program_id(2) == 0)
    def _():
        s_ref[...] = jnp.zeros_like(s_ref)

    bdm = bdm_ref[...]
    bdm_h = bdm.astype(BF16)
    eye = eye_ref[...]
    hms = [hm_ref[h] for h in range(heads)]

    def rows_bd(m):
        return jnp.concatenate([m * hms[h] for h in range(heads)], axis=0)

    def blocks_bd(m):
        return jnp.concatenate([m] * heads, axis=0) * bdm_h

    steps = chunk.bit_length() - 2
    units = [(d, q) for d in range(2) for q in range(nsub)]
    n_u = len(units)

    def lanes(q):
        return slice(q * SCAN_LANES, (q + 1) * SCAN_LANES)

    def plane(d, q, p):
        return ops_refs[d][0, 0, :, lanes(q * SCAN_PLANES + p)]

    kt = [plane(d, q, 2) for d, q in units]
    bt = [plane(d, q, 3) for d, q in units]
    v = [plane(d, q, 4) for d, q in units]
    x = [jnp.concatenate([plane(d, q, 0), plane(d, q, 1)], axis=0) for d, q in units]
    g = [_dot_nt(x[i], jnp.concatenate([rows_bd(kt[i]), rows_bd(bt[i])], axis=0)) * gm_ref[units[i][0]]
         for i in range(n_u)]
    s_old = [s_ref[i] for i in range(n_u)]
    xs = [_dot_nt(x[i], s_old[i].astype(BF16)) for i in range(n_u)]
    gv = [_dot(g[i][:, :SCAN_LANES].astype(BF16), rows_bd(v[i])) for i in range(n_u)]
    rhs = [xs[i][:chunk] + gv[i][:chunk] for i in range(n_u)]
    t = [eye - g[i][:chunk, SCAN_LANES:] for i in range(n_u)]
    lm = [g[i][:chunk, SCAN_LANES:].astype(BF16) for i in range(n_u)]
    pw = [_dot(lm[i], blocks_bd(lm[i])).astype(BF16) for i in range(n_u)]
    for _ in range(steps - 1):
        both = [_dot(jnp.concatenate([t[i].astype(BF16), pw[i]], axis=0), blocks_bd(pw[i])) for i in range(n_u)]
        t = [t[i] + both[i][:chunk] for i in range(n_u)]
        pw = [both[i][chunk:].astype(BF16) for i in range(n_u)]
    t = [t[i] + _dot(t[i].astype(BF16), blocks_bd(pw[i])) for i in range(n_u)]
    u = [(-_dot(t[i].astype(BF16), rows_bd(rhs[i].astype(BF16)))).astype(BF16) for i in range(n_u)]
    for i, (d, q) in enumerate(units):
        y_refs[d][0, :, lanes(q)] = (xs[i][chunk:] + gv[i][chunk:]
                                     + _dot(g[i][chunk:, SCAN_LANES:].astype(BF16), rows_bd(u[i])))
    for i, (d, q) in enumerate(units):
        upd = _dot_tn(jnp.concatenate([v[i], u[i]], axis=0), jnp.concatenate([kt[i], bt[i]], axis=0))
        s_ref[i] = (s_old[i] + upd) * dec_refs[d][0, 0, 0, :, lanes(q)] * bdm


def _scan(ops, dec, t_lat, lane_block):
    _, bsz, tt, c = ops.shape
    c //= SCAN_PLANES
    chunk = SCAN_CHUNK
    heads = SCAN_LANES // RWKV_HEAD
    n_tot = tt // chunk
    n_lat = t_lat // chunk
    n_ctx = n_tot - n_lat
    nsub = lane_block // SCAN_LANES

    ti = np.arange(chunk)
    before = [ti[None, :] < ti[:, None], ti[None, :] > ti[:, None]]
    gmask = np.stack([np.concatenate([np.tile(before[d], (1, 2 * heads)),
                                      np.tile(before[d] | np.eye(chunk, dtype=bool), (1, 2 * heads))], axis=0)
                      for d in range(2)]).astype(np.float32)
    li = np.arange(SCAN_LANES) // RWKV_HEAD
    bdm = (li[:, None] == li[None, :]).astype(np.float32)
    hmask = jnp.asarray((li[None, None, :] == np.arange(heads)[:, None, None]), BF16)
    eye = np.tile(np.eye(chunk, dtype=np.float32), (1, heads))

    def chunk_index(d, s):
        return jnp.where(s < n_ctx, n_lat + s, s - n_ctx) if d == 0 else n_tot - 1 - s

    def const(shape):
        nd = len(shape)
        return pl.BlockSpec(shape, lambda b, h, s: (0,) * nd)

    in_specs = [const(gmask.shape), const(bdm.shape), const(hmask.shape), const(eye.shape)]
    args = [jnp.asarray(gmask), jnp.asarray(bdm), hmask, jnp.asarray(eye)]
    out_specs = []
    for d in range(2):
        in_specs += [pl.BlockSpec((1, 1, chunk, SCAN_PLANES * lane_block),
                                  lambda b, h, s, d=d: (d, b, chunk_index(d, s), h)),
                     pl.BlockSpec((1, 1, 1, 1, lane_block), lambda b, h, s, d=d: (d, b, chunk_index(d, s), 0, h))]
        args += [ops, dec]
        out_specs.append(pl.BlockSpec((1, chunk, lane_block), lambda b, h, s, d=d: (b, chunk_index(d, s), h)))
    ys = jax.ShapeDtypeStruct((bsz, tt, c), F32)
    return pl.pallas_call(
        functools.partial(_scan_kernel, chunk=chunk, nsub=nsub),
        grid=(bsz, c // lane_block, n_tot),
        in_specs=in_specs,
        out_specs=out_specs,
        out_shape=[ys, ys],
        scratch_shapes=[pltpu.VMEM((2 * nsub, SCAN_LANES, SCAN_LANES), F32)],
        compiler_params=_cparams(("arbitrary", "arbitrary", "arbitrary")),
        name="rwkv_scan",
    )(*args)


def _rout_kernel(yf, yb, bonus, g, gng, gnb, ones_ref, o_ref):
    ones_bd = ones_ref[...]
    inv_n = 1.0 / RWKV_HEAD
    y = yf[0] + yb[0]
    mu = _head_sum(y, ones_bd) * inv_n
    yc = y - mu
    var = _head_sum(yc * yc, ones_bd) * inv_n
    yn = yc * lax.rsqrt(var + GN_EPS) * gng[...] + gnb[...]
    o_ref[0] = ((yn + bonus[0]) * g[0]).astype(o_ref.dtype)


def _rwkv_out(yf, yb, bonus, g, gn_g, gn_b, t_lat, tb):
    bsz, _, c = g.shape
    li = np.arange(SCAN_LANES) // RWKV_HEAD
    ones_bd = jnp.asarray(li[:, None] == li[None, :], BF16)
    one = pl.BlockSpec((1, tb, c), lambda b, j: (b, j, 0))

    def const(shape):
        return pl.BlockSpec(shape, lambda b, j: (0, 0))

    return pl.pallas_call(
        _rout_kernel,
        grid=(bsz, t_lat // tb),
        in_specs=[one, one, one, one, const((1, c)), const((1, c)), const((SCAN_LANES, SCAN_LANES))],
        out_specs=one,
        out_shape=jax.ShapeDtypeStruct((bsz, t_lat, c), BF16),
        compiler_params=_cparams(("arbitrary", "arbitrary")),
        name="rwkv_out",
    )(yf, yb, bonus, g, gn_g.reshape(1, c), gn_b.reshape(1, c), ones_bd)


def _rope(x, c, s):
    return x * c + pltpu.roll(x, ATT_HEAD // 2, 1) * s


def _attn_kernel(sink_ref, q_ref, km_ref, k0_ref, kp_ref, vm_ref, v0_ref, vp_ref, kc_ref, vc_ref,
                 c0_ref, s0_ref, cm_ref, sm_ref, cp_ref, sp_ref, o_ref, *, n_blk):
    n = pl.program_id(1)
    blk = ATT_HEAD
    scale = ATT_HEAD ** -0.5
    kvh = range(ATT_KV_HEADS)
    c0, s0 = c0_ref[...], s0_ref[...]
    cm, sm = cm_ref[...], sm_ref[...]
    cp, sp = cp_ref[...], sp_ref[...]

    def head(ref, i):
        return ref[0, :, i * ATT_HEAD:(i + 1) * ATT_HEAD]

    qq = [jnp.concatenate([_rope(head(q_ref, g * ATT_GROUPS + h), c0, s0) for h in range(ATT_GROUPS)],
                          axis=0).astype(BF16) for g in kvh]
    kw = [jnp.concatenate([_rope(head(km_ref, g), cm, sm), _rope(head(k0_ref, g), c0, s0),
                           _rope(head(kp_ref, g), cp, sp)], axis=0).astype(BF16) for g in kvh]
    vw = [jnp.concatenate([head(vm_ref, g), head(v0_ref, g), head(vp_ref, g)], axis=0).astype(BF16) for g in kvh]
    kc = [head(kc_ref, g).astype(BF16) for g in kvh]
    vc = [head(vc_ref, g).astype(BF16) for g in kvh]
    ii = lax.broadcasted_iota(jnp.int32, (ATT_GROUPS * blk, 3 * blk), 0) % blk
    jj = lax.broadcasted_iota(jnp.int32, (ATT_GROUPS * blk, 3 * blk), 1)
    in_prev = jnp.logical_and(jnp.logical_and(jj < blk, jj >= ii), n > 0)
    in_self = jnp.logical_and(jj >= blk, jj < 2 * blk)
    in_next = jnp.logical_and(jnp.logical_and(jj >= 2 * blk, jj - 2 * blk <= ii), n < n_blk - 1)
    valid = jnp.logical_or(jnp.logical_or(in_prev, in_self), in_next)
    s_w = [jnp.where(valid, _dot_nt(qq[g], kw[g]) * scale, NEG_INF) for g in kvh]
    s_c = [_dot_nt(qq[g], kc[g]) * scale for g in kvh]
    s_s = [jnp.concatenate([jnp.full((blk, 1), sink_ref[g * ATT_GROUPS + h], F32) for h in range(ATT_GROUPS)],
                           axis=0) for g in kvh]
    m = [jnp.maximum(jnp.maximum(jnp.max(s_w[g], axis=-1, keepdims=True),
                                 jnp.max(s_c[g], axis=-1, keepdims=True)), s_s[g]) for g in kvh]
    p_w = [jnp.exp(s_w[g] - m[g]) for g in kvh]
    p_c = [jnp.exp(s_c[g] - m[g]) for g in kvh]
    den = [jnp.sum(p_w[g], axis=-1, keepdims=True) + jnp.sum(p_c[g], axis=-1, keepdims=True)
           + jnp.exp(s_s[g] - m[g]) for g in kvh]
    o = [(_dot(p_w[g].astype(BF16), vw[g]) + _dot(p_c[g].astype(BF16), vc[g])) / den[g] for g in kvh]
    o_ref[0] = jnp.concatenate([o[g][h * blk:(h + 1) * blk] for g in kvh for h in range(ATT_GROUPS)],
                               axis=1).astype(o_ref.dtype)


def _attention(proj, col, sink, cos2, sin2, t_lat):
    bsz, tt, _ = proj.shape
    blk = ATT_HEAD
    n_blk = t_lat // blk
    l = tt - t_lat
    qw = ATT_KV_HEADS * ATT_GROUPS * ATT_HEAD
    kw = ATT_KV_HEADS * ATT_HEAD
    q_blk = col["q"] * LANES // qw

    def kv(name, off):
        cb = col[name] * LANES // kw
        return pl.BlockSpec((1, blk, kw), lambda b, n: (b, jnp.clip(n + off, 0, n_blk - 1), cb))

    def ctx(name):
        cb = col[name] * LANES // kw
        return pl.BlockSpec((1, l, kw), lambda b, n: (b, t_lat // l, cb))

    def tab(off):
        return pl.BlockSpec((blk, ATT_HEAD), lambda b, n: (jnp.clip(n + off, 0, n_blk - 1), 0))

    in_specs = [pl.BlockSpec(memory_space=pltpu.SMEM),
                pl.BlockSpec((1, blk, qw), lambda b, n: (b, n, q_blk)),
                kv("ak", -1), kv("ak", 0), kv("ak", 1), kv("av", -1), kv("av", 0), kv("av", 1),
                ctx("ak"), ctx("av"), tab(0), tab(0), tab(-1), tab(-1), tab(1), tab(1)]
    return pl.pallas_call(
        functools.partial(_attn_kernel, n_blk=n_blk),
        grid=(bsz, n_blk),
        in_specs=in_specs,
        out_specs=pl.BlockSpec((1, blk, qw), lambda b, n: (b, n, 0)),
        out_shape=jax.ShapeDtypeStruct((bsz, t_lat, qw), BF16),
        compiler_params=_cparams(("arbitrary", "arbitrary")),
        name="window_attention",
    )(sink, proj, proj, proj, proj, proj, proj, proj, proj, proj, cos2, sin2, cos2, sin2, cos2, sin2)


def _merge_kernel(o1_ref, o2_ref, w1_ref, w2_ref, g1_ref, g2_ref, o_ref):
    y1 = _dot(o1_ref[0], w1_ref[...])
    y2 = _dot(o2_ref[0], w2_ref[...])
    o_ref[0] = (jax.nn.sigmoid(g1_ref[0].astype(F32)) * y1
                + jax.nn.sigmoid(g2_ref[0].astype(F32)) * y2).astype(o_ref.dtype)


def _merge(o_rwkv, o_att, w1, w2, gates, tm, tn):
    bsz, t, c = o_rwkv.shape
    n = w1.shape[1]
    gr = 0
    ga = n // tn
    return pl.pallas_call(
        _merge_kernel,
        grid=(bsz, t // tm, n // tn),
        in_specs=[pl.BlockSpec((1, tm, c), lambda b, i, j: (b, i, 0)),
                  pl.BlockSpec((1, tm, c), lambda b, i, j: (b, i, 0)),
                  pl.BlockSpec((c, tn), lambda b, i, j: (0, j)),
                  pl.BlockSpec((c, tn), lambda b, i, j: (0, j)),
                  pl.BlockSpec((1, tm, tn), lambda b, i, j: (b, i, gr + j)),
                  pl.BlockSpec((1, tm, tn), lambda b, i, j: (b, i, ga + j))],
        out_specs=pl.BlockSpec((1, tm, tn), lambda b, i, j: (b, i, j)),
        out_shape=jax.ShapeDtypeStruct((bsz, t, n), BF16),
        compiler_params=_cparams(("arbitrary", "arbitrary", "arbitrary")),
        name="gated_merge",
    )(o_rwkv, o_att, w1, w2, gates, gates)


def _mm_resid_kernel(a_ref, w_ref, x_ref, gt_ref, o_ref, *, tiles_per_batch):
    bi = pl.program_id(0) // tiles_per_batch
    y = _dot(a_ref[...], w_ref[...])
    o_ref[...] = DEEPNORM_ALPHA * x_ref[...] + gt_ref[pl.ds(bi, 1), :] * y


def _mm_resid(a, w, x, mod, gate_col, rows_per_batch, tm, tn, name):
    m, k = a.shape
    n = w.shape[1]
    gcb = gate_col * (n // tn)
    return pl.pallas_call(
        functools.partial(_mm_resid_kernel, tiles_per_batch=rows_per_batch // tm),
        grid=(m // tm, n // tn),
        in_specs=[pl.BlockSpec((tm, k), lambda i, j: (i, 0)),
                  pl.BlockSpec((k, tn), lambda i, j: (0, j)),
                  pl.BlockSpec((tm, tn), lambda i, j: (i, j)),
                  pl.BlockSpec((MOD_ROWS, tn), lambda i, j: (0, gcb + j))],
        out_specs=pl.BlockSpec((tm, tn), lambda i, j: (i, j)),
        out_shape=jax.ShapeDtypeStruct((m, n), F32),
        compiler_params=_cparams(("arbitrary", "arbitrary")),
        name=name,
    )(a, w, x, mod)


def _ln_kernel(z_ref, g_ref, b_ref, *rest, with_mod):
    bi = pl.program_id(0)
    z = z_ref[0]
    mu = jnp.mean(z, axis=-1, keepdims=True)
    zc = z - mu
    var = jnp.mean(zc * zc, axis=-1, keepdims=True)
    out = zc * lax.rsqrt(var + LN_EPS) * g_ref[...] + b_ref[...]
    if with_mod:
        sh_ref, sc_ref, o_ref, h_ref = rest
        o_ref[0] = out
        h_ref[0] = (out * (1.0 + sc_ref[pl.ds(bi, 1), :]) + sh_ref[pl.ds(bi, 1), :]).astype(BF16)
    else:
        (o_ref,) = rest
        o_ref[0] = out


def _res_ln(z, mod, g, b, tb, mod_cols=None):
    bsz, t, d = z.shape
    blk = pl.BlockSpec((1, tb, d), lambda bi, j: (bi, j, 0))
    vec = pl.BlockSpec((1, d), lambda bi, j: (0, 0))

    def modspec(cb):
        return pl.BlockSpec((MOD_ROWS, d), lambda bi, j: (0, cb))

    in_specs = [blk, vec, vec]
    args = [z, g.reshape(1, d), b.reshape(1, d)]
    out_specs = [blk]
    out_shape = [jax.ShapeDtypeStruct((bsz, t, d), F32)]
    if mod_cols is not None:
        in_specs += [modspec(mod_cols[0]), modspec(mod_cols[1])]
        args += [mod, mod]
        out_specs.append(blk)
        out_shape.append(jax.ShapeDtypeStruct((bsz, t, d), BF16))
    return pl.pallas_call(
        functools.partial(_ln_kernel, with_mod=mod_cols is not None),
        grid=(bsz, t // tb),
        in_specs=in_specs,
        out_specs=out_specs,
        out_shape=out_shape,
        compiler_params=_cparams(("arbitrary", "arbitrary")),
        name="residual_layernorm",
    )(*args)


def _ffn_up_kernel(a_ref, wg_ref, wu_ref, o_ref):
    a = a_ref[...]
    gg = _dot(a, wg_ref[...].astype(BF16))
    uu = _dot(a, wu_ref[...].astype(BF16))
    o_ref[...] = (gg * jax.nn.sigmoid(gg) * uu).astype(o_ref.dtype)


def _ffn_up(a, wg, wu, tm, tf):
    m, k = a.shape
    f = wg.shape[1]
    return pl.pallas_call(
        _ffn_up_kernel,
        grid=(m // tm, f // tf),
        in_specs=[pl.BlockSpec((tm, k), lambda i, j: (i, 0)),
                  pl.BlockSpec((k, tf), lambda i, j: (0, j)),
                  pl.BlockSpec((k, tf), lambda i, j: (0, j))],
        out_specs=pl.BlockSpec((tm, tf), lambda i, j: (i, j)),
        out_shape=jax.ShapeDtypeStruct((m, f), BF16),
        compiler_params=_cparams(("arbitrary", "arbitrary")),
        name="swiglu_up",
    )(a, wg, wu)


def _rope_tables(t):
    rows = t // GRID_W
    row = jnp.broadcast_to(jnp.arange(rows, dtype=F32)[:, None], (rows, GRID_W)).reshape(t)
    colp = jnp.broadcast_to(jnp.arange(GRID_W, dtype=F32)[None, :], (rows, GRID_W)).reshape(t)
    axis_dim = ATT_HEAD // 2
    inv = ROPE_BASE ** (-jnp.arange(0, axis_dim, 2, dtype=F32) / axis_dim)
    ang = jnp.concatenate([row[:, None] * inv, colp[:, None] * inv], -1)
    cos, sin = jnp.cos(ang), jnp.sin(ang)
    return jnp.concatenate([cos, cos], -1), jnp.concatenate([-sin, sin], -1)


def _block(x, c, ctx, c_ctx, w_ada, b_ada, w_in, rwkv_shift, rwkv_w0, rwkv_w_up, rwkv_a0, rwkv_a_up,
           rwkv_g_up, rwkv_k_k, rwkv_k_a, rwkv_r_k, rwkv_gn_g, rwkv_gn_b, attn_sink, w_rwkv_o, w_att_o,
           w_out, ln1_g, ln1_b, w_ff_gate, w_ff_up, w_ff_down, ln2_g, ln2_b, tiles):
    bsz, t, d = x.shape
    l = ctx.shape[1]
    tt = t + l
    crw = rwkv_w0.shape[1]
    d_att = ATT_KV_HEADS * ATT_GROUPS * ATT_HEAD
    d_kv = ATT_KV_HEADS * ATT_HEAD
    n_lora = 2 * DECAY_LORA + 2 * ICLR_LORA + GATE_LORA
    n_rw = 3 * crw + n_lora

    assert bsz + 1 <= MOD_ROWS and l == tiles["prep_tb"] and t % tiles["prep_tb"] == 0
    cc = jnp.zeros((MOD_ROWS, d), F32).at[:bsz].set(c).at[bsz].set(c_ctx)
    mod = _ada(cc, w_ada, b_ada, tiles["ada_tn"])

    o_q = n_rw
    o_g = o_q + d_att + 2 * d_kv
    w_rest = _stage_columns(w_in, [(0, 3 * crw), (o_q, o_g - o_q), (3 * crw, n_lora)], "stage_w_rest")
    w_gates = _stage_columns(w_in, [(o_g, w_in.shape[1] - o_g)], "stage_w_gates")
    col = {}
    off = 0
    for name, width in (("r", crw), ("k", crw), ("v", crw), ("q", d_att), ("ak", d_kv), ("av", d_kv),
                        ("lora", n_lora)):
        col[name] = off // LANES
        off += width
    shift_rkv = rwkv_shift[:, :3 * crw]
    shift_lora = jnp.concatenate([rwkv_shift[:, 3 * crw:], jnp.zeros((3, LORA_PAD - n_lora), F32)], axis=1)

    h = _prep(x, ctx, mod)
    proj, w_out16, w_ro16, w_ao16 = _mm_wstat(
        h.reshape(bsz * tt, d), w_rest, tiles["proj_tm"], tiles["proj_tn"], F32, "in_proj",
        side=[(w_out, tiles["side_rows"]), (w_rwkv_o, tiles["side_rows"]), (w_att_o, tiles["side_rows"])])
    proj = proj.reshape(bsz, tt, -1)
    gates, w_down16 = _mm_lead_rows(h, w_gates, t, tiles["gate_tm"], tiles["gate_tn"], BF16, "gate_proj",
                                    side=[(w_ff_down, tiles["side_rows_down"])])

    g, bonus, ops, dec = _features(proj, col, shift_rkv, shift_lora, rwkv_w0, rwkv_w_up, rwkv_a0, rwkv_a_up,
                                   rwkv_g_up, rwkv_k_k, rwkv_k_a, rwkv_r_k.reshape(-1), t, tiles["feat_tb"])
    yf, yb = _scan(ops, dec, t, tiles["scan_lanes"])
    o_rwkv = _rwkv_out(yf, yb, bonus, g, rwkv_gn_g, rwkv_gn_b, t, tiles["rout_tb"])

    cos2, sin2 = _rope_tables(t)
    o_att = _attention(proj, col, attn_sink, cos2, sin2, t)

    ym = _merge(o_rwkv, o_att, w_ro16, w_ao16, gates, tiles["merge_tm"], tiles["merge_tn"])
    z1 = _mm_resid(ym.reshape(bsz * t, d), w_out16, x.reshape(bsz * t, d), mod, 2, t,
                   tiles["out_tm"], tiles["out_tn"], "out_proj")
    x1, h2 = _res_ln(z1.reshape(bsz, t, d), mod, ln1_g, ln1_b, tiles["ln_tb"], mod_cols=(3, 4))

    u = _ffn_up(h2.reshape(bsz * t, d), w_ff_gate, w_ff_up, tiles["ffn_tm"], tiles["ffn_tf"])
    z2 = _mm_resid(u, w_down16, x1.reshape(bsz * t, d), mod, 5, t,
                   tiles["down_tm"], tiles["down_tn"], "swiglu_down")
    (out,) = _res_ln(z2.reshape(bsz, t, d), mod, ln2_g, ln2_b, tiles["ln_tb"])
    return out


_TILES = dict(ada_tn=512, prep_tb=256, proj_tm=512, proj_tn=1024, gate_tm=1024, gate_tn=1024,
              side_rows=128, side_rows_down=256, feat_tb=128, scan_lanes=2048, rout_tb=256,
              merge_tm=1024, merge_tn=1024, out_tm=1024, out_tn=1024, ln_tb=512,
              ffn_tm=2048, ffn_tf=256, down_tm=512, down_tn=512)


def kernel(x, c, ctx, c_ctx, w_ada, b_ada, w_in, rwkv_shift, rwkv_w0, rwkv_w_up, rwkv_a0, rwkv_a_up, rwkv_g_up, rwkv_k_k, rwkv_k_a, rwkv_r_k, rwkv_gn_g, rwkv_gn_b, attn_sink, w_rwkv_o, w_att_o, w_out, ln1_g, ln1_b, w_ff_gate, w_ff_up, w_ff_down, ln2_g, ln2_b):
    assert w_ada.shape[0] == DEPTH
    return _block(x, c, ctx, c_ctx, w_ada[0], b_ada[0], w_in[0], rwkv_shift[0], rwkv_w0[0], rwkv_w_up[0],
                  rwkv_a0[0], rwkv_a_up[0], rwkv_g_up[0], rwkv_k_k[0], rwkv_k_a[0], rwkv_r_k[0], rwkv_gn_g[0],
                  rwkv_gn_b[0], attn_sink[0], w_rwkv_o[0], w_att_o[0], w_out[0], ln1_g[0], ln1_b[0],
                  w_ff_gate[0], w_ff_up[0], w_ff_down[0], ln2_g[0], ln2_b[0], _TILES)
```

```python
import functools

import jax
import jax.numpy as jnp
import numpy as np
from jax import lax
from jax.experimental import pallas as pl
from jax.experimental.pallas import tpu as pltpu

F32 = jnp.float32
BF16 = jnp.bfloat16

RWKV_HEAD = 64
DECAY_LORA = 96
ICLR_LORA = 96
GATE_LORA = 256
GN_EPS = 64e-5
ATT_HEAD = 128
ATT_KV_HEADS = 4
ATT_GROUPS = 4
GRID_W = 64
ROPE_BASE = 10000.0
LN_EPS = 1e-5
DEPTH = 1
DEEPNORM_ALPHA = (2 * DEPTH) ** 0.25
NEG_INF = -1e30

LANES = 128
SUBLANES = 8
VMEM_LIMIT = 56 * 1024 * 1024
MOD_ROWS = 8
SCAN_CHUNK = 64
SCAN_LANES = 256
SCAN_PLANES = 5


def _cparams(sem):
    return pltpu.CompilerParams(dimension_semantics=sem, vmem_limit_bytes=VMEM_LIMIT)


def _dot(a, b):
    return jnp.dot(a, b, preferred_element_type=F32)


def _dot_nt(a, b):
    return lax.dot_general(a, b, (((1,), (1,)), ((), ())), preferred_element_type=F32)


def _dot_tn(a, b):
    return lax.dot_general(a, b, (((0,), (0,)), ((), ())), preferred_element_type=F32)


def _split2(x):
    hi = x.astype(BF16)
    return hi, (x - hi.astype(F32)).astype(BF16)


def _head_sum(x, ones_bd):
    rows = x.shape[0]
    both = jnp.concatenate(_split2(x), axis=0)
    out = []
    for s in range(x.shape[1] // SCAN_LANES):
        part = _dot(both[:, s * SCAN_LANES:(s + 1) * SCAN_LANES], ones_bd)
        out.append(part[:rows] + part[rows:])
    return jnp.concatenate(out, axis=1)


def _ada_kernel(c_ref, w_ref, b_ref, o_ref):
    a = c_ref[...]
    a = a * jax.nn.sigmoid(a)
    a_hi, a_lo = _split2(a)
    w_hi, w_lo = _split2(w_ref[...])
    part = _dot(jnp.concatenate([a_hi, a_lo], axis=0), w_hi)
    o_ref[...] = part[:MOD_ROWS] + part[MOD_ROWS:] + _dot(a_hi, w_lo) + b_ref[...]


def _ada(cc, w_ada, b_ada, tn):
    d, n = w_ada.shape
    return pl.pallas_call(
        _ada_kernel,
        grid=(n // tn,),
        in_specs=[pl.BlockSpec((MOD_ROWS, d), lambda j: (0, 0)),
                  pl.BlockSpec((d, tn), lambda j: (0, j)),
                  pl.BlockSpec((1, tn), lambda j: (0, j))],
        out_specs=pl.BlockSpec((MOD_ROWS, tn), lambda j: (0, j)),
        out_shape=jax.ShapeDtypeStruct((MOD_ROWS, n), F32),
        compiler_params=_cparams(("arbitrary",)),
        name="ada",
    )(cc, w_ada, b_ada.reshape(1, n))


def _prep_kernel(x_ref, ctx_ref, sh_ref, sc_ref, o_ref, *, n_lat, ctx_row):
    b = pl.program_id(0)
    j = pl.program_id(1)
    is_ctx = j >= n_lat
    row = jnp.where(is_ctx, ctx_row, b)
    sh = sh_ref[pl.ds(row, 1), :]
    sc = sc_ref[pl.ds(row, 1), :]
    xin = jnp.where(is_ctx, ctx_ref[0], x_ref[0])
    o_ref[0] = (xin * (1.0 + sc) + sh).astype(BF16)


def _prep(x, ctx, mod):
    bsz, t, d = x.shape
    l = ctx.shape[1]
    tb = l
    n_lat = t // tb
    return pl.pallas_call(
        functools.partial(_prep_kernel, n_lat=n_lat, ctx_row=bsz),
        grid=(bsz, n_lat + 1),
        in_specs=[pl.BlockSpec((1, tb, d), lambda b, j: (b, jnp.minimum(j, n_lat - 1), 0)),
                  pl.BlockSpec((1, l, d), lambda b, j: (b, 0, 0)),
                  pl.BlockSpec((MOD_ROWS, d), lambda b, j: (0, 0)),
                  pl.BlockSpec((MOD_ROWS, d), lambda b, j: (0, 1))],
        out_specs=pl.BlockSpec((1, tb, d), lambda b, j: (b, j, 0)),
        out_shape=jax.ShapeDtypeStruct((bsz, t + l, d), BF16),
        compiler_params=_cparams(("arbitrary", "arbitrary")),
        name="prep",
    )(x, ctx, mod, mod)


def _cast_kernel(w_ref, o_ref):
    o_ref[...] = w_ref[...].astype(o_ref.dtype)


def _stage_columns(w, segments, name):
    rows = w.shape[0]
    starts, src = [], []
    off = 0
    for start, width in segments:
        assert start % LANES == 0 and width % LANES == 0
        starts.append(off // LANES)
        src.append(start // LANES)
        off += width

    def src_block(j):
        blk = j - starts[0] + src[0]
        for s0, b0 in zip(starts[1:], src[1:]):
            blk = jnp.where(j >= s0, j - s0 + b0, blk)
        return blk

    return pl.pallas_call(
        _cast_kernel,
        grid=(off // LANES,),
        in_specs=[pl.BlockSpec((rows, LANES), lambda j: (0, src_block(j)))],
        out_specs=pl.BlockSpec((rows, LANES), lambda j: (0, j)),
        out_shape=jax.ShapeDtypeStruct((rows, off), BF16),
        compiler_params=_cparams(("arbitrary",)),
        name=name,
    )(w)


def _mm_kernel(a_ref, w_ref, *refs, side_blocks, strides, n_steps):
    n_side = len(side_blocks)
    o_ref = refs[n_side]
    if len(o_ref.shape) == 3:
        o_ref[0] = _dot(a_ref[0], w_ref[...]).astype(o_ref.dtype)
    else:
        o_ref[...] = _dot(a_ref[...], w_ref[...]).astype(o_ref.dtype)
    step = sum(pl.program_id(ax) * st for ax, st in enumerate(strides))
    for s_in, s_out, nblk in zip(refs[:n_side], refs[n_side + 1:], side_blocks):
        fresh = jnp.logical_or(step == 0, (step * nblk) // n_steps != ((step - 1) * nblk) // n_steps)

        @pl.when(fresh)
        def _(s_in=s_in, s_out=s_out):
            s_out[...] = s_in[...].astype(s_out.dtype)


def _side_specs(side, n_steps, step_of):
    in_specs, out_specs, out_shape = [], [], []
    for arr, rows in side:
        nblk = arr.shape[0] // rows
        assert arr.shape[0] % rows == 0 and nblk <= n_steps
        spec = pl.BlockSpec((rows, arr.shape[1]), lambda *g, nblk=nblk: ((step_of(*g) * nblk) // n_steps, 0))
        in_specs.append(spec)
        out_specs.append(spec)
        out_shape.append(jax.ShapeDtypeStruct(arr.shape, BF16))
    return in_specs, out_specs, out_shape


def _mm_wstat(a, w, tm, tn, out_dtype, name, side=()):
    m, k = a.shape
    n = w.shape[1]
    nj, ni = pl.cdiv(n, tn), m // tm
    s_in, s_out, s_shape = _side_specs(side, nj * ni, lambda j, i: j * ni + i)
    return pl.pallas_call(
        functools.partial(_mm_kernel, side_blocks=tuple(arr.shape[0] // r for arr, r in side),
                          strides=(ni, 1), n_steps=nj * ni),
        grid=(nj, ni),
        in_specs=[pl.BlockSpec((tm, k), lambda j, i: (i, 0)),
                  pl.BlockSpec((k, tn), lambda j, i: (0, j))] + s_in,
        out_specs=[pl.BlockSpec((tm, tn), lambda j, i: (i, j))] + s_out,
        out_shape=[jax.ShapeDtypeStruct((m, n), out_dtype)] + s_shape,
        compiler_params=_cparams(("arbitrary", "arbitrary")),
        name=name,
    )(a, w, *[arr for arr, _ in side])


def _mm_lead_rows(a, w, rows, tm, tn, out_dtype, name, side=()):
    bsz, _, k = a.shape
    n = w.shape[1]
    nj, ni = n // tn, rows // tm
    s_in, s_out, s_shape = _side_specs(side, nj * bsz * ni, lambda j, b, i: (j * bsz + b) * ni + i)
    return pl.pallas_call(
        functools.partial(_mm_kernel, side_blocks=tuple(arr.shape[0] // r for arr, r in side),
                          strides=(bsz * ni, ni, 1), n_steps=nj * bsz * ni),
        grid=(nj, bsz, ni),
        in_specs=[pl.BlockSpec((1, tm, k), lambda j, b, i: (b, i, 0)),
                  pl.BlockSpec((k, tn), lambda j, b, i: (0, j))] + s_in,
        out_specs=[pl.BlockSpec((1, tm, tn), lambda j, b, i: (b, i, j))] + s_out,
        out_shape=[jax.ShapeDtypeStruct((bsz, rows, n), out_dtype)] + s_shape,
        compiler_params=_cparams(("arbitrary", "arbitrary", "arbitrary")),
        name=name,
    )(a, w, *[arr for arr, _ in side])


def _conv3(prev_ref, cur_ref, next_ref, w_ref, at_start, at_end):
    cur = cur_ref[0]
    rows = cur.shape[0]
    prev_row = jnp.where(at_start, 0.0, prev_ref[0][SUBLANES - 1:SUBLANES, :])
    next_row = jnp.where(at_end, 0.0, next_ref[0][0:1, :])
    up = jnp.concatenate([prev_row, cur[:rows - 1]], axis=0)
    down = jnp.concatenate([cur[1:], next_row], axis=0)
    w = w_ref[...]
    return up * w[0:1] + cur * w[1:2] + down * w[2:3]


def _feat_kernel(rp, rc, rn, kp, kc, kn, vp, vc, vn, lp, lc, ln_,
                 shr, shk, shv, shl, w0, wup, a0, aup, gup, kk_w, ka_w, rk_w, ones_ref, cm_ref,
                 g_o, bonus_o, ops_o, dec_o, *, n_lat, n_tot, chunk):
    j = pl.program_id(1)
    at_start = jnp.logical_or(j == 0, j == n_lat)
    at_end = jnp.logical_or(j == n_lat - 1, j == n_tot - 1)
    r = _conv3(rp, rc, rn, shr, at_start, at_end)
    k = _conv3(kp, kc, kn, shk, at_start, at_end)
    v = _conv3(vp, vc, vn, shv, at_start, at_end)
    lo = _conv3(lp, lc, ln_, shl, at_start, at_end)
    rows = r.shape[0]
    v_h = v.astype(BF16)
    gd = lo[:, 2 * DECAY_LORA + 2 * ICLR_LORA:2 * DECAY_LORA + 2 * ICLR_LORA + GATE_LORA]
    ones_bd = ones_ref[...]
    g_o[0] = _dot(jax.nn.sigmoid(gd).astype(BF16), gup[...])
    kk = k * kk_w[...]
    kk = kk * lax.rsqrt(jnp.maximum(_head_sum(kk * kk, ones_bd), 1e-24))
    k_sum = jnp.zeros_like(k)
    for d in range(2):
        wd = lo[:, d * DECAY_LORA:(d + 1) * DECAY_LORA]
        ad = lo[:, 2 * DECAY_LORA + d * ICLR_LORA:2 * DECAY_LORA + (d + 1) * ICLR_LORA]
        z = w0[d:d + 1, :] + _dot(jnp.tanh(wd).astype(BF16), wup[d])
        lw = jax.nn.sigmoid(z) * (-np.exp(-0.5))
        iclr = jax.nn.sigmoid(a0[d:d + 1, :] + _dot(ad.astype(BF16), aup[d]))
        k_dir = k * (1.0 + (iclr - 1.0) * ka_w[...])
        k_sum = k_sum + k_dir
        lw_hi, lw_lo = _split2(lw)
        cum = _dot(cm_ref[d], lw_hi) + _dot(cm_ref[d], lw_lo)
        p_inv = jnp.exp(-cum)
        planes = [(kk * jnp.exp(cum - lw)).astype(BF16), (r * jnp.exp(cum)).astype(BF16),
                  (k_dir * p_inv).astype(BF16), (kk * iclr * p_inv).astype(BF16), v_h]
        for hg in range(r.shape[1] // SCAN_LANES):
            for p, plane in enumerate(planes):
                dst = (hg * SCAN_PLANES + p) * SCAN_LANES
                ops_o[d, 0, :, dst:dst + SCAN_LANES] = plane[:, hg * SCAN_LANES:(hg + 1) * SCAN_LANES]
        for q in range(rows // chunk):
            last = (q + 1) * chunk - 1 if d == 0 else q * chunk
            dec_o[d, 0, q] = jnp.exp(cum[last:last + 1, :])
    bonus_o[0] = _head_sum(r * k_sum * rk_w[...], ones_bd) * v


def _features(proj, proj_lora, col, shift_rkv, shift_lora, w0, w_up, a0, a_up, g_up, k_k, k_a, r_k, t_lat, tb):
    bsz, tt, _ = proj.shape
    c = w0.shape[1]
    n_lora = proj_lora.shape[2]
    chunk = SCAN_CHUNK
    n_tot = tt // tb
    n_lat = t_lat // tb
    hb = tb // SUBLANES
    n_h = tt // SUBLANES
    li = np.arange(SCAN_LANES) // RWKV_HEAD
    ones_bd = jnp.asarray(li[:, None] == li[None, :], BF16)
    ti = np.arange(tb)
    same = (ti[:, None] // chunk) == (ti[None, :] // chunk)
    cmask = jnp.asarray(np.stack([same & (ti[None, :] <= ti[:, None]), same & (ti[None, :] >= ti[:, None])]), BF16)

    def main(cb, width):
        blk = cb * LANES // width
        return pl.BlockSpec((1, tb, width), lambda b, j: (b, j, blk))

    def prev(cb, width):
        blk = cb * LANES // width
        return pl.BlockSpec((1, SUBLANES, width), lambda b, j: (b, jnp.maximum(j * hb - 1, 0), blk))

    def nxt(cb, width):
        blk = cb * LANES // width
        return pl.BlockSpec((1, SUBLANES, width), lambda b, j: (b, jnp.minimum((j + 1) * hb, n_h - 1), blk))

    def const(shape):
        nd = len(shape)
        return pl.BlockSpec(shape, lambda b, j: (0,) * nd)

    in_specs = []
    for name in ("r", "k", "v"):
        in_specs += [prev(col[name], c), main(col[name], c), nxt(col[name], c)]
    in_specs += [prev(0, n_lora), main(0, n_lora), nxt(0, n_lora)]
    in_specs += [const((3, c)), const((3, c)), const((3, c)), const((3, n_lora)),
                 const((2, c)), const((2, DECAY_LORA, c)), const((2, c)), const((2, ICLR_LORA, c)),
                 const((GATE_LORA, c)), const((1, c)), const((1, c)), const((1, c)),
                 const((SCAN_LANES, SCAN_LANES)), const((2, tb, tb))]
    one = pl.BlockSpec((1, tb, c), lambda b, j: (b, j, 0))
    ops = pl.BlockSpec((2, 1, tb, SCAN_PLANES * c), lambda b, j: (0, b, j, 0))
    dec = pl.BlockSpec((2, 1, tb // chunk, 1, c), lambda b, j: (0, b, j, 0, 0))
    s1 = jax.ShapeDtypeStruct((bsz, tt, c), F32)
    sops = jax.ShapeDtypeStruct((2, bsz, tt, SCAN_PLANES * c), BF16)
    sdec = jax.ShapeDtypeStruct((2, bsz, tt // chunk, 1, c), F32)
    args = [proj] * 9 + [proj_lora] * 3 + [shift_rkv[:, :c], shift_rkv[:, c:2 * c], shift_rkv[:, 2 * c:], shift_lora,
                          w0, w_up.astype(BF16), a0, a_up.astype(BF16), g_up.astype(BF16),
                          k_k.reshape(1, c), k_a.reshape(1, c), r_k.reshape(1, c), ones_bd, cmask]
    return pl.pallas_call(
        functools.partial(_feat_kernel, n_lat=n_lat, n_tot=n_tot, chunk=chunk),
        grid=(bsz, n_tot),
        in_specs=in_specs,
        out_specs=[one, one, ops, dec],
        out_shape=[s1, s1, sops, sdec],
        compiler_params=_cparams(("arbitrary", "arbitrary")),
        name="rwkv_features",
    )(*args)


def _scan_kernel(gm_ref, bdm_ref, hm_ref, eye_ref, *refs, chunk, nsub):
    ops_refs = refs[0:4:2]
    dec_refs = refs[1:4:2]
    y_refs = refs[4:6]
    s_ref = refs[6]
    heads = SCAN_LANES // RWKV_HEAD

    @pl.when(pl.program_id(2) == 0)
    def _():
        s_ref[...] = jnp.zeros_like(s_ref)

    bdm = bdm_ref[...]
    bdm_h = bdm.astype(BF16)
    eye = eye_ref[...]
    hms = [hm_ref[h] for h in range(heads)]

    def rows_bd(m):
        return jnp.concatenate([m * hms[h] for h in range(heads)], axis=0)

    def blocks_bd(m):
        return jnp.concatenate([m] * heads, axis=0) * bdm_h

    steps = chunk.bit_length() - 2
    units = [(d, q) for d in range(2) for q in range(nsub)]
    n_u = len(units)

    def lanes(q):
        return slice(q * SCAN_LANES, (q + 1) * SCAN_LANES)

    def plane(d, q, p):
        return ops_refs[d][0, 0, :, lanes(q * SCAN_PLANES + p)]

    kt = [plane(d, q, 2) for d, q in units]
    bt = [plane(d, q, 3) for d, q in units]
    v = [plane(d, q, 4) for d, q in units]
    x = [jnp.concatenate([plane(d, q, 0), plane(d, q, 1)], axis=0) for d, q in units]
    g = [_dot_nt(x[i], jnp.concatenate([rows_bd(kt[i]), rows_bd(bt[i])], axis=0)) * gm_ref[units[i][0]]
         for i in range(n_u)]
    s_old = [s_ref[i] for i in range(n_u)]
    xs = [_dot_nt(x[i], s_old[i].astype(BF16)) for i in range(n_u)]
    gv = [_dot(g[i][:, :SCAN_LANES].astype(BF16), rows_bd(v[i])) for i in range(n_u)]
    rhs = [xs[i][:chunk] + gv[i][:chunk] for i in range(n_u)]
    t = [eye - g[i][:chunk, SCAN_LANES:] for i in range(n_u)]
    lm = [g[i][:chunk, SCAN_LANES:].astype(BF16) for i in range(n_u)]
    pw = [_dot(lm[i], blocks_bd(lm[i])).astype(BF16) for i in range(n_u)]
    for _ in range(steps - 1):
        both = [_dot(jnp.concatenate([t[i].astype(BF16), pw[i]], axis=0), blocks_bd(pw[i])) for i in range(n_u)]
        t = [t[i] + both[i][:chunk] for i in range(n_u)]
        pw = [both[i][chunk:].astype(BF16) for i in range(n_u)]
    t = [t[i] + _dot(t[i].astype(BF16), blocks_bd(pw[i])) for i in range(n_u)]
    u = [(-_dot(t[i].astype(BF16), rows_bd(rhs[i].astype(BF16)))).astype(BF16) for i in range(n_u)]
    for i, (d, q) in enumerate(units):
        y_refs[d][0, :, lanes(q)] = (xs[i][chunk:] + gv[i][chunk:]
                                     + _dot(g[i][chunk:, SCAN_LANES:].astype(BF16), rows_bd(u[i])))
    for i, (d, q) in enumerate(units):
        upd = _dot_tn(jnp.concatenate([v[i], u[i]], axis=0), jnp.concatenate([kt[i], bt[i]], axis=0))
        s_ref[i] = (s_old[i] + upd) * dec_refs[d][0, 0, 0, :, lanes(q)] * bdm


def _scan(ops, dec, t_lat, lane_block):
    _, bsz, tt, c = ops.shape
    c //= SCAN_PLANES
    chunk = SCAN_CHUNK
    heads = SCAN_LANES // RWKV_HEAD
    n_tot = tt // chunk
    n_lat = t_lat // chunk
    n_ctx = n_tot - n_lat
    nsub = lane_block // SCAN_LANES

    ti = np.arange(chunk)
    before = [ti[None, :] < ti[:, None], ti[None, :] > ti[:, None]]
    gmask = np.stack([np.concatenate([np.tile(before[d], (1, 2 * heads)),
                                      np.tile(before[d] | np.eye(chunk, dtype=bool), (1, 2 * heads))], axis=0)
                      for d in range(2)]).astype(np.float32)
    li = np.arange(SCAN_LANES) // RWKV_HEAD
    bdm = (li[:, None] == li[None, :]).astype(np.float32)
    hmask = jnp.asarray((li[None, None, :] == np.arange(heads)[:, None, None]), BF16)
    eye = np.tile(np.eye(chunk, dtype=np.float32), (1, heads))

    def chunk_index(d, s):
        return jnp.where(s < n_ctx, n_lat + s, s - n_ctx) if d == 0 else n_tot - 1 - s

    def const(shape):
        nd = len(shape)
        return pl.BlockSpec(shape, lambda b, h, s: (0,) * nd)

    in_specs = [const(gmask.shape), const(bdm.shape), const(hmask.shape), const(eye.shape)]
    args = [jnp.asarray(gmask), jnp.asarray(bdm), hmask, jnp.asarray(eye)]
    out_specs = []
    for d in range(2):
        in_specs += [pl.BlockSpec((1, 1, chunk, SCAN_PLANES * lane_block),
                                  lambda b, h, s, d=d: (d, b, chunk_index(d, s), h)),
                     pl.BlockSpec((1, 1, 1, 1, lane_block), lambda b, h, s, d=d: (d, b, chunk_index(d, s), 0, h))]
        args += [ops, dec]
        out_specs.append(pl.BlockSpec((1, chunk, lane_block), lambda b, h, s, d=d: (b, chunk_index(d, s), h)))
    ys = jax.ShapeDtypeStruct((bsz, tt, c), F32)
    return pl.pallas_call(
        functools.partial(_scan_kernel, chunk=chunk, nsub=nsub),
        grid=(bsz, c // lane_block, n_tot),
        in_specs=in_specs,
        out_specs=out_specs,
        out_shape=[ys, ys],
        scratch_shapes=[pltpu.VMEM((2 * nsub, SCAN_LANES, SCAN_LANES), F32)],
        compiler_params=_cparams(("arbitrary", "arbitrary", "arbitrary")),
        name="rwkv_scan",
    )(*args)


def _rout_kernel(yf, yb, bonus, g, gng, gnb, ones_ref, o_ref):
    ones_bd = ones_ref[...]
    inv_n = 1.0 / RWKV_HEAD
    y = yf[0] + yb[0]
    mu = _head_sum(y, ones_bd) * inv_n
    yc = y - mu
    var = _head_sum(yc * yc, ones_bd) * inv_n
    yn = yc * lax.rsqrt(var + GN_EPS) * gng[...] + gnb[...]
    o_ref[0] = ((yn + bonus[0]) * g[0]).astype(o_ref.dtype)


def _rwkv_out(yf, yb, bonus, g, gn_g, gn_b, t_lat, tb):
    bsz, _, c = g.shape
    li = np.arange(SCAN_LANES) // RWKV_HEAD
    ones_bd = jnp.asarray(li[:, None] == li[None, :], BF16)
    one = pl.BlockSpec((1, tb, c), lambda b, j: (b, j, 0))

    def const(shape):
        return pl.BlockSpec(shape, lambda b, j: (0, 0))

    return pl.pallas_call(
        _rout_kernel,
        grid=(bsz, t_lat // tb),
        in_specs=[one, one, one, one, const((1, c)), const((1, c)), const((SCAN_LANES, SCAN_LANES))],
        out_specs=one,
        out_shape=jax.ShapeDtypeStruct((bsz, t_lat, c), BF16),
        compiler_params=_cparams(("arbitrary", "arbitrary")),
        name="rwkv_out",
    )(yf, yb, bonus, g, gn_g.reshape(1, c), gn_b.reshape(1, c), ones_bd)


def _rope(x, c, s):
    return x * c + pltpu.roll(x, ATT_HEAD // 2, 1) * s


def _attn_kernel(sink_ref, q_ref, km_ref, k0_ref, kp_ref, vm_ref, v0_ref, vp_ref, kc_ref, vc_ref,
                 c0_ref, s0_ref, cm_ref, sm_ref, cp_ref, sp_ref, o_ref, *, n_blk):
    n = pl.program_id(1)
    blk = ATT_HEAD
    scale = ATT_HEAD ** -0.5
    kvh = range(ATT_KV_HEADS)
    c0, s0 = c0_ref[...], s0_ref[...]
    cm, sm = cm_ref[...], sm_ref[...]
    cp, sp = cp_ref[...], sp_ref[...]

    def head(ref, i):
        return ref[0, :, i * ATT_HEAD:(i + 1) * ATT_HEAD]

    qq = [jnp.concatenate([_rope(head(q_ref, g * ATT_GROUPS + h), c0, s0) for h in range(ATT_GROUPS)],
                          axis=0).astype(BF16) for g in kvh]
    kw = [jnp.concatenate([_rope(head(km_ref, g), cm, sm), _rope(head(k0_ref, g), c0, s0),
                           _rope(head(kp_ref, g), cp, sp)], axis=0).astype(BF16) for g in kvh]
    vw = [jnp.concatenate([head(vm_ref, g), head(v0_ref, g), head(vp_ref, g)], axis=0).astype(BF16) for g in kvh]
    kc = [head(kc_ref, g).astype(BF16) for g in kvh]
    vc = [head(vc_ref, g).astype(BF16) for g in kvh]
    ii = lax.broadcasted_iota(jnp.int32, (ATT_GROUPS * blk, 3 * blk), 0) % blk
    jj = lax.broadcasted_iota(jnp.int32, (ATT_GROUPS * blk, 3 * blk), 1)
    in_prev = jnp.logical_and(jnp.logical_and(jj < blk, jj >= ii), n > 0)
    in_self = jnp.logical_and(jj >= blk, jj < 2 * blk)
    in_next = jnp.logical_and(jnp.logical_and(jj >= 2 * blk, jj - 2 * blk <= ii), n < n_blk - 1)
    valid = jnp.logical_or(jnp.logical_or(in_prev, in_self), in_next)
    s_w = [jnp.where(valid, _dot_nt(qq[g], kw[g]) * scale, NEG_INF) for g in kvh]
    s_c = [_dot_nt(qq[g], kc[g]) * scale for g in kvh]
    s_s = [jnp.concatenate([jnp.full((blk, 1), sink_ref[g * ATT_GROUPS + h], F32) for h in range(ATT_GROUPS)],
                           axis=0) for g in kvh]
    m = [jnp.maximum(jnp.maximum(jnp.max(s_w[g], axis=-1, keepdims=True),
                                 jnp.max(s_c[g], axis=-1, keepdims=True)), s_s[g]) for g in kvh]
    p_w = [jnp.exp(s_w[g] - m[g]) for g in kvh]
    p_c = [jnp.exp(s_c[g] - m[g]) for g in kvh]
    den = [jnp.sum(p_w[g], axis=-1, keepdims=True) + jnp.sum(p_c[g], axis=-1, keepdims=True)
           + jnp.exp(s_s[g] - m[g]) for g in kvh]
    o = [(_dot(p_w[g].astype(BF16), vw[g]) + _dot(p_c[g].astype(BF16), vc[g])) / den[g] for g in kvh]
    o_ref[0] = jnp.concatenate([o[g][h * blk:(h + 1) * blk] for g in kvh for h in range(ATT_GROUPS)],
                               axis=1).astype(o_ref.dtype)


def _attention(proj, col, sink, cos2, sin2, t_lat):
    bsz, tt, _ = proj.shape
    blk = ATT_HEAD
    n_blk = t_lat // blk
    l = tt - t_lat
    qw = ATT_KV_HEADS * ATT_GROUPS * ATT_HEAD
    kw = ATT_KV_HEADS * ATT_HEAD
    q_blk = col["q"] * LANES // qw

    def kv(name, off):
        cb = col[name] * LANES // kw
        return pl.BlockSpec((1, blk, kw), lambda b, n: (b, jnp.clip(n + off, 0, n_blk - 1), cb))

    def ctx(name):
        cb = col[name] * LANES // kw
        return pl.BlockSpec((1, l, kw), lambda b, n: (b, t_lat // l, cb))

    def tab(off):
        return pl.BlockSpec((blk, ATT_HEAD), lambda b, n: (jnp.clip(n + off, 0, n_blk - 1), 0))

    in_specs = [pl.BlockSpec(memory_space=pltpu.SMEM),
                pl.BlockSpec((1, blk, qw), lambda b, n: (b, n, q_blk)),
                kv("ak", -1), kv("ak", 0), kv("ak", 1), kv("av", -1), kv("av", 0), kv("av", 1),
                ctx("ak"), ctx("av"), tab(0), tab(0), tab(-1), tab(-1), tab(1), tab(1)]
    return pl.pallas_call(
        functools.partial(_attn_kernel, n_blk=n_blk),
        grid=(bsz, n_blk),
        in_specs=in_specs,
        out_specs=pl.BlockSpec((1, blk, qw), lambda b, n: (b, n, 0)),
        out_shape=jax.ShapeDtypeStruct((bsz, t_lat, qw), BF16),
        compiler_params=_cparams(("arbitrary", "arbitrary")),
        name="window_attention",
    )(sink, proj, proj, proj, proj, proj, proj, proj, proj, proj, cos2, sin2, cos2, sin2, cos2, sin2)


def _merge_kernel(o1_ref, o2_ref, w1_ref, w2_ref, g1_ref, g2_ref, o_ref):
    y1 = _dot(o1_ref[0], w1_ref[...])
    y2 = _dot(o2_ref[0], w2_ref[...])
    o_ref[0] = (jax.nn.sigmoid(g1_ref[0].astype(F32)) * y1
                + jax.nn.sigmoid(g2_ref[0].astype(F32)) * y2).astype(o_ref.dtype)


def _merge(o_rwkv, o_att, w1, w2, gates, tm, tn):
    bsz, t, c = o_rwkv.shape
    n = w1.shape[1]
    gr = 0
    ga = n // tn
    return pl.pallas_call(
        _merge_kernel,
        grid=(bsz, t // tm, n // tn),
        in_specs=[pl.BlockSpec((1, tm, c), lambda b, i, j: (b, i, 0)),
                  pl.BlockSpec((1, tm, c), lambda b, i, j: (b, i, 0)),
                  pl.BlockSpec((c, tn), lambda b, i, j: (0, j)),
                  pl.BlockSpec((c, tn), lambda b, i, j: (0, j)),
                  pl.BlockSpec((1, tm, tn), lambda b, i, j: (b, i, gr + j)),
                  pl.BlockSpec((1, tm, tn), lambda b, i, j: (b, i, ga + j))],
        out_specs=pl.BlockSpec((1, tm, tn), lambda b, i, j: (b, i, j)),
        out_shape=jax.ShapeDtypeStruct((bsz, t, n), BF16),
        compiler_params=_cparams(("arbitrary", "arbitrary", "arbitrary")),
        name="gated_merge",
    )(o_rwkv, o_att, w1, w2, gates, gates)


def _mm_resid_kernel(a_ref, w_ref, x_ref, gt_ref, o_ref, *, tiles_per_batch):
    bi = pl.program_id(0) // tiles_per_batch
    y = _dot(a_ref[...], w_ref[...])
    o_ref[...] = DEEPNORM_ALPHA * x_ref[...] + gt_ref[pl.ds(bi, 1), :] * y


def _mm_resid(a, w, x, mod, gate_col, rows_per_batch, tm, tn, name):
    m, k = a.shape
    n = w.shape[1]
    gcb = gate_col * (n // tn)
    return pl.pallas_call(
        functools.partial(_mm_resid_kernel, tiles_per_batch=rows_per_batch // tm),
        grid=(m // tm, n // tn),
        in_specs=[pl.BlockSpec((tm, k), lambda i, j: (i, 0)),
                  pl.BlockSpec((k, tn), lambda i, j: (0, j)),
                  pl.BlockSpec((tm, tn), lambda i, j: (i, j)),
                  pl.BlockSpec((MOD_ROWS, tn), lambda i, j: (0, gcb + j))],
        out_specs=pl.BlockSpec((tm, tn), lambda i, j: (i, j)),
        out_shape=jax.ShapeDtypeStruct((m, n), F32),
        compiler_params=_cparams(("arbitrary", "arbitrary")),
        name=name,
    )(a, w, x, mod)


def _ln_kernel(z_ref, g_ref, b_ref, *rest, with_mod):
    bi = pl.program_id(0)
    z = z_ref[0]
    mu = jnp.mean(z, axis=-1, keepdims=True)
    zc = z - mu
    var = jnp.mean(zc * zc, axis=-1, keepdims=True)
    out = zc * lax.rsqrt(var + LN_EPS) * g_ref[...] + b_ref[...]
    if with_mod:
        sh_ref, sc_ref, o_ref, h_ref = rest
        o_ref[0] = out
        h_ref[0] = (out * (1.0 + sc_ref[pl.ds(bi, 1), :]) + sh_ref[pl.ds(bi, 1), :]).astype(BF16)
    else:
        (o_ref,) = rest
        o_ref[0] = out


def _res_ln(z, mod, g, b, tb, mod_cols=None):
    bsz, t, d = z.shape
    blk = pl.BlockSpec((1, tb, d), lambda bi, j: (bi, j, 0))
    vec = pl.BlockSpec((1, d), lambda bi, j: (0, 0))

    def modspec(cb):
        return pl.BlockSpec((MOD_ROWS, d), lambda bi, j: (0, cb))

    in_specs = [blk, vec, vec]
    args = [z, g.reshape(1, d), b.reshape(1, d)]
    out_specs = [blk]
    out_shape = [jax.ShapeDtypeStruct((bsz, t, d), F32)]
    if mod_cols is not None:
        in_specs += [modspec(mod_cols[0]), modspec(mod_cols[1])]
        args += [mod, mod]
        out_specs.append(blk)
        out_shape.append(jax.ShapeDtypeStruct((bsz, t, d), BF16))
    return pl.pallas_call(
        functools.partial(_ln_kernel, with_mod=mod_cols is not None),
        grid=(bsz, t // tb),
        in_specs=in_specs,
        out_specs=out_specs,
        out_shape=out_shape,
        compiler_params=_cparams(("arbitrary", "arbitrary")),
        name="residual_layernorm",
    )(*args)


def _ffn_up_kernel(a_ref, wg_ref, wu_ref, o_ref):
    a = a_ref[...]
    gg = _dot(a, wg_ref[...].astype(BF16))
    uu = _dot(a, wu_ref[...].astype(BF16))
    o_ref[...] = (gg * jax.nn.sigmoid(gg) * uu).astype(o_ref.dtype)


def _ffn_up(a, wg, wu, tm, tf):
    m, k = a.shape
    f = wg.shape[1]
    return pl.pallas_call(
        _ffn_up_kernel,
        grid=(m // tm, f // tf),
        in_specs=[pl.BlockSpec((tm, k), lambda i, j: (i, 0)),
                  pl.BlockSpec((k, tf), lambda i, j: (0, j)),
                  pl.BlockSpec((k, tf), lambda i, j: (0, j))],
        out_specs=pl.BlockSpec((tm, tf), lambda i, j: (i, j)),
        out_shape=jax.ShapeDtypeStruct((m, f), BF16),
        compiler_params=_cparams(("arbitrary", "arbitrary")),
        name="swiglu_up",
    )(a, wg, wu)


def _rope_tables(t):
    rows = t // GRID_W
    row = jnp.broadcast_to(jnp.arange(rows, dtype=F32)[:, None], (rows, GRID_W)).reshape(t)
    colp = jnp.broadcast_to(jnp.arange(GRID_W, dtype=F32)[None, :], (rows, GRID_W)).reshape(t)
    axis_dim = ATT_HEAD // 2
    inv = ROPE_BASE ** (-jnp.arange(0, axis_dim, 2, dtype=F32) / axis_dim)
    ang = jnp.concatenate([row[:, None] * inv, colp[:, None] * inv], -1)
    cos, sin = jnp.cos(ang), jnp.sin(ang)
    return jnp.concatenate([cos, cos], -1), jnp.concatenate([-sin, sin], -1)


def _block(x, c, ctx, c_ctx, w_ada, b_ada, w_in, rwkv_shift, rwkv_w0, rwkv_w_up, rwkv_a0, rwkv_a_up,
           rwkv_g_up, rwkv_k_k, rwkv_k_a, rwkv_r_k, rwkv_gn_g, rwkv_gn_b, attn_sink, w_rwkv_o, w_att_o,
           w_out, ln1_g, ln1_b, w_ff_gate, w_ff_up, w_ff_down, ln2_g, ln2_b, tiles):
    bsz, t, d = x.shape
    l = ctx.shape[1]
    tt = t + l
    crw = rwkv_w0.shape[1]
    d_att = ATT_KV_HEADS * ATT_GROUPS * ATT_HEAD
    d_kv = ATT_KV_HEADS * ATT_HEAD
    n_lora = 2 * DECAY_LORA + 2 * ICLR_LORA + GATE_LORA
    n_rw = 3 * crw + n_lora

    assert bsz + 1 <= MOD_ROWS and l == tiles["prep_tb"] and t % tiles["prep_tb"] == 0
    cc = jnp.zeros((MOD_ROWS, d), F32).at[:bsz].set(c).at[bsz].set(c_ctx)
    mod = _ada(cc, w_ada, b_ada, tiles["ada_tn"])

    o_q = n_rw
    o_g = o_q + d_att + 2 * d_kv
    w_main = _stage_columns(w_in, [(0, 3 * crw), (o_q, o_g - o_q)], "stage_w_main")
    w_lora = _stage_columns(w_in, [(3 * crw, n_lora)], "stage_w_lora")
    w_gates = _stage_columns(w_in, [(o_g, w_in.shape[1] - o_g)], "stage_w_gates")
    col = {}
    off = 0
    for name, width in (("r", crw), ("k", crw), ("v", crw), ("q", d_att), ("ak", d_kv), ("av", d_kv)):
        col[name] = off // LANES
        off += width
    shift_rkv = rwkv_shift[:, :3 * crw]
    shift_lora = rwkv_shift[:, 3 * crw:]

    h = _prep(x, ctx, mod)
    h2d = h.reshape(bsz * tt, d)
    (proj,) = _mm_wstat(h2d, w_main, tiles["proj_tm"], tiles["proj_tn"], F32, "in_proj")
    proj = proj.reshape(bsz, tt, -1)
    proj_lora, w_out16, w_ro16, w_ao16 = _mm_wstat(
        h2d, w_lora, tiles["proj_tm"], n_lora, F32, "lora_proj",
        side=[(w_out, tiles["side_rows"]), (w_rwkv_o, tiles["side_rows"]), (w_att_o, tiles["side_rows"])])
    proj_lora = proj_lora.reshape(bsz, tt, -1)
    gates, w_down16 = _mm_lead_rows(h, w_gates, t, tiles["gate_tm"], tiles["gate_tn"], BF16, "gate_proj",
                                    side=[(w_ff_down, tiles["side_rows_down"])])

    g, bonus, ops, dec = _features(proj, proj_lora, col, shift_rkv, shift_lora, rwkv_w0, rwkv_w_up, rwkv_a0,
                                   rwkv_a_up, rwkv_g_up, rwkv_k_k, rwkv_k_a, rwkv_r_k.reshape(-1), t,
                                   tiles["feat_tb"])
    yf, yb = _scan(ops, dec, t, tiles["scan_lanes"])
    o_rwkv = _rwkv_out(yf, yb, bonus, g, rwkv_gn_g, rwkv_gn_b, t, tiles["rout_tb"])

    cos2, sin2 = _rope_tables(t)
    o_att = _attention(proj, col, attn_sink, cos2, sin2, t)

    ym = _merge(o_rwkv, o_att, w_ro16, w_ao16, gates, tiles["merge_tm"], tiles["merge_tn"])
    z1 = _mm_resid(ym.reshape(bsz * t, d), w_out16, x.reshape(bsz * t, d), mod, 2, t,
                   tiles["out_tm"], tiles["out_tn"], "out_proj")
    x1, h2 = _res_ln(z1.reshape(bsz, t, d), mod, ln1_g, ln1_b, tiles["ln_tb"], mod_cols=(3, 4))

    u = _ffn_up(h2.reshape(bsz * t, d), w_ff_gate, w_ff_up, tiles["ffn_tm"], tiles["ffn_tf"])
    z2 = _mm_resid(u, w_down16, x1.reshape(bsz * t, d), mod, 5, t,
                   tiles["down_tm"], tiles["down_tn"], "swiglu_down")
    (out,) = _res_ln(z2.reshape(bsz, t, d), mod, ln2_g, ln2_b, tiles["ln_tb"])
    return out


_TILES = dict(ada_tn=512, prep_tb=256, proj_tm=512, proj_tn=1024, gate_tm=1024, gate_tn=1024,
              side_rows=128, side_rows_down=256, feat_tb=128, scan_lanes=2048, rout_tb=256,
              merge_tm=1024, merge_tn=1024, out_tm=1024, out_tn=1024, ln_tb=512,
              ffn_tm=2048, ffn_tf=256, down_tm=512, down_tn=512)


def kernel(x, c, ctx, c_ctx, w_ada, b_ada, w_in, rwkv_shift, rwkv_w0, rwkv_w_up, rwkv_a0, rwkv_a_up, rwkv_g_up, rwkv_k_k, rwkv_k_a, rwkv_r_k, rwkv_gn_g, rwkv_gn_b, attn_sink, w_rwkv_o, w_att_o, w_out, ln1_g, ln1_b, w_ff_gate, w_ff_up, w_ff_down, ln2_g, ln2_b):
    assert w_ada.shape[0] == DEPTH
    return _block(x, c, ctx, c_ctx, w_ada[0], b_ada[0], w_in[0], rwkv_shift[0], rwkv_w0[0], rwkv_w_up[0],
                  rwkv_a0[0], rwkv_a_up[0], rwkv_g_up[0], rwkv_k_k[0], rwkv_k_a[0], rwkv_r_k[0], rwkv_gn_g[0],
                  rwkv_gn_b[0], attn_sink[0], w_rwkv_o[0], w_att_o[0], w_out[0], ln1_g[0], ln1_b[0],
                  w_ff_gate[0], w_ff_up[0], w_ff_down[0], ln2_g[0], ln2_b[0], _TILES)
```

```python
import functools

import jax
import jax.numpy as jnp
import numpy as np
from jax import lax
from jax.experimental import pallas as pl
from jax.experimental.pallas import tpu as pltpu

F32 = jnp.float32
BF16 = jnp.bfloat16

RWKV_HEAD = 64
DECAY_LORA = 96
ICLR_LORA = 96
GATE_LORA = 256
GN_EPS = 64e-5
ATT_HEAD = 128
ATT_KV_HEADS = 4
ATT_GROUPS = 4
GRID_W = 64
ROPE_BASE = 10000.0
LN_EPS = 1e-5
DEPTH = 1
DEEPNORM_ALPHA = (2 * DEPTH) ** 0.25
NEG_INF = -1e30

LANES = 128
SUBLANES = 8
VMEM_LIMIT = 56 * 1024 * 1024
MOD_ROWS = 8
SCAN_CHUNK = 64
SCAN_LANES = 256
SCAN_PLANES = 5


def _cparams(sem):
    return pltpu.CompilerParams(dimension_semantics=sem, vmem_limit_bytes=VMEM_LIMIT)


def _dot(a, b):
    return jnp.dot(a, b, preferred_element_type=F32)


def _dot_nt(a, b):
    return lax.dot_general(a, b, (((1,), (1,)), ((), ())), preferred_element_type=F32)


def _dot_tn(a, b):
    return lax.dot_general(a, b, (((0,), (0,)), ((), ())), preferred_element_type=F32)


def _split2(x):
    hi = x.astype(BF16)
    return hi, (x - hi.astype(F32)).astype(BF16)


def _head_sum(x, ones_bd):
    rows = x.shape[0]
    both = jnp.concatenate(_split2(x), axis=0)
    out = []
    for s in range(x.shape[1] // SCAN_LANES):
        part = _dot(both[:, s * SCAN_LANES:(s + 1) * SCAN_LANES], ones_bd)
        out.append(part[:rows] + part[rows:])
    return jnp.concatenate(out, axis=1)


def _ada_kernel(c_ref, w_ref, b_ref, o_ref):
    a = c_ref[...]
    a = a * jax.nn.sigmoid(a)
    a_hi, a_lo = _split2(a)
    w_hi, w_lo = _split2(w_ref[...])
    part = _dot(jnp.concatenate([a_hi, a_lo], axis=0), w_hi)
    o_ref[...] = part[:MOD_ROWS] + part[MOD_ROWS:] + _dot(a_hi, w_lo) + b_ref[...]


def _ada(cc, w_ada, b_ada, tn):
    d, n = w_ada.shape
    return pl.pallas_call(
        _ada_kernel,
        grid=(n // tn,),
        in_specs=[pl.BlockSpec((MOD_ROWS, d), lambda j: (0, 0)),
                  pl.BlockSpec((d, tn), lambda j: (0, j)),
                  pl.BlockSpec((1, tn), lambda j: (0, j))],
        out_specs=pl.BlockSpec((MOD_ROWS, tn), lambda j: (0, j)),
        out_shape=jax.ShapeDtypeStruct((MOD_ROWS, n), F32),
        compiler_params=_cparams(("arbitrary",)),
        name="ada",
    )(cc, w_ada, b_ada.reshape(1, n))


def _prep_kernel(x_ref, ctx_ref, sh_ref, sc_ref, o_ref, *, n_lat, ctx_row):
    b = pl.program_id(0)
    j = pl.program_id(1)
    is_ctx = j >= n_lat
    row = jnp.where(is_ctx, ctx_row, b)
    sh = sh_ref[pl.ds(row, 1), :]
    sc = sc_ref[pl.ds(row, 1), :]
    xin = jnp.where(is_ctx, ctx_ref[0], x_ref[0])
    o_ref[0] = (xin * (1.0 + sc) + sh).astype(BF16)


def _prep(x, ctx, mod):
    bsz, t, d = x.shape
    l = ctx.shape[1]
    tb = l
    n_lat = t // tb
    return pl.pallas_call(
        functools.partial(_prep_kernel, n_lat=n_lat, ctx_row=bsz),
        grid=(bsz, n_lat + 1),
        in_specs=[pl.BlockSpec((1, tb, d), lambda b, j: (b, jnp.minimum(j, n_lat - 1), 0)),
                  pl.BlockSpec((1, l, d), lambda b, j: (b, 0, 0)),
                  pl.BlockSpec((MOD_ROWS, d), lambda b, j: (0, 0)),
                  pl.BlockSpec((MOD_ROWS, d), lambda b, j: (0, 1))],
        out_specs=pl.BlockSpec((1, tb, d), lambda b, j: (b, j, 0)),
        out_shape=jax.ShapeDtypeStruct((bsz, t + l, d), BF16),
        compiler_params=_cparams(("arbitrary", "arbitrary")),
        name="prep",
    )(x, ctx, mod, mod)


def _cast_kernel(w_ref, o_ref):
    o_ref[...] = w_ref[...].astype(o_ref.dtype)


def _stage_columns(w, segments, name):
    rows = w.shape[0]
    starts, src = [], []
    off = 0
    for start, width in segments:
        assert start % LANES == 0 and width % LANES == 0
        starts.append(off // LANES)
        src.append(start // LANES)
        off += width

    def src_block(j):
        blk = j - starts[0] + src[0]
        for s0, b0 in zip(starts[1:], src[1:]):
            blk = jnp.where(j >= s0, j - s0 + b0, blk)
        return blk

    return pl.pallas_call(
        _cast_kernel,
        grid=(off // LANES,),
        in_specs=[pl.BlockSpec((rows, LANES), lambda j: (0, src_block(j)))],
        out_specs=pl.BlockSpec((rows, LANES), lambda j: (0, j)),
        out_shape=jax.ShapeDtypeStruct((rows, off), BF16),
        compiler_params=_cparams(("arbitrary",)),
        name=name,
    )(w)


def _mm_kernel(a_ref, w_ref, *refs, side_blocks, strides, n_steps):
    n_side = len(side_blocks)
    o_ref = refs[n_side]
    if len(o_ref.shape) == 3:
        o_ref[0] = _dot(a_ref[0], w_ref[...]).astype(o_ref.dtype)
    else:
        o_ref[...] = _dot(a_ref[...], w_ref[...]).astype(o_ref.dtype)
    step = sum(pl.program_id(ax) * st for ax, st in enumerate(strides))
    for s_in, s_out, nblk in zip(refs[:n_side], refs[n_side + 1:], side_blocks):
        fresh = jnp.logical_or(step == 0, (step * nblk) // n_steps != ((step - 1) * nblk) // n_steps)

        @pl.when(fresh)
        def _(s_in=s_in, s_out=s_out):
            s_out[...] = s_in[...].astype(s_out.dtype)


def _side_specs(side, n_steps, step_of):
    in_specs, out_specs, out_shape = [], [], []
    for arr, rows in side:
        nblk = arr.shape[0] // rows
        assert arr.shape[0] % rows == 0 and nblk <= n_steps
        spec = pl.BlockSpec((rows, arr.shape[1]), lambda *g, nblk=nblk: ((step_of(*g) * nblk) // n_steps, 0))
        in_specs.append(spec)
        out_specs.append(spec)
        out_shape.append(jax.ShapeDtypeStruct(arr.shape, BF16))
    return in_specs, out_specs, out_shape


def _mm_wstat(a, w, tm, tn, out_dtype, name, side=()):
    m, k = a.shape
    n = w.shape[1]
    nj, ni = pl.cdiv(n, tn), m // tm
    s_in, s_out, s_shape = _side_specs(side, nj * ni, lambda j, i: j * ni + i)
    return pl.pallas_call(
        functools.partial(_mm_kernel, side_blocks=tuple(arr.shape[0] // r for arr, r in side),
                          strides=(ni, 1), n_steps=nj * ni),
        grid=(nj, ni),
        in_specs=[pl.BlockSpec((tm, k), lambda j, i: (i, 0)),
                  pl.BlockSpec((k, tn), lambda j, i: (0, j))] + s_in,
        out_specs=[pl.BlockSpec((tm, tn), lambda j, i: (i, j))] + s_out,
        out_shape=[jax.ShapeDtypeStruct((m, n), out_dtype)] + s_shape,
        compiler_params=_cparams(("arbitrary", "arbitrary")),
        name=name,
    )(a, w, *[arr for arr, _ in side])


def _mm_lead_rows(a, w, rows, tm, tn, out_dtype, name, side=()):
    bsz, _, k = a.shape
    n = w.shape[1]
    nj, ni = n // tn, rows // tm
    s_in, s_out, s_shape = _side_specs(side, nj * bsz * ni, lambda j, b, i: (j * bsz + b) * ni + i)
    return pl.pallas_call(
        functools.partial(_mm_kernel, side_blocks=tuple(arr.shape[0] // r for arr, r in side),
                          strides=(bsz * ni, ni, 1), n_steps=nj * bsz * ni),
        grid=(nj, bsz, ni),
        in_specs=[pl.BlockSpec((1, tm, k), lambda j, b, i: (b, i, 0)),
                  pl.BlockSpec((k, tn), lambda j, b, i: (0, j))] + s_in,
        out_specs=[pl.BlockSpec((1, tm, tn), lambda j, b, i: (b, i, j))] + s_out,
        out_shape=[jax.ShapeDtypeStruct((bsz, rows, n), out_dtype)] + s_shape,
        compiler_params=_cparams(("arbitrary", "arbitrary", "arbitrary")),
        name=name,
    )(a, w, *[arr for arr, _ in side])


def _conv3(prev_ref, cur_ref, next_ref, w_ref, at_start, at_end):
    cur = cur_ref[0]
    rows = cur.shape[0]
    prev_row = jnp.where(at_start, 0.0, prev_ref[0][SUBLANES - 1:SUBLANES, :])
    next_row = jnp.where(at_end, 0.0, next_ref[0][0:1, :])
    up = jnp.concatenate([prev_row, cur[:rows - 1]], axis=0)
    down = jnp.concatenate([cur[1:], next_row], axis=0)
    w = w_ref[...]
    return up * w[0:1] + cur * w[1:2] + down * w[2:3]


def _feat_kernel(rp, rc, rn, kp, kc, kn, vp, vc, vn, lp, lc, ln_,
                 shr, shk, shv, shl, w0, wup, a0, aup, gup, kk_w, ka_w, rk_w, ones_ref, cm_ref,
                 g_o, bonus_o, ops_o, dec_o, *, n_lat, n_tot, chunk):
    j = pl.program_id(1)
    at_start = jnp.logical_or(j == 0, j == n_lat)
    at_end = jnp.logical_or(j == n_lat - 1, j == n_tot - 1)
    r = _conv3(rp, rc, rn, shr, at_start, at_end)
    k = _conv3(kp, kc, kn, shk, at_start, at_end)
    v = _conv3(vp, vc, vn, shv, at_start, at_end)
    lo = _conv3(lp, lc, ln_, shl, at_start, at_end)
    rows = r.shape[0]
    v_h = v.astype(BF16)
    gd = lo[:, 2 * DECAY_LORA + 2 * ICLR_LORA:2 * DECAY_LORA + 2 * ICLR_LORA + GATE_LORA]
    ones_bd = ones_ref[...]
    g_o[0] = _dot(jax.nn.sigmoid(gd).astype(BF16), gup[...])
    kk = k * kk_w[...]
    kk = kk * lax.rsqrt(jnp.maximum(_head_sum(kk * kk, ones_bd), 1e-24))
    k_sum = jnp.zeros_like(k)
    for d in range(2):
        wd = lo[:, d * DECAY_LORA:(d + 1) * DECAY_LORA]
        ad = lo[:, 2 * DECAY_LORA + d * ICLR_LORA:2 * DECAY_LORA + (d + 1) * ICLR_LORA]
        z = w0[d:d + 1, :] + _dot(jnp.tanh(wd).astype(BF16), wup[d])
        lw = jax.nn.sigmoid(z) * (-np.exp(-0.5))
        iclr = jax.nn.sigmoid(a0[d:d + 1, :] + _dot(ad.astype(BF16), aup[d]))
        k_dir = k * (1.0 + (iclr - 1.0) * ka_w[...])
        k_sum = k_sum + k_dir
        lw_hi, lw_lo = _split2(lw)
        cum = _dot(cm_ref[d], lw_hi) + _dot(cm_ref[d], lw_lo)
        p_inv = jnp.exp(-cum)
        planes = [(kk * jnp.exp(cum - lw)).astype(BF16), (r * jnp.exp(cum)).astype(BF16),
                  (k_dir * p_inv).astype(BF16), (kk * iclr * p_inv).astype(BF16), v_h]
        for hg in range(r.shape[1] // SCAN_LANES):
            for p, plane in enumerate(planes):
                dst = (hg * SCAN_PLANES + p) * SCAN_LANES
                ops_o[d, 0, :, dst:dst + SCAN_LANES] = plane[:, hg * SCAN_LANES:(hg + 1) * SCAN_LANES]
        for q in range(rows // chunk):
            last = (q + 1) * chunk - 1 if d == 0 else q * chunk
            dec_o[d, 0, q] = jnp.exp(cum[last:last + 1, :])
    bonus_o[0] = _head_sum(r * k_sum * rk_w[...], ones_bd) * v


def _features(proj, proj_lora, col, shift_rkv, shift_lora, w0, w_up, a0, a_up, g_up, k_k, k_a, r_k, t_lat, tb):
    bsz, tt, _ = proj.shape
    c = w0.shape[1]
    n_lora = proj_lora.shape[2]
    chunk = SCAN_CHUNK
    n_tot = tt // tb
    n_lat = t_lat // tb
    hb = tb // SUBLANES
    n_h = tt // SUBLANES
    li = np.arange(SCAN_LANES) // RWKV_HEAD
    ones_bd = jnp.asarray(li[:, None] == li[None, :], BF16)
    ti = np.arange(tb)
    same = (ti[:, None] // chunk) == (ti[None, :] // chunk)
    cmask = jnp.asarray(np.stack([same & (ti[None, :] <= ti[:, None]), same & (ti[None, :] >= ti[:, None])]), BF16)

    def main(cb, width):
        blk = cb * LANES // width
        return pl.BlockSpec((1, tb, width), lambda b, j: (b, j, blk))

    def prev(cb, width):
        blk = cb * LANES // width
        return pl.BlockSpec((1, SUBLANES, width), lambda b, j: (b, jnp.maximum(j * hb - 1, 0), blk))

    def nxt(cb, width):
        blk = cb * LANES // width
        return pl.BlockSpec((1, SUBLANES, width), lambda b, j: (b, jnp.minimum((j + 1) * hb, n_h - 1), blk))

    def const(shape):
        nd = len(shape)
        return pl.BlockSpec(shape, lambda b, j: (0,) * nd)

    in_specs = []
    for name in ("r", "k", "v"):
        in_specs += [prev(col[name], c), main(col[name], c), nxt(col[name], c)]
    in_specs += [prev(0, n_lora), main(0, n_lora), nxt(0, n_lora)]
    in_specs += [const((3, c)), const((3, c)), const((3, c)), const((3, n_lora)),
                 const((2, c)), const((2, DECAY_LORA, c)), const((2, c)), const((2, ICLR_LORA, c)),
                 const((GATE_LORA, c)), const((1, c)), const((1, c)), const((1, c)),
                 const((SCAN_LANES, SCAN_LANES)), const((2, tb, tb))]
    one = pl.BlockSpec((1, tb, c), lambda b, j: (b, j, 0))
    ops = pl.BlockSpec((2, 1, tb, SCAN_PLANES * c), lambda b, j: (0, b, j, 0))
    dec = pl.BlockSpec((2, 1, tb // chunk, 1, c), lambda b, j: (0, b, j, 0, 0))
    s1 = jax.ShapeDtypeStruct((bsz, tt, c), F32)
    sops = jax.ShapeDtypeStruct((2, bsz, tt, SCAN_PLANES * c), BF16)
    sdec = jax.ShapeDtypeStruct((2, bsz, tt // chunk, 1, c), F32)
    args = [proj] * 9 + [proj_lora] * 3 + [shift_rkv[:, :c], shift_rkv[:, c:2 * c], shift_rkv[:, 2 * c:], shift_lora,
                          w0, w_up.astype(BF16), a0, a_up.astype(BF16), g_up.astype(BF16),
                          k_k.reshape(1, c), k_a.reshape(1, c), r_k.reshape(1, c), ones_bd, cmask]
    return pl.pallas_call(
        functools.partial(_feat_kernel, n_lat=n_lat, n_tot=n_tot, chunk=chunk),
        grid=(bsz, n_tot),
        in_specs=in_specs,
        out_specs=[one, one, ops, dec],
        out_shape=[s1, s1, sops, sdec],
        compiler_params=_cparams(("arbitrary", "arbitrary")),
        name="rwkv_features",
    )(*args)


def _scan_kernel(gm_ref, bdm_ref, hm_ref, eye_ref, *refs, chunk, nsub):
    ops_refs = refs[0:4:2]
    dec_refs = refs[1:4:2]
    y_refs = refs[4:6]
    s_ref = refs[6]
    heads = SCAN_LANES // RWKV_HEAD

    @pl.when(pl.program_id(2) == 0)
    def _():
        s_ref[...] = jnp.zeros_like(s_ref)

    bdm = bdm_ref[...]
    bdm_h = bdm.astype(BF16)
    eye = eye_ref[...]
    hms = [hm_ref[h] for h in range(heads)]

    def rows_bd(m):
        return jnp.concatenate([m * hms[h] for h in range(heads)], axis=0)

    def blocks_bd(m):
        return jnp.concatenate([m] * heads, axis=0) * bdm_h

    steps = chunk.bit_length() - 2
    units = [(d, q) for d in range(2) for q in range(nsub)]
    n_u = len(units)

    def lanes(q):
        return slice(q * SCAN_LANES, (q + 1) * SCAN_LANES)

    def plane(d, q, p):
        return ops_refs[d][0, 0, :, lanes(q * SCAN_PLANES + p)]

    kt = [plane(d, q, 2) for d, q in units]
    bt = [plane(d, q, 3) for d, q in units]
    v = [plane(d, q, 4) for d, q in units]
    x = [jnp.concatenate([plane(d, q, 0), plane(d, q, 1)], axis=0) for d, q in units]
    g = [_dot_nt(x[i], jnp.concatenate([rows_bd(kt[i]), rows_bd(bt[i])], axis=0)) * gm_ref[units[i][0]]
         for i in range(n_u)]
    s_old = [s_ref[i] for i in range(n_u)]
    xs = [_dot_nt(x[i], s_old[i].astype(BF16)) for i in range(n_u)]
    gv = [_dot(g[i][:, :SCAN_LANES].astype(BF16), rows_bd(v[i])) for i in range(n_u)]
    rhs = [xs[i][:chunk] + gv[i][:chunk] for i in range(n_u)]
    t = [eye - g[i][:chunk, SCAN_LANES:] for i in range(n_u)]
    lm = [g[i][:chunk, SCAN_LANES:].astype(BF16) for i in range(n_u)]
    pw = [_dot(lm[i], blocks_bd(lm[i])).astype(BF16) for i in range(n_u)]
    for _ in range(steps - 1):
        both = [_dot(jnp.concatenate([t[i].astype(BF16), pw[i]], axis=0), blocks_bd(pw[i])) for i in range(n_u)]
        t = [t[i] + both[i][:chunk] for i in range(n_u)]
        pw = [both[i][chunk:].astype(BF16) for i in range(n_u)]
    t = [t[i] + _dot(t[i].astype(BF16), blocks_bd(pw[i])) for i in range(n_u)]
    u = [(-_dot(t[i].astype(BF16), rows_bd(rhs[i].astype(BF16)))).astype(BF16) for i in range(n_u)]
    for i, (d, q) in enumerate(units):
        y_refs[d][0, :, lanes(q)] = (xs[i][chunk:] + gv[i][chunk:]
                                     + _dot(g[i][chunk:, SCAN_LANES:].astype(BF16), rows_bd(u[i])))
    for i, (d, q) in enumerate(units):
        upd = _dot_tn(jnp.concatenate([v[i], u[i]], axis=0), jnp.concatenate([kt[i], bt[i]], axis=0))
        s_ref[i] = (s_old[i] + upd) * dec_refs[d][0, 0, 0, :, lanes(q)] * bdm


def _scan(ops, dec, t_lat, lane_block):
    _, bsz, tt, c = ops.shape
    c //= SCAN_PLANES
    chunk = SCAN_CHUNK
    heads = SCAN_LANES // RWKV_HEAD
    n_tot = tt // chunk
    n_lat = t_lat // chunk
    n_ctx = n_tot - n_lat
    nsub = lane_block // SCAN_LANES

    ti = np.arange(chunk)
    before = [ti[None, :] < ti[:, None], ti[None, :] > ti[:, None]]
    gmask = np.stack([np.concatenate([np.tile(before[d], (1, 2 * heads)),
                                      np.tile(before[d] | np.eye(chunk, dtype=bool), (1, 2 * heads))], axis=0)
                      for d in range(2)]).astype(np.float32)
    li = np.arange(SCAN_LANES) // RWKV_HEAD
    bdm = (li[:, None] == li[None, :]).astype(np.float32)
    hmask = jnp.asarray((li[None, None, :] == np.arange(heads)[:, None, None]), BF16)
    eye = np.tile(np.eye(chunk, dtype=np.float32), (1, heads))

    def chunk_index(d, s):
        return jnp.where(s < n_ctx, n_lat + s, s - n_ctx) if d == 0 else n_tot - 1 - s

    def const(shape):
        nd = len(shape)
        return pl.BlockSpec(shape, lambda b, h, s: (0,) * nd)

    in_specs = [const(gmask.shape), const(bdm.shape), const(hmask.shape), const(eye.shape)]
    args = [jnp.asarray(gmask), jnp.asarray(bdm), hmask, jnp.asarray(eye)]
    out_specs = []
    for d in range(2):
        in_specs += [pl.BlockSpec((1, 1, chunk, SCAN_PLANES * lane_block),
                                  lambda b, h, s, d=d: (d, b, chunk_index(d, s), h)),
                     pl.BlockSpec((1, 1, 1, 1, lane_block), lambda b, h, s, d=d: (d, b, chunk_index(d, s), 0, h))]
        args += [ops, dec]
        out_specs.append(pl.BlockSpec((1, chunk, lane_block), lambda b, h, s, d=d: (b, chunk_index(d, s), h)))
    ys = jax.ShapeDtypeStruct((bsz, tt, c), F32)
    return pl.pallas_call(
        functools.partial(_scan_kernel, chunk=chunk, nsub=nsub),
        grid=(bsz, c // lane_block, n_tot),
        in_specs=in_specs,
        out_specs=out_specs,
        out_shape=[ys, ys],
        scratch_shapes=[pltpu.VMEM((2 * nsub, SCAN_LANES, SCAN_LANES), F32)],
        compiler_params=_cparams(("arbitrary", "arbitrary", "arbitrary")),
        name="rwkv_scan",
    )(*args)


def _rout_kernel(yf, yb, bonus, g, gng, gnb, ones_ref, o_ref):
    ones_bd = ones_ref[...]
    inv_n = 1.0 / RWKV_HEAD
    y = yf[0] + yb[0]
    mu = _head_sum(y, ones_bd) * inv_n
    yc = y - mu
    var = _head_sum(yc * yc, ones_bd) * inv_n
    yn = yc * lax.rsqrt(var + GN_EPS) * gng[...] + gnb[...]
    o_ref[0] = ((yn + bonus[0]) * g[0]).astype(o_ref.dtype)


def _rwkv_out(yf, yb, bonus, g, gn_g, gn_b, t_lat, tb):
    bsz, _, c = g.shape
    li = np.arange(SCAN_LANES) // RWKV_HEAD
    ones_bd = jnp.asarray(li[:, None] == li[None, :], BF16)
    one = pl.BlockSpec((1, tb, c), lambda b, j: (b, j, 0))

    def const(shape):
        return pl.BlockSpec(shape, lambda b, j: (0, 0))

    return pl.pallas_call(
        _rout_kernel,
        grid=(bsz, t_lat // tb),
        in_specs=[one, one, one, one, const((1, c)), const((1, c)), const((SCAN_LANES, SCAN_LANES))],
        out_specs=one,
        out_shape=jax.ShapeDtypeStruct((bsz, t_lat, c), BF16),
        compiler_params=_cparams(("arbitrary", "arbitrary")),
        name="rwkv_out",
    )(yf, yb, bonus, g, gn_g.reshape(1, c), gn_b.reshape(1, c), ones_bd)


def _rope(x, c, s):
    return x * c + pltpu.roll(x, ATT_HEAD // 2, 1) * s


def _attn_kernel(sink_ref, q_ref, km_ref, k0_ref, kp_ref, vm_ref, v0_ref, vp_ref, kc_ref, vc_ref,
                 c0_ref, s0_ref, cm_ref, sm_ref, cp_ref, sp_ref, o_ref, *, n_blk):
    n = pl.program_id(1)
    blk = ATT_HEAD
    scale = ATT_HEAD ** -0.5
    kvh = range(ATT_KV_HEADS)
    c0, s0 = c0_ref[...], s0_ref[...]
    cm, sm = cm_ref[...], sm_ref[...]
    cp, sp = cp_ref[...], sp_ref[...]

    def head(ref, i):
        return ref[0, :, i * ATT_HEAD:(i + 1) * ATT_HEAD]

    qq = [jnp.concatenate([_rope(head(q_ref, g * ATT_GROUPS + h), c0, s0) for h in range(ATT_GROUPS)],
                          axis=0).astype(BF16) for g in kvh]
    kw = [jnp.concatenate([_rope(head(km_ref, g), cm, sm), _rope(head(k0_ref, g), c0, s0),
                           _rope(head(kp_ref, g), cp, sp)], axis=0).astype(BF16) for g in kvh]
    vw = [jnp.concatenate([head(vm_ref, g), head(v0_ref, g), head(vp_ref, g)], axis=0).astype(BF16) for g in kvh]
    kc = [head(kc_ref, g).astype(BF16) for g in kvh]
    vc = [head(vc_ref, g).astype(BF16) for g in kvh]
    ii = lax.broadcasted_iota(jnp.int32, (ATT_GROUPS * blk, 3 * blk), 0) % blk
    jj = lax.broadcasted_iota(jnp.int32, (ATT_GROUPS * blk, 3 * blk), 1)
    in_prev = jnp.logical_and(jnp.logical_and(jj < blk, jj >= ii), n > 0)
    in_self = jnp.logical_and(jj >= blk, jj < 2 * blk)
    in_next = jnp.logical_and(jnp.logical_and(jj >= 2 * blk, jj - 2 * blk <= ii), n < n_blk - 1)
    valid = jnp.logical_or(jnp.logical_or(in_prev, in_self), in_next)
    s_w = [jnp.where(valid, _dot_nt(qq[g], kw[g]) * scale, NEG_INF) for g in kvh]
    s_c = [_dot_nt(qq[g], kc[g]) * scale for g in kvh]
    s_s = [jnp.concatenate([jnp.full((blk, 1), sink_ref[g * ATT_GROUPS + h], F32) for h in range(ATT_GROUPS)],
                           axis=0) for g in kvh]
    m = [jnp.maximum(jnp.maximum(jnp.max(s_w[g], axis=-1, keepdims=True),
                                 jnp.max(s_c[g], axis=-1, keepdims=True)), s_s[g]) for g in kvh]
    p_w = [jnp.exp(s_w[g] - m[g]) for g in kvh]
    p_c = [jnp.exp(s_c[g] - m[g]) for g in kvh]
    den = [jnp.sum(p_w[g], axis=-1, keepdims=True) + jnp.sum(p_c[g], axis=-1, keepdims=True)
           + jnp.exp(s_s[g] - m[g]) for g in kvh]
    o = [(_dot(p_w[g].astype(BF16), vw[g]) + _dot(p_c[g].astype(BF16), vc[g])) / den[g] for g in kvh]
    o_ref[0] = jnp.concatenate([o[g][h * blk:(h + 1) * blk] for g in kvh for h in range(ATT_GROUPS)],
                               axis=1).astype(o_ref.dtype)


def _attention(proj, col, sink, cos2, sin2, t_lat):
    bsz, tt, _ = proj.shape
    blk = ATT_HEAD
    n_blk = t_lat // blk
    l = tt - t_lat
    qw = ATT_KV_HEADS * ATT_GROUPS * ATT_HEAD
    kw = ATT_KV_HEADS * ATT_HEAD
    q_blk = col["q"] * LANES // qw

    def kv(name, off):
        cb = col[name] * LANES // kw
        return pl.BlockSpec((1, blk, kw), lambda b, n: (b, jnp.clip(n + off, 0, n_blk - 1), cb))

    def ctx(name):
        cb = col[name] * LANES // kw
        return pl.BlockSpec((1, l, kw), lambda b, n: (b, t_lat // l, cb))

    def tab(off):
        return pl.BlockSpec((blk, ATT_HEAD), lambda b, n: (jnp.clip(n + off, 0, n_blk - 1), 0))

    in_specs = [pl.BlockSpec(memory_space=pltpu.SMEM),
                pl.BlockSpec((1, blk, qw), lambda b, n: (b, n, q_blk)),
                kv("ak", -1), kv("ak", 0), kv("ak", 1), kv("av", -1), kv("av", 0), kv("av", 1),
                ctx("ak"), ctx("av"), tab(0), tab(0), tab(-1), tab(-1), tab(1), tab(1)]
    return pl.pallas_call(
        functools.partial(_attn_kernel, n_blk=n_blk),
        grid=(bsz, n_blk),
        in_specs=in_specs,
        out_specs=pl.BlockSpec((1, blk, qw), lambda b, n: (b, n, 0)),
        out_shape=jax.ShapeDtypeStruct((bsz, t_lat, qw), BF16),
        compiler_params=_cparams(("arbitrary", "arbitrary")),
        name="window_attention",
    )(sink, proj, proj, proj, proj, proj, proj, proj, proj, proj, cos2, sin2, cos2, sin2, cos2, sin2)


def _merge_kernel(o1_ref, o2_ref, w1_ref, w2_ref, g1_ref, g2_ref, o_ref):
    y1 = _dot(o1_ref[0], w1_ref[...])
    y2 = _dot(o2_ref[0], w2_ref[...])
    o_ref[0] = (jax.nn.sigmoid(g1_ref[0].astype(F32)) * y1
                + jax.nn.sigmoid(g2_ref[0].astype(F32)) * y2).astype(o_ref.dtype)


def _merge(o_rwkv, o_att, w1, w2, gates, tm, tn):
    bsz, t, c = o_rwkv.shape
    n = w1.shape[1]
    gr = 0
    ga = n // tn
    return pl.pallas_call(
        _merge_kernel,
        grid=(bsz, t // tm, n // tn),
        in_specs=[pl.BlockSpec((1, tm, c), lambda b, i, j: (b, i, 0)),
                  pl.BlockSpec((1, tm, c), lambda b, i, j: (b, i, 0)),
                  pl.BlockSpec((c, tn), lambda b, i, j: (0, j)),
                  pl.BlockSpec((c, tn), lambda b, i, j: (0, j)),
                  pl.BlockSpec((1, tm, tn), lambda b, i, j: (b, i, gr + j)),
                  pl.BlockSpec((1, tm, tn), lambda b, i, j: (b, i, ga + j))],
        out_specs=pl.BlockSpec((1, tm, tn), lambda b, i, j: (b, i, j)),
        out_shape=jax.ShapeDtypeStruct((bsz, t, n), BF16),
        compiler_params=_cparams(("arbitrary", "arbitrary", "arbitrary")),
        name="gated_merge",
    )(o_rwkv, o_att, w1, w2, gates, gates)


def _mm_resid_kernel(a_ref, w_ref, x_ref, gt_ref, o_ref, *, tiles_per_batch):
    bi = pl.program_id(0) // tiles_per_batch
    y = _dot(a_ref[...], w_ref[...])
    o_ref[...] = DEEPNORM_ALPHA * x_ref[...] + gt_ref[pl.ds(bi, 1), :] * y


def _mm_resid(a, w, x, mod, gate_col, rows_per_batch, tm, tn, name):
    m, k = a.shape
    n = w.shape[1]
    gcb = gate_col * (n // tn)
    return pl.pallas_call(
        functools.partial(_mm_resid_kernel, tiles_per_batch=rows_per_batch // tm),
        grid=(m // tm, n // tn),
        in_specs=[pl.BlockSpec((tm, k), lambda i, j: (i, 0)),
                  pl.BlockSpec((k, tn), lambda i, j: (0, j)),
                  pl.BlockSpec((tm, tn), lambda i, j: (i, j)),
                  pl.BlockSpec((MOD_ROWS, tn), lambda i, j: (0, gcb + j))],
        out_specs=pl.BlockSpec((tm, tn), lambda i, j: (i, j)),
        out_shape=jax.ShapeDtypeStruct((m, n), F32),
        compiler_params=_cparams(("arbitrary", "arbitrary")),
        name=name,
    )(a, w, x, mod)


def _ln_kernel(z_ref, g_ref, b_ref, *rest, with_mod):
    bi = pl.program_id(0)
    z = z_ref[0]
    mu = jnp.mean(z, axis=-1, keepdims=True)
    zc = z - mu
    var = jnp.mean(zc * zc, axis=-1, keepdims=True)
    out = zc * lax.rsqrt(var + LN_EPS) * g_ref[...] + b_ref[...]
    if with_mod:
        sh_ref, sc_ref, o_ref, h_ref = rest
        o_ref[0] = out
        h_ref[0] = (out * (1.0 + sc_ref[pl.ds(bi, 1), :]) + sh_ref[pl.ds(bi, 1), :]).astype(BF16)
    else:
        (o_ref,) = rest
        o_ref[0] = out


def _res_ln(z, mod, g, b, tb, mod_cols=None):
    bsz, t, d = z.shape
    blk = pl.BlockSpec((1, tb, d), lambda bi, j: (bi, j, 0))
    vec = pl.BlockSpec((1, d), lambda bi, j: (0, 0))

    def modspec(cb):
        return pl.BlockSpec((MOD_ROWS, d), lambda bi, j: (0, cb))

    in_specs = [blk, vec, vec]
    args = [z, g.reshape(1, d), b.reshape(1, d)]
    out_specs = [blk]
    out_shape = [jax.ShapeDtypeStruct((bsz, t, d), F32)]
    if mod_cols is not None:
        in_specs += [modspec(mod_cols[0]), modspec(mod_cols[1])]
        args += [mod, mod]
        out_specs.append(blk)
        out_shape.append(jax.ShapeDtypeStruct((bsz, t, d), BF16))
    return pl.pallas_call(
        functools.partial(_ln_kernel, with_mod=mod_cols is not None),
        grid=(bsz, t // tb),
        in_specs=in_specs,
        out_specs=out_specs,
        out_shape=out_shape,
        compiler_params=_cparams(("arbitrary", "arbitrary")),
        name="residual_layernorm",
    )(*args)


def _ffn_up_kernel(a_ref, wg_ref, wu_ref, o_ref):
    a = a_ref[...]
    gg = _dot(a, wg_ref[...].astype(BF16))
    uu = _dot(a, wu_ref[...].astype(BF16))
    o_ref[...] = (gg * jax.nn.sigmoid(gg) * uu).astype(o_ref.dtype)


def _ffn_up(a, wg, wu, tm, tf):
    m, k = a.shape
    f = wg.shape[1]
    return pl.pallas_call(
        _ffn_up_kernel,
        grid=(m // tm, f // tf),
        in_specs=[pl.BlockSpec((tm, k), lambda i, j: (i, 0)),
                  pl.BlockSpec((k, tf), lambda i, j: (0, j)),
                  pl.BlockSpec((k, tf), lambda i, j: (0, j))],
        out_specs=pl.BlockSpec((tm, tf), lambda i, j: (i, j)),
        out_shape=jax.ShapeDtypeStruct((m, f), BF16),
        compiler_params=_cparams(("arbitrary", "arbitrary")),
        name="swiglu_up",
    )(a, wg, wu)


def _rope_tables(t):
    rows = t // GRID_W
    row = jnp.broadcast_to(jnp.arange(rows, dtype=F32)[:, None], (rows, GRID_W)).reshape(t)
    colp = jnp.broadcast_to(jnp.arange(GRID_W, dtype=F32)[None, :], (rows, GRID_W)).reshape(t)
    axis_dim = ATT_HEAD // 2
    inv = ROPE_BASE ** (-jnp.arange(0, axis_dim, 2, dtype=F32) / axis_dim)
    ang = jnp.concatenate([row[:, None] * inv, colp[:, None] * inv], -1)
    cos, sin = jnp.cos(ang), jnp.sin(ang)
    return jnp.concatenate([cos, cos], -1), jnp.concatenate([-sin, sin], -1)


def _block(x, c, ctx, c_ctx, w_ada, b_ada, w_in, rwkv_shift, rwkv_w0, rwkv_w_up, rwkv_a0, rwkv_a_up,
           rwkv_g_up, rwkv_k_k, rwkv_k_a, rwkv_r_k, rwkv_gn_g, rwkv_gn_b, attn_sink, w_rwkv_o, w_att_o,
           w_out, ln1_g, ln1_b, w_ff_gate, w_ff_up, w_ff_down, ln2_g, ln2_b, tiles):
    bsz, t, d = x.shape
    l = ctx.shape[1]
    tt = t + l
    crw = rwkv_w0.shape[1]
    d_att = ATT_KV_HEADS * ATT_GROUPS * ATT_HEAD
    d_kv = ATT_KV_HEADS * ATT_HEAD
    n_lora = 2 * DECAY_LORA + 2 * ICLR_LORA + GATE_LORA
    n_rw = 3 * crw + n_lora

    assert bsz + 1 <= MOD_ROWS and l == tiles["prep_tb"] and t % tiles["prep_tb"] == 0
    cc = jnp.zeros((MOD_ROWS, d), F32).at[:bsz].set(c).at[bsz].set(c_ctx)
    mod = _ada(cc, w_ada, b_ada, tiles["ada_tn"])

    o_q = n_rw
    o_g = o_q + d_att + 2 * d_kv
    w_main = _stage_columns(w_in, [(0, 3 * crw), (o_q, o_g - o_q)], "stage_w_main")
    w_lora = _stage_columns(w_in, [(3 * crw, n_lora)], "stage_w_lora")
    w_gates = _stage_columns(w_in, [(o_g, w_in.shape[1] - o_g)], "stage_w_gates")
    col = {}
    off = 0
    for name, width in (("r", crw), ("k", crw), ("v", crw), ("q", d_att), ("ak", d_kv), ("av", d_kv)):
        col[name] = off // LANES
        off += width
    shift_rkv = rwkv_shift[:, :3 * crw]
    shift_lora = rwkv_shift[:, 3 * crw:]

    h = _prep(x, ctx, mod)
    h2d = h.reshape(bsz * tt, d)
    (proj,) = _mm_wstat(h2d, w_main, tiles["proj_tm"], tiles["proj_tn"], F32, "in_proj")
    proj = proj.reshape(bsz, tt, -1)
    proj_lora, w_out16, w_ro16, w_ao16 = _mm_wstat(
        h2d, w_lora, tiles["proj_tm"], n_lora, F32, "lora_proj",
        side=[(w_out, tiles["side_rows"]), (w_rwkv_o, tiles["side_rows"]), (w_att_o, tiles["side_rows"])])
    proj_lora = proj_lora.reshape(bsz, tt, -1)
    gates, w_down16 = _mm_lead_rows(h, w_gates, t, tiles["gate_tm"], tiles["gate_tn"], BF16, "gate_proj",
                                    side=[(w_ff_down, tiles["side_rows_down"])])

    g, bonus, ops, dec = _features(proj, proj_lora, col, shift_rkv, shift_lora, rwkv_w0, rwkv_w_up, rwkv_a0,
                                   rwkv_a_up, rwkv_g_up, rwkv_k_k, rwkv_k_a, rwkv_r_k.reshape(-1), t,
                                   tiles["feat_tb"])
    yf, yb = _scan(ops, dec, t, tiles["scan_lanes"])
    o_rwkv = _rwkv_out(yf, yb, bonus, g, rwkv_gn_g, rwkv_gn_b, t, tiles["rout_tb"])

    cos2, sin2 = _rope_tables(t)
    o_att = _attention(proj, col, attn_sink, cos2, sin2, t)

    ym = _merge(o_rwkv, o_att, w_ro16, w_ao16, gates, tiles["merge_tm"], tiles["merge_tn"])
    z1 = _mm_resid(ym.reshape(bsz * t, d), w_out16, x.reshape(bsz * t, d), mod, 2, t,
                   tiles["out_tm"], tiles["out_tn"], "out_proj")
    x1, h2 = _res_ln(z1.reshape(bsz, t, d), mod, ln1_g, ln1_b, tiles["ln_tb"], mod_cols=(3, 4))

    u = _ffn_up(h2.reshape(bsz * t, d), w_ff_gate, w_ff_up, tiles["ffn_tm"], tiles["ffn_tf"])
    z2 = _mm_resid(u, w_down16, x1.reshape(bsz * t, d), mod, 5, t,
                   tiles["down_tm"], tiles["down_tn"], "swiglu_down")
    (out,) = _res_ln(z2.reshape(bsz, t, d), mod, ln2_g, ln2_b, tiles["ln_tb"])
    return out


_TILES = dict(ada_tn=512, prep_tb=256, proj_tm=512, proj_tn=1536, gate_tm=1024, gate_tn=1024,
              side_rows=128, side_rows_down=256, feat_tb=128, scan_lanes=2048, rout_tb=256,
              merge_tm=1024, merge_tn=1024, out_tm=1024, out_tn=1024, ln_tb=512,
              ffn_tm=2048, ffn_tf=256, down_tm=512, down_tn=512)


def kernel(x, c, ctx, c_ctx, w_ada, b_ada, w_in, rwkv_shift, rwkv_w0, rwkv_w_up, rwkv_a0, rwkv_a_up, rwkv_g_up, rwkv_k_k, rwkv_k_a, rwkv_r_k, rwkv_gn_g, rwkv_gn_b, attn_sink, w_rwkv_o, w_att_o, w_out, ln1_g, ln1_b, w_ff_gate, w_ff_up, w_ff_down, ln2_g, ln2_b):
    assert w_ada.shape[0] == DEPTH
    return _block(x, c, ctx, c_ctx, w_ada[0], b_ada[0], w_in[0], rwkv_shift[0], rwkv_w0[0], rwkv_w_up[0],
                  rwkv_a0[0], rwkv_a_up[0], rwkv_g_up[0], rwkv_k_k[0], rwkv_k_a[0], rwkv_r_k[0], rwkv_gn_g[0],
                  rwkv_gn_b[0], attn_sink[0], w_rwkv_o[0], w_att_o[0], w_out[0], ln1_g[0], ln1_b[0],
                  w_ff_gate[0], w_ff_up[0], w_ff_down[0], ln2_g[0], ln2_b[0], _TILES)
```

```python
import functools

import jax
import jax.numpy as jnp
import numpy as np
from jax import lax
from jax.experimental import pallas as pl
from jax.experimental.pallas import tpu as pltpu

F32 = jnp.float32
BF16 = jnp.bfloat16

RWKV_HEAD = 64
DECAY_LORA = 96
ICLR_LORA = 96
GATE_LORA = 256
GN_EPS = 64e-5
ATT_HEAD = 128
ATT_KV_HEADS = 4
ATT_GROUPS = 4
GRID_W = 64
ROPE_BASE = 10000.0
LN_EPS = 1e-5
DEPTH = 1
DEEPNORM_ALPHA = (2 * DEPTH) ** 0.25
NEG_INF = -1e30

LANES = 128
SUBLANES = 8
VMEM_LIMIT = 56 * 1024 * 1024
MOD_ROWS = 8
SCAN_CHUNK = 64
SCAN_LANES = 256
SCAN_PLANES = 5


def _cparams(sem):
    return pltpu.CompilerParams(dimension_semantics=sem, vmem_limit_bytes=VMEM_LIMIT)


def _dot(a, b):
    return jnp.dot(a, b, preferred_element_type=F32)


def _dot_nt(a, b):
    return lax.dot_general(a, b, (((1,), (1,)), ((), ())), preferred_element_type=F32)


def _dot_tn(a, b):
    return lax.dot_general(a, b, (((0,), (0,)), ((), ())), preferred_element_type=F32)


def _split2(x):
    hi = x.astype(BF16)
    return hi, (x - hi.astype(F32)).astype(BF16)


def _head_sum(x, ones_bd):
    rows = x.shape[0]
    both = jnp.concatenate(_split2(x), axis=0)
    out = []
    for s in range(x.shape[1] // SCAN_LANES):
        part = _dot(both[:, s * SCAN_LANES:(s + 1) * SCAN_LANES], ones_bd)
        out.append(part[:rows] + part[rows:])
    return jnp.concatenate(out, axis=1)


def _ada_kernel(c_ref, w_ref, b_ref, o_ref):
    a = c_ref[...]
    a = a * jax.nn.sigmoid(a)
    a_hi, a_lo = _split2(a)
    w_hi, w_lo = _split2(w_ref[...])
    part = _dot(jnp.concatenate([a_hi, a_lo], axis=0), w_hi)
    o_ref[...] = part[:MOD_ROWS] + part[MOD_ROWS:] + _dot(a_hi, w_lo) + b_ref[...]


def _ada(cc, w_ada, b_ada, tn):
    d, n = w_ada.shape
    return pl.pallas_call(
        _ada_kernel,
        grid=(n // tn,),
        in_specs=[pl.BlockSpec((MOD_ROWS, d), lambda j: (0, 0)),
                  pl.BlockSpec((d, tn), lambda j: (0, j)),
                  pl.BlockSpec((1, tn), lambda j: (0, j))],
        out_specs=pl.BlockSpec((MOD_ROWS, tn), lambda j: (0, j)),
        out_shape=jax.ShapeDtypeStruct((MOD_ROWS, n), F32),
        compiler_params=_cparams(("arbitrary",)),
        name="ada",
    )(cc, w_ada, b_ada.reshape(1, n))


def _prep_kernel(x_ref, ctx_ref, sh_ref, sc_ref, o_ref, *, n_lat, ctx_row):
    b = pl.program_id(0)
    j = pl.program_id(1)
    is_ctx = j >= n_lat
    row = jnp.where(is_ctx, ctx_row, b)
    sh = sh_ref[pl.ds(row, 1), :]
    sc = sc_ref[pl.ds(row, 1), :]
    xin = jnp.where(is_ctx, ctx_ref[0], x_ref[0])
    o_ref[0] = (xin * (1.0 + sc) + sh).astype(BF16)


def _prep(x, ctx, mod):
    bsz, t, d = x.shape
    l = ctx.shape[1]
    tb = l
    n_lat = t // tb
    return pl.pallas_call(
        functools.partial(_prep_kernel, n_lat=n_lat, ctx_row=bsz),
        grid=(bsz, n_lat + 1),
        in_specs=[pl.BlockSpec((1, tb, d), lambda b, j: (b, jnp.minimum(j, n_lat - 1), 0)),
                  pl.BlockSpec((1, l, d), lambda b, j: (b, 0, 0)),
                  pl.BlockSpec((MOD_ROWS, d), lambda b, j: (0, 0)),
                  pl.BlockSpec((MOD_ROWS, d), lambda b, j: (0, 1))],
        out_specs=pl.BlockSpec((1, tb, d), lambda b, j: (b, j, 0)),
        out_shape=jax.ShapeDtypeStruct((bsz, t + l, d), BF16),
        compiler_params=_cparams(("arbitrary", "arbitrary")),
        name="prep",
    )(x, ctx, mod, mod)


def _cast_kernel(w_ref, o_ref):
    o_ref[...] = w_ref[...].astype(o_ref.dtype)


def _stage_columns(w, segments, name):
    rows = w.shape[0]
    starts, src = [], []
    off = 0
    for start, width in segments:
        assert start % LANES == 0 and width % LANES == 0
        starts.append(off // LANES)
        src.append(start // LANES)
        off += width

    def src_block(j):
        blk = j - starts[0] + src[0]
        for s0, b0 in zip(starts[1:], src[1:]):
            blk = jnp.where(j >= s0, j - s0 + b0, blk)
        return blk

    return pl.pallas_call(
        _cast_kernel,
        grid=(off // LANES,),
        in_specs=[pl.BlockSpec((rows, LANES), lambda j: (0, src_block(j)))],
        out_specs=pl.BlockSpec((rows, LANES), lambda j: (0, j)),
        out_shape=jax.ShapeDtypeStruct((rows, off), BF16),
        compiler_params=_cparams(("arbitrary",)),
        name=name,
    )(w)


def _mm_kernel(a_ref, w_ref, *refs, side_blocks, strides, n_steps):
    n_side = len(side_blocks)
    o_ref = refs[n_side]
    if len(o_ref.shape) == 3:
        o_ref[0] = _dot(a_ref[0], w_ref[...]).astype(o_ref.dtype)
    else:
        o_ref[...] = _dot(a_ref[...], w_ref[...]).astype(o_ref.dtype)
    step = sum(pl.program_id(ax) * st for ax, st in enumerate(strides))
    for s_in, s_out, nblk in zip(refs[:n_side], refs[n_side + 1:], side_blocks):
        fresh = jnp.logical_or(step == 0, (step * nblk) // n_steps != ((step - 1) * nblk) // n_steps)

        @pl.when(fresh)
        def _(s_in=s_in, s_out=s_out):
            s_out[...] = s_in[...].astype(s_out.dtype)


def _side_specs(side, n_steps, step_of):
    in_specs, out_specs, out_shape = [], [], []
    for arr, rows in side:
        nblk = arr.shape[0] // rows
        assert arr.shape[0] % rows == 0 and nblk <= n_steps
        spec = pl.BlockSpec((rows, arr.shape[1]), lambda *g, nblk=nblk: ((step_of(*g) * nblk) // n_steps, 0))
        in_specs.append(spec)
        out_specs.append(spec)
        out_shape.append(jax.ShapeDtypeStruct(arr.shape, BF16))
    return in_specs, out_specs, out_shape


def _mm_wstat(a, w, tm, tn, out_dtype, name, side=()):
    m, k = a.shape
    n = w.shape[1]
    nj, ni = pl.cdiv(n, tn), m // tm
    s_in, s_out, s_shape = _side_specs(side, nj * ni, lambda j, i: j * ni + i)
    return pl.pallas_call(
        functools.partial(_mm_kernel, side_blocks=tuple(arr.shape[0] // r for arr, r in side),
                          strides=(ni, 1), n_steps=nj * ni),
        grid=(nj, ni),
        in_specs=[pl.BlockSpec((tm, k), lambda j, i: (i, 0)),
                  pl.BlockSpec((k, tn), lambda j, i: (0, j))] + s_in,
        out_specs=[pl.BlockSpec((tm, tn), lambda j, i: (i, j))] + s_out,
        out_shape=[jax.ShapeDtypeStruct((m, n), out_dtype)] + s_shape,
        compiler_params=_cparams(("arbitrary", "arbitrary")),
        name=name,
    )(a, w, *[arr for arr, _ in side])


def _mm_lead_rows(a, w, rows, tm, tn, out_dtype, name, side=()):
    bsz, _, k = a.shape
    n = w.shape[1]
    nj, ni = n // tn, rows // tm
    s_in, s_out, s_shape = _side_specs(side, nj * bsz * ni, lambda j, b, i: (j * bsz + b) * ni + i)
    return pl.pallas_call(
        functools.partial(_mm_kernel, side_blocks=tuple(arr.shape[0] // r for arr, r in side),
                          strides=(bsz * ni, ni, 1), n_steps=nj * bsz * ni),
        grid=(nj, bsz, ni),
        in_specs=[pl.BlockSpec((1, tm, k), lambda j, b, i: (b, i, 0)),
                  pl.BlockSpec((k, tn), lambda j, b, i: (0, j))] + s_in,
        out_specs=[pl.BlockSpec((1, tm, tn), lambda j, b, i: (b, i, j))] + s_out,
        out_shape=[jax.ShapeDtypeStruct((bsz, rows, n), out_dtype)] + s_shape,
        compiler_params=_cparams(("arbitrary", "arbitrary", "arbitrary")),
        name=name,
    )(a, w, *[arr for arr, _ in side])


def _conv3(prev_ref, cur_ref, next_ref, w_ref, at_start, at_end):
    cur = cur_ref[0]
    rows = cur.shape[0]
    prev_row = jnp.where(at_start, 0.0, prev_ref[0][SUBLANES - 1:SUBLANES, :])
    next_row = jnp.where(at_end, 0.0, next_ref[0][0:1, :])
    up = jnp.concatenate([prev_row, cur[:rows - 1]], axis=0)
    down = jnp.concatenate([cur[1:], next_row], axis=0)
    w = w_ref[...]
    return up * w[0:1] + cur * w[1:2] + down * w[2:3]


def _feat_kernel(rp, rc, rn, kp, kc, kn, vp, vc, vn, lp, lc, ln_,
                 shr, shk, shv, shl, w0, wup, a0, aup, gup, kk_w, ka_w, rk_w, ones_ref, cm_ref,
                 g_o, bonus_o, ops_o, dec_o, *, n_lat, n_tot, chunk):
    j = pl.program_id(1)
    at_start = jnp.logical_or(j == 0, j == n_lat)
    at_end = jnp.logical_or(j == n_lat - 1, j == n_tot - 1)
    r = _conv3(rp, rc, rn, shr, at_start, at_end)
    k = _conv3(kp, kc, kn, shk, at_start, at_end)
    v = _conv3(vp, vc, vn, shv, at_start, at_end)
    lo = _conv3(lp, lc, ln_, shl, at_start, at_end)
    rows = r.shape[0]
    v_h = v.astype(BF16)
    gd = lo[:, 2 * DECAY_LORA + 2 * ICLR_LORA:2 * DECAY_LORA + 2 * ICLR_LORA + GATE_LORA]
    ones_bd = ones_ref[...]
    g_o[0] = _dot(jax.nn.sigmoid(gd).astype(BF16), gup[...])
    kk = k * kk_w[...]
    kk = kk * lax.rsqrt(jnp.maximum(_head_sum(kk * kk, ones_bd), 1e-24))
    k_sum = jnp.zeros_like(k)
    for d in range(2):
        wd = lo[:, d * DECAY_LORA:(d + 1) * DECAY_LORA]
        ad = lo[:, 2 * DECAY_LORA + d * ICLR_LORA:2 * DECAY_LORA + (d + 1) * ICLR_LORA]
        z = w0[d:d + 1, :] + _dot(jnp.tanh(wd).astype(BF16), wup[d])
        lw = jax.nn.sigmoid(z) * (-np.exp(-0.5))
        iclr = jax.nn.sigmoid(a0[d:d + 1, :] + _dot(ad.astype(BF16), aup[d]))
        k_dir = k * (1.0 + (iclr - 1.0) * ka_w[...])
        k_sum = k_sum + k_dir
        lw_hi, lw_lo = _split2(lw)
        cum = _dot(cm_ref[d], lw_hi) + _dot(cm_ref[d], lw_lo)
        p_inv = jnp.exp(-cum)
        planes = [(kk * jnp.exp(cum - lw)).astype(BF16), (r * jnp.exp(cum)).astype(BF16),
                  (k_dir * p_inv).astype(BF16), (kk * iclr * p_inv).astype(BF16), v_h]
        for hg in range(r.shape[1] // SCAN_LANES):
            for p, plane in enumerate(planes):
                dst = (hg * SCAN_PLANES + p) * SCAN_LANES
                ops_o[d, 0, :, dst:dst + SCAN_LANES] = plane[:, hg * SCAN_LANES:(hg + 1) * SCAN_LANES]
        for q in range(rows // chunk):
            last = (q + 1) * chunk - 1 if d == 0 else q * chunk
            dec_o[d, 0, q] = jnp.exp(cum[last:last + 1, :])
    bonus_o[0] = _head_sum(r * k_sum * rk_w[...], ones_bd) * v


def _features(proj, proj_lora, col, shift_rkv, shift_lora, w0, w_up, a0, a_up, g_up, k_k, k_a, r_k, t_lat, tb):
    bsz, tt, _ = proj.shape
    c = w0.shape[1]
    n_lora = proj_lora.shape[2]
    chunk = SCAN_CHUNK
    n_tot = tt // tb
    n_lat = t_lat // tb
    hb = tb // SUBLANES
    n_h = tt // SUBLANES
    li = np.arange(SCAN_LANES) // RWKV_HEAD
    ones_bd = jnp.asarray(li[:, None] == li[None, :], BF16)
    ti = np.arange(tb)
    same = (ti[:, None] // chunk) == (ti[None, :] // chunk)
    cmask = jnp.asarray(np.stack([same & (ti[None, :] <= ti[:, None]), same & (ti[None, :] >= ti[:, None])]), BF16)

    def main(cb, width):
        blk = cb * LANES // width
        return pl.BlockSpec((1, tb, width), lambda b, j: (b, j, blk))

    def prev(cb, width):
        blk = cb * LANES // width
        return pl.BlockSpec((1, SUBLANES, width), lambda b, j: (b, jnp.maximum(j * hb - 1, 0), blk))

    def nxt(cb, width):
        blk = cb * LANES // width
        return pl.BlockSpec((1, SUBLANES, width), lambda b, j: (b, jnp.minimum((j + 1) * hb, n_h - 1), blk))

    def const(shape):
        nd = len(shape)
        return pl.BlockSpec(shape, lambda b, j: (0,) * nd)

    in_specs = []
    for name in ("r", "k", "v"):
        in_specs += [prev(col[name], c), main(col[name], c), nxt(col[name], c)]
    in_specs += [prev(0, n_lora), main(0, n_lora), nxt(0, n_lora)]
    in_specs += [const((3, c)), const((3, c)), const((3, c)), const((3, n_lora)),
                 const((2, c)), const((2, DECAY_LORA, c)), const((2, c)), const((2, ICLR_LORA, c)),
                 const((GATE_LORA, c)), const((1, c)), const((1, c)), const((1, c)),
                 const((SCAN_LANES, SCAN_LANES)), const((2, tb, tb))]
    one = pl.BlockSpec((1, tb, c), lambda b, j: (b, j, 0))
    ops = pl.BlockSpec((2, 1, tb, SCAN_PLANES * c), lambda b, j: (0, b, j, 0))
    dec = pl.BlockSpec((2, 1, tb // chunk, 1, c), lambda b, j: (0, b, j, 0, 0))
    s1 = jax.ShapeDtypeStruct((bsz, tt, c), F32)
    sops = jax.ShapeDtypeStruct((2, bsz, tt, SCAN_PLANES * c), BF16)
    sdec = jax.ShapeDtypeStruct((2, bsz, tt // chunk, 1, c), F32)
    args = [proj] * 9 + [proj_lora] * 3 + [shift_rkv[:, :c], shift_rkv[:, c:2 * c], shift_rkv[:, 2 * c:], shift_lora,
                          w0, w_up.astype(BF16), a0, a_up.astype(BF16), g_up.astype(BF16),
                          k_k.reshape(1, c), k_a.reshape(1, c), r_k.reshape(1, c), ones_bd, cmask]
    return pl.pallas_call(
        functools.partial(_feat_kernel, n_lat=n_lat, n_tot=n_tot, chunk=chunk),
        grid=(bsz, n_tot),
        in_specs=in_specs,
        out_specs=[one, one, ops, dec],
        out_shape=[s1, s1, sops, sdec],
        compiler_params=_cparams(("arbitrary", "arbitrary")),
        name="rwkv_features",
    )(*args)


def _scan_kernel(gm_ref, bdm_ref, hm_ref, eye_ref, *refs, chunk, nsub):
    ops_refs = refs[0:4:2]
    dec_refs = refs[1:4:2]
    y_refs = refs[4:6]
    s_ref = refs[6]
    heads = SCAN_LANES // RWKV_HEAD

    @pl.when(pl.program_id(2) == 0)
    def _():
        s_ref[...] = jnp.zeros_like(s_ref)

    bdm = bdm_ref[...]
    bdm_h = bdm.astype(BF16)
    eye = eye_ref[...]
    hms = [hm_ref[h] for h in range(heads)]

    def rows_bd(m):
        return jnp.concatenate([m * hms[h] for h in range(heads)], axis=0)

    def blocks_bd(m):
        return jnp.concatenate([m] * heads, axis=0) * bdm_h

    steps = chunk.bit_length() - 2
    units = [(d, q) for d in range(2) for q in range(nsub)]
    n_u = len(units)

    def lanes(q):
        return slice(q * SCAN_LANES, (q + 1) * SCAN_LANES)

    def plane(d, q, p):
        return ops_refs[d][0, 0, :, lanes(q * SCAN_PLANES + p)]

    kt = [plane(d, q, 2) for d, q in units]
    bt = [plane(d, q, 3) for d, q in units]
    v = [plane(d, q, 4) for d, q in units]
    x = [jnp.concatenate([plane(d, q, 0), plane(d, q, 1)], axis=0) for d, q in units]
    g = [_dot_nt(x[i], jnp.concatenate([rows_bd(kt[i]), rows_bd(bt[i])], axis=0)) * gm_ref[units[i][0]]
         for i in range(n_u)]
    s_old = [s_ref[i] for i in range(n_u)]
    xs = [_dot_nt(x[i], s_old[i].astype(BF16)) for i in range(n_u)]
    gv = [_dot(g[i][:, :SCAN_LANES].astype(BF16), rows_bd(v[i])) for i in range(n_u)]
    rhs = [xs[i][:chunk] + gv[i][:chunk] for i in range(n_u)]
    t = [eye - g[i][:chunk, SCAN_LANES:] for i in range(n_u)]
    lm = [g[i][:chunk, SCAN_LANES:].astype(BF16) for i in range(n_u)]
    pw = [_dot(lm[i], blocks_bd(lm[i])).astype(BF16) for i in range(n_u)]
    for _ in range(steps - 1):
        both = [_dot(jnp.concatenate([t[i].astype(BF16), pw[i]], axis=0), blocks_bd(pw[i])) for i in range(n_u)]
        t = [t[i] + both[i][:chunk] for i in range(n_u)]
        pw = [both[i][chunk:].astype(BF16) for i in range(n_u)]
    t = [t[i] + _dot(t[i].astype(BF16), blocks_bd(pw[i])) for i in range(n_u)]
    u = [(-_dot(t[i].astype(BF16), rows_bd(rhs[i].astype(BF16)))).astype(BF16) for i in range(n_u)]
    for i, (d, q) in enumerate(units):
        y_refs[d][0, :, lanes(q)] = (xs[i][chunk:] + gv[i][chunk:]
                                     + _dot(g[i][chunk:, SCAN_LANES:].astype(BF16), rows_bd(u[i])))
    for i, (d, q) in enumerate(units):
        upd = _dot_tn(jnp.concatenate([v[i], u[i]], axis=0), jnp.concatenate([kt[i], bt[i]], axis=0))
        s_ref[i] = (s_old[i] + upd) * dec_refs[d][0, 0, 0, :, lanes(q)] * bdm


def _scan(ops, dec, t_lat, lane_block):
    _, bsz, tt, c = ops.shape
    c //= SCAN_PLANES
    chunk = SCAN_CHUNK
    heads = SCAN_LANES // RWKV_HEAD
    n_tot = tt // chunk
    n_lat = t_lat // chunk
    n_ctx = n_tot - n_lat
    nsub = lane_block // SCAN_LANES

    ti = np.arange(chunk)
    before = [ti[None, :] < ti[:, None], ti[None, :] > ti[:, None]]
    gmask = np.stack([np.concatenate([np.tile(before[d], (1, 2 * heads)),
                                      np.tile(before[d] | np.eye(chunk, dtype=bool), (1, 2 * heads))], axis=0)
                      for d in range(2)]).astype(np.float32)
    li = np.arange(SCAN_LANES) // RWKV_HEAD
    bdm = (li[:, None] == li[None, :]).astype(np.float32)
    hmask = jnp.asarray((li[None, None, :] == np.arange(heads)[:, None, None]), BF16)
    eye = np.tile(np.eye(chunk, dtype=np.float32), (1, heads))

    def chunk_index(d, s):
        return jnp.where(s < n_ctx, n_lat + s, s - n_ctx) if d == 0 else n_tot - 1 - s

    def const(shape):
        nd = len(shape)
        return pl.BlockSpec(shape, lambda b, h, s: (0,) * nd)

    in_specs = [const(gmask.shape), const(bdm.shape), const(hmask.shape), const(eye.shape)]
    args = [jnp.asarray(gmask), jnp.asarray(bdm), hmask, jnp.asarray(eye)]
    out_specs = []
    for d in range(2):
        in_specs += [pl.BlockSpec((1, 1, chunk, SCAN_PLANES * lane_block),
                                  lambda b, h, s, d=d: (d, b, chunk_index(d, s), h)),
                     pl.BlockSpec((1, 1, 1, 1, lane_block), lambda b, h, s, d=d: (d, b, chunk_index(d, s), 0, h))]
        args += [ops, dec]
        out_specs.append(pl.BlockSpec((1, chunk, lane_block), lambda b, h, s, d=d: (b, chunk_index(d, s), h)))
    ys = jax.ShapeDtypeStruct((bsz, tt, c), F32)
    return pl.pallas_call(
        functools.partial(_scan_kernel, chunk=chunk, nsub=nsub),
        grid=(bsz, c // lane_block, n_tot),
        in_specs=in_specs,
        out_specs=out_specs,
        out_shape=[ys, ys],
        scratch_shapes=[pltpu.VMEM((2 * nsub, SCAN_LANES, SCAN_LANES), F32)],
        compiler_params=_cparams(("arbitrary", "arbitrary", "arbitrary")),
        name="rwkv_scan",
    )(*args)


def _rout_kernel(yf, yb, bonus, g, gng, gnb, ones_ref, o_ref):
    ones_bd = ones_ref[...]
    inv_n = 1.0 / RWKV_HEAD
    y = yf[0] + yb[0]
    mu = _head_sum(y, ones_bd) * inv_n
    yc = y - mu
    var = _head_sum(yc * yc, ones_bd) * inv_n
    yn = yc * lax.rsqrt(var + GN_EPS) * gng[...] + gnb[...]
    o_ref[0] = ((yn + bonus[0]) * g[0]).astype(o_ref.dtype)


def _rwkv_out(yf, yb, bonus, g, gn_g, gn_b, t_lat, tb):
    bsz, _, c = g.shape
    li = np.arange(SCAN_LANES) // RWKV_HEAD
    ones_bd = jnp.asarray(li[:, None] == li[None, :], BF16)
    one = pl.BlockSpec((1, tb, c), lambda b, j: (b, j, 0))

    def const(shape):
        return pl.BlockSpec(shape, lambda b, j: (0, 0))

    return pl.pallas_call(
        _rout_kernel,
        grid=(bsz, t_lat // tb),
        in_specs=[one, one, one, one, const((1, c)), const((1, c)), const((SCAN_LANES, SCAN_LANES))],
        out_specs=one,
        out_shape=jax.ShapeDtypeStruct((bsz, t_lat, c), BF16),
        compiler_params=_cparams(("arbitrary", "arbitrary")),
        name="rwkv_out",
    )(yf, yb, bonus, g, gn_g.reshape(1, c), gn_b.reshape(1, c), ones_bd)


def _rope(x, c, s):
    return x * c + pltpu.roll(x, ATT_HEAD // 2, 1) * s


def _attn_kernel(sink_ref, q_ref, km_ref, k0_ref, kp_ref, vm_ref, v0_ref, vp_ref, kc_ref, vc_ref,
                 c0_ref, s0_ref, cm_ref, sm_ref, cp_ref, sp_ref, wb_ref, o_ref, *, n_blk):
    n = pl.program_id(1)
    blk = ATT_HEAD
    scale = ATT_HEAD ** -0.5
    kvh = range(ATT_KV_HEADS)
    c0, s0 = c0_ref[...], s0_ref[...]
    cm, sm = cm_ref[...], sm_ref[...]
    cp, sp = cp_ref[...], sp_ref[...]

    def head(ref, i):
        return ref[0, :, i * ATT_HEAD:(i + 1) * ATT_HEAD]

    qq = [jnp.concatenate([_rope(head(q_ref, g * ATT_GROUPS + h), c0, s0) for h in range(ATT_GROUPS)],
                          axis=0).astype(BF16) for g in kvh]
    kw = [jnp.concatenate([_rope(head(km_ref, g), cm, sm), _rope(head(k0_ref, g), c0, s0),
                           _rope(head(kp_ref, g), cp, sp)], axis=0).astype(BF16) for g in kvh]
    vw = [jnp.concatenate([head(vm_ref, g), head(v0_ref, g), head(vp_ref, g)], axis=0).astype(BF16) for g in kvh]
    kc = [head(kc_ref, g).astype(BF16) for g in kvh]
    vc = [head(vc_ref, g).astype(BF16) for g in kvh]
    jl = lax.broadcasted_iota(jnp.int32, (1, 3 * blk), 1)
    outside = jnp.logical_or(jnp.logical_and(jl < blk, n == 0), jnp.logical_and(jl >= 2 * blk, n == n_blk - 1))
    bias = wb_ref[...] + jnp.where(outside, NEG_INF, 0.0)
    s_w = [_dot_nt(qq[g], kw[g]) * scale + bias for g in kvh]
    s_c = [_dot_nt(qq[g], kc[g]) * scale for g in kvh]
    s_s = [jnp.concatenate([jnp.full((blk, 1), sink_ref[g * ATT_GROUPS + h], F32) for h in range(ATT_GROUPS)],
                           axis=0) for g in kvh]
    m = [jnp.maximum(jnp.maximum(jnp.max(s_w[g], axis=-1, keepdims=True),
                                 jnp.max(s_c[g], axis=-1, keepdims=True)), s_s[g]) for g in kvh]
    pb_w, pb_c, den = [], [], []
    for g in kvh:
        p_w = jnp.exp(s_w[g] - m[g])
        p_c = jnp.exp(s_c[g] - m[g])
        den.append(jnp.sum(p_w, axis=-1, keepdims=True) + jnp.sum(p_c, axis=-1, keepdims=True)
                   + jnp.exp(s_s[g] - m[g]))
        pb_w.append(p_w.astype(BF16))
        pb_c.append(p_c.astype(BF16))
    o = [(_dot(pb_w[g], vw[g]) + _dot(pb_c[g], vc[g])) / den[g] for g in kvh]
    o_ref[0] = jnp.concatenate([o[g][h * blk:(h + 1) * blk] for g in kvh for h in range(ATT_GROUPS)],
                               axis=1).astype(o_ref.dtype)


def _attention(proj, col, sink, cos2, sin2, t_lat):
    bsz, tt, _ = proj.shape
    blk = ATT_HEAD
    n_blk = t_lat // blk
    l = tt - t_lat
    qw = ATT_KV_HEADS * ATT_GROUPS * ATT_HEAD
    kw = ATT_KV_HEADS * ATT_HEAD
    q_blk = col["q"] * LANES // qw

    def kv(name, off):
        cb = col[name] * LANES // kw
        return pl.BlockSpec((1, blk, kw), lambda b, n: (b, jnp.clip(n + off, 0, n_blk - 1), cb))

    def ctx(name):
        cb = col[name] * LANES // kw
        return pl.BlockSpec((1, l, kw), lambda b, n: (b, t_lat // l, cb))

    def tab(off):
        return pl.BlockSpec((blk, ATT_HEAD), lambda b, n: (jnp.clip(n + off, 0, n_blk - 1), 0))

    in_specs = [pl.BlockSpec(memory_space=pltpu.SMEM),
                pl.BlockSpec((1, blk, qw), lambda b, n: (b, n, q_blk)),
                kv("ak", -1), kv("ak", 0), kv("ak", 1), kv("av", -1), kv("av", 0), kv("av", 1),
                ctx("ak"), ctx("av"), tab(0), tab(0), tab(-1), tab(-1), tab(1), tab(1),
                pl.BlockSpec((ATT_GROUPS * blk, 3 * blk), lambda b, n: (0, 0))]
    qi = np.arange(ATT_GROUPS * blk)[:, None] % blk
    kj = np.arange(3 * blk)[None, :] - blk
    band_bias = jnp.asarray(np.where(np.abs(qi - kj) <= blk, 0.0, NEG_INF), F32)
    return pl.pallas_call(
        functools.partial(_attn_kernel, n_blk=n_blk),
        grid=(bsz, n_blk),
        in_specs=in_specs,
        out_specs=pl.BlockSpec((1, blk, qw), lambda b, n: (b, n, 0)),
        out_shape=jax.ShapeDtypeStruct((bsz, t_lat, qw), BF16),
        compiler_params=_cparams(("arbitrary", "arbitrary")),
        name="window_attention",
    )(sink, proj, proj, proj, proj, proj, proj, proj, proj, proj, cos2, sin2, cos2, sin2, cos2, sin2, band_bias)


def _merge_kernel(o1_ref, o2_ref, w1_ref, w2_ref, g1_ref, g2_ref, o_ref):
    y1 = _dot(o1_ref[0], w1_ref[...])
    y2 = _dot(o2_ref[0], w2_ref[...])
    o_ref[0] = (jax.nn.sigmoid(g1_ref[0].astype(F32)) * y1
                + jax.nn.sigmoid(g2_ref[0].astype(F32)) * y2).astype(o_ref.dtype)


def _merge(o_rwkv, o_att, w1, w2, gates, tm, tn):
    bsz, t, c = o_rwkv.shape
    n = w1.shape[1]
    gr = 0
    ga = n // tn
    return pl.pallas_call(
        _merge_kernel,
        grid=(bsz, t // tm, n // tn),
        in_specs=[pl.BlockSpec((1, tm, c), lambda b, i, j: (b, i, 0)),
                  pl.BlockSpec((1, tm, c), lambda b, i, j: (b, i, 0)),
                  pl.BlockSpec((c, tn), lambda b, i, j: (0, j)),
                  pl.BlockSpec((c, tn), lambda b, i, j: (0, j)),
                  pl.BlockSpec((1, tm, tn), lambda b, i, j: (b, i, gr + j)),
                  pl.BlockSpec((1, tm, tn), lambda b, i, j: (b, i, ga + j))],
        out_specs=pl.BlockSpec((1, tm, tn), lambda b, i, j: (b, i, j)),
        out_shape=jax.ShapeDtypeStruct((bsz, t, n), BF16),
        compiler_params=_cparams(("arbitrary", "arbitrary", "arbitrary")),
        name="gated_merge",
    )(o_rwkv, o_att, w1, w2, gates, gates)


def _mm_resid_kernel(a_ref, w_ref, x_ref, gt_ref, o_ref, *, tiles_per_batch):
    bi = pl.program_id(0) // tiles_per_batch
    y = _dot(a_ref[...], w_ref[...])
    o_ref[...] = DEEPNORM_ALPHA * x_ref[...] + gt_ref[pl.ds(bi, 1), :] * y


def _mm_resid(a, w, x, mod, gate_col, rows_per_batch, tm, tn, name):
    m, k = a.shape
    n = w.shape[1]
    gcb = gate_col * (n // tn)
    return pl.pallas_call(
        functools.partial(_mm_resid_kernel, tiles_per_batch=rows_per_batch // tm),
        grid=(m // tm, n // tn),
        in_specs=[pl.BlockSpec((tm, k), lambda i, j: (i, 0)),
                  pl.BlockSpec((k, tn), lambda i, j: (0, j)),
                  pl.BlockSpec((tm, tn), lambda i, j: (i, j)),
                  pl.BlockSpec((MOD_ROWS, tn), lambda i, j: (0, gcb + j))],
        out_specs=pl.BlockSpec((tm, tn), lambda i, j: (i, j)),
        out_shape=jax.ShapeDtypeStruct((m, n), F32),
        compiler_params=_cparams(("arbitrary", "arbitrary")),
        name=name,
    )(a, w, x, mod)


def _ln_kernel(z_ref, g_ref, b_ref, *rest, with_mod):
    bi = pl.program_id(0)
    z = z_ref[0]
    mu = jnp.mean(z, axis=-1, keepdims=True)
    zc = z - mu
    var = jnp.mean(zc * zc, axis=-1, keepdims=True)
    out = zc * lax.rsqrt(var + LN_EPS) * g_ref[...] + b_ref[...]
    if with_mod:
        sh_ref, sc_ref, o_ref, h_ref = rest
        o_ref[0] = out
        h_ref[0] = (out * (1.0 + sc_ref[pl.ds(bi, 1), :]) + sh_ref[pl.ds(bi, 1), :]).astype(BF16)
    else:
        (o_ref,) = rest
        o_ref[0] = out


def _res_ln(z, mod, g, b, tb, mod_cols=None):
    bsz, t, d = z.shape
    blk = pl.BlockSpec((1, tb, d), lambda bi, j: (bi, j, 0))
    vec = pl.BlockSpec((1, d), lambda bi, j: (0, 0))

    def modspec(cb):
        return pl.BlockSpec((MOD_ROWS, d), lambda bi, j: (0, cb))

    in_specs = [blk, vec, vec]
    args = [z, g.reshape(1, d), b.reshape(1, d)]
    out_specs = [blk]
    out_shape = [jax.ShapeDtypeStruct((bsz, t, d), F32)]
    if mod_cols is not None:
        in_specs += [modspec(mod_cols[0]), modspec(mod_cols[1])]
        args += [mod, mod]
        out_specs.append(blk)
        out_shape.append(jax.ShapeDtypeStruct((bsz, t, d), BF16))
    return pl.pallas_call(
        functools.partial(_ln_kernel, with_mod=mod_cols is not None),
        grid=(bsz, t // tb),
        in_specs=in_specs,
        out_specs=out_specs,
        out_shape=out_shape,
        compiler_params=_cparams(("arbitrary", "arbitrary")),
        name="residual_layernorm",
    )(*args)


def _ffn_up_kernel(a_ref, wg_ref, wu_ref, o_ref):
    a = a_ref[...]
    gg = _dot(a, wg_ref[...].astype(BF16))
    uu = _dot(a, wu_ref[...].astype(BF16))
    o_ref[...] = (gg * jax.nn.sigmoid(gg) * uu).astype(o_ref.dtype)


def _ffn_up(a, wg, wu, tm, tf):
    m, k = a.shape
    f = wg.shape[1]
    return pl.pallas_call(
        _ffn_up_kernel,
        grid=(m // tm, f // tf),
        in_specs=[pl.BlockSpec((tm, k), lambda i, j: (i, 0)),
                  pl.BlockSpec((k, tf), lambda i, j: (0, j)),
                  pl.BlockSpec((k, tf), lambda i, j: (0, j))],
        out_specs=pl.BlockSpec((tm, tf), lambda i, j: (i, j)),
        out_shape=jax.ShapeDtypeStruct((m, f), BF16),
        compiler_params=_cparams(("arbitrary", "arbitrary")),
        name="swiglu_up",
    )(a, wg, wu)


def _rope_tables(t):
    rows = t // GRID_W
    row = jnp.broadcast_to(jnp.arange(rows, dtype=F32)[:, None], (rows, GRID_W)).reshape(t)
    colp = jnp.broadcast_to(jnp.arange(GRID_W, dtype=F32)[None, :], (rows, GRID_W)).reshape(t)
    axis_dim = ATT_HEAD // 2
    inv = ROPE_BASE ** (-jnp.arange(0, axis_dim, 2, dtype=F32) / axis_dim)
    ang = jnp.concatenate([row[:, None] * inv, colp[:, None] * inv], -1)
    cos, sin = jnp.cos(ang), jnp.sin(ang)
    return jnp.concatenate([cos, cos], -1), jnp.concatenate([-sin, sin], -1)


def _block(x, c, ctx, c_ctx, w_ada, b_ada, w_in, rwkv_shift, rwkv_w0, rwkv_w_up, rwkv_a0, rwkv_a_up,
           rwkv_g_up, rwkv_k_k, rwkv_k_a, rwkv_r_k, rwkv_gn_g, rwkv_gn_b, attn_sink, w_rwkv_o, w_att_o,
           w_out, ln1_g, ln1_b, w_ff_gate, w_ff_up, w_ff_down, ln2_g, ln2_b, tiles):
    bsz, t, d = x.shape
    l = ctx.shape[1]
    tt = t + l
    crw = rwkv_w0.shape[1]
    d_att = ATT_KV_HEADS * ATT_GROUPS * ATT_HEAD
    d_kv = ATT_KV_HEADS * ATT_HEAD
    n_lora = 2 * DECAY_LORA + 2 * ICLR_LORA + GATE_LORA
    n_rw = 3 * crw + n_lora

    assert bsz + 1 <= MOD_ROWS and l == tiles["prep_tb"] and t % tiles["prep_tb"] == 0
    cc = jnp.zeros((MOD_ROWS, d), F32).at[:bsz].set(c).at[bsz].set(c_ctx)
    mod = _ada(cc, w_ada, b_ada, tiles["ada_tn"])

    o_q = n_rw
    o_g = o_q + d_att + 2 * d_kv
    w_main = _stage_columns(w_in, [(0, 3 * crw), (o_q, o_g - o_q)], "stage_w_main")
    w_lora = _stage_columns(w_in, [(3 * crw, n_lora)], "stage_w_lora")
    w_gates = _stage_columns(w_in, [(o_g, w_in.shape[1] - o_g)], "stage_w_gates")
    col = {}
    off = 0
    for name, width in (("r", crw), ("k", crw), ("v", crw), ("q", d_att), ("ak", d_kv), ("av", d_kv)):
        col[name] = off // LANES
        off += width
    shift_rkv = rwkv_shift[:, :3 * crw]
    shift_lora = rwkv_shift[:, 3 * crw:]

    h = _prep(x, ctx, mod)
    h2d = h.reshape(bsz * tt, d)
    (proj,) = _mm_wstat(h2d, w_main, tiles["proj_tm"], tiles["proj_tn"], F32, "in_proj")
    proj = proj.reshape(bsz, tt, -1)
    proj_lora, w_out16, w_ro16, w_ao16 = _mm_wstat(
        h2d, w_lora, tiles["proj_tm"], n_lora, F32, "lora_proj",
        side=[(w_out, tiles["side_rows"]), (w_rwkv_o, tiles["side_rows"]), (w_att_o, tiles["side_rows"])])
    proj_lora = proj_lora.reshape(bsz, tt, -1)
    gates, w_down16 = _mm_lead_rows(h, w_gates, t, tiles["gate_tm"], tiles["gate_tn"], BF16, "gate_proj",
                                    side=[(w_ff_down, tiles["side_rows_down"])])

    g, bonus, ops, dec = _features(proj, proj_lora, col, shift_rkv, shift_lora, rwkv_w0, rwkv_w_up, rwkv_a0,
                                   rwkv_a_up, rwkv_g_up, rwkv_k_k, rwkv_k_a, rwkv_r_k.reshape(-1), t,
                                   tiles["feat_tb"])
    yf, yb = _scan(ops, dec, t, tiles["scan_lanes"])
    o_rwkv = _rwkv_out(yf, yb, bonus, g, rwkv_gn_g, rwkv_gn_b, t, tiles["rout_tb"])

    cos2, sin2 = _rope_tables(t)
    o_att = _attention(proj, col, attn_sink, cos2, sin2, t)

    ym = _merge(o_rwkv, o_att, w_ro16, w_ao16, gates, tiles["merge_tm"], tiles["merge_tn"])
    z1 = _mm_resid(ym.reshape(bsz * t, d), w_out16, x.reshape(bsz * t, d), mod, 2, t,
                   tiles["out_tm"], tiles["out_tn"], "out_proj")
    x1, h2 = _res_ln(z1.reshape(bsz, t, d), mod, ln1_g, ln1_b, tiles["ln_tb"], mod_cols=(3, 4))

    u = _ffn_up(h2.reshape(bsz * t, d), w_ff_gate, w_ff_up, tiles["ffn_tm"], tiles["ffn_tf"])
    z2 = _mm_resid(u, w_down16, x1.reshape(bsz * t, d), mod, 5, t,
                   tiles["down_tm"], tiles["down_tn"], "swiglu_down")
    (out,) = _res_ln(z2.reshape(bsz, t, d), mod, ln2_g, ln2_b, tiles["ln_tb"])
    return out


_TILES = dict(ada_tn=512, prep_tb=256, proj_tm=512, proj_tn=1536, gate_tm=1024, gate_tn=1024,
              side_rows=128, side_rows_down=256, feat_tb=128, scan_lanes=2048, rout_tb=256,
              merge_tm=1024, merge_tn=1024, out_tm=1024, out_tn=1024, ln_tb=512,
              ffn_tm=2048, ffn_tf=256, down_tm=512, down_tn=512)


def kernel(x, c, ctx, c_ctx, w_ada, b_ada, w_in, rwkv_shift, rwkv_w0, rwkv_w_up, rwkv_a0, rwkv_a_up, rwkv_g_up, rwkv_k_k, rwkv_k_a, rwkv_r_k, rwkv_gn_g, rwkv_gn_b, attn_sink, w_rwkv_o, w_att_o, w_out, ln1_g, ln1_b, w_ff_gate, w_ff_up, w_ff_down, ln2_g, ln2_b):
    assert w_ada.shape[0] == DEPTH
    return _block(x, c, ctx, c_ctx, w_ada[0], b_ada[0], w_in[0], rwkv_shift[0], rwkv_w0[0], rwkv_w_up[0],
                  rwkv_a0[0], rwkv_a_up[0], rwkv_g_up[0], rwkv_k_k[0], rwkv_k_a[0], rwkv_r_k[0], rwkv_gn_g[0],
                  rwkv_gn_b[0], attn_sink[0], w_rwkv_o[0], w_att_o[0], w_out[0], ln1_g[0], ln1_b[0],
                  w_ff_gate[0], w_ff_up[0], w_ff_down[0], ln2_g[0], ln2_b[0], _TILES)
```

```python
import functools

import jax
import jax.numpy as jnp
import numpy as np
from jax import lax
from jax.experimental import pallas as pl
from jax.experimental.pallas import tpu as pltpu

F32 = jnp.float32
BF16 = jnp.bfloat16

RWKV_HEAD = 64
DECAY_LORA = 96
ICLR_LORA = 96
GATE_LORA = 256
GN_EPS = 64e-5
ATT_HEAD = 128
ATT_KV_HEADS = 4
ATT_GROUPS = 4
GRID_W = 64
ROPE_BASE = 10000.0
LN_EPS = 1e-5
DEPTH = 1
DEEPNORM_ALPHA = (2 * DEPTH) ** 0.25
NEG_INF = -1e30

LANES = 128
SUBLANES = 8
VMEM_LIMIT = 56 * 1024 * 1024
MOD_ROWS = 8
SCAN_CHUNK = 64
SCAN_LANES = 256
SCAN_PLANES = 5


def _cparams(sem):
    return pltpu.CompilerParams(dimension_semantics=sem, vmem_limit_bytes=VMEM_LIMIT)


def _dot(a, b):
    return jnp.dot(a, b, preferred_element_type=F32)


def _dot_nt(a, b):
    return lax.dot_general(a, b, (((1,), (1,)), ((), ())), preferred_element_type=F32)


def _dot_tn(a, b):
    return lax.dot_general(a, b, (((0,), (0,)), ((), ())), preferred_element_type=F32)


def _split2(x):
    hi = x.astype(BF16)
    return hi, (x - hi.astype(F32)).astype(BF16)


def _head_sum(x, ones_bd):
    rows = x.shape[0]
    both = jnp.concatenate(_split2(x), axis=0)
    out = []
    for s in range(x.shape[1] // SCAN_LANES):
        part = _dot(both[:, s * SCAN_LANES:(s + 1) * SCAN_LANES], ones_bd)
        out.append(part[:rows] + part[rows:])
    return jnp.concatenate(out, axis=1)


def _ada_kernel(c_ref, w_ref, b_ref, o_ref):
    a = c_ref[...]
    a = a * jax.nn.sigmoid(a)
    a_hi, a_lo = _split2(a)
    w_hi, w_lo = _split2(w_ref[...])
    part = _dot(jnp.concatenate([a_hi, a_lo], axis=0), w_hi)
    o_ref[...] = part[:MOD_ROWS] + part[MOD_ROWS:] + _dot(a_hi, w_lo) + b_ref[...]


def _ada(cc, w_ada, b_ada, tn):
    d, n = w_ada.shape
    return pl.pallas_call(
        _ada_kernel,
        grid=(n // tn,),
        in_specs=[pl.BlockSpec((MOD_ROWS, d), lambda j: (0, 0)),
                  pl.BlockSpec((d, tn), lambda j: (0, j)),
                  pl.BlockSpec((1, tn), lambda j: (0, j))],
        out_specs=pl.BlockSpec((MOD_ROWS, tn), lambda j: (0, j)),
        out_shape=jax.ShapeDtypeStruct((MOD_ROWS, n), F32),
        compiler_params=_cparams(("arbitrary",)),
        name="ada",
    )(cc, w_ada, b_ada.reshape(1, n))


def _prep_kernel(x_ref, ctx_ref, sh_ref, sc_ref, o_ref, *, n_lat, ctx_row):
    b = pl.program_id(0)
    j = pl.program_id(1)
    is_ctx = j >= n_lat
    row = jnp.where(is_ctx, ctx_row, b)
    sh = sh_ref[pl.ds(row, 1), :]
    sc = sc_ref[pl.ds(row, 1), :]
    xin = jnp.where(is_ctx, ctx_ref[0], x_ref[0])
    o_ref[0] = (xin * (1.0 + sc) + sh).astype(BF16)


def _prep(x, ctx, mod):
    bsz, t, d = x.shape
    l = ctx.shape[1]
    tb = l
    n_lat = t // tb
    return pl.pallas_call(
        functools.partial(_prep_kernel, n_lat=n_lat, ctx_row=bsz),
        grid=(bsz, n_lat + 1),
        in_specs=[pl.BlockSpec((1, tb, d), lambda b, j: (b, jnp.minimum(j, n_lat - 1), 0)),
                  pl.BlockSpec((1, l, d), lambda b, j: (b, 0, 0)),
                  pl.BlockSpec((MOD_ROWS, d), lambda b, j: (0, 0)),
                  pl.BlockSpec((MOD_ROWS, d), lambda b, j: (0, 1))],
        out_specs=pl.BlockSpec((1, tb, d), lambda b, j: (b, j, 0)),
        out_shape=jax.ShapeDtypeStruct((bsz, t + l, d), BF16),
        compiler_params=_cparams(("arbitrary", "arbitrary")),
        name="prep",
    )(x, ctx, mod, mod)


def _cast_kernel(w_ref, o_ref):
    o_ref[...] = w_ref[...].astype(o_ref.dtype)


def _stage_columns(w, segments, name):
    rows = w.shape[0]
    starts, src = [], []
    off = 0
    for start, width in segments:
        assert start % LANES == 0 and width % LANES == 0
        starts.append(off // LANES)
        src.append(start // LANES)
        off += width

    def src_block(j):
        blk = j - starts[0] + src[0]
        for s0, b0 in zip(starts[1:], src[1:]):
            blk = jnp.where(j >= s0, j - s0 + b0, blk)
        return blk

    return pl.pallas_call(
        _cast_kernel,
        grid=(off // LANES,),
        in_specs=[pl.BlockSpec((rows, LANES), lambda j: (0, src_block(j)))],
        out_specs=pl.BlockSpec((rows, LANES), lambda j: (0, j)),
        out_shape=jax.ShapeDtypeStruct((rows, off), BF16),
        compiler_params=_cparams(("arbitrary",)),
        name=name,
    )(w)


def _mm_kernel(a_ref, w_ref, *refs, side_blocks, strides, n_steps):
    n_side = len(side_blocks)
    o_ref = refs[n_side]
    if len(o_ref.shape) == 3:
        o_ref[0] = _dot(a_ref[0], w_ref[...]).astype(o_ref.dtype)
    else:
        o_ref[...] = _dot(a_ref[...], w_ref[...]).astype(o_ref.dtype)
    step = sum(pl.program_id(ax) * st for ax, st in enumerate(strides))
    for s_in, s_out, nblk in zip(refs[:n_side], refs[n_side + 1:], side_blocks):
        fresh = jnp.logical_or(step == 0, (step * nblk) // n_steps != ((step - 1) * nblk) // n_steps)

        @pl.when(fresh)
        def _(s_in=s_in, s_out=s_out):
            s_out[...] = s_in[...].astype(s_out.dtype)


def _side_specs(side, n_steps, step_of):
    in_specs, out_specs, out_shape = [], [], []
    for arr, rows in side:
        nblk = arr.shape[0] // rows
        assert arr.shape[0] % rows == 0 and nblk <= n_steps
        spec = pl.BlockSpec((rows, arr.shape[1]), lambda *g, nblk=nblk: ((step_of(*g) * nblk) // n_steps, 0))
        in_specs.append(spec)
        out_specs.append(spec)
        out_shape.append(jax.ShapeDtypeStruct(arr.shape, BF16))
    return in_specs, out_specs, out_shape


def _mm_wstat(a, w, tm, tn, out_dtype, name, side=()):
    m, k = a.shape
    n = w.shape[1]
    nj, ni = pl.cdiv(n, tn), m // tm
    s_in, s_out, s_shape = _side_specs(side, nj * ni, lambda j, i: j * ni + i)
    return pl.pallas_call(
        functools.partial(_mm_kernel, side_blocks=tuple(arr.shape[0] // r for arr, r in side),
                          strides=(ni, 1), n_steps=nj * ni),
        grid=(nj, ni),
        in_specs=[pl.BlockSpec((tm, k), lambda j, i: (i, 0)),
                  pl.BlockSpec((k, tn), lambda j, i: (0, j))] + s_in,
        out_specs=[pl.BlockSpec((tm, tn), lambda j, i: (i, j))] + s_out,
        out_shape=[jax.ShapeDtypeStruct((m, n), out_dtype)] + s_shape,
        compiler_params=_cparams(("arbitrary", "arbitrary")),
        name=name,
    )(a, w, *[arr for arr, _ in side])


def _mm_lead_rows(a, w, rows, tm, tn, out_dtype, name, side=()):
    bsz, _, k = a.shape
    n = w.shape[1]
    nj, ni = n // tn, rows // tm
    s_in, s_out, s_shape = _side_specs(side, nj * bsz * ni, lambda j, b, i: (j * bsz + b) * ni + i)
    return pl.pallas_call(
        functools.partial(_mm_kernel, side_blocks=tuple(arr.shape[0] // r for arr, r in side),
                          strides=(bsz * ni, ni, 1), n_steps=nj * bsz * ni),
        grid=(nj, bsz, ni),
        in_specs=[pl.BlockSpec((1, tm, k), lambda j, b, i: (b, i, 0)),
                  pl.BlockSpec((k, tn), lambda j, b, i: (0, j))] + s_in,
        out_specs=[pl.BlockSpec((1, tm, tn), lambda j, b, i: (b, i, j))] + s_out,
        out_shape=[jax.ShapeDtypeStruct((bsz, rows, n), out_dtype)] + s_shape,
        compiler_params=_cparams(("arbitrary", "arbitrary", "arbitrary")),
        name=name,
    )(a, w, *[arr for arr, _ in side])


def _conv3(prev_ref, cur_ref, next_ref, w_ref, at_start, at_end):
    cur = cur_ref[0]
    rows = cur.shape[0]
    prev_row = jnp.where(at_start, 0.0, prev_ref[0][SUBLANES - 1:SUBLANES, :])
    next_row = jnp.where(at_end, 0.0, next_ref[0][0:1, :])
    up = jnp.concatenate([prev_row, cur[:rows - 1]], axis=0)
    down = jnp.concatenate([cur[1:], next_row], axis=0)
    w = w_ref[...]
    return up * w[0:1] + cur * w[1:2] + down * w[2:3]


def _feat_kernel(rp, rc, rn, kp, kc, kn, vp, vc, vn, lp, lc, ln_,
                 shr, shk, shv, shl, w0, wup, a0, aup, gup, kk_w, ka_w, rk_w, ones_ref, cm_ref,
                 g_o, bonus_o, ops_o, dec_o, *, n_lat, n_tot, chunk):
    j = pl.program_id(1)
    at_start = jnp.logical_or(j == 0, j == n_lat)
    at_end = jnp.logical_or(j == n_lat - 1, j == n_tot - 1)
    r = _conv3(rp, rc, rn, shr, at_start, at_end)
    k = _conv3(kp, kc, kn, shk, at_start, at_end)
    v = _conv3(vp, vc, vn, shv, at_start, at_end)
    lo = _conv3(lp, lc, ln_, shl, at_start, at_end)
    rows = r.shape[0]
    v_h = v.astype(BF16)
    gd = lo[:, 2 * DECAY_LORA + 2 * ICLR_LORA:2 * DECAY_LORA + 2 * ICLR_LORA + GATE_LORA]
    ones_bd = ones_ref[...]
    g_o[0] = _dot(jax.nn.sigmoid(gd).astype(BF16), gup[...])
    kk = k * kk_w[...]
    kk = kk * lax.rsqrt(jnp.maximum(_head_sum(kk * kk, ones_bd), 1e-24))
    k_sum = jnp.zeros_like(k)
    for d in range(2):
        wd = lo[:, d * DECAY_LORA:(d + 1) * DECAY_LORA]
        ad = lo[:, 2 * DECAY_LORA + d * ICLR_LORA:2 * DECAY_LORA + (d + 1) * ICLR_LORA]
        z = w0[d:d + 1, :] + _dot(jnp.tanh(wd).astype(BF16), wup[d])
        lw = jax.nn.sigmoid(z) * (-np.exp(-0.5))
        iclr = jax.nn.sigmoid(a0[d:d + 1, :] + _dot(ad.astype(BF16), aup[d]))
        k_dir = k * (1.0 + (iclr - 1.0) * ka_w[...])
        k_sum = k_sum + k_dir
        lw_hi, lw_lo = _split2(lw)
        cum = _dot(cm_ref[d], lw_hi) + _dot(cm_ref[d], lw_lo)
        p_inv = jnp.exp(-cum)
        planes = [(kk * jnp.exp(cum - lw)).astype(BF16), (r * jnp.exp(cum)).astype(BF16),
                  (k_dir * p_inv).astype(BF16), (kk * iclr * p_inv).astype(BF16), v_h]
        for hg in range(r.shape[1] // SCAN_LANES):
            for p, plane in enumerate(planes):
                dst = (hg * SCAN_PLANES + p) * SCAN_LANES
                ops_o[d, 0, :, dst:dst + SCAN_LANES] = plane[:, hg * SCAN_LANES:(hg + 1) * SCAN_LANES]
        for q in range(rows // chunk):
            last = (q + 1) * chunk - 1 if d == 0 else q * chunk
            dec_o[d, 0, q] = jnp.exp(cum[last:last + 1, :])
    bonus_o[0] = _head_sum(r * k_sum * rk_w[...], ones_bd) * v


def _features(proj, proj_lora, col, shift_rkv, shift_lora, w0, w_up, a0, a_up, g_up, k_k, k_a, r_k, t_lat, tb):
    bsz, tt, _ = proj.shape
    c = w0.shape[1]
    n_lora = proj_lora.shape[2]
    chunk = SCAN_CHUNK
    n_tot = tt // tb
    n_lat = t_lat // tb
    hb = tb // SUBLANES
    n_h = tt // SUBLANES
    li = np.arange(SCAN_LANES) // RWKV_HEAD
    ones_bd = jnp.asarray(li[:, None] == li[None, :], BF16)
    ti = np.arange(tb)
    same = (ti[:, None] // chunk) == (ti[None, :] // chunk)
    cmask = jnp.asarray(np.stack([same & (ti[None, :] <= ti[:, None]), same & (ti[None, :] >= ti[:, None])]), BF16)

    def main(cb, width):
        blk = cb * LANES // width
        return pl.BlockSpec((1, tb, width), lambda b, j: (b, j, blk))

    def prev(cb, width):
        blk = cb * LANES // width
        return pl.BlockSpec((1, SUBLANES, width), lambda b, j: (b, jnp.maximum(j * hb - 1, 0), blk))

    def nxt(cb, width):
        blk = cb * LANES // width
        return pl.BlockSpec((1, SUBLANES, width), lambda b, j: (b, jnp.minimum((j + 1) * hb, n_h - 1), blk))

    def const(shape):
        nd = len(shape)
        return pl.BlockSpec(shape, lambda b, j: (0,) * nd)

    in_specs = []
    for name in ("r", "k", "v"):
        in_specs += [prev(col[name], c), main(col[name], c), nxt(col[name], c)]
    in_specs += [prev(0, n_lora), main(0, n_lora), nxt(0, n_lora)]
    in_specs += [const((3, c)), const((3, c)), const((3, c)), const((3, n_lora)),
                 const((2, c)), const((2, DECAY_LORA, c)), const((2, c)), const((2, ICLR_LORA, c)),
                 const((GATE_LORA, c)), const((1, c)), const((1, c)), const((1, c)),
                 const((SCAN_LANES, SCAN_LANES)), const((2, tb, tb))]
    one = pl.BlockSpec((1, tb, c), lambda b, j: (b, j, 0))
    ops = pl.BlockSpec((2, 1, tb, SCAN_PLANES * c), lambda b, j: (0, b, j, 0))
    dec = pl.BlockSpec((2, 1, tb // chunk, 1, c), lambda b, j: (0, b, j, 0, 0))
    s1 = jax.ShapeDtypeStruct((bsz, tt, c), F32)
    sops = jax.ShapeDtypeStruct((2, bsz, tt, SCAN_PLANES * c), BF16)
    sdec = jax.ShapeDtypeStruct((2, bsz, tt // chunk, 1, c), F32)
    args = [proj] * 9 + [proj_lora] * 3 + [shift_rkv[:, :c], shift_rkv[:, c:2 * c], shift_rkv[:, 2 * c:], shift_lora,
                          w0, w_up.astype(BF16), a0, a_up.astype(BF16), g_up.astype(BF16),
                          k_k.reshape(1, c), k_a.reshape(1, c), r_k.reshape(1, c), ones_bd, cmask]
    return pl.pallas_call(
        functools.partial(_feat_kernel, n_lat=n_lat, n_tot=n_tot, chunk=chunk),
        grid=(bsz, n_tot),
        in_specs=in_specs,
        out_specs=[one, one, ops, dec],
        out_shape=[s1, s1, sops, sdec],
        compiler_params=_cparams(("arbitrary", "arbitrary")),
        name="rwkv_features",
    )(*args)


def _scan_kernel(gm_ref, bdm_ref, hm_ref, eye_ref, *refs, chunk, nsub):
    ops_refs = refs[0:4:2]
    dec_refs = refs[1:4:2]
    y_refs = refs[4:6]
    s_ref = refs[6]
    heads = SCAN_LANES // RWKV_HEAD

    @pl.when(pl.program_id(2) == 0)
    def _():
        s_ref[...] = jnp.zeros_like(s_ref)

    bdm = bdm_ref[...]
    bdm_h = bdm.astype(BF16)
    eye = eye_ref[...]
    hms = [hm_ref[h] for h in range(heads)]

    def rows_bd(m):
        return jnp.concatenate([m * hms[h] for h in range(heads)], axis=0)

    def blocks_bd(m):
        return jnp.concatenate([m] * heads, axis=0) * bdm_h

    steps = chunk.bit_length() - 2
    units = [(d, q) for d in range(2) for q in range(nsub)]
    n_u = len(units)

    def lanes(q):
        return slice(q * SCAN_LANES, (q + 1) * SCAN_LANES)

    def plane(d, q, p):
        return ops_refs[d][0, 0, :, lanes(q * SCAN_PLANES + p)]

    kt = [plane(d, q, 2) for d, q in units]
    bt = [plane(d, q, 3) for d, q in units]
    v = [plane(d, q, 4) for d, q in units]
    x = [jnp.concatenate([plane(d, q, 0), plane(d, q, 1)], axis=0) for d, q in units]
    g = [_dot_nt(x[i], jnp.concatenate([rows_bd(kt[i]), rows_bd(bt[i])], axis=0)) * gm_ref[units[i][0]]
         for i in range(n_u)]
    s_old = [s_ref[i] for i in range(n_u)]
    xs = [_dot_nt(x[i], s_old[i].astype(BF16)) for i in range(n_u)]
    gv = [_dot(g[i][:, :SCAN_LANES].astype(BF16), rows_bd(v[i])) for i in range(n_u)]
    rhs = [xs[i][:chunk] + gv[i][:chunk] for i in range(n_u)]
    t = [eye - g[i][:chunk, SCAN_LANES:] for i in range(n_u)]
    lm = [g[i][:chunk, SCAN_LANES:].astype(BF16) for i in range(n_u)]
    pw = [_dot(lm[i], blocks_bd(lm[i])).astype(BF16) for i in range(n_u)]
    for _ in range(steps - 1):
        both = [_dot(jnp.concatenate([t[i].astype(BF16), pw[i]], axis=0), blocks_bd(pw[i])) for i in range(n_u)]
        t = [t[i] + both[i][:chunk] for i in range(n_u)]
        pw = [both[i][chunk:].astype(BF16) for i in range(n_u)]
    t = [t[i] + _dot(t[i].astype(BF16), blocks_bd(pw[i])) for i in range(n_u)]
    u = [(-_dot(t[i].astype(BF16), rows_bd(rhs[i].astype(BF16)))).astype(BF16) for i in range(n_u)]
    for i, (d, q) in enumerate(units):
        y_refs[d][0, :, lanes(q)] = (xs[i][chunk:] + gv[i][chunk:]
                                     + _dot(g[i][chunk:, SCAN_LANES:].astype(BF16), rows_bd(u[i])))
    for i, (d, q) in enumerate(units):
        upd = _dot_tn(jnp.concatenate([v[i], u[i]], axis=0), jnp.concatenate([kt[i], bt[i]], axis=0))
        s_ref[i] = (s_old[i] + upd) * dec_refs[d][0, 0, 0, :, lanes(q)] * bdm


def _scan(ops, dec, t_lat, lane_block):
    _, bsz, tt, c = ops.shape
    c //= SCAN_PLANES
    chunk = SCAN_CHUNK
    heads = SCAN_LANES // RWKV_HEAD
    n_tot = tt // chunk
    n_lat = t_lat // chunk
    n_ctx = n_tot - n_lat
    nsub = lane_block // SCAN_LANES

    ti = np.arange(chunk)
    before = [ti[None, :] < ti[:, None], ti[None, :] > ti[:, None]]
    gmask = np.stack([np.concatenate([np.tile(before[d], (1, 2 * heads)),
                                      np.tile(before[d] | np.eye(chunk, dtype=bool), (1, 2 * heads))], axis=0)
                      for d in range(2)]).astype(np.float32)
    li = np.arange(SCAN_LANES) // RWKV_HEAD
    bdm = (li[:, None] == li[None, :]).astype(np.float32)
    hmask = jnp.asarray((li[None, None, :] == np.arange(heads)[:, None, None]), BF16)
    eye = np.tile(np.eye(chunk, dtype=np.float32), (1, heads))

    def chunk_index(d, s):
        return jnp.where(s < n_ctx, n_lat + s, s - n_ctx) if d == 0 else n_tot - 1 - s

    def const(shape):
        nd = len(shape)
        return pl.BlockSpec(shape, lambda b, h, s: (0,) * nd)

    in_specs = [const(gmask.shape), const(bdm.shape), const(hmask.shape), const(eye.shape)]
    args = [jnp.asarray(gmask), jnp.asarray(bdm), hmask, jnp.asarray(eye)]
    out_specs = []
    for d in range(2):
        in_specs += [pl.BlockSpec((1, 1, chunk, SCAN_PLANES * lane_block),
                                  lambda b, h, s, d=d: (d, b, chunk_index(d, s), h)),
                     pl.BlockSpec((1, 1, 1, 1, lane_block), lambda b, h, s, d=d: (d, b, chunk_index(d, s), 0, h))]
        args += [ops, dec]
        out_specs.append(pl.BlockSpec((1, chunk, lane_block), lambda b, h, s, d=d: (b, chunk_index(d, s), h)))
    ys = jax.ShapeDtypeStruct((bsz, tt, c), F32)
    return pl.pallas_call(
        functools.partial(_scan_kernel, chunk=chunk, nsub=nsub),
        grid=(bsz, c // lane_block, n_tot),
        in_specs=in_specs,
        out_specs=out_specs,
        out_shape=[ys, ys],
        scratch_shapes=[pltpu.VMEM((2 * nsub, SCAN_LANES, SCAN_LANES), F32)],
        compiler_params=_cparams(("arbitrary", "arbitrary", "arbitrary")),
        name="rwkv_scan",
    )(*args)


def _rout_kernel(yf, yb, bonus, g, gng, gnb, ones_ref, o_ref):
    ones_bd = ones_ref[...]
    inv_n = 1.0 / RWKV_HEAD
    y = yf[0] + yb[0]
    mu = _head_sum(y, ones_bd) * inv_n
    yc = y - mu
    var = _head_sum(yc * yc, ones_bd) * inv_n
    yn = yc * lax.rsqrt(var + GN_EPS) * gng[...] + gnb[...]
    o_ref[0] = ((yn + bonus[0]) * g[0]).astype(o_ref.dtype)


def _rwkv_out(yf, yb, bonus, g, gn_g, gn_b, t_lat, tb):
    bsz, _, c = g.shape
    li = np.arange(SCAN_LANES) // RWKV_HEAD
    ones_bd = jnp.asarray(li[:, None] == li[None, :], BF16)
    one = pl.BlockSpec((1, tb, c), lambda b, j: (b, j, 0))

    def const(shape):
        return pl.BlockSpec(shape, lambda b, j: (0, 0))

    return pl.pallas_call(
        _rout_kernel,
        grid=(bsz, t_lat // tb),
        in_specs=[one, one, one, one, const((1, c)), const((1, c)), const((SCAN_LANES, SCAN_LANES))],
        out_specs=one,
        out_shape=jax.ShapeDtypeStruct((bsz, t_lat, c), BF16),
        compiler_params=_cparams(("arbitrary", "arbitrary")),
        name="rwkv_out",
    )(yf, yb, bonus, g, gn_g.reshape(1, c), gn_b.reshape(1, c), ones_bd)


def _rope(x, c, s):
    return x * c + pltpu.roll(x, ATT_HEAD // 2, 1) * s


def _attn_kernel(sink_ref, q_ref, km_ref, k0_ref, kp_ref, vm_ref, v0_ref, vp_ref, kc_ref, vc_ref,
                 c0_ref, s0_ref, cm_ref, sm_ref, cp_ref, sp_ref, wb_ref, o_ref, *, n_blk):
    n = pl.program_id(1)
    blk = ATT_HEAD
    scale = ATT_HEAD ** -0.5
    kvh = range(ATT_KV_HEADS)
    c0, s0 = c0_ref[...], s0_ref[...]
    cm, sm = cm_ref[...], sm_ref[...]
    cp, sp = cp_ref[...], sp_ref[...]

    def head(ref, i):
        return ref[0, :, i * ATT_HEAD:(i + 1) * ATT_HEAD]

    qq = [jnp.concatenate([_rope(head(q_ref, g * ATT_GROUPS + h), c0, s0) for h in range(ATT_GROUPS)],
                          axis=0).astype(BF16) for g in kvh]
    kw = [jnp.concatenate([_rope(head(km_ref, g), cm, sm), _rope(head(k0_ref, g), c0, s0),
                           _rope(head(kp_ref, g), cp, sp)], axis=0).astype(BF16) for g in kvh]
    vw = [jnp.concatenate([head(vm_ref, g), head(v0_ref, g), head(vp_ref, g)], axis=0).astype(BF16) for g in kvh]
    kc = [head(kc_ref, g).astype(BF16) for g in kvh]
    vc = [head(vc_ref, g).astype(BF16) for g in kvh]
    jl = lax.broadcasted_iota(jnp.int32, (1, 3 * blk), 1)
    outside = jnp.logical_or(jnp.logical_and(jl < blk, n == 0), jnp.logical_and(jl >= 2 * blk, n == n_blk - 1))
    bias = wb_ref[...] + jnp.where(outside, NEG_INF, 0.0)
    s_s = [jnp.concatenate([jnp.full((blk, 1), sink_ref[g * ATT_GROUPS + h], F32) for h in range(ATT_GROUPS)],
                           axis=0) for g in kvh]
    s_w, s_c, m = [], [], []
    for g in kvh:
        s_w.append(_dot_nt(qq[g], kw[g]) * scale + bias)
        s_c.append(_dot_nt(qq[g], kc[g]) * scale)
        m.append(jnp.maximum(jnp.maximum(jnp.max(s_w[g], axis=-1, keepdims=True),
                                         jnp.max(s_c[g], axis=-1, keepdims=True)), s_s[g]))
    pb_w, pb_c, den = [], [], []
    for g in kvh:
        p_w = jnp.exp(s_w[g] - m[g])
        p_c = jnp.exp(s_c[g] - m[g])
        den.append(jnp.sum(p_w, axis=-1, keepdims=True) + jnp.sum(p_c, axis=-1, keepdims=True)
                   + jnp.exp(s_s[g] - m[g]))
        pb_w.append(p_w.astype(BF16))
        pb_c.append(p_c.astype(BF16))
    o = [(_dot(pb_w[g], vw[g]) + _dot(pb_c[g], vc[g])) / den[g] for g in kvh]
    o_ref[0] = jnp.concatenate([o[g][h * blk:(h + 1) * blk] for g in kvh for h in range(ATT_GROUPS)],
                               axis=1).astype(o_ref.dtype)


def _attention(proj, col, sink, cos2, sin2, t_lat):
    bsz, tt, _ = proj.shape
    blk = ATT_HEAD
    n_blk = t_lat // blk
    l = tt - t_lat
    qw = ATT_KV_HEADS * ATT_GROUPS * ATT_HEAD
    kw = ATT_KV_HEADS * ATT_HEAD
    q_blk = col["q"] * LANES // qw

    def kv(name, off):
        cb = col[name] * LANES // kw
        return pl.BlockSpec((1, blk, kw), lambda b, n: (b, jnp.clip(n + off, 0, n_blk - 1), cb))

    def ctx(name):
        cb = col[name] * LANES // kw
        return pl.BlockSpec((1, l, kw), lambda b, n: (b, t_lat // l, cb))

    def tab(off):
        return pl.BlockSpec((blk, ATT_HEAD), lambda b, n: (jnp.clip(n + off, 0, n_blk - 1), 0))

    in_specs = [pl.BlockSpec(memory_space=pltpu.SMEM),
                pl.BlockSpec((1, blk, qw), lambda b, n: (b, n, q_blk)),
                kv("ak", -1), kv("ak", 0), kv("ak", 1), kv("av", -1), kv("av", 0), kv("av", 1),
                ctx("ak"), ctx("av"), tab(0), tab(0), tab(-1), tab(-1), tab(1), tab(1),
                pl.BlockSpec((ATT_GROUPS * blk, 3 * blk), lambda b, n: (0, 0))]
    qi = np.arange(ATT_GROUPS * blk)[:, None] % blk
    kj = np.arange(3 * blk)[None, :] - blk
    band_bias = jnp.asarray(np.where(np.abs(qi - kj) <= blk, 0.0, NEG_INF), F32)
    return pl.pallas_call(
        functools.partial(_attn_kernel, n_blk=n_blk),
        grid=(bsz, n_blk),
        in_specs=in_specs,
        out_specs=pl.BlockSpec((1, blk, qw), lambda b, n: (b, n, 0)),
        out_shape=jax.ShapeDtypeStruct((bsz, t_lat, qw), BF16),
        compiler_params=_cparams(("arbitrary", "arbitrary")),
        name="window_attention",
    )(sink, proj, proj, proj, proj, proj, proj, proj, proj, proj, cos2, sin2, cos2, sin2, cos2, sin2, band_bias)


def _merge_kernel(o1_ref, o2_ref, w1_ref, w2_ref, g1_ref, g2_ref, o_ref):
    y1 = _dot(o1_ref[0], w1_ref[...])
    y2 = _dot(o2_ref[0], w2_ref[...])
    o_ref[0] = (jax.nn.sigmoid(g1_ref[0].astype(F32)) * y1
                + jax.nn.sigmoid(g2_ref[0].astype(F32)) * y2).astype(o_ref.dtype)


def _merge(o_rwkv, o_att, w1, w2, gates, tm, tn):
    bsz, t, c = o_rwkv.shape
    n = w1.shape[1]
    gr = 0
    ga = n // tn
    return pl.pallas_call(
        _merge_kernel,
        grid=(bsz, t // tm, n // tn),
        in_specs=[pl.BlockSpec((1, tm, c), lambda b, i, j: (b, i, 0)),
                  pl.BlockSpec((1, tm, c), lambda b, i, j: (b, i, 0)),
                  pl.BlockSpec((c, tn), lambda b, i, j: (0, j)),
                  pl.BlockSpec((c, tn), lambda b, i, j: (0, j)),
                  pl.BlockSpec((1, tm, tn), lambda b, i, j: (b, i, gr + j)),
                  pl.BlockSpec((1, tm, tn), lambda b, i, j: (b, i, ga + j))],
        out_specs=pl.BlockSpec((1, tm, tn), lambda b, i, j: (b, i, j)),
        out_shape=jax.ShapeDtypeStruct((bsz, t, n), BF16),
        compiler_params=_cparams(("arbitrary", "arbitrary", "arbitrary")),
        name="gated_merge",
    )(o_rwkv, o_att, w1, w2, gates, gates)


def _mm_resid_kernel(a_ref, w_ref, x_ref, gt_ref, o_ref, *, tiles_per_batch):
    bi = pl.program_id(0) // tiles_per_batch
    y = _dot(a_ref[...], w_ref[...])
    o_ref[...] = DEEPNORM_ALPHA * x_ref[...] + gt_ref[pl.ds(bi, 1), :] * y


def _mm_resid(a, w, x, mod, gate_col, rows_per_batch, tm, tn, name):
    m, k = a.shape
    n = w.shape[1]
    gcb = gate_col * (n // tn)
    return pl.pallas_call(
        functools.partial(_mm_resid_kernel, tiles_per_batch=rows_per_batch // tm),
        grid=(m // tm, n // tn),
        in_specs=[pl.BlockSpec((tm, k), lambda i, j: (i, 0)),
                  pl.BlockSpec((k, tn), lambda i, j: (0, j)),
                  pl.BlockSpec((tm, tn), lambda i, j: (i, j)),
                  pl.BlockSpec((MOD_ROWS, tn), lambda i, j: (0, gcb + j))],
        out_specs=pl.BlockSpec((tm, tn), lambda i, j: (i, j)),
        out_shape=jax.ShapeDtypeStruct((m, n), F32),
        compiler_params=_cparams(("arbitrary", "arbitrary")),
        name=name,
    )(a, w, x, mod)


def _ln_kernel(z_ref, g_ref, b_ref, *rest, with_mod):
    bi = pl.program_id(0)
    z = z_ref[0]
    mu = jnp.mean(z, axis=-1, keepdims=True)
    zc = z - mu
    var = jnp.mean(zc * zc, axis=-1, keepdims=True)
    out = zc * lax.rsqrt(var + LN_EPS) * g_ref[...] + b_ref[...]
    if with_mod:
        sh_ref, sc_ref, o_ref, h_ref = rest
        o_ref[0] = out
        h_ref[0] = (out * (1.0 + sc_ref[pl.ds(bi, 1), :]) + sh_ref[pl.ds(bi, 1), :]).astype(BF16)
    else:
        (o_ref,) = rest
        o_ref[0] = out


def _res_ln(z, mod, g, b, tb, mod_cols=None):
    bsz, t, d = z.shape
    blk = pl.BlockSpec((1, tb, d), lambda bi, j: (bi, j, 0))
    vec = pl.BlockSpec((1, d), lambda bi, j: (0, 0))

    def modspec(cb):
        return pl.BlockSpec((MOD_ROWS, d), lambda bi, j: (0, cb))

    in_specs = [blk, vec, vec]
    args = [z, g.reshape(1, d), b.reshape(1, d)]
    out_specs = [blk]
    out_shape = [jax.ShapeDtypeStruct((bsz, t, d), F32)]
    if mod_cols is not None:
        in_specs += [modspec(mod_cols[0]), modspec(mod_cols[1])]
        args += [mod, mod]
        out_specs.append(blk)
        out_shape.append(jax.ShapeDtypeStruct((bsz, t, d), BF16))
    return pl.pallas_call(
        functools.partial(_ln_kernel, with_mod=mod_cols is not None),
        grid=(bsz, t // tb),
        in_specs=in_specs,
        out_specs=out_specs,
        out_shape=out_shape,
        compiler_params=_cparams(("arbitrary", "arbitrary")),
        name="residual_layernorm",
    )(*args)


def _ffn_up_kernel(a_ref, wg_ref, wu_ref, o_ref):
    a = a_ref[...]
    gg = _dot(a, wg_ref[...].astype(BF16))
    uu = _dot(a, wu_ref[...].astype(BF16))
    o_ref[...] = (gg * jax.nn.sigmoid(gg) * uu).astype(o_ref.dtype)


def _ffn_up(a, wg, wu, tm, tf):
    m, k = a.shape
    f = wg.shape[1]
    return pl.pallas_call(
        _ffn_up_kernel,
        grid=(m // tm, f // tf),
        in_specs=[pl.BlockSpec((tm, k), lambda i, j: (i, 0)),
                  pl.BlockSpec((k, tf), lambda i, j: (0, j)),
                  pl.BlockSpec((k, tf), lambda i, j: (0, j))],
        out_specs=pl.BlockSpec((tm, tf), lambda i, j: (i, j)),
        out_shape=jax.ShapeDtypeStruct((m, f), BF16),
        compiler_params=_cparams(("arbitrary", "arbitrary")),
        name="swiglu_up",
    )(a, wg, wu)


def _rope_tables(t):
    rows = t // GRID_W
    row = jnp.broadcast_to(jnp.arange(rows, dtype=F32)[:, None], (rows, GRID_W)).reshape(t)
    colp = jnp.broadcast_to(jnp.arange(GRID_W, dtype=F32)[None, :], (rows, GRID_W)).reshape(t)
    axis_dim = ATT_HEAD // 2
    inv = ROPE_BASE ** (-jnp.arange(0, axis_dim, 2, dtype=F32) / axis_dim)
    ang = jnp.concatenate([row[:, None] * inv, colp[:, None] * inv], -1)
    cos, sin = jnp.cos(ang), jnp.sin(ang)
    return jnp.concatenate([cos, cos], -1), jnp.concatenate([-sin, sin], -1)


def _block(x, c, ctx, c_ctx, w_ada, b_ada, w_in, rwkv_shift, rwkv_w0, rwkv_w_up, rwkv_a0, rwkv_a_up,
           rwkv_g_up, rwkv_k_k, rwkv_k_a, rwkv_r_k, rwkv_gn_g, rwkv_gn_b, attn_sink, w_rwkv_o, w_att_o,
           w_out, ln1_g, ln1_b, w_ff_gate, w_ff_up, w_ff_down, ln2_g, ln2_b, tiles):
    bsz, t, d = x.shape
    l = ctx.shape[1]
    tt = t + l
    crw = rwkv_w0.shape[1]
    d_att = ATT_KV_HEADS * ATT_GROUPS * ATT_HEAD
    d_kv = ATT_KV_HEADS * ATT_HEAD
    n_lora = 2 * DECAY_LORA + 2 * ICLR_LORA + GATE_LORA
    n_rw = 3 * crw + n_lora

    assert bsz + 1 <= MOD_ROWS and l == tiles["prep_tb"] and t % tiles["prep_tb"] == 0
    cc = jnp.zeros((MOD_ROWS, d), F32).at[:bsz].set(c).at[bsz].set(c_ctx)
    mod = _ada(cc, w_ada, b_ada, tiles["ada_tn"])

    o_q = n_rw
    o_g = o_q + d_att + 2 * d_kv
    w_main = _stage_columns(w_in, [(0, 3 * crw), (o_q, o_g - o_q)], "stage_w_main")
    w_lora = _stage_columns(w_in, [(3 * crw, n_lora)], "stage_w_lora")
    w_gates = _stage_columns(w_in, [(o_g, w_in.shape[1] - o_g)], "stage_w_gates")
    col = {}
    off = 0
    for name, width in (("r", crw), ("k", crw), ("v", crw), ("q", d_att), ("ak", d_kv), ("av", d_kv)):
        col[name] = off // LANES
        off += width
    shift_rkv = rwkv_shift[:, :3 * crw]
    shift_lora = rwkv_shift[:, 3 * crw:]

    h = _prep(x, ctx, mod)
    h2d = h.reshape(bsz * tt, d)
    (proj,) = _mm_wstat(h2d, w_main, tiles["proj_tm"], tiles["proj_tn"], F32, "in_proj")
    proj = proj.reshape(bsz, tt, -1)
    proj_lora, w_out16, w_ro16, w_ao16 = _mm_wstat(
        h2d, w_lora, tiles["proj_tm"], n_lora, F32, "lora_proj",
        side=[(w_out, tiles["side_rows"]), (w_rwkv_o, tiles["side_rows"]), (w_att_o, tiles["side_rows"])])
    proj_lora = proj_lora.reshape(bsz, tt, -1)
    gates, w_down16 = _mm_lead_rows(h, w_gates, t, tiles["gate_tm"], tiles["gate_tn"], BF16, "gate_proj",
                                    side=[(w_ff_down, tiles["side_rows_down"])])

    g, bonus, ops, dec = _features(proj, proj_lora, col, shift_rkv, shift_lora, rwkv_w0, rwkv_w_up, rwkv_a0,
                                   rwkv_a_up, rwkv_g_up, rwkv_k_k, rwkv_k_a, rwkv_r_k.reshape(-1), t,
                                   tiles["feat_tb"])
    yf, yb = _scan(ops, dec, t, tiles["scan_lanes"])
    o_rwkv = _rwkv_out(yf, yb, bonus, g, rwkv_gn_g, rwkv_gn_b, t, tiles["rout_tb"])

    cos2, sin2 = _rope_tables(t)
    o_att = _attention(proj, col, attn_sink, cos2, sin2, t)

    ym = _merge(o_rwkv, o_att, w_ro16, w_ao16, gates, tiles["merge_tm"], tiles["merge_tn"])
    z1 = _mm_resid(ym.reshape(bsz * t, d), w_out16, x.reshape(bsz * t, d), mod, 2, t,
                   tiles["out_tm"], tiles["out_tn"], "out_proj")
    x1, h2 = _res_ln(z1.reshape(bsz, t, d), mod, ln1_g, ln1_b, tiles["ln_tb"], mod_cols=(3, 4))

    u = _ffn_up(h2.reshape(bsz * t, d), w_ff_gate, w_ff_up, tiles["ffn_tm"], tiles["ffn_tf"])
    z2 = _mm_resid(u, w_down16, x1.reshape(bsz * t, d), mod, 5, t,
                   tiles["down_tm"], tiles["down_tn"], "swiglu_down")
    (out,) = _res_ln(z2.reshape(bsz, t, d), mod, ln2_g, ln2_b, tiles["ln_tb"])
    return out


_TILES = dict(ada_tn=512, prep_tb=256, proj_tm=512, proj_tn=1536, gate_tm=1024, gate_tn=1024,
              side_rows=128, side_rows_down=256, feat_tb=128, scan_lanes=2048, rout_tb=256,
              merge_tm=1024, merge_tn=1024, out_tm=1024, out_tn=1024, ln_tb=512,
              ffn_tm=2048, ffn_tf=256, down_tm=512, down_tn=512)


def kernel(x, c, ctx, c_ctx, w_ada, b_ada, w_in, rwkv_shift, rwkv_w0, rwkv_w_up, rwkv_a0, rwkv_a_up, rwkv_g_up, rwkv_k_k, rwkv_k_a, rwkv_r_k, rwkv_gn_g, rwkv_gn_b, attn_sink, w_rwkv_o, w_att_o, w_out, ln1_g, ln1_b, w_ff_gate, w_ff_up, w_ff_down, ln2_g, ln2_b):
    assert w_ada.shape[0] == DEPTH
    return _block(x, c, ctx, c_ctx, w_ada[0], b_ada[0], w_in[0], rwkv_shift[0], rwkv_w0[0], rwkv_w_up[0],
                  rwkv_a0[0], rwkv_a_up[0], rwkv_g_up[0], rwkv_k_k[0], rwkv_k_a[0], rwkv_r_k[0], rwkv_gn_g[0],
                  rwkv_gn_b[0], attn_sink[0], w_rwkv_o[0], w_att_o[0], w_out[0], ln1_g[0], ln1_b[0],
                  w_ff_gate[0], w_ff_up[0], w_ff_down[0], ln2_g[0], ln2_b[0], _TILES)
```
